```python
import math
import jax, jax.numpy as jnp
from jax import lax
import numpy as np

D_MODEL = 2048
BATCH = 4
SEQ = 2048
DEPTH = 2

CTX_LEN = 256
GRID_W = 64
N_MIXERS = 2
N_ATTN_LAYERS = (DEPTH + N_MIXERS - 1) // N_MIXERS
N_HYENA_LAYERS = DEPTH // N_MIXERS

HEAD_DIM = 128
N_HEADS = D_MODEL // HEAD_DIM
N_KV_HEADS = N_HEADS // 4
GQA_GROUP = N_HEADS // N_KV_HEADS
ATTN_WIDTH = N_HEADS * HEAD_DIM
KV_WIDTH = N_KV_HEADS * HEAD_DIM
ATTN_IN = 2 * ATTN_WIDTH + 2 * KV_WIDTH
WINDOW = 128
BLOCK = 128
ROPE_BASE = 10000.0

HYENA_WIDTH = D_MODEL
HYENA_ORDER = 2
HYENA_IN = (HYENA_ORDER + 2) * HYENA_WIDTH
SHORT_CONV = 3
FILTER_EMB = 33
FILTER_HIDDEN = 64
DECAY_FAST = 0.3
DECAY_SLOW = 1.5
DECAY_TARGET = 1e-2
WINDOW_SHIFT = 0.05

NORM_EPS = 1e-6
NEG_INF = -1e30

kernel_name = "hybrid_swa_hyena_prefix_dit"


def rms_norm(x, g):
    x32 = x.astype(jnp.float32)
    y = x32 * lax.rsqrt(jnp.mean(x32 * x32, axis=-1, keepdims=True) + NORM_EPS)
    return (y * g.astype(jnp.float32)).astype(x.dtype)


def modulate(h, shift, scale):
    return h * (1.0 + scale) + shift


def axial_rope_tables(S):
    rows = S // GRID_W
    row = jnp.repeat(jnp.arange(rows), GRID_W).astype(jnp.float32)
    col = jnp.tile(jnp.arange(GRID_W), rows).astype(jnp.float32)
    half = HEAD_DIM // 2
    inv = ROPE_BASE ** (-jnp.arange(0, half, 2, dtype=jnp.float32) / half)
    ang = jnp.stack([row[:, None] * inv[None], col[:, None] * inv[None]], axis=1)
    return jnp.cos(ang), jnp.sin(ang)


def apply_axial_rope(t, cos, sin):
    B, S, nh, _ = t.shape
    half = HEAD_DIM // 2
    q = half // 2
    seg = t.astype(jnp.float32).reshape(B, S, nh, 2, half)
    a, b = seg[..., :q], seg[..., q:]
    cs = cos[None, :, None]
    sn = sin[None, :, None]
    out = jnp.concatenate([a * cs - b * sn, b * cs + a * sn], axis=-1)
    return out.reshape(t.shape).astype(t.dtype)


def _band(t, nb):
    B = t.shape[0]
    tb = t.reshape(B, nb, BLOCK, t.shape[2], t.shape[3])
    tp = jnp.pad(tb, ((0, 0), (1, 1), (0, 0), (0, 0), (0, 0)))
    return jnp.concatenate([tp[:, :-2], tp[:, 1:-1], tp[:, 2:]], axis=2)


def banded_attention(q, k, v, kc, vc, sink):
    B, S = q.shape[0], q.shape[1]
    C = kc.shape[1]
    nb = S // BLOCK
    scale = HEAD_DIM ** -0.5
    qb = q.reshape(B, nb, BLOCK, N_KV_HEADS, GQA_GROUP, HEAD_DIM)
    kb = _band(k, nb)
    vb = _band(v, nb)
    s_loc = jnp.einsum("bnqkgd,bnskd->bnkgqs", qb, kb).astype(jnp.float32) * scale
    qpos = jnp.arange(nb)[:, None] * BLOCK + jnp.arange(BLOCK)[None]
    kpos = jnp.arange(nb)[:, None] * BLOCK - BLOCK + jnp.arange(3 * BLOCK)[None]
    rel = kpos[:, None, :] - qpos[:, :, None]
    valid = (jnp.abs(rel) <= WINDOW) & (kpos[:, None, :] >= 0) & (kpos[:, None, :] < S)
    s_loc = jnp.where(valid[None, :, None, None], s_loc, NEG_INF)
    s_ctx = jnp.einsum("bnqkgd,bckd->bnkgqc", qb, kc).astype(jnp.float32) * scale
    sk = sink.astype(jnp.float32).reshape(N_KV_HEADS, GQA_GROUP)[None, None, :, :, None, None]
    s_sink = jnp.broadcast_to(sk, s_loc.shape[:-1] + (1,))
    p = jax.nn.softmax(jnp.concatenate([s_loc, s_ctx, s_sink], axis=-1), axis=-1).astype(v.dtype)
    nl = 3 * BLOCK
    o = (jnp.einsum("bnkgqs,bnskd->bnqkgd", p[..., :nl], vb)
         + jnp.einsum("bnkgqc,bckd->bnqkgd", p[..., nl:nl + C], vc))
    return o.reshape(B, S, ATTN_WIDTH)


def context_attention(qc, kc, vc, sink):
    B, C = qc.shape[0], qc.shape[1]
    scale = HEAD_DIM ** -0.5
    qg = qc.reshape(B, C, N_KV_HEADS, GQA_GROUP, HEAD_DIM)
    s = jnp.einsum("bqkgd,bskd->bkgqs", qg, kc).astype(jnp.float32) * scale
    sk = sink.astype(jnp.float32).reshape(N_KV_HEADS, GQA_GROUP)[None, :, :, None, None]
    s_sink = jnp.broadcast_to(sk, s.shape[:-1] + (1,))
    p = jax.nn.softmax(jnp.concatenate([s, s_sink], axis=-1), axis=-1).astype(vc.dtype)
    o = jnp.einsum("bkgqs,bskd->bqkgd", p[..., :C], vc)
    return o.reshape(B, C, ATTN_WIDTH)


def attention_layer(hx, hc, w_in, w_out, sink, cos, sin, ctx_out):
    B, S, _ = hx.shape
    C = hc.shape[1]
    px = hx @ w_in
    q = px[..., :ATTN_WIDTH].reshape(B, S, N_HEADS, HEAD_DIM)
    k = px[..., ATTN_WIDTH:ATTN_WIDTH + KV_WIDTH].reshape(B, S, N_KV_HEADS, HEAD_DIM)
    v = px[..., ATTN_WIDTH + KV_WIDTH:ATTN_WIDTH + 2 * KV_WIDTH].reshape(B, S, N_KV_HEADS, HEAD_DIM)
    g = px[..., ATTN_WIDTH + 2 * KV_WIDTH:]
    q = apply_axial_rope(q, cos, sin)
    k = apply_axial_rope(k, cos, sin)
    if ctx_out:
        pc = hc @ w_in
        kvc = pc[..., ATTN_WIDTH:ATTN_WIDTH + 2 * KV_WIDTH]
    else:
        kvc = hc @ w_in[:, ATTN_WIDTH:ATTN_WIDTH + 2 * KV_WIDTH]
    kc = kvc[..., :KV_WIDTH].reshape(B, C, N_KV_HEADS, HEAD_DIM)
    vc = kvc[..., KV_WIDTH:].reshape(B, C, N_KV_HEADS, HEAD_DIM)
    o = banded_attention(q, k, v, kc, vc, sink)
    out_x = (o * jax.nn.silu(g)) @ w_out
    out_c = None
    if ctx_out:
        qc = pc[..., :ATTN_WIDTH].reshape(B, C, N_HEADS, HEAD_DIM)
        gc = pc[..., ATTN_WIDTH + 2 * KV_WIDTH:]
        oc = context_attention(qc, kc, vc, sink)
        out_c = (oc * jax.nn.silu(gc)) @ w_out
    return out_x, out_c


def short_conv(u, w, b):
    L = u.shape[1]
    pad = SHORT_CONV // 2
    up = jnp.pad(u, ((0, 0), (pad, pad), (0, 0)))
    y = up[:, 0:L] * w[0]
    for j in range(1, SHORT_CONV):
        y = y + up[:, j:j + L] * w[j]
    return y + b


def implicit_filters(L, w1, b1, w2, b2, w3, b3, freq):
    f32 = jnp.float32
    t = jnp.linspace(0.0, 1.0, L, dtype=f32)[:, None]
    bands = (FILTER_EMB - 1) // 2
    w = 2.0 * math.pi * jnp.arange(L, dtype=f32) / L
    fr = jnp.linspace(1e-4, bands - 1, bands, dtype=f32)
    ang = w[:, None] * fr[None]
    z = jnp.concatenate([t, jnp.cos(ang), -jnp.sin(ang)], axis=-1)
    fq = freq.astype(f32)
    hid = jnp.sin(fq * (z @ w1.astype(f32) + b1.astype(f32)))
    hid = jnp.sin(fq * (hid @ w2.astype(f32) + b2.astype(f32)))
    hf = (hid @ w3.astype(f32) + b3.astype(f32)).reshape(L, HYENA_ORDER, 2, HYENA_WIDTH)
    max_decay = math.log(DECAY_TARGET) / DECAY_FAST
    min_decay = math.log(DECAY_TARGET) / DECAY_SLOW
    deltas = jnp.linspace(min_decay, max_decay, HYENA_WIDTH, dtype=f32)
    window = jnp.exp(-t * jnp.abs(deltas)[None])[:, None, None, :] + WINDOW_SHIFT
    return hf * window


def bidir_long_conv(u, h_fwd, h_bwd, d):
    L = u.shape[1]
    k = jnp.concatenate([h_fwd, jnp.zeros((1, h_fwd.shape[1]), jnp.float32), h_bwd[1:][::-1]], axis=0)
    u32 = u.astype(jnp.float32)
    U = jnp.fft.rfft(u32, n=2 * L, axis=1)
    K = jnp.fft.rfft(k, n=2 * L, axis=0)
    y = jnp.fft.irfft(U * K[None], n=2 * L, axis=1)[:, :L]
    return (y + u32 * d.astype(jnp.float32)).astype(u.dtype)


def hyena_branch(h, w_in, conv_w, conv_b, w1, b1, w2, b2, w3, b3, freq, bias_d, w_out):
    L = h.shape[1]
    p = h @ w_in
    u = short_conv(p[..., :3 * HYENA_WIDTH], conv_w, conv_b)
    g = p[..., 3 * HYENA_WIDTH:]
    x1 = u[..., :HYENA_WIDTH]
    x2 = u[..., HYENA_WIDTH:2 * HYENA_WIDTH]
    v = u[..., 2 * HYENA_WIDTH:]
    filt = implicit_filters(L, w1, b1, w2, b2, w3, b3, freq)
    z = x1 * bidir_long_conv(v, filt[:, 0, 0], filt[:, 0, 1], bias_d[0])
    y = x2 * bidir_long_conv(z, filt[:, 1, 0], filt[:, 1, 1], bias_d[1])
    return (y * jax.nn.silu(g)) @ w_out


def setup_inputs(seed: int = 0) -> dict:
    key = jax.random.key(seed)
    ks = jax.random.split(key, 24)
    D = D_MODEL
    nrm = jax.random.normal
    return {
        "x": nrm(ks[0], (BATCH, SEQ, D), jnp.float32),
        "c": nrm(ks[1], (BATCH, D), jnp.float32),
        "ctx": nrm(ks[2], (BATCH, CTX_LEN, D), jnp.float32),
        "c_ctx": nrm(ks[3], (D,), jnp.float32),
        "norm_g": 1.0 + 0.02 * nrm(ks[4], (DEPTH, D), jnp.float32),
        "ada_w": 0.5 * D ** -0.5 * nrm(ks[5], (DEPTH, D, 3 * D), jnp.float32),
        "ada_b": 0.02 * nrm(ks[6], (DEPTH, 3 * D), jnp.float32),
        "attn_w_in": D ** -0.5 * nrm(ks[7], (N_ATTN_LAYERS, D, ATTN_IN), jnp.float32),
        "attn_w_out": ATTN_WIDTH ** -0.5 * nrm(ks[8], (N_ATTN_LAYERS, ATTN_WIDTH, D), jnp.float32),
        "attn_sink": 0.5 * nrm(ks[9], (N_ATTN_LAYERS, N_HEADS), jnp.float32),
        "hy_w_in": D ** -0.5 * nrm(ks[10], (N_HYENA_LAYERS, D, HYENA_IN), jnp.float32),
        "hy_conv_w": SHORT_CONV ** -0.5 * nrm(ks[11], (N_HYENA_LAYERS, SHORT_CONV, 3 * HYENA_WIDTH), jnp.float32),
        "hy_conv_b": 0.02 * nrm(ks[12], (N_HYENA_LAYERS, 3 * HYENA_WIDTH), jnp.float32),
        "hy_w1": FILTER_EMB ** -0.5 * nrm(ks[13], (N_HYENA_LAYERS, FILTER_EMB, FILTER_HIDDEN), jnp.float32),
        "hy_b1": 0.02 * nrm(ks[14], (N_HYENA_LAYERS, FILTER_HIDDEN), jnp.float32),
        "hy_w2": FILTER_HIDDEN ** -0.5 * nrm(ks[15], (N_HYENA_LAYERS, FILTER_HIDDEN, FILTER_HIDDEN), jnp.float32),
        "hy_b2": 0.02 * nrm(ks[16], (N_HYENA_LAYERS, FILTER_HIDDEN), jnp.float32),
        "hy_w3": 0.05 * FILTER_HIDDEN ** -0.5 * nrm(ks[17], (N_HYENA_LAYERS, FILTER_HIDDEN, HYENA_ORDER * 2 * HYENA_WIDTH), jnp.float32),
        "hy_b3": 0.01 * nrm(ks[18], (N_HYENA_LAYERS, HYENA_ORDER * 2 * HYENA_WIDTH), jnp.float32),
        "hy_freq": 1.0 + 0.02 * nrm(ks[19], (N_HYENA_LAYERS, FILTER_HIDDEN), jnp.float32),
        "hy_bias_d": 0.1 * nrm(ks[20], (N_HYENA_LAYERS, HYENA_ORDER, HYENA_WIDTH), jnp.float32),
        "hy_w_out": HYENA_WIDTH ** -0.5 * nrm(ks[21], (N_HYENA_LAYERS, HYENA_WIDTH, D), jnp.float32),
        "final_g": 1.0 + 0.02 * nrm(ks[22], (D,), jnp.float32),
    }


def reference(x, c, ctx, c_ctx, norm_g, ada_w, ada_b, attn_w_in, attn_w_out, attn_sink,
              hy_w_in, hy_conv_w, hy_conv_b, hy_w1, hy_b1, hy_w2, hy_b2, hy_w3, hy_b3,
              hy_freq, hy_bias_d, hy_w_out, final_g):
    S = x.shape[1]
    cos, sin = axial_rope_tables(S)
    xc = ctx
    sc = jax.nn.silu(c)
    scc = jax.nn.silu(c_ctx)
    for i in range(DEPTH):
        li = i // N_MIXERS
        ctx_later = any(j % N_MIXERS == 0 for j in range(i + 1, DEPTH))
        mod = sc @ ada_w[i] + ada_b[i]
        shift, scale, gate = jnp.split(mod, 3, axis=-1)
        hx = modulate(rms_norm(x, norm_g[i]), shift[:, None], scale[:, None])
        if i % N_MIXERS == 0:
            mod_c = scc @ ada_w[i] + ada_b[i]
            shift_c, scale_c, gate_c = jnp.split(mod_c, 3, axis=-1)
            hc = modulate(rms_norm(xc, norm_g[i]), shift_c, scale_c)
            out_x, out_c = attention_layer(hx, hc, attn_w_in[li], attn_w_out[li], attn_sink[li],
                                           cos, sin, ctx_later)
            x = x + gate[:, None] * out_x
            if ctx_later:
                xc = xc + gate_c * out_c
        else:
            hp = (hy_w_in[li], hy_conv_w[li], hy_conv_b[li], hy_w1[li], hy_b1[li], hy_w2[li],
                  hy_b2[li], hy_w3[li], hy_b3[li], hy_freq[li], hy_bias_d[li], hy_w_out[li])
            x = x + gate[:, None] * hyena_branch(hx, *hp)
            if ctx_later:
                mod_c = scc @ ada_w[i] + ada_b[i]
                shift_c, scale_c, gate_c = jnp.split(mod_c, 3, axis=-1)
                hc = modulate(rms_norm(xc, norm_g[i]), shift_c, scale_c)
                xc = xc + gate_c * hyena_branch(hc, *hp)
    return rms_norm(x, final_g)
```

```python
import functools
import math

import jax
import jax.numpy as jnp
from jax import lax
from jax.experimental import pallas as pl
from jax.experimental.pallas import tpu as pltpu

F32 = jnp.float32
BF16 = jnp.bfloat16

D_MODEL = 2048
SEQ = 2048
CTX_LEN = 256
GRID_W = 64
HEAD_DIM = 128
N_HEADS = 16
N_KV_HEADS = 4
GQA_GROUP = 4
ATTN_WIDTH = 2048
KV_WIDTH = 512
ATTN_IN = 2 * ATTN_WIDTH + 2 * KV_WIDTH
WINDOW = 128
BLOCK = 128
ROPE_BASE = 10000.0
HYENA_WIDTH = 2048
FILTER_EMB = 33
FILTER_HIDDEN = 64
DECAY_FAST = 0.3
DECAY_SLOW = 1.5
DECAY_TARGET = 1e-2
WINDOW_SHIFT = 0.05
NORM_EPS = 1e-6
NEG_INF = -1e30

LANES = 128
FFT_N = 2 * SEQ
VMEM_LIMIT = 56 * 1024 * 1024


def _cparams(sem):
    return pltpu.CompilerParams(dimension_semantics=sem, vmem_limit_bytes=VMEM_LIMIT)


def _mods_kernel(c_ref, w_ref, b_ref, o_ref):
    c = c_ref[...]
    s = c * jax.nn.sigmoid(c)
    s_hi = s.astype(BF16)
    s_lo = (s - s_hi.astype(F32)).astype(BF16)
    w = w_ref[0]
    w_hi = w.astype(BF16)
    w_lo = (w - w_hi.astype(F32)).astype(BF16)
    lhs = jnp.concatenate([s_hi, s_lo], axis=0)
    r1 = jnp.dot(lhs, w_hi, preferred_element_type=F32)
    r2 = jnp.dot(s_hi, w_lo, preferred_element_type=F32)
    o_ref[0] = r1[:8] + r1[8:] + r2 + b_ref[0]


def _mods(cc, ada_w, ada_b):
    depth, d, n = ada_w.shape
    tn = 512
    return pl.pallas_call(
        _mods_kernel,
        grid=(depth, n // tn),
        in_specs=[
            pl.BlockSpec((8, d), lambda l, j: (0, 0)),
            pl.BlockSpec((1, d, tn), lambda l, j: (l, 0, j)),
            pl.BlockSpec((1, 1, tn), lambda l, j: (l, 0, j)),
        ],
        out_specs=pl.BlockSpec((1, 8, tn), lambda l, j: (l, 0, j)),
        out_shape=jax.ShapeDtypeStruct((depth, 8, n), F32),
        compiler_params=_cparams(("arbitrary", "arbitrary")),
        name="adaln_mods",
    )(cc, ada_w, ada_b.reshape(depth, 1, n))


def _rope_slab(t, cos, sin):
    lane = lax.broadcasted_iota(jnp.int32, t.shape, 1)
    first = (lane % 64) < 32
    partner = jnp.where(first, pltpu.roll(t, 96, 1), pltpu.roll(t, 32, 1))
    return t * cos + partner * sin


def _norm_proj_kernel(*refs, head_types, row_chunk):
    if head_types is None:
        x_ref, g_ref, sh_ref, sc_ref, w_ref, o_ref, hx_ref = refs
    else:
        x_ref, g_ref, sh_ref, sc_ref, w_ref, cos_ref, sin_ref, o_ref, hx_ref = refs
    j = pl.program_id(1)
    tm = x_ref.shape[0]

    @pl.when(j == 0)
    def _():
        g = g_ref[...]
        mul = 1.0 + sc_ref[0]
        add = sh_ref[0]
        for r in range(0, tm, row_chunk):
            x = x_ref[r:r + row_chunk, :]
            ms = jnp.mean(x * x, axis=-1, keepdims=True)
            y = x * lax.rsqrt(ms + NORM_EPS) * g
            hx_ref[r:r + row_chunk, :] = (y * mul + add).astype(BF16)

    acc = jnp.dot(hx_ref[...], w_ref[...], preferred_element_type=F32)

    if head_types is None:
        o_ref[...] = acc.astype(o_ref.dtype)
        return

    q_scale = HEAD_DIM ** -0.5
    n_rope = len(head_types)
    for jj, types in enumerate(head_types):
        @pl.when(j == jj)
        def _(types=types):
            cos = cos_ref[...]
            sin = sin_ref[...]
            for h, ty in enumerate(types):
                slab = acc[:, h * LANES:(h + 1) * LANES]
                if ty is not None:
                    slab = _rope_slab(slab, cos, sin)
                    if ty == "q":
                        slab = slab * q_scale
                o_ref[:, h * LANES:(h + 1) * LANES] = slab.astype(o_ref.dtype)

    @pl.when(j >= n_rope)
    def _():
        o_ref[...] = acc.astype(o_ref.dtype)


def _norm_proj(x, g, shift, scale, w, *, tm, tn, rows_per_mod, rope=None):
    m, d = x.shape
    n = w.shape[1]
    nb = shift.shape[0]
    if nb > 1:
        mod_map = lambda i, j: ((i * tm) // rows_per_mod, 0, 0)
    else:
        mod_map = lambda i, j: (0, 0, 0)
    in_specs = [
        pl.BlockSpec((tm, d), lambda i, j: (i, 0)),
        pl.BlockSpec((1, d), lambda i, j: (0, 0)),
        pl.BlockSpec((1, 1, d), mod_map),
        pl.BlockSpec((1, 1, d), mod_map),
        pl.BlockSpec((d, tn), lambda i, j: (0, j)),
    ]
    args = [x, g.reshape(1, d), shift, scale, w]
    head_types = None
    if rope is not None:
        cos, sin, head_types = rope
        seq_tiles = cos.shape[0] // tm
        in_specs += [pl.BlockSpec((tm, LANES), lambda i, j: (i % seq_tiles, 0))] * 2
        args += [cos, sin]
    return pl.pallas_call(
        functools.partial(_norm_proj_kernel, head_types=head_types, row_chunk=min(tm, 256)),
        grid=(m // tm, n // tn),
        in_specs=in_specs,
        out_specs=pl.BlockSpec((tm, tn), lambda i, j: (i, j)),
        out_shape=jax.ShapeDtypeStruct((m, n), BF16),
        scratch_shapes=[pltpu.VMEM((tm, d), BF16)],
        compiler_params=_cparams(("arbitrary", "arbitrary")),
        name="norm_proj",
    )(*args)


def _attn_kernel(sink_ref, q_ref, k_ref, v_ref, g_ref, kc_ref, vc_ref, o_ref):
    kh = pl.program_id(1)
    band = 3 * BLOCK
    rows = GQA_GROUP * BLOCK
    kc = kc_ref[...]
    vc = vc_ref[...]
    sink_col = jnp.concatenate(
        [jnp.full((BLOCK, 1), sink_ref[kh * GQA_GROUP + h], F32) for h in range(GQA_GROUP)], axis=0)
    qrow = lax.broadcasted_iota(jnp.int32, (rows, band), 0) % BLOCK
    kcol = lax.broadcasted_iota(jnp.int32, (rows, band), 1)
    nt = (((1,), (1,)), ((), ()))

    def body(n, carry):
        q0 = pl.multiple_of(n * BLOCK, BLOCK)
        ks = pl.multiple_of(jnp.clip((n - 1) * BLOCK, 0, SEQ - band), BLOCK)
        qs = q_ref[pl.ds(q0, BLOCK), :]
        q4 = jnp.concatenate([qs[:, h * LANES:(h + 1) * LANES] for h in range(GQA_GROUP)], axis=0)
        kb = k_ref[pl.ds(ks, band), :]
        vb = v_ref[pl.ds(ks, band), :]
        s_loc = lax.dot_general(q4, kb, nt, preferred_element_type=F32)
        s_ctx = lax.dot_general(q4, kc, nt, preferred_element_type=F32)
        rel = (ks + kcol) - (q0 + qrow)
        s_loc = jnp.where(jnp.abs(rel) <= WINDOW, s_loc, NEG_INF)
        m = jnp.maximum(jnp.maximum(jnp.max(s_loc, axis=-1, keepdims=True),
                                    jnp.max(s_ctx, axis=-1, keepdims=True)), sink_col)
        p_loc = jnp.exp(s_loc - m)
        p_ctx = jnp.exp(s_ctx - m)
        den = (jnp.sum(p_loc, axis=-1, keepdims=True) + jnp.sum(p_ctx, axis=-1, keepdims=True)
               + jnp.exp(sink_col - m))
        o = (jnp.dot(p_loc.astype(BF16), vb, preferred_element_type=F32)
             + jnp.dot(p_ctx.astype(BF16), vc, preferred_element_type=F32))
        o = o / den
        gs = g_ref[pl.ds(q0, BLOCK), :].astype(F32)
        for h in range(GQA_GROUP):
            gh = gs[:, h * LANES:(h + 1) * LANES]
            oh = o[h * BLOCK:(h + 1) * BLOCK, :] * (gh * jax.nn.sigmoid(gh))
            o_ref[pl.ds(q0, BLOCK), h * LANES:(h + 1) * LANES] = oh.astype(o_ref.dtype)
        return carry

    lax.fori_loop(0, SEQ // BLOCK, body, 0)


def _attention(px, ckv, sink, batch):
    gw = GQA_GROUP * HEAD_DIM
    k_blk0 = ATTN_WIDTH // HEAD_DIM
    v_blk0 = (ATTN_WIDTH + KV_WIDTH) // HEAD_DIM
    g_blk0 = (ATTN_WIDTH + 2 * KV_WIDTH) // gw
    return pl.pallas_call(
        _attn_kernel,
        grid=(batch, N_KV_HEADS),
        in_specs=[
            pl.BlockSpec(memory_space=pltpu.SMEM),
            pl.BlockSpec((SEQ, gw), lambda b, h: (b, h)),
            pl.BlockSpec((SEQ, HEAD_DIM), lambda b, h: (b, k_blk0 + h)),
            pl.BlockSpec((SEQ, HEAD_DIM), lambda b, h: (b, v_blk0 + h)),
            pl.BlockSpec((SEQ, gw), lambda b, h: (b, g_blk0 + h)),
            pl.BlockSpec((CTX_LEN, HEAD_DIM), lambda b, h: (b, h)),
            pl.BlockSpec((CTX_LEN, HEAD_DIM), lambda b, h: (b, N_KV_HEADS + h)),
        ],
        out_specs=pl.BlockSpec((SEQ, gw), lambda b, h: (b, h)),
        out_shape=jax.ShapeDtypeStruct((batch * SEQ, ATTN_WIDTH), BF16),
        compiler_params=_cparams(("arbitrary", "arbitrary")),
        name="banded_attention",
    )(sink, px, px, px, px, ckv, ckv)


def _out_proj_kernel(*refs, final):
    if final:
        a_ref, w_ref, x_ref, gate_ref, fg_ref, o_ref = refs
    else:
        a_ref, w_ref, x_ref, gate_ref, o_ref = refs
    acc = jnp.dot(a_ref[...], w_ref[...], preferred_element_type=F32)
    y = x_ref[...] + gate_ref[0] * acc
    if final:
        ms = jnp.mean(y * y, axis=-1, keepdims=True)
        y = y * lax.rsqrt(ms + NORM_EPS) * fg_ref[...]
    o_ref[...] = y


def _out_proj(a, w, x, gate, final_g=None, *, tm=512):
    m, d = x.shape
    kdim = a.shape[1]
    final = final_g is not None
    in_specs = [
        pl.BlockSpec((tm, kdim), lambda i: (i, 0)),
        pl.BlockSpec((kdim, d), lambda i: (0, 0)),
        pl.BlockSpec((tm, d), lambda i: (i, 0)),
        pl.BlockSpec((1, 1, d), lambda i: ((i * tm) // SEQ, 0, 0)),
    ]
    args = [a, w, x, gate]
    if final:
        in_specs.append(pl.BlockSpec((1, d), lambda i: (0, 0)))
        args.append(final_g.reshape(1, d))
    return pl.pallas_call(
        functools.partial(_out_proj_kernel, final=final),
        grid=(m // tm,),
        in_specs=in_specs,
        out_specs=pl.BlockSpec((tm, d), lambda i: (i, 0)),
        out_shape=jax.ShapeDtypeStruct((m, d), F32),
        compiler_params=_cparams(("arbitrary",)),
        name="out_proj",
    )(*args)


def _short_conv_kernel(p_ref, w_ref, b_ref, o_ref):
    length = p_ref.shape[0]
    row = lax.broadcasted_iota(jnp.int32, (length, LANES), 0)
    for c in range(0, p_ref.shape[1], LANES):
        u = p_ref[:, c:c + LANES].astype(F32)
        w = w_ref[:, c:c + LANES]
        prev = jnp.where(row == 0, 0.0, pltpu.roll(u, 1, 0))
        nxt = jnp.where(row == length - 1, 0.0, pltpu.roll(u, length - 1, 0))
        y = prev * w[0:1] + u * w[1:2] + nxt * w[2:3] + b_ref[:, c:c + LANES]
        o_ref[:, c:c + LANES] = y.astype(o_ref.dtype)


def _short_conv(p, conv_w, conv_b, batch, *, tn=512):
    n = conv_w.shape[1]
    return pl.pallas_call(
        _short_conv_kernel,
        grid=(batch, n // tn),
        in_specs=[
            pl.BlockSpec((SEQ, tn), lambda b, j: (b, j)),
            pl.BlockSpec((3, tn), lambda b, j: (0, j)),
            pl.BlockSpec((1, tn), lambda b, j: (0, j)),
        ],
        out_specs=pl.BlockSpec((SEQ, tn), lambda b, j: (b, j)),
        out_shape=jax.ShapeDtypeStruct((batch * SEQ, n), BF16),
        compiler_params=_cparams(("arbitrary", "arbitrary")),
        name="short_conv",
    )(p, conv_w, conv_b.reshape(1, n))


def _filter_kernel(w1t_ref, w1c_ref, w1s_ref, b1_ref, w2_ref, b2_ref, fq_ref, w3_ref, b3_ref,
                   o_ref, hid_ref):
    j = pl.program_id(0)
    length = o_ref.shape[0]
    tn = o_ref.shape[1]
    hp = lax.Precision.HIGHEST

    @pl.when(j == 0)
    def _():
        n = lax.broadcasted_iota(jnp.int32, (length, LANES), 0).astype(F32)
        lane = lax.broadcasted_iota(jnp.int32, (length, LANES), 1)
        bands = (FILTER_EMB - 1) // 2
        fr_step = (bands - 1 - 1e-4) / (bands - 1)
        fr = jnp.where(lane < bands, 1e-4 + lane.astype(F32) * fr_step, 0.0)
        ang = (2.0 * math.pi * n / length) * fr
        t = n / (length - 1)
        fq = fq_ref[...]
        pre = (t * w1t_ref[...]
               + jnp.dot(jnp.cos(ang), w1c_ref[...], precision=hp, preferred_element_type=F32)
               + jnp.dot(-jnp.sin(ang), w1s_ref[...], precision=hp, preferred_element_type=F32)
               + b1_ref[...])
        hid = jnp.sin(fq * pre)
        hid = jnp.sin(fq * (jnp.dot(hid, w2_ref[...], precision=hp, preferred_element_type=F32)
                            + b2_ref[...]))
        hid_ref[...] = hid

    hf = jnp.dot(hid_ref[...], w3_ref[...], precision=hp, preferred_element_type=F32) + b3_ref[...]
    row = lax.broadcasted_iota(jnp.int32, (length, tn), 0)
    col = lax.broadcasted_iota(jnp.int32, (length, tn), 1) + j * tn
    chan = (col % HYENA_WIDTH).astype(F32)
    min_decay = math.log(DECAY_TARGET) / DECAY_SLOW
    max_decay = math.log(DECAY_TARGET) / DECAY_FAST
    delta = min_decay + chan * ((max_decay - min_decay) / (HYENA_WIDTH - 1))
    t = row.astype(F32) / (length - 1)
    filt = hf * (jnp.exp(-t * jnp.abs(delta)) + WINDOW_SHIFT)
    backward = (col // HYENA_WIDTH) % 2 == 1
    filt = jnp.where(backward & (row == 0), 0.0, filt)
    o_ref[...] = filt.astype(o_ref.dtype)


def _filters(w1, b1, w2, b2, w3, b3, freq, *, tn=1024):
    hpad = LANES - FILTER_HIDDEN
    bands = (FILTER_EMB - 1) // 2
    n = w3.shape[1]
    w1t = jnp.pad(w1[0:1], ((0, 0), (0, hpad)))
    w1c = jnp.pad(w1[1:1 + bands], ((0, LANES - bands), (0, hpad)))
    w1s = jnp.pad(w1[1 + bands:], ((0, LANES - bands), (0, hpad)))
    b1p = jnp.pad(b1.reshape(1, -1), ((0, 0), (0, hpad)))
    w2p = jnp.pad(w2, ((0, hpad), (0, hpad)))
    b2p = jnp.pad(b2.reshape(1, -1), ((0, 0), (0, hpad)))
    fqp = jnp.pad(freq.reshape(1, -1), ((0, 0), (0, hpad)))
    w3p = jnp.pad(w3, ((0, hpad), (0, 0)))
    small = lambda shape: pl.BlockSpec(shape, lambda j: (0, 0))
    return pl.pallas_call(
        _filter_kernel,
        grid=(n // tn,),
        in_specs=[small((1, LANES)), small((LANES, LANES)), small((LANES, LANES)), small((1, LANES)),
                  small((LANES, LANES)), small((1, LANES)), small((1, LANES)),
                  pl.BlockSpec((LANES, tn), lambda j: (0, j)),
                  pl.BlockSpec((1, tn), lambda j: (0, j))],
        out_specs=pl.BlockSpec((SEQ, tn), lambda j: (0, j)),
        out_shape=jax.ShapeDtypeStruct((SEQ, n), BF16),
        scratch_shapes=[pltpu.VMEM((SEQ, LANES), F32)],
        compiler_params=_cparams(("arbitrary",)),
        name="hyena_filters",
    )(w1t, w1c, w1s, b1p, w2p, b2p, fqp, w3p, b3.reshape(1, n))


def _dft_tables():
    f = lax.broadcasted_iota(jnp.int32, (SEQ, SEQ), 0)
    t = lax.broadcasted_iota(jnp.int32, (SEQ, SEQ), 1)
    ang = ((f * t) % FFT_N).astype(F32) * (2.0 * math.pi / FFT_N)
    cm = jnp.cos(ang)
    sm = jnp.where(f == 0, jnp.where(t % 2 == 0, 1.0, -1.0), jnp.sin(ang))
    return cm.astype(BF16), sm.astype(BF16), sm.T.astype(BF16)


def _spec_scale(i, tf, shape):
    row = lax.broadcasted_iota(jnp.int32, shape, 0) + i * tf
    is0 = row == 0
    return is0, jnp.where(is0, 1.0 / FFT_N, 2.0 / FFT_N)


def _filter_spec_kernel(hf_ref, hb_ref, cm_ref, sm_ref, ka_ref, kb_ref):
    i = pl.program_id(2)
    tf = cm_ref.shape[0]
    cm = cm_ref[...]
    sm = sm_ref[...]
    hf = hf_ref[...]
    hb = hb_ref[...]
    a_f = jnp.dot(cm, hf, preferred_element_type=F32)
    b_f = jnp.dot(sm, hf, preferred_element_type=F32)
    a_b = jnp.dot(cm, hb, preferred_element_type=F32)
    b_b = jnp.dot(sm, hb, preferred_element_type=F32)
    is0, scale = _spec_scale(i, tf, a_f.shape)
    ka_ref[0] = ((a_f + a_b) * scale).astype(ka_ref.dtype)
    kb_ref[0] = (jnp.where(is0, b_f + b_b, b_f - b_b) * scale).astype(kb_ref.dtype)


def _filter_spectra(filt, cm, sm, *, tf=512, tn=512):
    w = HYENA_WIDTH
    ct = w // tn
    out = jax.ShapeDtypeStruct((2, SEQ, w), BF16)
    return pl.pallas_call(
        _filter_spec_kernel,
        grid=(2, ct, SEQ // tf),
        in_specs=[
            pl.BlockSpec((SEQ, tn), lambda o, j, i: (0, 2 * o * ct + j)),
            pl.BlockSpec((SEQ, tn), lambda o, j, i: (0, (2 * o + 1) * ct + j)),
            pl.BlockSpec((tf, SEQ), lambda o, j, i: (i, 0)),
            pl.BlockSpec((tf, SEQ), lambda o, j, i: (i, 0)),
        ],
        out_specs=[pl.BlockSpec((1, tf, tn), lambda o, j, i: (o, i, j))] * 2,
        out_shape=[out, out],
        compiler_params=_cparams(("arbitrary", "arbitrary", "arbitrary")),
        name="filter_spectra",
    )(filt, filt, cm, sm)


def _conv_fwd_kernel(u_ref, cm_ref, sm_ref, ka_ref, kb_ref, ya_ref, yb_ref):
    i = pl.program_id(2)
    tf = cm_ref.shape[0]
    u = u_ref[...]
    a = jnp.dot(cm_ref[...], u, preferred_element_type=F32)
    b = jnp.dot(sm_ref[...], u, preferred_element_type=F32)
    ka = ka_ref[0].astype(F32)
    kb = kb_ref[0].astype(F32)
    is0, _ = _spec_scale(i, tf, a.shape)
    bkb = b * kb
    ya_ref[0] = (a * ka - jnp.where(is0, 0.0, bkb)).astype(ya_ref.dtype)
    yb_ref[0] = jnp.where(is0, bkb, a * kb + b * ka).astype(yb_ref.dtype)


def _conv_fwd(u, col_blk0, ka, kb, order, cm, sm, batch, *, tf=512, tn=1024):
    w = HYENA_WIDTH
    out = jax.ShapeDtypeStruct((batch, SEQ, w), BF16)
    c0 = col_blk0 * (w // tn)
    return pl.pallas_call(
        _conv_fwd_kernel,
        grid=(batch, w // tn, SEQ // tf),
        in_specs=[
            pl.BlockSpec((SEQ, tn), lambda b, j, i: (b, c0 + j)),
            pl.BlockSpec((tf, SEQ), lambda b, j, i: (i, 0)),
            pl.BlockSpec((tf, SEQ), lambda b, j, i: (i, 0)),
            pl.BlockSpec((1, tf, tn), lambda b, j, i: (order, i, j)),
            pl.BlockSpec((1, tf, tn), lambda b, j, i: (order, i, j)),
        ],
        out_specs=[pl.BlockSpec((1, tf, tn), lambda b, j, i: (b, i, j))] * 2,
        out_shape=[out, out],
        compiler_params=_cparams(("arbitrary", "arbitrary", "arbitrary")),
        name="long_conv_fwd",
    )(u, cm, sm, ka, kb)


def _conv_inv_kernel(*refs, gated):
    if gated:
        ya_ref, yb_ref, cm_ref, st_ref, u_ref, d_ref, x_ref, g_ref, o_ref = refs
    else:
        ya_ref, yb_ref, cm_ref, st_ref, u_ref, d_ref, x_ref, o_ref = refs
    y = (jnp.dot(cm_ref[...], ya_ref[0], preferred_element_type=F32)
         + jnp.dot(st_ref[...], yb_ref[0], preferred_element_type=F32))
    y = y + u_ref[...].astype(F32) * d_ref[...]
    y = y * x_ref[...].astype(F32)
    if gated:
        g = g_ref[...].astype(F32)
        y = y * (g * jax.nn.sigmoid(g))
    o_ref[...] = y.astype(o_ref.dtype)


def _conv_inv(ya, yb, cm, st, u, u_blk0, d, x, x_blk0, g=None, g_blk0=0, *, tt=512, tn=1024):
    batch = ya.shape[0]
    w = HYENA_WIDTH
    ct = w // tn
    tiles = SEQ // tt
    gated = g is not None
    in_specs = [
        pl.BlockSpec((1, SEQ, tn), lambda b, j, i: (b, 0, j)),
        pl.BlockSpec((1, SEQ, tn), lambda b, j, i: (b, 0, j)),
        pl.BlockSpec((tt, SEQ), lambda b, j, i: (i, 0)),
        pl.BlockSpec((tt, SEQ), lambda b, j, i: (i, 0)),
        pl.BlockSpec((tt, tn), lambda b, j, i: (b * tiles + i, u_blk0 * ct + j)),
        pl.BlockSpec((1, tn), lambda b, j, i: (0, j)),
        pl.BlockSpec((tt, tn), lambda b, j, i: (b * tiles + i, x_blk0 * ct + j)),
    ]
    args = [ya, yb, cm, st, u, d.reshape(1, w), x]
    if gated:
        in_specs.append(pl.BlockSpec((tt, tn), lambda b, j, i: (b * tiles + i, g_blk0 * ct + j)))
        args.append(g)
    return pl.pallas_call(
        functools.partial(_conv_inv_kernel, gated=gated),
        grid=(batch, ct, tiles),
        in_specs=in_specs,
        out_specs=pl.BlockSpec((tt, tn), lambda b, j, i: (b * tiles + i, j)),
        out_shape=jax.ShapeDtypeStruct((batch * SEQ, w), BF16),
        compiler_params=_cparams(("arbitrary", "arbitrary", "arbitrary")),
        name="long_conv_inv",
    )(*args)


def _rope_tables():
    rows = SEQ // GRID_W
    row = jnp.repeat(jnp.arange(rows), GRID_W).astype(F32)
    col = jnp.tile(jnp.arange(GRID_W), rows).astype(F32)
    half = HEAD_DIM // 2
    inv = ROPE_BASE ** (-jnp.arange(0, half, 2, dtype=F32) / half)
    ar = row[:, None] * inv[None]
    ac = col[:, None] * inv[None]
    cos = jnp.concatenate([jnp.cos(ar), jnp.cos(ar), jnp.cos(ac), jnp.cos(ac)], axis=1)
    sin = jnp.concatenate([-jnp.sin(ar), jnp.sin(ar), -jnp.sin(ac), jnp.sin(ac)], axis=1)
    return cos, sin


def _head_types(tn):
    per_tile = tn // HEAD_DIM
    kinds = (["q"] * N_HEADS) + (["k"] * N_KV_HEADS)
    tiles = []
    for start in range(0, len(kinds), per_tile):
        chunk = kinds[start:start + per_tile]
        tiles.append(chunk + [None] * (per_tile - len(chunk)))
    return tiles


def kernel(x, c, ctx, c_ctx, norm_g, ada_w, ada_b, attn_w_in, attn_w_out, attn_sink, hy_w_in,
           hy_conv_w, hy_conv_b, hy_w1, hy_b1, hy_w2, hy_b2, hy_w3, hy_b3, hy_freq, hy_bias_d,
           hy_w_out, final_g):
    batch, seq, d = x.shape
    assert (seq, d) == (SEQ, D_MODEL) and ctx.shape[1] == CTX_LEN
    assert norm_g.shape[0] == 2 and attn_w_in.shape[0] == 1 and hy_w_in.shape[0] == 1
    w = HYENA_WIDTH

    cc = jnp.concatenate([c, c_ctx[None], jnp.zeros((8 - batch - 1, d), F32)], axis=0)
    mods = _mods(cc, ada_w, ada_b)
    part = lambda layer, r0, r1, k: mods[layer, r0:r1, None, k * d:(k + 1) * d]

    x2 = x.reshape(batch * seq, d)
    ctx2 = ctx.reshape(batch * CTX_LEN, d)

    w_in = attn_w_in[0].astype(BF16)
    tn0 = 1280
    cos, sin = _rope_tables()
    px = _norm_proj(x2, norm_g[0], part(0, 0, batch, 0), part(0, 0, batch, 1), w_in,
                    tm=512, tn=tn0, rows_per_mod=seq, rope=(cos, sin, _head_types(tn0)))
    ckv = _norm_proj(ctx2, norm_g[0], part(0, batch, batch + 1, 0), part(0, batch, batch + 1, 1),
                     w_in[:, ATTN_WIDTH:ATTN_WIDTH + 2 * KV_WIDTH],
                     tm=512, tn=2 * KV_WIDTH, rows_per_mod=CTX_LEN)
    og = _attention(px, ckv, attn_sink[0], batch)
    x2 = _out_proj(og, attn_w_out[0].astype(BF16), x2, part(0, 0, batch, 2))

    p = _norm_proj(x2, norm_g[1], part(1, 0, batch, 0), part(1, 0, batch, 1),
                   hy_w_in[0].astype(BF16), tm=512, tn=1024, rows_per_mod=seq)
    u = _short_conv(p, hy_conv_w[0], hy_conv_b[0], batch)
    filt = _filters(hy_w1[0], hy_b1[0], hy_w2[0], hy_b2[0], hy_w3[0], hy_b3[0], hy_freq[0])
    cm, sm, st = _dft_tables()
    ka, kb = _filter_spectra(filt, cm, sm)
    ya, yb = _conv_fwd(u, 2, ka, kb, 0, cm, sm, batch)
    z = _conv_inv(ya, yb, cm, st, u, 2, hy_bias_d[0, 0], u, 0)
    ya, yb = _conv_fwd(z, 0, ka, kb, 1, cm, sm, batch)
    yg = _conv_inv(ya, yb, cm, st, z, 0, hy_bias_d[0, 1], u, 1, g=p, g_blk0=3)
    out = _out_proj(yg, hy_w_out[0].astype(BF16), x2, part(1, 0, batch, 2), final_g)
    return out.reshape(batch, seq, d)
```

```python
import functools
import math

import jax
import jax.numpy as jnp
import numpy as np
from jax import lax
from jax.experimental import pallas as pl
from jax.experimental.pallas import tpu as pltpu

F32 = jnp.float32
BF16 = jnp.bfloat16

D_MODEL = 2048
SEQ = 2048
CTX_LEN = 256
GRID_W = 64
HEAD_DIM = 128
N_HEADS = 16
N_KV_HEADS = 4
GQA_GROUP = 4
ATTN_WIDTH = 2048
KV_WIDTH = 512
ATTN_IN = 2 * ATTN_WIDTH + 2 * KV_WIDTH
WINDOW = 128
BLOCK = 128
ROPE_BASE = 10000.0
HYENA_WIDTH = 2048
FILTER_EMB = 33
FILTER_HIDDEN = 64
DECAY_FAST = 0.3
DECAY_SLOW = 1.5
DECAY_TARGET = 1e-2
WINDOW_SHIFT = 0.05
NORM_EPS = 1e-6
NEG_INF = -1e30

LANES = 128
MXU_COLS = 256
ONES_ROWS = 16
LOG2E = math.log2(math.e)
FFT_N = 2 * SEQ
VMEM_LIMIT = 56 * 1024 * 1024


def _cparams(sem):
    return pltpu.CompilerParams(dimension_semantics=sem, vmem_limit_bytes=VMEM_LIMIT)


def _mods_kernel(c_ref, w_ref, b_ref, o_ref):
    c = c_ref[...]
    s = c * jax.nn.sigmoid(c)
    s_hi = s.astype(BF16)
    s_lo = (s - s_hi.astype(F32)).astype(BF16)
    w = w_ref[0]
    w_hi = w.astype(BF16)
    w_lo = (w - w_hi.astype(F32)).astype(BF16)
    lhs = jnp.concatenate([s_hi, s_lo], axis=0)
    r1 = jnp.dot(lhs, w_hi, preferred_element_type=F32)
    r2 = jnp.dot(s_hi, w_lo, preferred_element_type=F32)
    o_ref[0] = r1[:8] + r1[8:] + r2 + b_ref[0]


def _mods(cc, ada_w, ada_b):
    depth, d, n = ada_w.shape
    tn = 512
    return pl.pallas_call(
        _mods_kernel,
        grid=(depth, n // tn),
        in_specs=[
            pl.BlockSpec((8, d), lambda l, j: (0, 0)),
            pl.BlockSpec((1, d, tn), lambda l, j: (l, 0, j)),
            pl.BlockSpec((1, 1, tn), lambda l, j: (l, 0, j)),
        ],
        out_specs=pl.BlockSpec((1, 8, tn), lambda l, j: (l, 0, j)),
        out_shape=jax.ShapeDtypeStruct((depth, 8, n), F32),
        compiler_params=_cparams(("arbitrary", "arbitrary")),
        name="adaln_mods",
    )(cc, ada_w, ada_b.reshape(depth, 1, n))


def _rope_slab(t, cos, sin):
    lane = lax.broadcasted_iota(jnp.int32, t.shape, 1)
    first = (lane % 64) < 32
    partner = jnp.where(first, pltpu.roll(t, 96, 1), pltpu.roll(t, 32, 1))
    return t * cos + partner * sin


def _norm_proj_kernel(*refs, tile_types, has_rope, row_chunk, q_scale):
    if has_rope:
        x_ref, g_ref, sh_ref, sc_ref, w_ref, cos_ref, sin_ref, o_ref, hx_ref = refs
    else:
        x_ref, g_ref, sh_ref, sc_ref, w_ref, o_ref, hx_ref = refs
    j = pl.program_id(1)
    tm = x_ref.shape[0]
    tn = o_ref.shape[1]

    @pl.when(j == 0)
    def _():
        g = g_ref[...]
        mul = 1.0 + sc_ref[0]
        add = sh_ref[0]
        for r in range(0, tm, row_chunk):
            x = x_ref[r:r + row_chunk, :]
            ms = jnp.mean(x * x, axis=-1, keepdims=True)
            y = x * lax.rsqrt(ms + NORM_EPS) * g
            hx_ref[r:r + row_chunk, :] = (y * mul + add).astype(BF16)

    def plain():
        o_ref[...] = jnp.dot(hx_ref[...], w_ref[...], preferred_element_type=F32).astype(o_ref.dtype)

    def gated():
        for c0 in range(0, tn, MXU_COLS):
            acc = jnp.dot(hx_ref[...], w_ref[:, c0:c0 + MXU_COLS], preferred_element_type=F32)
            o_ref[:, c0:c0 + MXU_COLS] = (acc * jax.nn.sigmoid(acc)).astype(o_ref.dtype)

    def roped(types):
        cos = cos_ref[...]
        sin = sin_ref[...]
        for c0 in range(0, tn, MXU_COLS):
            acc = jnp.dot(hx_ref[...], w_ref[:, c0:c0 + MXU_COLS], preferred_element_type=F32)
            for h in range(MXU_COLS // LANES):
                ty = types[c0 // LANES + h]
                slab = acc[:, h * LANES:(h + 1) * LANES]
                if ty is not None:
                    slab = _rope_slab(slab, cos, sin)
                    if ty == "q":
                        slab = slab * q_scale
                lo = c0 + h * LANES
                o_ref[:, lo:lo + LANES] = slab.astype(o_ref.dtype)

    if tile_types is None:
        plain()
        return

    groups = {}
    for t, types in enumerate(tile_types):
        groups.setdefault(tuple(types), []).append(t)
    for types, tiles in groups.items():
        cond = functools.reduce(jnp.logical_or, [j == t for t in tiles])
        if all(ty is None for ty in types):
            pl.when(cond)(plain)
        elif all(ty == "g" for ty in types):
            pl.when(cond)(gated)
        else:
            assert "g" not in types
            pl.when(cond)(functools.partial(roped, types))


def _norm_proj(x, g, shift, scale, w, *, tm, tn, rows_per_mod, tile_types=None, rope=None,
               q_scale=1.0):
    m, d = x.shape
    n = w.shape[1]
    nb = shift.shape[0]
    if nb > 1:
        mod_map = lambda i, j: ((i * tm) // rows_per_mod, 0, 0)
    else:
        mod_map = lambda i, j: (0, 0, 0)
    in_specs = [
        pl.BlockSpec((tm, d), lambda i, j: (i, 0)),
        pl.BlockSpec((1, d), lambda i, j: (0, 0)),
        pl.BlockSpec((1, 1, d), mod_map),
        pl.BlockSpec((1, 1, d), mod_map),
        pl.BlockSpec((d, tn), lambda i, j: (0, j)),
    ]
    args = [x, g.reshape(1, d), shift, scale, w]
    if rope is not None:
        cos, sin = rope
        seq_tiles = cos.shape[0] // tm
        in_specs += [pl.BlockSpec((tm, LANES), lambda i, j: (i % seq_tiles, 0))] * 2
        args += [cos, sin]
    return pl.pallas_call(
        functools.partial(_norm_proj_kernel, tile_types=tile_types, has_rope=rope is not None,
                          row_chunk=min(tm, 256), q_scale=q_scale),
        grid=(m // tm, n // tn),
        in_specs=in_specs,
        out_specs=pl.BlockSpec((tm, tn), lambda i, j: (i, j)),
        out_shape=jax.ShapeDtypeStruct((m, n), BF16),
        scratch_shapes=[pltpu.VMEM((tm, d), BF16)],
        compiler_params=_cparams(("arbitrary", "arbitrary")),
        name="norm_proj",
    )(*args)


def _attn_kernel(sink_ref, q_ref, k_ref, v_ref, g_ref, kc_ref, vc_ref, o_ref,
                 vt_ref, vct_ref, bias_ref):
    kh = pl.program_id(1)
    band = 3 * BLOCK
    cols = GQA_GROUP * BLOCK
    n_blocks = SEQ // BLOCK

    vt_ref[:HEAD_DIM, :] = v_ref[...].T
    vt_ref[HEAD_DIM:, :] = jnp.ones((ONES_ROWS, SEQ), BF16)
    vct_ref[:HEAD_DIM, :] = vc_ref[...].T
    vct_ref[HEAD_DIM:, :] = jnp.ones((ONES_ROWS, CTX_LEN), BF16)
    krow = lax.broadcasted_iota(jnp.int32, (band, cols), 0)
    qcol = lax.broadcasted_iota(jnp.int32, (band, cols), 1) % BLOCK
    for idx, off in enumerate((0, -BLOCK, -2 * BLOCK)):
        bias_ref[idx] = jnp.where(jnp.abs(krow - qcol + off) <= WINDOW, 0.0, NEG_INF)

    kc = kc_ref[...]
    vct = vct_ref[...]
    sink_row = jnp.concatenate(
        [jnp.full((1, BLOCK), sink_ref[kh * GQA_GROUP + h] * LOG2E, F32) for h in range(GQA_GROUP)],
        axis=1)

    def body(n, carry):
        q0 = pl.multiple_of(n * BLOCK, BLOCK)
        ks = pl.multiple_of(jnp.clip((n - 1) * BLOCK, 0, SEQ - band), BLOCK)
        bidx = jnp.where(n == 0, 0, jnp.where(n == n_blocks - 1, 2, 1))
        qs = q_ref[pl.ds(q0, BLOCK), :]
        q4t = jnp.concatenate([qs[:, h * LANES:(h + 1) * LANES].T for h in range(GQA_GROUP)],
                              axis=1)
        kb = k_ref[pl.ds(ks, band), :]
        s_loc = jnp.dot(kb, q4t, preferred_element_type=F32) + bias_ref[bidx]
        s_ctx = jnp.dot(kc, q4t, preferred_element_type=F32)
        m = jnp.maximum(jnp.maximum(jnp.max(s_loc, axis=0, keepdims=True),
                                    jnp.max(s_ctx, axis=0, keepdims=True)), sink_row)
        p_loc = jnp.exp2(s_loc - m).astype(BF16)
        p_ctx = jnp.exp2(s_ctx - m).astype(BF16)
        ox = (jnp.dot(vt_ref[:, pl.ds(ks, band)], p_loc, preferred_element_type=F32)
              + jnp.dot(vct, p_ctx, preferred_element_type=F32))
        den = ox[HEAD_DIM:HEAD_DIM + 1, :] + jnp.exp2(sink_row - m)
        ot = ox[:HEAD_DIM, :] * (1.0 / den)
        gs = g_ref[pl.ds(q0, BLOCK), :].astype(F32)
        for h in range(GQA_GROUP):
            oh = ot[:, h * LANES:(h + 1) * LANES].T * gs[:, h * LANES:(h + 1) * LANES]
            o_ref[pl.ds(q0, BLOCK), h * LANES:(h + 1) * LANES] = oh.astype(o_ref.dtype)
        return carry

    lax.fori_loop(0, n_blocks, body, 0, unroll=4)


def _attention(px, ckv, sink, batch):
    gw = GQA_GROUP * HEAD_DIM
    k_blk0 = ATTN_WIDTH // HEAD_DIM
    v_blk0 = (ATTN_WIDTH + KV_WIDTH) // HEAD_DIM
    g_blk0 = (ATTN_WIDTH + 2 * KV_WIDTH) // gw
    return pl.pallas_call(
        _attn_kernel,
        grid=(batch, N_KV_HEADS),
        in_specs=[
            pl.BlockSpec(memory_space=pltpu.SMEM),
            pl.BlockSpec((SEQ, gw), lambda b, h: (b, h)),
            pl.BlockSpec((SEQ, HEAD_DIM), lambda b, h: (b, k_blk0 + h)),
            pl.BlockSpec((SEQ, HEAD_DIM), lambda b, h: (b, v_blk0 + h)),
            pl.BlockSpec((SEQ, gw), lambda b, h: (b, g_blk0 + h)),
            pl.BlockSpec((CTX_LEN, HEAD_DIM), lambda b, h: (b, h)),
            pl.BlockSpec((CTX_LEN, HEAD_DIM), lambda b, h: (b, N_KV_HEADS + h)),
        ],
        out_specs=pl.BlockSpec((SEQ, gw), lambda b, h: (b, h)),
        out_shape=jax.ShapeDtypeStruct((batch * SEQ, ATTN_WIDTH), BF16),
        scratch_shapes=[pltpu.VMEM((HEAD_DIM + ONES_ROWS, SEQ), BF16),
                        pltpu.VMEM((HEAD_DIM + ONES_ROWS, CTX_LEN), BF16),
                        pltpu.VMEM((3, 3 * BLOCK, GQA_GROUP * BLOCK), F32)],
        compiler_params=_cparams(("arbitrary", "arbitrary")),
        name="banded_attention",
    )(sink, px, px, px, px, ckv, ckv)


def _out_proj_kernel(*refs, final):
    if final:
        a_ref, w_ref, x_ref, gate_ref, fg_ref, o_ref = refs
    else:
        a_ref, w_ref, x_ref, gate_ref, o_ref = refs
    acc = jnp.dot(a_ref[...], w_ref[...], preferred_element_type=F32)
    y = x_ref[...] + gate_ref[0] * acc
    if final:
        ms = jnp.mean(y * y, axis=-1, keepdims=True)
        y = y * lax.rsqrt(ms + NORM_EPS) * fg_ref[...]
    o_ref[...] = y


def _out_proj(a, w, x, gate, final_g=None, *, tm=512):
    m, d = x.shape
    kdim = a.shape[1]
    final = final_g is not None
    in_specs = [
        pl.BlockSpec((tm, kdim), lambda i: (i, 0)),
        pl.BlockSpec((kdim, d), lambda i: (0, 0)),
        pl.BlockSpec((tm, d), lambda i: (i, 0)),
        pl.BlockSpec((1, 1, d), lambda i: ((i * tm) // SEQ, 0, 0)),
    ]
    args = [a, w, x, gate]
    if final:
        in_specs.append(pl.BlockSpec((1, d), lambda i: (0, 0)))
        args.append(final_g.reshape(1, d))
    return pl.pallas_call(
        functools.partial(_out_proj_kernel, final=final),
        grid=(m // tm,),
        in_specs=in_specs,
        out_specs=pl.BlockSpec((tm, d), lambda i: (i, 0)),
        out_shape=jax.ShapeDtypeStruct((m, d), F32),
        compiler_params=_cparams(("arbitrary",)),
        name="out_proj",
    )(*args)


def _short_conv_kernel(p_ref, w_ref, b_ref, o_ref):
    length = p_ref.shape[0]
    row = lax.broadcasted_iota(jnp.int32, (length, LANES), 0)
    for c in range(0, p_ref.shape[1], LANES):
        u = p_ref[:, c:c + LANES].astype(F32)
        w = w_ref[:, c:c + LANES]
        prev = jnp.where(row == 0, 0.0, pltpu.roll(u, 1, 0))
        nxt = jnp.where(row == length - 1, 0.0, pltpu.roll(u, length - 1, 0))
        y = prev * w[0:1] + u * w[1:2] + nxt * w[2:3] + b_ref[:, c:c + LANES]
        o_ref[:, c:c + LANES] = y.astype(o_ref.dtype)


def _short_conv(p, conv_w, conv_b, batch, *, tn=512):
    n = conv_w.shape[1]
    return pl.pallas_call(
        _short_conv_kernel,
        grid=(batch, n // tn),
        in_specs=[
            pl.BlockSpec((SEQ, tn), lambda b, j: (b, j)),
            pl.BlockSpec((3, tn), lambda b, j: (0, j)),
            pl.BlockSpec((1, tn), lambda b, j: (0, j)),
        ],
        out_specs=pl.BlockSpec((SEQ, tn), lambda b, j: (b, j)),
        out_shape=jax.ShapeDtypeStruct((batch * SEQ, n), BF16),
        compiler_params=_cparams(("arbitrary", "arbitrary")),
        name="short_conv",
    )(p, conv_w, conv_b.reshape(1, n))


def _filter_kernel(w1t_ref, w1c_ref, w1s_ref, b1_ref, w2_ref, b2_ref, fq_ref, w3_ref, b3_ref,
                   o_ref, hid_ref):
    j = pl.program_id(0)
    length = o_ref.shape[0]
    tn = o_ref.shape[1]
    hp = lax.Precision.HIGHEST

    @pl.when(j == 0)
    def _():
        n = lax.broadcasted_iota(jnp.int32, (length, LANES), 0).astype(F32)
        lane = lax.broadcasted_iota(jnp.int32, (length, LANES), 1)
        bands = (FILTER_EMB - 1) // 2
        fr_step = (bands - 1 - 1e-4) / (bands - 1)
        fr = jnp.where(lane < bands, 1e-4 + lane.astype(F32) * fr_step, 0.0)
        ang = (2.0 * math.pi * n / length) * fr
        t = n / (length - 1)
        fq = fq_ref[...]
        pre = (t * w1t_ref[...]
               + jnp.dot(jnp.cos(ang), w1c_ref[...], precision=hp, preferred_element_type=F32)
               + jnp.dot(-jnp.sin(ang), w1s_ref[...], precision=hp, preferred_element_type=F32)
               + b1_ref[...])
        hid = jnp.sin(fq * pre)
        hid = jnp.sin(fq * (jnp.dot(hid, w2_ref[...], precision=hp, preferred_element_type=F32)
                            + b2_ref[...]))
        hid_ref[...] = hid

    hf = jnp.dot(hid_ref[...], w3_ref[...], precision=hp, preferred_element_type=F32) + b3_ref[...]
    row = lax.broadcasted_iota(jnp.int32, (length, tn), 0)
    col = lax.broadcasted_iota(jnp.int32, (length, tn), 1) + j * tn
    chan = (col % HYENA_WIDTH).astype(F32)
    min_decay = math.log(DECAY_TARGET) / DECAY_SLOW
    max_decay = math.log(DECAY_TARGET) / DECAY_FAST
    delta = min_decay + chan * ((max_decay - min_decay) / (HYENA_WIDTH - 1))
    t = row.astype(F32) / (length - 1)
    filt = hf * (jnp.exp(-t * jnp.abs(delta)) + WINDOW_SHIFT)
    backward = (col // HYENA_WIDTH) % 2 == 1
    filt = jnp.where(backward & (row == 0), 0.0, filt)
    o_ref[...] = filt.astype(o_ref.dtype)


def _filters(w1, b1, w2, b2, w3, b3, freq, *, tn=1024):
    hpad = LANES - FILTER_HIDDEN
    bands = (FILTER_EMB - 1) // 2
    n = w3.shape[1]
    w1t = jnp.pad(w1[0:1], ((0, 0), (0, hpad)))
    w1c = jnp.pad(w1[1:1 + bands], ((0, LANES - bands), (0, hpad)))
    w1s = jnp.pad(w1[1 + bands:], ((0, LANES - bands), (0, hpad)))
    b1p = jnp.pad(b1.reshape(1, -1), ((0, 0), (0, hpad)))
    w2p = jnp.pad(w2, ((0, hpad), (0, hpad)))
    b2p = jnp.pad(b2.reshape(1, -1), ((0, 0), (0, hpad)))
    fqp = jnp.pad(freq.reshape(1, -1), ((0, 0), (0, hpad)))
    w3p = jnp.pad(w3, ((0, hpad), (0, 0)))
    small = lambda shape: pl.BlockSpec(shape, lambda j: (0, 0))
    return pl.pallas_call(
        _filter_kernel,
        grid=(n // tn,),
        in_specs=[small((1, LANES)), small((LANES, LANES)), small((LANES, LANES)), small((1, LANES)),
                  small((LANES, LANES)), small((1, LANES)), small((1, LANES)),
                  pl.BlockSpec((LANES, tn), lambda j: (0, j)),
                  pl.BlockSpec((1, tn), lambda j: (0, j))],
        out_specs=pl.BlockSpec((SEQ, tn), lambda j: (0, j)),
        out_shape=jax.ShapeDtypeStruct((SEQ, n), BF16),
        scratch_shapes=[pltpu.VMEM((SEQ, LANES), F32)],
        compiler_params=_cparams(("arbitrary",)),
        name="hyena_filters",
    )(w1t, w1c, w1s, b1p, w2p, b2p, fqp, w3p, b3.reshape(1, n))


def _dft_tables():
    idx = np.arange(SEQ, dtype=np.int64)
    ang = ((idx[:, None] * idx[None, :]) % FFT_N).astype(np.float64) * (2.0 * math.pi / FFT_N)
    cm = np.cos(ang)
    sm = np.sin(ang)
    sm[0, :] = np.where(idx % 2 == 0, 1.0, -1.0)
    as_bf16 = lambda a: jnp.asarray(a.astype(np.float32)).astype(BF16)
    return as_bf16(cm), as_bf16(sm), as_bf16(np.ascontiguousarray(sm.T))


def _spec_scale(i, tf, shape):
    row = lax.broadcasted_iota(jnp.int32, shape, 0) + i * tf
    is0 = row == 0
    return is0, jnp.where(is0, 1.0 / FFT_N, 2.0 / FFT_N)


def _filter_spec_kernel(hf_ref, hb_ref, cm_ref, sm_ref, ka_ref, kb_ref):
    i = pl.program_id(2)
    tf = cm_ref.shape[0]
    cm = cm_ref[...]
    sm = sm_ref[...]
    hf = hf_ref[...]
    hb = hb_ref[...]
    a_f = jnp.dot(cm, hf, preferred_element_type=F32)
    b_f = jnp.dot(sm, hf, preferred_element_type=F32)
    a_b = jnp.dot(cm, hb, preferred_element_type=F32)
    b_b = jnp.dot(sm, hb, preferred_element_type=F32)
    is0, scale = _spec_scale(i, tf, a_f.shape)
    ka_ref[0] = ((a_f + a_b) * scale).astype(ka_ref.dtype)
    kb_ref[0] = (jnp.where(is0, b_f + b_b, b_f - b_b) * scale).astype(kb_ref.dtype)


def _filter_spectra(filt, cm, sm, *, tf=512, tn=512):
    w = HYENA_WIDTH
    ct = w // tn
    out = jax.ShapeDtypeStruct((2, SEQ, w), BF16)
    return pl.pallas_call(
        _filter_spec_kernel,
        grid=(2, ct, SEQ // tf),
        in_specs=[
            pl.BlockSpec((SEQ, tn), lambda o, j, i: (0, 2 * o * ct + j)),
            pl.BlockSpec((SEQ, tn), lambda o, j, i: (0, (2 * o + 1) * ct + j)),
            pl.BlockSpec((tf, SEQ), lambda o, j, i: (i, 0)),
            pl.BlockSpec((tf, SEQ), lambda o, j, i: (i, 0)),
        ],
        out_specs=[pl.BlockSpec((1, tf, tn), lambda o, j, i: (o, i, j))] * 2,
        out_shape=[out, out],
        compiler_params=_cparams(("arbitrary", "arbitrary", "arbitrary")),
        name="filter_spectra",
    )(filt, filt, cm, sm)


def _conv_fwd_kernel(u_ref, cm_ref, sm_ref, ka_ref, kb_ref, ya_ref, yb_ref):
    i = pl.program_id(2)
    tf = cm_ref.shape[0]
    u = u_ref[...]
    a = jnp.dot(cm_ref[...], u, preferred_element_type=F32)
    b = jnp.dot(sm_ref[...], u, preferred_element_type=F32)
    ka = ka_ref[0].astype(F32)
    kb = kb_ref[0].astype(F32)
    is0, _ = _spec_scale(i, tf, a.shape)
    bkb = b * kb
    ya_ref[0] = (a * ka - jnp.where(is0, 0.0, bkb)).astype(ya_ref.dtype)
    yb_ref[0] = jnp.where(is0, bkb, a * kb + b * ka).astype(yb_ref.dtype)


def _conv_fwd(u, col_blk0, ka, kb, order, cm, sm, batch, *, tf=512, tn=1024):
    w = HYENA_WIDTH
    out = jax.ShapeDtypeStruct((batch, SEQ, w), BF16)
    c0 = col_blk0 * (w // tn)
    return pl.pallas_call(
        _conv_fwd_kernel,
        grid=(batch, w // tn, SEQ // tf),
        in_specs=[
            pl.BlockSpec((SEQ, tn), lambda b, j, i: (b, c0 + j)),
            pl.BlockSpec((tf, SEQ), lambda b, j, i: (i, 0)),
            pl.BlockSpec((tf, SEQ), lambda b, j, i: (i, 0)),
            pl.BlockSpec((1, tf, tn), lambda b, j, i: (order, i, j)),
            pl.BlockSpec((1, tf, tn), lambda b, j, i: (order, i, j)),
        ],
        out_specs=[pl.BlockSpec((1, tf, tn), lambda b, j, i: (b, i, j))] * 2,
        out_shape=[out, out],
        compiler_params=_cparams(("arbitrary", "arbitrary", "arbitrary")),
        name="long_conv_fwd",
    )(u, cm, sm, ka, kb)


def _conv_inv_kernel(*refs, gated):
    if gated:
        ya_ref, yb_ref, cm_ref, st_ref, u_ref, d_ref, x_ref, g_ref, o_ref = refs
    else:
        ya_ref, yb_ref, cm_ref, st_ref, u_ref, d_ref, x_ref, o_ref = refs
    y = (jnp.dot(cm_ref[...], ya_ref[0], preferred_element_type=F32)
         + jnp.dot(st_ref[...], yb_ref[0], preferred_element_type=F32))
    y = y + u_ref[...].astype(F32) * d_ref[...]
    y = y * x_ref[...].astype(F32)
    if gated:
        y = y * g_ref[...].astype(F32)
    o_ref[...] = y.astype(o_ref.dtype)


def _conv_inv(ya, yb, cm, st, u, u_blk0, d, x, x_blk0, g=None, g_blk0=0, *, tt=512, tn=1024):
    batch = ya.shape[0]
    w = HYENA_WIDTH
    ct = w // tn
    tiles = SEQ // tt
    gated = g is not None
    in_specs = [
        pl.BlockSpec((1, SEQ, tn), lambda b, j, i: (b, 0, j)),
        pl.BlockSpec((1, SEQ, tn), lambda b, j, i: (b, 0, j)),
        pl.BlockSpec((tt, SEQ), lambda b, j, i: (i, 0)),
        pl.BlockSpec((tt, SEQ), lambda b, j, i: (i, 0)),
        pl.BlockSpec((tt, tn), lambda b, j, i: (b * tiles + i, u_blk0 * ct + j)),
        pl.BlockSpec((1, tn), lambda b, j, i: (0, j)),
        pl.BlockSpec((tt, tn), lambda b, j, i: (b * tiles + i, x_blk0 * ct + j)),
    ]
    args = [ya, yb, cm, st, u, d.reshape(1, w), x]
    if gated:
        in_specs.append(pl.BlockSpec((tt, tn), lambda b, j, i: (b * tiles + i, g_blk0 * ct + j)))
        args.append(g)
    return pl.pallas_call(
        functools.partial(_conv_inv_kernel, gated=gated),
        grid=(batch, ct, tiles),
        in_specs=in_specs,
        out_specs=pl.BlockSpec((tt, tn), lambda b, j, i: (b * tiles + i, j)),
        out_shape=jax.ShapeDtypeStruct((batch * SEQ, w), BF16),
        compiler_params=_cparams(("arbitrary", "arbitrary", "arbitrary")),
        name="long_conv_inv",
    )(*args)


def _rope_tables():
    pos = np.arange(SEQ)
    row = (pos // GRID_W).astype(np.float32)
    col = (pos % GRID_W).astype(np.float32)
    half = HEAD_DIM // 2
    inv = (ROPE_BASE ** (-np.arange(0, half, 2, dtype=np.float32) / half)).astype(np.float32)
    ar = row[:, None] * inv[None]
    ac = col[:, None] * inv[None]
    cos = np.concatenate([np.cos(ar), np.cos(ar), np.cos(ac), np.cos(ac)], axis=1)
    sin = np.concatenate([-np.sin(ar), np.sin(ar), -np.sin(ac), np.sin(ac)], axis=1)
    return jnp.asarray(cos, F32), jnp.asarray(sin, F32)


def _tile_types(tn, kinds):
    per_tile = tn // LANES
    return [kinds[s0:s0 + per_tile] for s0 in range(0, len(kinds), per_tile)]


def kernel(x, c, ctx, c_ctx, norm_g, ada_w, ada_b, attn_w_in, attn_w_out, attn_sink, hy_w_in,
           hy_conv_w, hy_conv_b, hy_w1, hy_b1, hy_w2, hy_b2, hy_w3, hy_b3, hy_freq, hy_bias_d,
           hy_w_out, final_g):
    batch, seq, d = x.shape
    assert (seq, d) == (SEQ, D_MODEL) and ctx.shape[1] == CTX_LEN
    assert norm_g.shape[0] == 2 and attn_w_in.shape[0] == 1 and hy_w_in.shape[0] == 1
    w = HYENA_WIDTH

    cc = jnp.concatenate([c, c_ctx[None], jnp.zeros((8 - batch - 1, d), F32)], axis=0)
    mods = _mods(cc, ada_w, ada_b)
    part = lambda layer, r0, r1, k: mods[layer, r0:r1, None, k * d:(k + 1) * d]

    x2 = x.reshape(batch * seq, d)
    ctx2 = ctx.reshape(batch * CTX_LEN, d)

    w_in = attn_w_in[0].astype(BF16)
    tn0 = 1024
    kinds0 = (["q"] * N_HEADS + ["k"] * N_KV_HEADS + [None] * N_KV_HEADS
              + ["g"] * (ATTN_WIDTH // LANES))
    px = _norm_proj(x2, norm_g[0], part(0, 0, batch, 0), part(0, 0, batch, 1), w_in,
                    tm=1024, tn=tn0, rows_per_mod=seq, tile_types=_tile_types(tn0, kinds0),
                    rope=_rope_tables(), q_scale=HEAD_DIM ** -0.5 * LOG2E)
    ckv = _norm_proj(ctx2, norm_g[0], part(0, batch, batch + 1, 0), part(0, batch, batch + 1, 1),
                     w_in[:, ATTN_WIDTH:ATTN_WIDTH + 2 * KV_WIDTH],
                     tm=512, tn=2 * KV_WIDTH, rows_per_mod=CTX_LEN)
    og = _attention(px, ckv, attn_sink[0], batch)
    x2 = _out_proj(og, attn_w_out[0].astype(BF16), x2, part(0, 0, batch, 2))

    kinds1 = [None] * (3 * w // LANES) + ["g"] * (w // LANES)
    p = _norm_proj(x2, norm_g[1], part(1, 0, batch, 0), part(1, 0, batch, 1),
                   hy_w_in[0].astype(BF16), tm=1024, tn=1024, rows_per_mod=seq,
                   tile_types=_tile_types(1024, kinds1))
    u = _short_conv(p, hy_conv_w[0], hy_conv_b[0], batch)
    filt = _filters(hy_w1[0], hy_b1[0], hy_w2[0], hy_b2[0], hy_w3[0], hy_b3[0], hy_freq[0])
    cm, sm, st = _dft_tables()
    ka, kb = _filter_spectra(filt, cm, sm)
    ya, yb = _conv_fwd(u, 2, ka, kb, 0, cm, sm, batch)
    z = _conv_inv(ya, yb, cm, st, u, 2, hy_bias_d[0, 0], u, 0)
    ya, yb = _conv_fwd(z, 0, ka, kb, 1, cm, sm, batch)
    yg = _conv_inv(ya, yb, cm, st, z, 0, hy_bias_d[0, 1], u, 1, g=p, g_blk0=3)
    out = _out_proj(yg, hy_w_out[0].astype(BF16), x2, part(1, 0, batch, 2), final_g)
    return out.reshape(batch, seq, d)
```

```python
import functools
import math

import jax
import jax.numpy as jnp
import numpy as np
from jax import lax
from jax.experimental import pallas as pl
from jax.experimental.pallas import tpu as pltpu

F32 = jnp.float32
BF16 = jnp.bfloat16

D_MODEL = 2048
SEQ = 2048
CTX_LEN = 256
GRID_W = 64
HEAD_DIM = 128
N_HEADS = 16
N_KV_HEADS = 4
GQA_GROUP = 4
ATTN_WIDTH = 2048
KV_WIDTH = 512
ATTN_IN = 2 * ATTN_WIDTH + 2 * KV_WIDTH
WINDOW = 128
BLOCK = 128
ROPE_BASE = 10000.0
HYENA_WIDTH = 2048
FILTER_EMB = 33
FILTER_HIDDEN = 64
DECAY_FAST = 0.3
DECAY_SLOW = 1.5
DECAY_TARGET = 1e-2
WINDOW_SHIFT = 0.05
NORM_EPS = 1e-6
NEG_INF = -1e30

LANES = 128
MXU_COLS = 256
ONES_ROWS = 16
LOG2E = math.log2(math.e)
FFT_N = 2 * SEQ
VMEM_LIMIT = 56 * 1024 * 1024


def _cparams(sem):
    return pltpu.CompilerParams(dimension_semantics=sem, vmem_limit_bytes=VMEM_LIMIT)


def _mods_kernel(c_ref, w_ref, b_ref, o_ref):
    c = c_ref[...]
    s = c * jax.nn.sigmoid(c)
    s_hi = s.astype(BF16)
    s_lo = (s - s_hi.astype(F32)).astype(BF16)
    w = w_ref[0]
    w_hi = w.astype(BF16)
    w_lo = (w - w_hi.astype(F32)).astype(BF16)
    lhs = jnp.concatenate([s_hi, s_lo], axis=0)
    r1 = jnp.dot(lhs, w_hi, preferred_element_type=F32)
    r2 = jnp.dot(s_hi, w_lo, preferred_element_type=F32)
    o_ref[0] = r1[:8] + r1[8:] + r2 + b_ref[0]


def _mods(cc, ada_w, ada_b):
    depth, d, n = ada_w.shape
    tn = 512
    return pl.pallas_call(
        _mods_kernel,
        grid=(depth, n // tn),
        in_specs=[
            pl.BlockSpec((8, d), lambda l, j: (0, 0)),
            pl.BlockSpec((1, d, tn), lambda l, j: (l, 0, j)),
            pl.BlockSpec((1, 1, tn), lambda l, j: (l, 0, j)),
        ],
        out_specs=pl.BlockSpec((1, 8, tn), lambda l, j: (l, 0, j)),
        out_shape=jax.ShapeDtypeStruct((depth, 8, n), F32),
        compiler_params=_cparams(("arbitrary", "arbitrary")),
        name="adaln_mods",
    )(cc, ada_w, ada_b.reshape(depth, 1, n))


def _rope_slab(t, cos, sin):
    lane = lax.broadcasted_iota(jnp.int32, t.shape, 1)
    first = (lane % 64) < 32
    partner = jnp.where(first, pltpu.roll(t, 96, 1), pltpu.roll(t, 32, 1))
    return t * cos + partner * sin


def _norm_proj_kernel(*refs, tile_types, has_rope, row_chunk, q_scale):
    if has_rope:
        x_ref, g_ref, sh_ref, sc_ref, w_ref, cos_ref, sin_ref, o_ref, hx_ref = refs
    else:
        x_ref, g_ref, sh_ref, sc_ref, w_ref, o_ref, hx_ref = refs
    j = pl.program_id(1)
    tm = x_ref.shape[0]
    tn = o_ref.shape[1]

    @pl.when(j == 0)
    def _():
        g = g_ref[...]
        mul = 1.0 + sc_ref[0]
        add = sh_ref[0]
        for r in range(0, tm, row_chunk):
            x = x_ref[r:r + row_chunk, :]
            ms = jnp.mean(x * x, axis=-1, keepdims=True)
            y = x * lax.rsqrt(ms + NORM_EPS) * g
            hx_ref[r:r + row_chunk, :] = (y * mul + add).astype(BF16)

    def plain():
        o_ref[...] = jnp.dot(hx_ref[...], w_ref[...], preferred_element_type=F32).astype(o_ref.dtype)

    def gated():
        for c0 in range(0, tn, MXU_COLS):
            acc = jnp.dot(hx_ref[...], w_ref[:, c0:c0 + MXU_COLS], preferred_element_type=F32)
            o_ref[:, c0:c0 + MXU_COLS] = (acc * jax.nn.sigmoid(acc)).astype(o_ref.dtype)

    def roped(types):
        cos = cos_ref[...]
        sin = sin_ref[...]
        for c0 in range(0, tn, MXU_COLS):
            acc = jnp.dot(hx_ref[...], w_ref[:, c0:c0 + MXU_COLS], preferred_element_type=F32)
            for h in range(MXU_COLS // LANES):
                ty = types[c0 // LANES + h]
                slab = acc[:, h * LANES:(h + 1) * LANES]
                if ty is not None:
                    slab = _rope_slab(slab, cos, sin)
                    if ty == "q":
                        slab = slab * q_scale
                lo = c0 + h * LANES
                o_ref[:, lo:lo + LANES] = slab.astype(o_ref.dtype)

    if tile_types is None:
        plain()
        return

    groups = {}
    for t, types in enumerate(tile_types):
        groups.setdefault(tuple(types), []).append(t)
    for types, tiles in groups.items():
        cond = functools.reduce(jnp.logical_or, [j == t for t in tiles])
        if all(ty is None for ty in types):
            pl.when(cond)(plain)
        elif all(ty == "g" for ty in types):
            pl.when(cond)(gated)
        else:
            assert "g" not in types
            pl.when(cond)(functools.partial(roped, types))


def _norm_proj(x, g, shift, scale, w, *, tm, tn, rows_per_mod, tile_types=None, rope=None,
               q_scale=1.0):
    m, d = x.shape
    n = w.shape[1]
    nb = shift.shape[0]
    if nb > 1:
        mod_map = lambda i, j: ((i * tm) // rows_per_mod, 0, 0)
    else:
        mod_map = lambda i, j: (0, 0, 0)
    in_specs = [
        pl.BlockSpec((tm, d), lambda i, j: (i, 0)),
        pl.BlockSpec((1, d), lambda i, j: (0, 0)),
        pl.BlockSpec((1, 1, d), mod_map),
        pl.BlockSpec((1, 1, d), mod_map),
        pl.BlockSpec((d, tn), lambda i, j: (0, j)),
    ]
    args = [x, g.reshape(1, d), shift, scale, w]
    if rope is not None:
        cos, sin = rope
        seq_tiles = cos.shape[0] // tm
        in_specs += [pl.BlockSpec((tm, LANES), lambda i, j: (i % seq_tiles, 0))] * 2
        args += [cos, sin]
    return pl.pallas_call(
        functools.partial(_norm_proj_kernel, tile_types=tile_types, has_rope=rope is not None,
                          row_chunk=min(tm, 256), q_scale=q_scale),
        grid=(m // tm, n // tn),
        in_specs=in_specs,
        out_specs=pl.BlockSpec((tm, tn), lambda i, j: (i, j)),
        out_shape=jax.ShapeDtypeStruct((m, n), BF16),
        scratch_shapes=[pltpu.VMEM((tm, d), BF16)],
        compiler_params=_cparams(("arbitrary", "arbitrary")),
        name="norm_proj",
    )(*args)


def _attn_kernel(sink_ref, q_ref, k_ref, v_ref, g_ref, kc_ref, vc_ref, o_ref,
                 vt_ref, vct_ref, bias_ref):
    kh = pl.program_id(1)
    band = 3 * BLOCK
    cols = GQA_GROUP * BLOCK
    n_blocks = SEQ // BLOCK

    vt_ref[:HEAD_DIM, :] = v_ref[...].T
    vt_ref[HEAD_DIM:, :] = jnp.ones((ONES_ROWS, SEQ), BF16)
    vct_ref[:HEAD_DIM, :] = vc_ref[...].T
    vct_ref[HEAD_DIM:, :] = jnp.ones((ONES_ROWS, CTX_LEN), BF16)
    krow = lax.broadcasted_iota(jnp.int32, (band, cols), 0)
    qcol = lax.broadcasted_iota(jnp.int32, (band, cols), 1) % BLOCK
    for idx, off in enumerate((0, -BLOCK, -2 * BLOCK)):
        bias_ref[idx] = jnp.where(jnp.abs(krow - qcol + off) <= WINDOW, 0.0, NEG_INF)

    kc = kc_ref[...]
    vct = vct_ref[...]
    sink_row = jnp.concatenate(
        [jnp.full((1, BLOCK), sink_ref[kh * GQA_GROUP + h] * LOG2E, F32) for h in range(GQA_GROUP)],
        axis=1)

    def body(n, carry):
        q0 = pl.multiple_of(n * BLOCK, BLOCK)
        ks = pl.multiple_of(jnp.clip((n - 1) * BLOCK, 0, SEQ - band), BLOCK)
        bidx = jnp.where(n == 0, 0, jnp.where(n == n_blocks - 1, 2, 1))
        qs = q_ref[pl.ds(q0, BLOCK), :]
        q4t = jnp.concatenate([qs[:, h * LANES:(h + 1) * LANES].T for h in range(GQA_GROUP)],
                              axis=1)
        kb = k_ref[pl.ds(ks, band), :]
        s_loc = jnp.dot(kb, q4t, preferred_element_type=F32) + bias_ref[bidx]
        s_ctx = jnp.dot(kc, q4t, preferred_element_type=F32)
        m = jnp.maximum(jnp.maximum(jnp.max(s_loc, axis=0, keepdims=True),
                                    jnp.max(s_ctx, axis=0, keepdims=True)), sink_row)
        p_loc = jnp.exp2(s_loc - m).astype(BF16)
        p_ctx = jnp.exp2(s_ctx - m).astype(BF16)
        ox = (jnp.dot(vt_ref[:, pl.ds(ks, band)], p_loc, preferred_element_type=F32)
              + jnp.dot(vct, p_ctx, preferred_element_type=F32))
        den = ox[HEAD_DIM:HEAD_DIM + 1, :] + jnp.exp2(sink_row - m)
        ot = ox[:HEAD_DIM, :] * (1.0 / den)
        gs = g_ref[pl.ds(q0, BLOCK), :].astype(F32)
        for h in range(GQA_GROUP):
            oh = ot[:, h * LANES:(h + 1) * LANES].T * gs[:, h * LANES:(h + 1) * LANES]
            o_ref[pl.ds(q0, BLOCK), h * LANES:(h + 1) * LANES] = oh.astype(o_ref.dtype)
        return carry

    lax.fori_loop(0, n_blocks, body, 0, unroll=4)


def _attention(px, ckv, sink, batch):
    gw = GQA_GROUP * HEAD_DIM
    k_blk0 = ATTN_WIDTH // HEAD_DIM
    v_blk0 = (ATTN_WIDTH + KV_WIDTH) // HEAD_DIM
    g_blk0 = (ATTN_WIDTH + 2 * KV_WIDTH) // gw
    return pl.pallas_call(
        _attn_kernel,
        grid=(batch, N_KV_HEADS),
        in_specs=[
            pl.BlockSpec(memory_space=pltpu.SMEM),
            pl.BlockSpec((SEQ, gw), lambda b, h: (b, h)),
            pl.BlockSpec((SEQ, HEAD_DIM), lambda b, h: (b, k_blk0 + h)),
            pl.BlockSpec((SEQ, HEAD_DIM), lambda b, h: (b, v_blk0 + h)),
            pl.BlockSpec((SEQ, gw), lambda b, h: (b, g_blk0 + h)),
            pl.BlockSpec((CTX_LEN, HEAD_DIM), lambda b, h: (b, h)),
            pl.BlockSpec((CTX_LEN, HEAD_DIM), lambda b, h: (b, N_KV_HEADS + h)),
        ],
        out_specs=pl.BlockSpec((SEQ, gw), lambda b, h: (b, h)),
        out_shape=jax.ShapeDtypeStruct((batch * SEQ, ATTN_WIDTH), BF16),
        scratch_shapes=[pltpu.VMEM((HEAD_DIM + ONES_ROWS, SEQ), BF16),
                        pltpu.VMEM((HEAD_DIM + ONES_ROWS, CTX_LEN), BF16),
                        pltpu.VMEM((3, 3 * BLOCK, GQA_GROUP * BLOCK), F32)],
        compiler_params=_cparams(("arbitrary", "arbitrary")),
        name="banded_attention",
    )(sink, px, px, px, px, ckv, ckv)


def _out_proj_kernel(*refs, final):
    if final:
        a_ref, w_ref, x_ref, gate_ref, fg_ref, o_ref = refs
    else:
        a_ref, w_ref, x_ref, gate_ref, o_ref = refs
    acc = jnp.dot(a_ref[...], w_ref[...], preferred_element_type=F32)
    y = x_ref[...] + gate_ref[0] * acc
    if final:
        ms = jnp.mean(y * y, axis=-1, keepdims=True)
        y = y * lax.rsqrt(ms + NORM_EPS) * fg_ref[...]
    o_ref[...] = y


def _out_proj(a, w, x, gate, final_g=None, *, tm=512):
    m, d = x.shape
    kdim = a.shape[1]
    final = final_g is not None
    in_specs = [
        pl.BlockSpec((tm, kdim), lambda i: (i, 0)),
        pl.BlockSpec((kdim, d), lambda i: (0, 0)),
        pl.BlockSpec((tm, d), lambda i: (i, 0)),
        pl.BlockSpec((1, 1, d), lambda i: ((i * tm) // SEQ, 0, 0)),
    ]
    args = [a, w, x, gate]
    if final:
        in_specs.append(pl.BlockSpec((1, d), lambda i: (0, 0)))
        args.append(final_g.reshape(1, d))
    return pl.pallas_call(
        functools.partial(_out_proj_kernel, final=final),
        grid=(m // tm,),
        in_specs=in_specs,
        out_specs=pl.BlockSpec((tm, d), lambda i: (i, 0)),
        out_shape=jax.ShapeDtypeStruct((m, d), F32),
        compiler_params=_cparams(("arbitrary",)),
        name="out_proj",
    )(*args)


def _short_conv_kernel(p_ref, w_ref, b_ref, o_ref):
    length = p_ref.shape[0]
    row = lax.broadcasted_iota(jnp.int32, (length, LANES), 0)
    for c in range(0, p_ref.shape[1], LANES):
        u = p_ref[:, c:c + LANES].astype(F32)
        w = w_ref[:, c:c + LANES]
        prev = jnp.where(row == 0, 0.0, pltpu.roll(u, 1, 0))
        nxt = jnp.where(row == length - 1, 0.0, pltpu.roll(u, length - 1, 0))
        y = prev * w[0:1] + u * w[1:2] + nxt * w[2:3] + b_ref[:, c:c + LANES]
        o_ref[:, c:c + LANES] = y.astype(o_ref.dtype)


def _short_conv(p, conv_w, conv_b, batch, *, tn=512):
    n = conv_w.shape[1]
    return pl.pallas_call(
        _short_conv_kernel,
        grid=(batch, n // tn),
        in_specs=[
            pl.BlockSpec((SEQ, tn), lambda b, j: (b, j)),
            pl.BlockSpec((3, tn), lambda b, j: (0, j)),
            pl.BlockSpec((1, tn), lambda b, j: (0, j)),
        ],
        out_specs=pl.BlockSpec((SEQ, tn), lambda b, j: (b, j)),
        out_shape=jax.ShapeDtypeStruct((batch * SEQ, n), BF16),
        compiler_params=_cparams(("arbitrary", "arbitrary")),
        name="short_conv",
    )(p, conv_w, conv_b.reshape(1, n))


def _filter_kernel(w1t_ref, w1c_ref, w1s_ref, b1_ref, w2_ref, b2_ref, fq_ref, w3_ref, b3_ref,
                   o_ref, hid_ref):
    j = pl.program_id(0)
    length = o_ref.shape[0]
    tn = o_ref.shape[1]
    hp = lax.Precision.HIGHEST

    @pl.when(j == 0)
    def _():
        n = lax.broadcasted_iota(jnp.int32, (length, LANES), 0).astype(F32)
        lane = lax.broadcasted_iota(jnp.int32, (length, LANES), 1)
        bands = (FILTER_EMB - 1) // 2
        fr_step = (bands - 1 - 1e-4) / (bands - 1)
        fr = jnp.where(lane < bands, 1e-4 + lane.astype(F32) * fr_step, 0.0)
        ang = (2.0 * math.pi * n / length) * fr
        t = n / (length - 1)
        fq = fq_ref[...]
        pre = (t * w1t_ref[...]
               + jnp.dot(jnp.cos(ang), w1c_ref[...], precision=hp, preferred_element_type=F32)
               + jnp.dot(-jnp.sin(ang), w1s_ref[...], precision=hp, preferred_element_type=F32)
               + b1_ref[...])
        hid = jnp.sin(fq * pre)
        hid = jnp.sin(fq * (jnp.dot(hid, w2_ref[...], precision=hp, preferred_element_type=F32)
                            + b2_ref[...]))
        hid_ref[...] = hid

    hf = jnp.dot(hid_ref[...], w3_ref[...], precision=hp, preferred_element_type=F32) + b3_ref[...]
    row = lax.broadcasted_iota(jnp.int32, (length, tn), 0)
    col = lax.broadcasted_iota(jnp.int32, (length, tn), 1) + j * tn
    chan = (col % HYENA_WIDTH).astype(F32)
    min_decay = math.log(DECAY_TARGET) / DECAY_SLOW
    max_decay = math.log(DECAY_TARGET) / DECAY_FAST
    delta = min_decay + chan * ((max_decay - min_decay) / (HYENA_WIDTH - 1))
    t = row.astype(F32) / (length - 1)
    filt = hf * (jnp.exp(-t * jnp.abs(delta)) + WINDOW_SHIFT)
    backward = (col // HYENA_WIDTH) % 2 == 1
    filt = jnp.where(backward & (row == 0), 0.0, filt)
    o_ref[...] = filt.astype(o_ref.dtype)


def _filters(w1, b1, w2, b2, w3, b3, freq, *, tn=1024):
    hpad = LANES - FILTER_HIDDEN
    bands = (FILTER_EMB - 1) // 2
    n = w3.shape[1]
    w1t = jnp.pad(w1[0:1], ((0, 0), (0, hpad)))
    w1c = jnp.pad(w1[1:1 + bands], ((0, LANES - bands), (0, hpad)))
    w1s = jnp.pad(w1[1 + bands:], ((0, LANES - bands), (0, hpad)))
    b1p = jnp.pad(b1.reshape(1, -1), ((0, 0), (0, hpad)))
    w2p = jnp.pad(w2, ((0, hpad), (0, hpad)))
    b2p = jnp.pad(b2.reshape(1, -1), ((0, 0), (0, hpad)))
    fqp = jnp.pad(freq.reshape(1, -1), ((0, 0), (0, hpad)))
    w3p = jnp.pad(w3, ((0, hpad), (0, 0)))
    small = lambda shape: pl.BlockSpec(shape, lambda j: (0, 0))
    return pl.pallas_call(
        _filter_kernel,
        grid=(n // tn,),
        in_specs=[small((1, LANES)), small((LANES, LANES)), small((LANES, LANES)), small((1, LANES)),
                  small((LANES, LANES)), small((1, LANES)), small((1, LANES)),
                  pl.BlockSpec((LANES, tn), lambda j: (0, j)),
                  pl.BlockSpec((1, tn), lambda j: (0, j))],
        out_specs=pl.BlockSpec((SEQ, tn), lambda j: (0, j)),
        out_shape=jax.ShapeDtypeStruct((SEQ, n), BF16),
        scratch_shapes=[pltpu.VMEM((SEQ, LANES), F32)],
        compiler_params=_cparams(("arbitrary",)),
        name="hyena_filters",
    )(w1t, w1c, w1s, b1p, w2p, b2p, fqp, w3p, b3.reshape(1, n))


def _dft_tables():
    idx = np.arange(SEQ, dtype=np.int64)
    ang = ((idx[:, None] * idx[None, :]) % FFT_N).astype(np.float64) * (2.0 * math.pi / FFT_N)
    cm = np.cos(ang)
    sm = np.sin(ang)
    sm[0, :] = np.where(idx % 2 == 0, 1.0, -1.0)
    as_bf16 = lambda a: jnp.asarray(a.astype(np.float32)).astype(BF16)
    return as_bf16(cm), as_bf16(sm), as_bf16(np.ascontiguousarray(sm.T))


def _spec_scale(i, tf, shape):
    row = lax.broadcasted_iota(jnp.int32, shape, 0) + i * tf
    is0 = row == 0
    return is0, jnp.where(is0, 1.0 / FFT_N, 2.0 / FFT_N)


def _filter_spec_kernel(hf_ref, hb_ref, cm_ref, sm_ref, ka_ref, kb_ref):
    i = pl.program_id(2)
    tf = cm_ref.shape[0]
    cm = cm_ref[...]
    sm = sm_ref[...]
    hf = hf_ref[...]
    hb = hb_ref[...]
    a_f = jnp.dot(cm, hf, preferred_element_type=F32)
    b_f = jnp.dot(sm, hf, preferred_element_type=F32)
    a_b = jnp.dot(cm, hb, preferred_element_type=F32)
    b_b = jnp.dot(sm, hb, preferred_element_type=F32)
    is0, scale = _spec_scale(i, tf, a_f.shape)
    ka_ref[0] = ((a_f + a_b) * scale).astype(ka_ref.dtype)
    kb_ref[0] = (jnp.where(is0, b_f + b_b, b_f - b_b) * scale).astype(kb_ref.dtype)


def _filter_spectra(filt, cm, sm, *, tf=512, tn=512):
    w = HYENA_WIDTH
    ct = w // tn
    out = jax.ShapeDtypeStruct((2, SEQ, w), BF16)
    return pl.pallas_call(
        _filter_spec_kernel,
        grid=(2, ct, SEQ // tf),
        in_specs=[
            pl.BlockSpec((SEQ, tn), lambda o, j, i: (0, 2 * o * ct + j)),
            pl.BlockSpec((SEQ, tn), lambda o, j, i: (0, (2 * o + 1) * ct + j)),
            pl.BlockSpec((tf, SEQ), lambda o, j, i: (i, 0)),
            pl.BlockSpec((tf, SEQ), lambda o, j, i: (i, 0)),
        ],
        out_specs=[pl.BlockSpec((1, tf, tn), lambda o, j, i: (o, i, j))] * 2,
        out_shape=[out, out],
        compiler_params=_cparams(("arbitrary", "arbitrary", "arbitrary")),
        name="filter_spectra",
    )(filt, filt, cm, sm)


def _conv_fwd_kernel(u_ref, cm_ref, sm_ref, ka_ref, kb_ref, ya_ref, yb_ref):
    i = pl.program_id(2)
    tf = cm_ref.shape[0]
    u = u_ref[...]
    a = jnp.dot(cm_ref[...], u, preferred_element_type=F32)
    b = jnp.dot(sm_ref[...], u, preferred_element_type=F32)
    ka = ka_ref[0].astype(F32)
    kb = kb_ref[0].astype(F32)
    is0, _ = _spec_scale(i, tf, a.shape)
    bkb = b * kb
    ya_ref[0] = (a * ka - jnp.where(is0, 0.0, bkb)).astype(ya_ref.dtype)
    yb_ref[0] = jnp.where(is0, bkb, a * kb + b * ka).astype(yb_ref.dtype)


def _conv_fwd(u, col_blk0, ka, kb, order, cm, sm, batch, *, tf=512, tn=1024):
    w = HYENA_WIDTH
    out = jax.ShapeDtypeStruct((batch, SEQ, w), BF16)
    c0 = col_blk0 * (w // tn)
    return pl.pallas_call(
        _conv_fwd_kernel,
        grid=(batch, w // tn, SEQ // tf),
        in_specs=[
            pl.BlockSpec((SEQ, tn), lambda b, j, i: (b, c0 + j)),
            pl.BlockSpec((tf, SEQ), lambda b, j, i: (i, 0)),
            pl.BlockSpec((tf, SEQ), lambda b, j, i: (i, 0)),
            pl.BlockSpec((1, tf, tn), lambda b, j, i: (order, i, j)),
            pl.BlockSpec((1, tf, tn), lambda b, j, i: (order, i, j)),
        ],
        out_specs=[pl.BlockSpec((1, tf, tn), lambda b, j, i: (b, i, j))] * 2,
        out_shape=[out, out],
        compiler_params=_cparams(("arbitrary", "arbitrary", "arbitrary")),
        name="long_conv_fwd",
    )(u, cm, sm, ka, kb)


def _conv_inv_kernel(*refs, gated):
    if gated:
        ya_ref, yb_ref, cm_ref, st_ref, u_ref, d_ref, x_ref, g_ref, o_ref = refs
    else:
        ya_ref, yb_ref, cm_ref, st_ref, u_ref, d_ref, x_ref, o_ref = refs
    y = (jnp.dot(cm_ref[...], ya_ref[0], preferred_element_type=F32)
         + jnp.dot(st_ref[...], yb_ref[0], preferred_element_type=F32))
    y = y + u_ref[...].astype(F32) * d_ref[...]
    y = y * x_ref[...].astype(F32)
    if gated:
        y = y * g_ref[...].astype(F32)
    o_ref[...] = y.astype(o_ref.dtype)


def _conv_inv(ya, yb, cm, st, u, u_blk0, d, x, x_blk0, g=None, g_blk0=0, *, tt=512, tn=1024):
    batch = ya.shape[0]
    w = HYENA_WIDTH
    ct = w // tn
    tiles = SEQ // tt
    gated = g is not None
    in_specs = [
        pl.BlockSpec((1, SEQ, tn), lambda b, j, i: (b, 0, j)),
        pl.BlockSpec((1, SEQ, tn), lambda b, j, i: (b, 0, j)),
        pl.BlockSpec((tt, SEQ), lambda b, j, i: (i, 0)),
        pl.BlockSpec((tt, SEQ), lambda b, j, i: (i, 0)),
        pl.BlockSpec((tt, tn), lambda b, j, i: (b * tiles + i, u_blk0 * ct + j)),
        pl.BlockSpec((1, tn), lambda b, j, i: (0, j)),
        pl.BlockSpec((tt, tn), lambda b, j, i: (b * tiles + i, x_blk0 * ct + j)),
    ]
    args = [ya, yb, cm, st, u, d.reshape(1, w), x]
    if gated:
        in_specs.append(pl.BlockSpec((tt, tn), lambda b, j, i: (b * tiles + i, g_blk0 * ct + j)))
        args.append(g)
    return pl.pallas_call(
        functools.partial(_conv_inv_kernel, gated=gated),
        grid=(batch, ct, tiles),
        in_specs=in_specs,
        out_specs=pl.BlockSpec((tt, tn), lambda b, j, i: (b * tiles + i, j)),
        out_shape=jax.ShapeDtypeStruct((batch * SEQ, w), BF16),
        compiler_params=_cparams(("arbitrary", "arbitrary", "arbitrary")),
        name="long_conv_inv",
    )(*args)


HALF = SEQ // 2
FLIP_BLOCK = 256


def _fold_tables():
    t2 = 2 * np.arange(HALF, dtype=np.int64) + 1
    g = np.arange(HALF, dtype=np.int64)

    def tab(f):
        ang = ((f[:, None] * t2[None, :]) % (2 * FFT_N)).astype(np.float64) * (math.pi / FFT_N)
        return np.cos(ang), np.sin(ang)

    ce, se = tab(2 * g)
    co, so = tab(2 * g + 1)
    se[0, :] = np.where(np.arange(HALF) % 2 == 0, 1.0, -1.0)
    const = lambda a, dt: jnp.asarray(np.ascontiguousarray(a).astype(np.float32)).astype(dt)
    fwd = const(np.stack([ce, se, co, so]), BF16)
    inv = const(np.stack([ce.T, so.T, se.T, co.T]), BF16)
    anti = const(np.eye(FLIP_BLOCK)[::-1], BF16)
    theta = np.stack([2 * g, 2 * g + 1]).astype(np.float64) * (math.pi / FFT_N)
    phase = const(np.stack([np.cos(theta), np.sin(theta)])[..., None], F32)
    return fwd, inv, anti, phase


def _flip_rows(h, anti):
    nb = HALF // FLIP_BLOCK
    return jnp.concatenate(
        [jnp.dot(anti, h[(nb - 1 - a) * FLIP_BLOCK:(nb - a) * FLIP_BLOCK, :], preferred_element_type=F32)
         for a in range(nb)], axis=0)


def _fold(x_ref, anti):
    x0 = x_ref[:HALF, :].astype(F32)
    xr = _flip_rows(x_ref[HALF:, :], anti)
    return (x0 + xr).astype(BF16), (x0 - xr).astype(BF16)


def _half_spectrum(tab_ref, xs, xa):
    dot = lambda k, v: jnp.dot(tab_ref[k], v, preferred_element_type=F32)
    return dot(0, xs), dot(1, xa), dot(2, xa), dot(3, xs)


def _short_conv_into(p_ref, w_ref, b_ref, dst_ref):
    length = p_ref.shape[0]
    row = lax.broadcasted_iota(jnp.int32, (length, LANES), 0)
    for c in range(0, p_ref.shape[1], LANES):
        u = p_ref[:, c:c + LANES].astype(F32)
        w = w_ref[:, c:c + LANES]
        prev = jnp.where(row == 0, 0.0, pltpu.roll(u, 1, 0))
        nxt = jnp.where(row == length - 1, 0.0, pltpu.roll(u, length - 1, 0))
        y = prev * w[0:1] + u * w[1:2] + nxt * w[2:3] + b_ref[:, c:c + LANES]
        dst_ref[:, c:c + LANES] = y.astype(dst_ref.dtype)


def _resident(shape):
    zeros = (0,) * len(shape)
    return pl.BlockSpec(shape, lambda *_: zeros, pipeline_mode=pl.Buffered(1))


def _filter_sd_kernel(w1t_ref, w1c_ref, w1s_ref, b1_ref, w2_ref, b2_ref, fq_ref,
                      w3f_ref, b3f_ref, w3b_ref, b3b_ref, o_ref, hid_ref):
    first = (pl.program_id(0) == 0) & (pl.program_id(1) == 0)
    j = pl.program_id(1)
    length = o_ref.shape[1]
    tn = o_ref.shape[2]
    hp = lax.Precision.HIGHEST

    @pl.when(first)
    def _():
        n = lax.broadcasted_iota(jnp.int32, (length, LANES), 0).astype(F32)
        lane = lax.broadcasted_iota(jnp.int32, (length, LANES), 1)
        bands = (FILTER_EMB - 1) // 2
        fr_step = (bands - 1 - 1e-4) / (bands - 1)
        fr = jnp.where(lane < bands, 1e-4 + lane.astype(F32) * fr_step, 0.0)
        ang = (2.0 * math.pi * n / length) * fr
        t = n / (length - 1)
        fq = fq_ref[...]
        pre = (t * w1t_ref[...]
               + jnp.dot(jnp.cos(ang), w1c_ref[...], precision=hp, preferred_element_type=F32)
               + jnp.dot(-jnp.sin(ang), w1s_ref[...], precision=hp, preferred_element_type=F32)
               + b1_ref[...])
        hid = jnp.sin(fq * pre)
        hid = jnp.sin(fq * (jnp.dot(hid, w2_ref[...], precision=hp, preferred_element_type=F32)
                            + b2_ref[...]))
        hid_ref[...] = hid

    hid = hid_ref[...]
    row = lax.broadcasted_iota(jnp.int32, (length, tn), 0)
    chan = (lax.broadcasted_iota(jnp.int32, (length, tn), 1) + j * tn).astype(F32)
    min_decay = math.log(DECAY_TARGET) / DECAY_SLOW
    max_decay = math.log(DECAY_TARGET) / DECAY_FAST
    delta = min_decay + chan * ((max_decay - min_decay) / (HYENA_WIDTH - 1))
    t = row.astype(F32) / (length - 1)
    window = jnp.exp(-t * jnp.abs(delta)) + WINDOW_SHIFT
    hf = (jnp.dot(hid, w3f_ref[...], precision=hp, preferred_element_type=F32) + b3f_ref[...]) * window
    hb = (jnp.dot(hid, w3b_ref[...], precision=hp, preferred_element_type=F32) + b3b_ref[...]) * window
    hb = jnp.where(row == 0, 0.0, hb)
    o_ref[0] = (hf + hb).astype(o_ref.dtype)
    o_ref[1] = (hf - hb).astype(o_ref.dtype)


def _filters_sd(w1, b1, w2, b2, w3, b3, freq, *, tn=512):
    hpad = LANES - FILTER_HIDDEN
    bands = (FILTER_EMB - 1) // 2
    w = HYENA_WIDTH
    ct = w // tn
    n = w3.shape[1]
    w1t = jnp.pad(w1[0:1], ((0, 0), (0, hpad)))
    w1c = jnp.pad(w1[1:1 + bands], ((0, LANES - bands), (0, hpad)))
    w1s = jnp.pad(w1[1 + bands:], ((0, LANES - bands), (0, hpad)))
    b1p = jnp.pad(b1.reshape(1, -1), ((0, 0), (0, hpad)))
    w2p = jnp.pad(w2, ((0, hpad), (0, hpad)))
    b2p = jnp.pad(b2.reshape(1, -1), ((0, 0), (0, hpad)))
    fqp = jnp.pad(freq.reshape(1, -1), ((0, 0), (0, hpad)))
    w3p = jnp.pad(w3, ((0, hpad), (0, 0)))
    b3r = b3.reshape(1, n)
    small = lambda shape: pl.BlockSpec(shape, lambda o, j: (0, 0))
    fwd_cols = lambda o, j: (0, 2 * o * ct + j)
    bwd_cols = lambda o, j: (0, (2 * o + 1) * ct + j)
    return pl.pallas_call(
        _filter_sd_kernel,
        grid=(2, ct),
        in_specs=[small((1, LANES)), small((LANES, LANES)), small((LANES, LANES)), small((1, LANES)),
                  small((LANES, LANES)), small((1, LANES)), small((1, LANES)),
                  pl.BlockSpec((LANES, tn), fwd_cols), pl.BlockSpec((1, tn), fwd_cols),
                  pl.BlockSpec((LANES, tn), bwd_cols), pl.BlockSpec((1, tn), bwd_cols)],
        out_specs=pl.BlockSpec((2, SEQ, tn), lambda o, j: (0, 0, o * ct + j)),
        out_shape=jax.ShapeDtypeStruct((2, SEQ, 2 * w), BF16),
        scratch_shapes=[pltpu.VMEM((SEQ, LANES), F32)],
        compiler_params=_cparams(("arbitrary", "arbitrary")),
        name="hyena_filters",
    )(w1t, w1c, w1s, b1p, w2p, b2p, fqp, w3p, b3r, w3p, b3r)


def _filter_spec_kernel2(s_ref, d_ref, tab_ref, anti_ref, ph_ref, k_ref):
    anti = anti_ref[...]
    sa_e, sb_e, sa_o, sb_o = _half_spectrum(tab_ref, *_fold(s_ref.at[0], anti))
    da_e, db_e, da_o, db_o = _half_spectrum(tab_ref, *_fold(d_ref.at[0], anti))
    is0 = lax.broadcasted_iota(jnp.int32, sa_e.shape, 0) == 0
    scale_e = jnp.where(is0, 1.0 / FFT_N, 2.0 / FFT_N)
    cos_e, sin_e = ph_ref[0, 0], ph_ref[1, 0]
    cos_o, sin_o = ph_ref[0, 1], ph_ref[1, 1]
    k_ref[0, 0] = ((sa_e * cos_e + sb_e * sin_e) * scale_e).astype(k_ref.dtype)
    k_ref[0, 1] = (jnp.where(is0, sb_e, db_e * cos_e - da_e * sin_e) * scale_e).astype(k_ref.dtype)
    k_ref[0, 2] = ((sa_o * cos_o + sb_o * sin_o) * (2.0 / FFT_N)).astype(k_ref.dtype)
    k_ref[0, 3] = ((db_o * cos_o - da_o * sin_o) * (2.0 / FFT_N)).astype(k_ref.dtype)


def _filter_spectra2(sd, fwd, anti, phase, *, tn=512):
    w = HYENA_WIDTH
    ct = w // tn
    return pl.pallas_call(
        _filter_spec_kernel2,
        grid=(2, ct),
        in_specs=[
            pl.BlockSpec((1, SEQ, tn), lambda o, j: (0, 0, o * ct + j)),
            pl.BlockSpec((1, SEQ, tn), lambda o, j: (1, 0, o * ct + j)),
            _resident(fwd.shape), _resident(anti.shape), _resident(phase.shape),
        ],
        out_specs=pl.BlockSpec((1, 4, HALF, tn), lambda o, j: (o, 0, 0, j)),
        out_shape=jax.ShapeDtypeStruct((2, 4, HALF, w), BF16),
        compiler_params=_cparams(("arbitrary", "arbitrary")),
        name="filter_spectra",
    )(sd, sd, fwd, anti, phase)


def _conv_fwd_kernel2(*refs, with_conv):
    if with_conv:
        p_ref, cw_ref, cb_ref, tab_ref, anti_ref, k_ref, y_ref, v_ref = refs
        _short_conv_into(p_ref, cw_ref, cb_ref, v_ref)
        x_ref = v_ref
    else:
        x_ref, tab_ref, anti_ref, k_ref, y_ref = refs
    a_e, b_e, a_o, b_o = _half_spectrum(tab_ref, *_fold(x_ref, anti_ref[...]))
    f32 = lambda r: r.astype(F32)
    ka_e, kb_e, ka_o, kb_o = f32(k_ref[0, 0]), f32(k_ref[0, 1]), f32(k_ref[0, 2]), f32(k_ref[0, 3])
    is0 = lax.broadcasted_iota(jnp.int32, a_e.shape, 0) == 0
    bkb = b_e * kb_e
    y_ref[0, 0] = (a_e * ka_e - jnp.where(is0, 0.0, bkb)).astype(y_ref.dtype)
    y_ref[0, 1] = jnp.where(is0, bkb, a_e * kb_e + b_e * ka_e).astype(y_ref.dtype)
    y_ref[0, 2] = (a_o * ka_o - b_o * kb_o).astype(y_ref.dtype)
    y_ref[0, 3] = (a_o * kb_o + b_o * ka_o).astype(y_ref.dtype)


def _conv_fwd2(x, x_blk0, kspec, order, fwd, anti, batch, conv=None, *, tn=512):
    w = HYENA_WIDTH
    ct = w // tn
    in_specs = [pl.BlockSpec((SEQ, tn), lambda b, j: (b, x_blk0 * ct + j))]
    args = [x]
    out_specs = [pl.BlockSpec((1, 4, HALF, tn), lambda b, j: (b, 0, 0, j))]
    out_shape = [jax.ShapeDtypeStruct((batch, 4, HALF, w), BF16)]
    if conv is not None:
        cw, cb, c_blk0 = conv
        in_specs += [pl.BlockSpec((3, tn), lambda b, j: (0, c_blk0 * ct + j)),
                     pl.BlockSpec((1, tn), lambda b, j: (0, c_blk0 * ct + j))]
        args += [cw, cb]
        out_specs.append(pl.BlockSpec((SEQ, tn), lambda b, j: (b, j)))
        out_shape.append(jax.ShapeDtypeStruct((batch * SEQ, w), BF16))
    in_specs += [_resident(fwd.shape), _resident(anti.shape),
                 pl.BlockSpec((1, 4, HALF, tn), lambda b, j: (order, 0, 0, j))]
    args += [fwd, anti, kspec]
    return pl.pallas_call(
        functools.partial(_conv_fwd_kernel2, with_conv=conv is not None),
        grid=(batch, ct),
        in_specs=in_specs,
        out_specs=out_specs,
        out_shape=out_shape,
        compiler_params=_cparams(("arbitrary", "arbitrary")),
        name="long_conv_fwd",
    )(*args)


def _conv_inv_kernel2(*refs, gated):
    if gated:
        y_ref, tab_ref, anti_ref, u_ref, d_ref, p_ref, cw_ref, cb_ref, g_ref, o_ref, xm_ref = refs
    else:
        y_ref, tab_ref, anti_ref, u_ref, d_ref, p_ref, cw_ref, cb_ref, o_ref, xm_ref = refs
    _short_conv_into(p_ref, cw_ref, cb_ref, xm_ref)
    dot = lambda k, plane: jnp.dot(tab_ref[k], y_ref[0, plane], preferred_element_type=F32)
    p1 = dot(0, 0) + dot(1, 3)
    p2 = dot(2, 1) + dot(3, 2)
    halves = (p1 + p2, _flip_rows((p1 - p2).astype(BF16), anti_ref[...]))
    d = d_ref[...]
    for k, y in enumerate(halves):
        rows = slice(k * HALF, (k + 1) * HALF)
        out = (y + u_ref[rows, :].astype(F32) * d) * xm_ref[rows, :]
        if gated:
            out = out * g_ref[rows, :].astype(F32)
        o_ref[rows, :] = out.astype(o_ref.dtype)


def _conv_inv2(y, inv, anti, u, u_blk0, d, p, p_blk0, conv_w, conv_b, g_blk0=None, *, tn=512):
    batch = y.shape[0]
    w = HYENA_WIDTH
    ct = w // tn
    gated = g_blk0 is not None
    col = lambda blk0: (lambda b, j: (b, blk0 * ct + j))
    in_specs = [
        pl.BlockSpec((1, 4, HALF, tn), lambda b, j: (b, 0, 0, j)),
        _resident(inv.shape), _resident(anti.shape),
        pl.BlockSpec((SEQ, tn), col(u_blk0)),
        pl.BlockSpec((1, tn), lambda b, j: (0, j)),
        pl.BlockSpec((SEQ, tn), col(p_blk0)),
        pl.BlockSpec((3, tn), lambda b, j: (0, p_blk0 * ct + j)),
        pl.BlockSpec((1, tn), lambda b, j: (0, p_blk0 * ct + j)),
    ]
    args = [y, inv, anti, u, d.reshape(1, w), p, conv_w, conv_b]
    if gated:
        in_specs.append(pl.BlockSpec((SEQ, tn), col(g_blk0)))
        args.append(p)
    return pl.pallas_call(
        functools.partial(_conv_inv_kernel2, gated=gated),
        grid=(batch, ct),
        in_specs=in_specs,
        out_specs=pl.BlockSpec((SEQ, tn), lambda b, j: (b, j)),
        out_shape=jax.ShapeDtypeStruct((batch * SEQ, w), BF16),
        scratch_shapes=[pltpu.VMEM((SEQ, tn), F32)],
        compiler_params=_cparams(("arbitrary", "arbitrary")),
        name="long_conv_inv",
    )(*args)


def _rope_tables():
    pos = np.arange(SEQ)
    row = (pos // GRID_W).astype(np.float32)
    col = (pos % GRID_W).astype(np.float32)
    half = HEAD_DIM // 2
    inv = (ROPE_BASE ** (-np.arange(0, half, 2, dtype=np.float32) / half)).astype(np.float32)
    ar = row[:, None] * inv[None]
    ac = col[:, None] * inv[None]
    cos = np.concatenate([np.cos(ar), np.cos(ar), np.cos(ac), np.cos(ac)], axis=1)
    sin = np.concatenate([-np.sin(ar), np.sin(ar), -np.sin(ac), np.sin(ac)], axis=1)
    return jnp.asarray(cos, F32), jnp.asarray(sin, F32)


def _tile_types(tn, kinds):
    per_tile = tn // LANES
    return [kinds[s0:s0 + per_tile] for s0 in range(0, len(kinds), per_tile)]


def kernel(x, c, ctx, c_ctx, norm_g, ada_w, ada_b, attn_w_in, attn_w_out, attn_sink, hy_w_in,
           hy_conv_w, hy_conv_b, hy_w1, hy_b1, hy_w2, hy_b2, hy_w3, hy_b3, hy_freq, hy_bias_d,
           hy_w_out, final_g):
    batch, seq, d = x.shape
    assert (seq, d) == (SEQ, D_MODEL) and ctx.shape[1] == CTX_LEN
    assert norm_g.shape[0] == 2 and attn_w_in.shape[0] == 1 and hy_w_in.shape[0] == 1
    w = HYENA_WIDTH

    cc = jnp.concatenate([c, c_ctx[None], jnp.zeros((8 - batch - 1, d), F32)], axis=0)
    mods = _mods(cc, ada_w, ada_b)
    part = lambda layer, r0, r1, k: mods[layer, r0:r1, None, k * d:(k + 1) * d]

    x2 = x.reshape(batch * seq, d)
    ctx2 = ctx.reshape(batch * CTX_LEN, d)

    w_in = attn_w_in[0].astype(BF16)
    tn0 = 1024
    kinds0 = (["q"] * N_HEADS + ["k"] * N_KV_HEADS + [None] * N_KV_HEADS
              + ["g"] * (ATTN_WIDTH // LANES))
    px = _norm_proj(x2, norm_g[0], part(0, 0, batch, 0), part(0, 0, batch, 1), w_in,
                    tm=1024, tn=tn0, rows_per_mod=seq, tile_types=_tile_types(tn0, kinds0),
                    rope=_rope_tables(), q_scale=HEAD_DIM ** -0.5 * LOG2E)
    ckv = _norm_proj(ctx2, norm_g[0], part(0, batch, batch + 1, 0), part(0, batch, batch + 1, 1),
                     w_in[:, ATTN_WIDTH:ATTN_WIDTH + 2 * KV_WIDTH],
                     tm=512, tn=2 * KV_WIDTH, rows_per_mod=CTX_LEN)
    og = _attention(px, ckv, attn_sink[0], batch)
    x2 = _out_proj(og, attn_w_out[0].astype(BF16), x2, part(0, 0, batch, 2))

    kinds1 = [None] * (3 * w // LANES) + ["g"] * (w // LANES)
    p = _norm_proj(x2, norm_g[1], part(1, 0, batch, 0), part(1, 0, batch, 1),
                   hy_w_in[0].astype(BF16), tm=1024, tn=1024, rows_per_mod=seq,
                   tile_types=_tile_types(1024, kinds1))
    conv_w = hy_conv_w[0]
    conv_b = hy_conv_b[0].reshape(1, 3 * w)
    fwd, inv, anti, phase = _fold_tables()
    sd = _filters_sd(hy_w1[0], hy_b1[0], hy_w2[0], hy_b2[0], hy_w3[0], hy_b3[0], hy_freq[0])
    kspec = _filter_spectra2(sd, fwd, anti, phase)
    y1, v = _conv_fwd2(p, 2, kspec, 0, fwd, anti, batch, conv=(conv_w, conv_b, 2))
    z = _conv_inv2(y1, inv, anti, v, 0, hy_bias_d[0, 0], p, 0, conv_w, conv_b)
    y2, = _conv_fwd2(z, 0, kspec, 1, fwd, anti, batch)
    yg = _conv_inv2(y2, inv, anti, z, 0, hy_bias_d[0, 1], p, 1, conv_w, conv_b, g_blk0=3)
    out = _out_proj(yg, hy_w_out[0].astype(BF16), x2, part(1, 0, batch, 2), final_g)
    return out.reshape(batch, seq, d)
```

```python
import functools
import math

import jax
import jax.numpy as jnp
import numpy as np
from jax import lax
from jax.experimental import pallas as pl
from jax.experimental.pallas import tpu as pltpu

F32 = jnp.float32
BF16 = jnp.bfloat16

D_MODEL = 2048
SEQ = 2048
CTX_LEN = 256
GRID_W = 64
HEAD_DIM = 128
N_HEADS = 16
N_KV_HEADS = 4
GQA_GROUP = 4
ATTN_WIDTH = 2048
KV_WIDTH = 512
ATTN_IN = 2 * ATTN_WIDTH + 2 * KV_WIDTH
WINDOW = 128
BLOCK = 128
ROPE_BASE = 10000.0
HYENA_WIDTH = 2048
FILTER_EMB = 33
FILTER_HIDDEN = 64
DECAY_FAST = 0.3
DECAY_SLOW = 1.5
DECAY_TARGET = 1e-2
WINDOW_SHIFT = 0.05
NORM_EPS = 1e-6
NEG_INF = -1e30

LANES = 128
MXU_COLS = 256
ONES_ROWS = 16
LOG2E = math.log2(math.e)
FFT_N = 2 * SEQ
VMEM_LIMIT = 56 * 1024 * 1024


def _cparams(sem):
    return pltpu.CompilerParams(dimension_semantics=sem, vmem_limit_bytes=VMEM_LIMIT)


def _mods_kernel(c_ref, w_ref, b_ref, o_ref):
    c = c_ref[...]
    s = c * jax.nn.sigmoid(c)
    s_hi = s.astype(BF16)
    s_lo = (s - s_hi.astype(F32)).astype(BF16)
    w = w_ref[0]
    w_hi = w.astype(BF16)
    w_lo = (w - w_hi.astype(F32)).astype(BF16)
    lhs = jnp.concatenate([s_hi, s_lo], axis=0)
    r1 = jnp.dot(lhs, w_hi, preferred_element_type=F32)
    r2 = jnp.dot(s_hi, w_lo, preferred_element_type=F32)
    o_ref[0] = r1[:8] + r1[8:] + r2 + b_ref[0]


def _mods(cc, ada_w, ada_b):
    depth, d, n = ada_w.shape
    tn = 512
    return pl.pallas_call(
        _mods_kernel,
        grid=(depth, n // tn),
        in_specs=[
            pl.BlockSpec((8, d), lambda l, j: (0, 0)),
            pl.BlockSpec((1, d, tn), lambda l, j: (l, 0, j)),
            pl.BlockSpec((1, 1, tn), lambda l, j: (l, 0, j)),
        ],
        out_specs=pl.BlockSpec((1, 8, tn), lambda l, j: (l, 0, j)),
        out_shape=jax.ShapeDtypeStruct((depth, 8, n), F32),
        compiler_params=_cparams(("arbitrary", "arbitrary")),
        name="adaln_mods",
    )(cc, ada_w, ada_b.reshape(depth, 1, n))


def _rope_slab(t, cos, sin):
    lane = lax.broadcasted_iota(jnp.int32, t.shape, 1)
    first = (lane % 64) < 32
    partner = jnp.where(first, pltpu.roll(t, 96, 1), pltpu.roll(t, 32, 1))
    return t * cos + partner * sin


def _norm_mod_rows(x, g, mul, add):
    ms = jnp.mean(x * x, axis=-1, keepdims=True)
    return x * lax.rsqrt(ms + NORM_EPS) * g * mul + add


def _norm_mod_kernel(x_ref, g_ref, sh_ref, sc_ref, o_ref, *, row_chunk):
    g = g_ref[...]
    mul = 1.0 + sc_ref[0]
    add = sh_ref[0]
    for r in range(0, x_ref.shape[0], row_chunk):
        o_ref[r:r + row_chunk, :] = _norm_mod_rows(x_ref[r:r + row_chunk, :], g, mul, add).astype(o_ref.dtype)


def _norm_mod(x, g, shift, scale, *, tm, rows_per_mod):
    m, d = x.shape
    if shift.shape[0] > 1:
        mod_map = lambda i: ((i * tm) // rows_per_mod, 0, 0)
    else:
        mod_map = lambda i: (0, 0, 0)
    return pl.pallas_call(
        functools.partial(_norm_mod_kernel, row_chunk=min(tm, 256)),
        grid=(m // tm,),
        in_specs=[pl.BlockSpec((tm, d), lambda i: (i, 0)), pl.BlockSpec((1, d), lambda i: (0, 0)),
                  pl.BlockSpec((1, 1, d), mod_map), pl.BlockSpec((1, 1, d), mod_map)],
        out_specs=pl.BlockSpec((tm, d), lambda i: (i, 0)),
        out_shape=jax.ShapeDtypeStruct((m, d), BF16),
        compiler_params=_cparams(("arbitrary",)),
        name="norm_mod",
    )(x, g.reshape(1, d), shift, scale)


def _proj_kernel(*refs, tile_types, has_rope, q_scale):
    if has_rope:
        hx_ref, w_ref, cos_ref, sin_ref, o_ref, wb_ref = refs
    else:
        hx_ref, w_ref, o_ref, wb_ref = refs
    j = pl.program_id(0)
    tn = o_ref.shape[1]

    @pl.when(pl.program_id(1) == 0)
    def _():
        wb_ref[...] = w_ref[...].astype(BF16)

    w_ref = wb_ref

    def plain():
        o_ref[...] = jnp.dot(hx_ref[...], w_ref[...], preferred_element_type=F32).astype(o_ref.dtype)

    def gated():
        for c0 in range(0, tn, MXU_COLS):
            acc = jnp.dot(hx_ref[...], w_ref[:, c0:c0 + MXU_COLS], preferred_element_type=F32)
            o_ref[:, c0:c0 + MXU_COLS] = (acc * jax.nn.sigmoid(acc)).astype(o_ref.dtype)

    def roped(types):
        cos = cos_ref[...]
        sin = sin_ref[...]
        for c0 in range(0, tn, MXU_COLS):
            acc = jnp.dot(hx_ref[...], w_ref[:, c0:c0 + MXU_COLS], preferred_element_type=F32)
            for h in range(MXU_COLS // LANES):
                ty = types[c0 // LANES + h]
                slab = acc[:, h * LANES:(h + 1) * LANES]
                if ty is not None:
                    slab = _rope_slab(slab, cos, sin)
                    if ty == "q":
                        slab = slab * q_scale
                lo = c0 + h * LANES
                o_ref[:, lo:lo + LANES] = slab.astype(o_ref.dtype)

    if tile_types is None:
        plain()
        return

    groups = {}
    for t, types in enumerate(tile_types):
        groups.setdefault(tuple(types), []).append(t)
    for types, tiles in groups.items():
        cond = functools.reduce(jnp.logical_or, [j == t for t in tiles])
        if all(ty is None for ty in types):
            pl.when(cond)(plain)
        elif all(ty == "g" for ty in types):
            pl.when(cond)(gated)
        else:
            assert "g" not in types
            pl.when(cond)(functools.partial(roped, types))


def _proj(hx, w, col_blk0, n, *, tm, tn, tile_types=None, rope=None, q_scale=1.0):
    m, d = hx.shape
    in_specs = [
        pl.BlockSpec((tm, d), lambda j, i: (i, 0)),
        pl.BlockSpec((d, tn), lambda j, i: (0, col_blk0 + j)),
    ]
    args = [hx, w]
    if rope is not None:
        cos, sin = rope
        seq_tiles = cos.shape[0] // tm
        in_specs += [pl.BlockSpec((tm, LANES), lambda j, i: (i % seq_tiles, 0))] * 2
        args += [cos, sin]
    return pl.pallas_call(
        functools.partial(_proj_kernel, tile_types=tile_types, has_rope=rope is not None,
                          q_scale=q_scale),
        grid=(n // tn, m // tm),
        in_specs=in_specs,
        out_specs=pl.BlockSpec((tm, tn), lambda j, i: (i, j)),
        out_shape=jax.ShapeDtypeStruct((m, n), BF16),
        scratch_shapes=[pltpu.VMEM((d, tn), BF16)],
        compiler_params=_cparams(("arbitrary", "arbitrary")),
        name="proj",
    )(*args)


def _attn_kernel(sink_ref, q_ref, k_ref, v_ref, g_ref, kc_ref, vc_ref, o_ref,
                 vt_ref, vct_ref, bias_ref):
    kh = pl.program_id(1)
    band = 3 * BLOCK
    cols = GQA_GROUP * BLOCK
    n_blocks = SEQ // BLOCK

    vt_ref[:HEAD_DIM, :] = v_ref[...].T
    vt_ref[HEAD_DIM:, :] = jnp.ones((ONES_ROWS, SEQ), BF16)
    vct_ref[:HEAD_DIM, :] = vc_ref[...].T
    vct_ref[HEAD_DIM:, :] = jnp.ones((ONES_ROWS, CTX_LEN), BF16)
    krow = lax.broadcasted_iota(jnp.int32, (band, cols), 0)
    qcol = lax.broadcasted_iota(jnp.int32, (band, cols), 1) % BLOCK
    for idx, off in enumerate((0, -BLOCK, -2 * BLOCK)):
        bias_ref[idx] = jnp.where(jnp.abs(krow - qcol + off) <= WINDOW, 0.0, NEG_INF)

    kc = kc_ref[...]
    vct = vct_ref[...]
    sink_row = jnp.concatenate(
        [jnp.full((1, BLOCK), sink_ref[kh * GQA_GROUP + h] * LOG2E, F32) for h in range(GQA_GROUP)],
        axis=1)

    def body(n, carry):
        q0 = pl.multiple_of(n * BLOCK, BLOCK)
        ks = pl.multiple_of(jnp.clip((n - 1) * BLOCK, 0, SEQ - band), BLOCK)
        bidx = jnp.where(n == 0, 0, jnp.where(n == n_blocks - 1, 2, 1))
        qs = q_ref[pl.ds(q0, BLOCK), :]
        q4t = jnp.concatenate([qs[:, h * LANES:(h + 1) * LANES].T for h in range(GQA_GROUP)],
                              axis=1)
        kb = k_ref[pl.ds(ks, band), :]
        s_loc = jnp.dot(kb, q4t, preferred_element_type=F32) + bias_ref[bidx]
        s_ctx = jnp.dot(kc, q4t, preferred_element_type=F32)
        m = jnp.maximum(jnp.maximum(jnp.max(s_loc, axis=0, keepdims=True),
                                    jnp.max(s_ctx, axis=0, keepdims=True)), sink_row)
        p_loc = jnp.exp2(s_loc - m).astype(BF16)
        p_ctx = jnp.exp2(s_ctx - m).astype(BF16)
        ox = (jnp.dot(vt_ref[:, pl.ds(ks, band)], p_loc, preferred_element_type=F32)
              + jnp.dot(vct, p_ctx, preferred_element_type=F32))
        den = ox[HEAD_DIM:HEAD_DIM + 1, :] + jnp.exp2(sink_row - m)
        ot = ox[:HEAD_DIM, :] * (1.0 / den)
        gs = g_ref[pl.ds(q0, BLOCK), :].astype(F32)
        for h in range(GQA_GROUP):
            oh = ot[:, h * LANES:(h + 1) * LANES].T * gs[:, h * LANES:(h + 1) * LANES]
            o_ref[pl.ds(q0, BLOCK), h * LANES:(h + 1) * LANES] = oh.astype(o_ref.dtype)
        return carry

    lax.fori_loop(0, n_blocks, body, 0, unroll=4)


def _attention(px, ckv, sink, batch):
    gw = GQA_GROUP * HEAD_DIM
    k_blk0 = ATTN_WIDTH // HEAD_DIM
    v_blk0 = (ATTN_WIDTH + KV_WIDTH) // HEAD_DIM
    g_blk0 = (ATTN_WIDTH + 2 * KV_WIDTH) // gw
    return pl.pallas_call(
        _attn_kernel,
        grid=(batch, N_KV_HEADS),
        in_specs=[
            pl.BlockSpec(memory_space=pltpu.SMEM),
            pl.BlockSpec((SEQ, gw), lambda b, h: (b, h)),
            pl.BlockSpec((SEQ, HEAD_DIM), lambda b, h: (b, k_blk0 + h)),
            pl.BlockSpec((SEQ, HEAD_DIM), lambda b, h: (b, v_blk0 + h)),
            pl.BlockSpec((SEQ, gw), lambda b, h: (b, g_blk0 + h)),
            pl.BlockSpec((CTX_LEN, HEAD_DIM), lambda b, h: (b, h)),
            pl.BlockSpec((CTX_LEN, HEAD_DIM), lambda b, h: (b, N_KV_HEADS + h)),
        ],
        out_specs=pl.BlockSpec((SEQ, gw), lambda b, h: (b, h)),
        out_shape=jax.ShapeDtypeStruct((batch * SEQ, ATTN_WIDTH), BF16),
        scratch_shapes=[pltpu.VMEM((HEAD_DIM + ONES_ROWS, SEQ), BF16),
                        pltpu.VMEM((HEAD_DIM + ONES_ROWS, CTX_LEN), BF16),
                        pltpu.VMEM((3, 3 * BLOCK, GQA_GROUP * BLOCK), F32)],
        compiler_params=_cparams(("arbitrary", "arbitrary")),
        name="banded_attention",
    )(sink, px, px, px, px, ckv, ckv)


def _out_proj_kernel(*refs, final):
    if final:
        a_ref, w_ref, x_ref, gate_ref, fg_ref, o_ref = refs
    else:
        a_ref, w_ref, x_ref, gate_ref, ng_ref, sh_ref, sc_ref, o_ref, hx_ref = refs
    acc = jnp.dot(a_ref[...], w_ref[...], preferred_element_type=F32)
    y = x_ref[...] + gate_ref[0] * acc
    if final:
        ms = jnp.mean(y * y, axis=-1, keepdims=True)
        y = y * lax.rsqrt(ms + NORM_EPS) * fg_ref[...]
    else:
        hx_ref[...] = _norm_mod_rows(y, ng_ref[...], 1.0 + sc_ref[0], sh_ref[0]).astype(hx_ref.dtype)
    o_ref[...] = y


def _out_proj(a, w, x, gate, *, final_g=None, next_norm=None, tm=512):
    m, d = x.shape
    kdim = a.shape[1]
    final = final_g is not None
    per_batch = lambda i: ((i * tm) // SEQ, 0, 0)
    row_tile = pl.BlockSpec((tm, d), lambda i: (i, 0))
    vec = pl.BlockSpec((1, d), lambda i: (0, 0))
    in_specs = [
        pl.BlockSpec((tm, kdim), lambda i: (i, 0)),
        pl.BlockSpec((kdim, d), lambda i: (0, 0)),
        row_tile,
        pl.BlockSpec((1, 1, d), per_batch),
    ]
    args = [a, w, x, gate]
    out_specs = [row_tile]
    out_shape = [jax.ShapeDtypeStruct((m, d), F32)]
    if final:
        in_specs.append(vec)
        args.append(final_g.reshape(1, d))
    else:
        ng, shift, scale = next_norm
        in_specs += [vec, pl.BlockSpec((1, 1, d), per_batch), pl.BlockSpec((1, 1, d), per_batch)]
        args += [ng.reshape(1, d), shift, scale]
        out_specs.append(row_tile)
        out_shape.append(jax.ShapeDtypeStruct((m, d), BF16))
    return pl.pallas_call(
        functools.partial(_out_proj_kernel, final=final),
        grid=(m // tm,),
        in_specs=in_specs,
        out_specs=out_specs,
        out_shape=out_shape,
        compiler_params=_cparams(("arbitrary",)),
        name="out_proj",
    )(*args)


def _short_conv_kernel(p_ref, w_ref, b_ref, o_ref):
    length = p_ref.shape[0]
    row = lax.broadcasted_iota(jnp.int32, (length, LANES), 0)
    for c in range(0, p_ref.shape[1], LANES):
        u = p_ref[:, c:c + LANES].astype(F32)
        w = w_ref[:, c:c + LANES]
        prev = jnp.where(row == 0, 0.0, pltpu.roll(u, 1, 0))
        nxt = jnp.where(row == length - 1, 0.0, pltpu.roll(u, length - 1, 0))
        y = prev * w[0:1] + u * w[1:2] + nxt * w[2:3] + b_ref[:, c:c + LANES]
        o_ref[:, c:c + LANES] = y.astype(o_ref.dtype)


def _short_conv(p, conv_w, conv_b, batch, *, tn=512):
    n = conv_w.shape[1]
    return pl.pallas_call(
        _short_conv_kernel,
        grid=(batch, n // tn),
        in_specs=[
            pl.BlockSpec((SEQ, tn), lambda b, j: (b, j)),
            pl.BlockSpec((3, tn), lambda b, j: (0, j)),
            pl.BlockSpec((1, tn), lambda b, j: (0, j)),
        ],
        out_specs=pl.BlockSpec((SEQ, tn), lambda b, j: (b, j)),
        out_shape=jax.ShapeDtypeStruct((batch * SEQ, n), BF16),
        compiler_params=_cparams(("arbitrary", "arbitrary")),
        name="short_conv",
    )(p, conv_w, conv_b.reshape(1, n))


def _filter_kernel(w1t_ref, w1c_ref, w1s_ref, b1_ref, w2_ref, b2_ref, fq_ref, w3_ref, b3_ref,
                   o_ref, hid_ref):
    j = pl.program_id(0)
    length = o_ref.shape[0]
    tn = o_ref.shape[1]
    hp = lax.Precision.HIGHEST

    @pl.when(j == 0)
    def _():
        n = lax.broadcasted_iota(jnp.int32, (length, LANES), 0).astype(F32)
        lane = lax.broadcasted_iota(jnp.int32, (length, LANES), 1)
        bands = (FILTER_EMB - 1) // 2
        fr_step = (bands - 1 - 1e-4) / (bands - 1)
        fr = jnp.where(lane < bands, 1e-4 + lane.astype(F32) * fr_step, 0.0)
        ang = (2.0 * math.pi * n / length) * fr
        t = n / (length - 1)
        fq = fq_ref[...]
        pre = (t * w1t_ref[...]
               + jnp.dot(jnp.cos(ang), w1c_ref[...], precision=hp, preferred_element_type=F32)
               + jnp.dot(-jnp.sin(ang), w1s_ref[...], precision=hp, preferred_element_type=F32)
               + b1_ref[...])
        hid = jnp.sin(fq * pre)
        hid = jnp.sin(fq * (jnp.dot(hid, w2_ref[...], precision=hp, preferred_element_type=F32)
                            + b2_ref[...]))
        hid_ref[...] = hid

    hf = jnp.dot(hid_ref[...], w3_ref[...], precision=hp, preferred_element_type=F32) + b3_ref[...]
    row = lax.broadcasted_iota(jnp.int32, (length, tn), 0)
    col = lax.broadcasted_iota(jnp.int32, (length, tn), 1) + j * tn
    chan = (col % HYENA_WIDTH).astype(F32)
    min_decay = math.log(DECAY_TARGET) / DECAY_SLOW
    max_decay = math.log(DECAY_TARGET) / DECAY_FAST
    delta = min_decay + chan * ((max_decay - min_decay) / (HYENA_WIDTH - 1))
    t = row.astype(F32) / (length - 1)
    filt = hf * (jnp.exp(-t * jnp.abs(delta)) + WINDOW_SHIFT)
    backward = (col // HYENA_WIDTH) % 2 == 1
    filt = jnp.where(backward & (row == 0), 0.0, filt)
    o_ref[...] = filt.astype(o_ref.dtype)


def _filters(w1, b1, w2, b2, w3, b3, freq, *, tn=1024):
    hpad = LANES - FILTER_HIDDEN
    bands = (FILTER_EMB - 1) // 2
    n = w3.shape[1]
    w1t = jnp.pad(w1[0:1], ((0, 0), (0, hpad)))
    w1c = jnp.pad(w1[1:1 + bands], ((0, LANES - bands), (0, hpad)))
    w1s = jnp.pad(w1[1 + bands:], ((0, LANES - bands), (0, hpad)))
    b1p = jnp.pad(b1.reshape(1, -1), ((0, 0), (0, hpad)))
    w2p = jnp.pad(w2, ((0, hpad), (0, hpad)))
    b2p = jnp.pad(b2.reshape(1, -1), ((0, 0), (0, hpad)))
    fqp = jnp.pad(freq.reshape(1, -1), ((0, 0), (0, hpad)))
    w3p = jnp.pad(w3, ((0, hpad), (0, 0)))
    small = lambda shape: pl.BlockSpec(shape, lambda j: (0, 0))
    return pl.pallas_call(
        _filter_kernel,
        grid=(n // tn,),
        in_specs=[small((1, LANES)), small((LANES, LANES)), small((LANES, LANES)), small((1, LANES)),
                  small((LANES, LANES)), small((1, LANES)), small((1, LANES)),
                  pl.BlockSpec((LANES, tn), lambda j: (0, j)),
                  pl.BlockSpec((1, tn), lambda j: (0, j))],
        out_specs=pl.BlockSpec((SEQ, tn), lambda j: (0, j)),
        out_shape=jax.ShapeDtypeStruct((SEQ, n), BF16),
        scratch_shapes=[pltpu.VMEM((SEQ, LANES), F32)],
        compiler_params=_cparams(("arbitrary",)),
        name="hyena_filters",
    )(w1t, w1c, w1s, b1p, w2p, b2p, fqp, w3p, b3.reshape(1, n))


def _dft_tables():
    idx = np.arange(SEQ, dtype=np.int64)
    ang = ((idx[:, None] * idx[None, :]) % FFT_N).astype(np.float64) * (2.0 * math.pi / FFT_N)
    cm = np.cos(ang)
    sm = np.sin(ang)
    sm[0, :] = np.where(idx % 2 == 0, 1.0, -1.0)
    as_bf16 = lambda a: jnp.asarray(a.astype(np.float32)).astype(BF16)
    return as_bf16(cm), as_bf16(sm), as_bf16(np.ascontiguousarray(sm.T))


def _spec_scale(i, tf, shape):
    row = lax.broadcasted_iota(jnp.int32, shape, 0) + i * tf
    is0 = row == 0
    return is0, jnp.where(is0, 1.0 / FFT_N, 2.0 / FFT_N)


def _filter_spec_kernel(hf_ref, hb_ref, cm_ref, sm_ref, ka_ref, kb_ref):
    i = pl.program_id(2)
    tf = cm_ref.shape[0]
    cm = cm_ref[...]
    sm = sm_ref[...]
    hf = hf_ref[...]
    hb = hb_ref[...]
    a_f = jnp.dot(cm, hf, preferred_element_type=F32)
    b_f = jnp.dot(sm, hf, preferred_element_type=F32)
    a_b = jnp.dot(cm, hb, preferred_element_type=F32)
    b_b = jnp.dot(sm, hb, preferred_element_type=F32)
    is0, scale = _spec_scale(i, tf, a_f.shape)
    ka_ref[0] = ((a_f + a_b) * scale).astype(ka_ref.dtype)
    kb_ref[0] = (jnp.where(is0, b_f + b_b, b_f - b_b) * scale).astype(kb_ref.dtype)


def _filter_spectra(filt, cm, sm, *, tf=512, tn=512):
    w = HYENA_WIDTH
    ct = w // tn
    out = jax.ShapeDtypeStruct((2, SEQ, w), BF16)
    return pl.pallas_call(
        _filter_spec_kernel,
        grid=(2, ct, SEQ // tf),
        in_specs=[
            pl.BlockSpec((SEQ, tn), lambda o, j, i: (0, 2 * o * ct + j)),
            pl.BlockSpec((SEQ, tn), lambda o, j, i: (0, (2 * o + 1) * ct + j)),
            pl.BlockSpec((tf, SEQ), lambda o, j, i: (i, 0)),
            pl.BlockSpec((tf, SEQ), lambda o, j, i: (i, 0)),
        ],
        out_specs=[pl.BlockSpec((1, tf, tn), lambda o, j, i: (o, i, j))] * 2,
        out_shape=[out, out],
        compiler_params=_cparams(("arbitrary", "arbitrary", "arbitrary")),
        name="filter_spectra",
    )(filt, filt, cm, sm)


def _conv_fwd_kernel(u_ref, cm_ref, sm_ref, ka_ref, kb_ref, ya_ref, yb_ref):
    i = pl.program_id(2)
    tf = cm_ref.shape[0]
    u = u_ref[...]
    a = jnp.dot(cm_ref[...], u, preferred_element_type=F32)
    b = jnp.dot(sm_ref[...], u, preferred_element_type=F32)
    ka = ka_ref[0].astype(F32)
    kb = kb_ref[0].astype(F32)
    is0, _ = _spec_scale(i, tf, a.shape)
    bkb = b * kb
    ya_ref[0] = (a * ka - jnp.where(is0, 0.0, bkb)).astype(ya_ref.dtype)
    yb_ref[0] = jnp.where(is0, bkb, a * kb + b * ka).astype(yb_ref.dtype)


def _conv_fwd(u, col_blk0, ka, kb, order, cm, sm, batch, *, tf=512, tn=1024):
    w = HYENA_WIDTH
    out = jax.ShapeDtypeStruct((batch, SEQ, w), BF16)
    c0 = col_blk0 * (w // tn)
    return pl.pallas_call(
        _conv_fwd_kernel,
        grid=(batch, w // tn, SEQ // tf),
        in_specs=[
            pl.BlockSpec((SEQ, tn), lambda b, j, i: (b, c0 + j)),
            pl.BlockSpec((tf, SEQ), lambda b, j, i: (i, 0)),
            pl.BlockSpec((tf, SEQ), lambda b, j, i: (i, 0)),
            pl.BlockSpec((1, tf, tn), lambda b, j, i: (order, i, j)),
            pl.BlockSpec((1, tf, tn), lambda b, j, i: (order, i, j)),
        ],
        out_specs=[pl.BlockSpec((1, tf, tn), lambda b, j, i: (b, i, j))] * 2,
        out_shape=[out, out],
        compiler_params=_cparams(("arbitrary", "arbitrary", "arbitrary")),
        name="long_conv_fwd",
    )(u, cm, sm, ka, kb)


def _conv_inv_kernel(*refs, gated):
    if gated:
        ya_ref, yb_ref, cm_ref, st_ref, u_ref, d_ref, x_ref, g_ref, o_ref = refs
    else:
        ya_ref, yb_ref, cm_ref, st_ref, u_ref, d_ref, x_ref, o_ref = refs
    y = (jnp.dot(cm_ref[...], ya_ref[0], preferred_element_type=F32)
         + jnp.dot(st_ref[...], yb_ref[0], preferred_element_type=F32))
    y = y + u_ref[...].astype(F32) * d_ref[...]
    y = y * x_ref[...].astype(F32)
    if gated:
        y = y * g_ref[...].astype(F32)
    o_ref[...] = y.astype(o_ref.dtype)


def _conv_inv(ya, yb, cm, st, u, u_blk0, d, x, x_blk0, g=None, g_blk0=0, *, tt=512, tn=1024):
    batch = ya.shape[0]
    w = HYENA_WIDTH
    ct = w // tn
    tiles = SEQ // tt
    gated = g is not None
    in_specs = [
        pl.BlockSpec((1, SEQ, tn), lambda b, j, i: (b, 0, j)),
        pl.BlockSpec((1, SEQ, tn), lambda b, j, i: (b, 0, j)),
        pl.BlockSpec((tt, SEQ), lambda b, j, i: (i, 0)),
        pl.BlockSpec((tt, SEQ), lambda b, j, i: (i, 0)),
        pl.BlockSpec((tt, tn), lambda b, j, i: (b * tiles + i, u_blk0 * ct + j)),
        pl.BlockSpec((1, tn), lambda b, j, i: (0, j)),
        pl.BlockSpec((tt, tn), lambda b, j, i: (b * tiles + i, x_blk0 * ct + j)),
    ]
    args = [ya, yb, cm, st, u, d.reshape(1, w), x]
    if gated:
        in_specs.append(pl.BlockSpec((tt, tn), lambda b, j, i: (b * tiles + i, g_blk0 * ct + j)))
        args.append(g)
    return pl.pallas_call(
        functools.partial(_conv_inv_kernel, gated=gated),
        grid=(batch, ct, tiles),
        in_specs=in_specs,
        out_specs=pl.BlockSpec((tt, tn), lambda b, j, i: (b * tiles + i, j)),
        out_shape=jax.ShapeDtypeStruct((batch * SEQ, w), BF16),
        compiler_params=_cparams(("arbitrary", "arbitrary", "arbitrary")),
        name="long_conv_inv",
    )(*args)


HALF = SEQ // 2
FLIP_BLOCK = 256


def _fold_tables():
    t2 = 2 * np.arange(HALF, dtype=np.int64) + 1
    g = np.arange(HALF, dtype=np.int64)

    def tab(f):
        ang = ((f[:, None] * t2[None, :]) % (2 * FFT_N)).astype(np.float64) * (math.pi / FFT_N)
        return np.cos(ang), np.sin(ang)

    ce, se = tab(2 * g)
    co, so = tab(2 * g + 1)
    se[0, :] = np.where(np.arange(HALF) % 2 == 0, 1.0, -1.0)
    const = lambda a, dt: jnp.asarray(np.ascontiguousarray(a).astype(np.float32)).astype(dt)
    fwd = const(np.stack([ce, se, co, so]), BF16)
    inv = const(np.stack([ce.T, so.T, se.T, co.T]), BF16)
    anti = const(np.eye(FLIP_BLOCK)[::-1], BF16)
    theta = np.stack([2 * g, 2 * g + 1]).astype(np.float64) * (math.pi / FFT_N)
    phase = const(np.stack([np.cos(theta), np.sin(theta)])[..., None], F32)
    return fwd, inv, anti, phase


def _flip_rows(h, anti):
    nb = HALF // FLIP_BLOCK
    return jnp.concatenate(
        [jnp.dot(anti, h[(nb - 1 - a) * FLIP_BLOCK:(nb - a) * FLIP_BLOCK, :], preferred_element_type=F32)
         for a in range(nb)], axis=0)


def _fold(x_ref, anti):
    x0 = x_ref[:HALF, :].astype(F32)
    xr = _flip_rows(x_ref[HALF:, :], anti)
    return (x0 + xr).astype(BF16), (x0 - xr).astype(BF16)


def _half_spectrum(tab_ref, xs, xa):
    dot = lambda k, v: jnp.dot(tab_ref[k], v, preferred_element_type=F32)
    return dot(0, xs), dot(1, xa), dot(2, xa), dot(3, xs)


def _short_conv_into(p_ref, w_ref, b_ref, dst_ref):
    length = p_ref.shape[0]
    row = lax.broadcasted_iota(jnp.int32, (length, LANES), 0)
    for c in range(0, p_ref.shape[1], LANES):
        u = p_ref[:, c:c + LANES].astype(F32)
        w = w_ref[:, c:c + LANES]
        prev = jnp.where(row == 0, 0.0, pltpu.roll(u, 1, 0))
        nxt = jnp.where(row == length - 1, 0.0, pltpu.roll(u, length - 1, 0))
        y = prev * w[0:1] + u * w[1:2] + nxt * w[2:3] + b_ref[:, c:c + LANES]
        dst_ref[:, c:c + LANES] = y.astype(dst_ref.dtype)


def _resident(shape):
    zeros = (0,) * len(shape)
    return pl.BlockSpec(shape, lambda *_: zeros, pipeline_mode=pl.Buffered(1))


def _filter_sd_kernel(w1t_ref, w1c_ref, w1s_ref, b1_ref, w2_ref, b2_ref, fq_ref,
                      w3f_ref, b3f_ref, w3b_ref, b3b_ref, o_ref, hid_ref):
    first = (pl.program_id(0) == 0) & (pl.program_id(1) == 0)
    j = pl.program_id(1)
    length = o_ref.shape[1]
    tn = o_ref.shape[2]
    hp = lax.Precision.HIGHEST

    @pl.when(first)
    def _():
        n = lax.broadcasted_iota(jnp.int32, (length, LANES), 0).astype(F32)
        lane = lax.broadcasted_iota(jnp.int32, (length, LANES), 1)
        bands = (FILTER_EMB - 1) // 2
        fr_step = (bands - 1 - 1e-4) / (bands - 1)
        fr = jnp.where(lane < bands, 1e-4 + lane.astype(F32) * fr_step, 0.0)
        ang = (2.0 * math.pi * n / length) * fr
        t = n / (length - 1)
        fq = fq_ref[...]
        pre = (t * w1t_ref[...]
               + jnp.dot(jnp.cos(ang), w1c_ref[...], precision=hp, preferred_element_type=F32)
               + jnp.dot(-jnp.sin(ang), w1s_ref[...], precision=hp, preferred_element_type=F32)
               + b1_ref[...])
        hid = jnp.sin(fq * pre)
        hid = jnp.sin(fq * (jnp.dot(hid, w2_ref[...], precision=hp, preferred_element_type=F32)
                            + b2_ref[...]))
        hid_hi = hid.astype(BF16)
        hid_ref[0] = hid_hi
        hid_ref[1] = (hid - hid_hi.astype(F32)).astype(BF16)

    def dot3(w_ref):
        wf = w_ref[...]
        w_hi = wf.astype(BF16)
        w_lo = (wf - w_hi.astype(F32)).astype(BF16)
        return (jnp.dot(hid_ref[0], w_hi, preferred_element_type=F32)
                + jnp.dot(hid_ref[1], w_hi, preferred_element_type=F32)
                + jnp.dot(hid_ref[0], w_lo, preferred_element_type=F32))

    row = lax.broadcasted_iota(jnp.int32, (length, tn), 0)
    chan = (lax.broadcasted_iota(jnp.int32, (length, tn), 1) + j * tn).astype(F32)
    min_decay = math.log(DECAY_TARGET) / DECAY_SLOW
    max_decay = math.log(DECAY_TARGET) / DECAY_FAST
    delta = min_decay + chan * ((max_decay - min_decay) / (HYENA_WIDTH - 1))
    t = row.astype(F32) / (length - 1)
    window = jnp.exp(-t * jnp.abs(delta)) + WINDOW_SHIFT
    hf = (dot3(w3f_ref) + b3f_ref[...]) * window
    hb = (dot3(w3b_ref) + b3b_ref[...]) * window
    hb = jnp.where(row == 0, 0.0, hb)
    o_ref[0] = (hf + hb).astype(o_ref.dtype)
    o_ref[1] = (hf - hb).astype(o_ref.dtype)


def _filters_sd(w1, b1, w2, b2, w3, b3, freq, *, tn=512):
    hpad = LANES - FILTER_HIDDEN
    bands = (FILTER_EMB - 1) // 2
    w = HYENA_WIDTH
    ct = w // tn
    n = w3.shape[1]
    w1t = jnp.pad(w1[0:1], ((0, 0), (0, hpad)))
    w1c = jnp.pad(w1[1:1 + bands], ((0, LANES - bands), (0, hpad)))
    w1s = jnp.pad(w1[1 + bands:], ((0, LANES - bands), (0, hpad)))
    b1p = jnp.pad(b1.reshape(1, -1), ((0, 0), (0, hpad)))
    w2p = jnp.pad(w2, ((0, hpad), (0, hpad)))
    b2p = jnp.pad(b2.reshape(1, -1), ((0, 0), (0, hpad)))
    fqp = jnp.pad(freq.reshape(1, -1), ((0, 0), (0, hpad)))
    w3p = jnp.pad(w3, ((0, hpad), (0, 0)))
    b3r = b3.reshape(1, n)
    small = lambda shape: pl.BlockSpec(shape, lambda o, j: (0, 0))
    fwd_cols = lambda o, j: (0, 2 * o * ct + j)
    bwd_cols = lambda o, j: (0, (2 * o + 1) * ct + j)
    return pl.pallas_call(
        _filter_sd_kernel,
        grid=(2, ct),
        in_specs=[small((1, LANES)), small((LANES, LANES)), small((LANES, LANES)), small((1, LANES)),
                  small((LANES, LANES)), small((1, LANES)), small((1, LANES)),
                  pl.BlockSpec((LANES, tn), fwd_cols), pl.BlockSpec((1, tn), fwd_cols),
                  pl.BlockSpec((LANES, tn), bwd_cols), pl.BlockSpec((1, tn), bwd_cols)],
        out_specs=pl.BlockSpec((2, SEQ, tn), lambda o, j: (0, 0, o * ct + j)),
        out_shape=jax.ShapeDtypeStruct((2, SEQ, 2 * w), BF16),
        scratch_shapes=[pltpu.VMEM((2, SEQ, LANES), BF16)],
        compiler_params=_cparams(("arbitrary", "arbitrary")),
        name="hyena_filters",
    )(w1t, w1c, w1s, b1p, w2p, b2p, fqp, w3p, b3r, w3p, b3r)


def _filter_spec_kernel2(s_ref, d_ref, tab_ref, anti_ref, ph_ref, k_ref):
    anti = anti_ref[...]
    sa_e, sb_e, sa_o, sb_o = _half_spectrum(tab_ref, *_fold(s_ref.at[0], anti))
    da_e, db_e, da_o, db_o = _half_spectrum(tab_ref, *_fold(d_ref.at[0], anti))
    is0 = lax.broadcasted_iota(jnp.int32, sa_e.shape, 0) == 0
    scale_e = jnp.where(is0, 1.0 / FFT_N, 2.0 / FFT_N)
    cos_e, sin_e = ph_ref[0, 0], ph_ref[1, 0]
    cos_o, sin_o = ph_ref[0, 1], ph_ref[1, 1]
    k_ref[0, 0] = ((sa_e * cos_e + sb_e * sin_e) * scale_e).astype(k_ref.dtype)
    k_ref[0, 1] = (jnp.where(is0, sb_e, db_e * cos_e - da_e * sin_e) * scale_e).astype(k_ref.dtype)
    k_ref[0, 2] = ((sa_o * cos_o + sb_o * sin_o) * (2.0 / FFT_N)).astype(k_ref.dtype)
    k_ref[0, 3] = ((db_o * cos_o - da_o * sin_o) * (2.0 / FFT_N)).astype(k_ref.dtype)


def _filter_spectra2(sd, fwd, anti, phase, *, tn=512):
    w = HYENA_WIDTH
    ct = w // tn
    return pl.pallas_call(
        _filter_spec_kernel2,
        grid=(2, ct),
        in_specs=[
            pl.BlockSpec((1, SEQ, tn), lambda o, j: (0, 0, o * ct + j)),
            pl.BlockSpec((1, SEQ, tn), lambda o, j: (1, 0, o * ct + j)),
            _resident(fwd.shape), _resident(anti.shape), _resident(phase.shape),
        ],
        out_specs=pl.BlockSpec((1, 4, HALF, tn), lambda o, j: (o, 0, 0, j)),
        out_shape=jax.ShapeDtypeStruct((2, 4, HALF, w), BF16),
        compiler_params=_cparams(("arbitrary", "arbitrary")),
        name="filter_spectra",
    )(sd, sd, fwd, anti, phase)


def _conv_fwd_kernel2(*refs, with_conv):
    if with_conv:
        p_ref, cw_ref, cb_ref, tab_ref, anti_ref, k_ref, y_ref, v_ref = refs
        _short_conv_into(p_ref, cw_ref, cb_ref, v_ref)
        x_ref = v_ref
    else:
        x_ref, tab_ref, anti_ref, k_ref, y_ref = refs
    a_e, b_e, a_o, b_o = _half_spectrum(tab_ref, *_fold(x_ref, anti_ref[...]))
    f32 = lambda r: r.astype(F32)
    ka_e, kb_e, ka_o, kb_o = f32(k_ref[0, 0]), f32(k_ref[0, 1]), f32(k_ref[0, 2]), f32(k_ref[0, 3])
    is0 = lax.broadcasted_iota(jnp.int32, a_e.shape, 0) == 0
    bkb = b_e * kb_e
    y_ref[0, 0] = (a_e * ka_e - jnp.where(is0, 0.0, bkb)).astype(y_ref.dtype)
    y_ref[0, 1] = jnp.where(is0, bkb, a_e * kb_e + b_e * ka_e).astype(y_ref.dtype)
    y_ref[0, 2] = (a_o * ka_o - b_o * kb_o).astype(y_ref.dtype)
    y_ref[0, 3] = (a_o * kb_o + b_o * ka_o).astype(y_ref.dtype)


def _conv_fwd2(x, x_blk0, kspec, order, fwd, anti, batch, conv=None, *, tn=512):
    w = HYENA_WIDTH
    ct = w // tn
    in_specs = [pl.BlockSpec((SEQ, tn), lambda b, j: (b, x_blk0 * ct + j))]
    args = [x]
    out_specs = [pl.BlockSpec((1, 4, HALF, tn), lambda b, j: (b, 0, 0, j))]
    out_shape = [jax.ShapeDtypeStruct((batch, 4, HALF, w), BF16)]
    if conv is not None:
        cw, cb, c_blk0 = conv
        in_specs += [pl.BlockSpec((3, tn), lambda b, j: (0, c_blk0 * ct + j)),
                     pl.BlockSpec((1, tn), lambda b, j: (0, c_blk0 * ct + j))]
        args += [cw, cb]
        out_specs.append(pl.BlockSpec((SEQ, tn), lambda b, j: (b, j)))
        out_shape.append(jax.ShapeDtypeStruct((batch * SEQ, w), BF16))
    in_specs += [_resident(fwd.shape), _resident(anti.shape),
                 pl.BlockSpec((1, 4, HALF, tn), lambda b, j: (order, 0, 0, j))]
    args += [fwd, anti, kspec]
    return pl.pallas_call(
        functools.partial(_conv_fwd_kernel2, with_conv=conv is not None),
        grid=(batch, ct),
        in_specs=in_specs,
        out_specs=out_specs,
        out_shape=out_shape,
        compiler_params=_cparams(("arbitrary", "arbitrary")),
        name="long_conv_fwd",
    )(*args)


def _conv_inv_kernel2(*refs, gated):
    if gated:
        y_ref, tab_ref, anti_ref, u_ref, d_ref, p_ref, cw_ref, cb_ref, g_ref, o_ref, xm_ref = refs
    else:
        y_ref, tab_ref, anti_ref, u_ref, d_ref, p_ref, cw_ref, cb_ref, o_ref, xm_ref = refs
    _short_conv_into(p_ref, cw_ref, cb_ref, xm_ref)
    dot = lambda k, plane: jnp.dot(tab_ref[k], y_ref[0, plane], preferred_element_type=F32)
    p1 = dot(0, 0) + dot(1, 3)
    p2 = dot(2, 1) + dot(3, 2)
    halves = (p1 + p2, _flip_rows((p1 - p2).astype(BF16), anti_ref[...]))
    d = d_ref[...]
    for k, y in enumerate(halves):
        rows = slice(k * HALF, (k + 1) * HALF)
        out = (y + u_ref[rows, :].astype(F32) * d) * xm_ref[rows, :]
        if gated:
            out = out * g_ref[rows, :].astype(F32)
        o_ref[rows, :] = out.astype(o_ref.dtype)


def _conv_inv2(y, inv, anti, u, u_blk0, d, p, p_blk0, conv_w, conv_b, g_blk0=None, *, tn=512):
    batch = y.shape[0]
    w = HYENA_WIDTH
    ct = w // tn
    gated = g_blk0 is not None
    col = lambda blk0: (lambda b, j: (b, blk0 * ct + j))
    in_specs = [
        pl.BlockSpec((1, 4, HALF, tn), lambda b, j: (b, 0, 0, j)),
        _resident(inv.shape), _resident(anti.shape),
        pl.BlockSpec((SEQ, tn), col(u_blk0)),
        pl.BlockSpec((1, tn), lambda b, j: (0, j)),
        pl.BlockSpec((SEQ, tn), col(p_blk0)),
        pl.BlockSpec((3, tn), lambda b, j: (0, p_blk0 * ct + j)),
        pl.BlockSpec((1, tn), lambda b, j: (0, p_blk0 * ct + j)),
    ]
    args = [y, inv, anti, u, d.reshape(1, w), p, conv_w, conv_b]
    if gated:
        in_specs.append(pl.BlockSpec((SEQ, tn), col(g_blk0)))
        args.append(p)
    return pl.pallas_call(
        functools.partial(_conv_inv_kernel2, gated=gated),
        grid=(batch, ct),
        in_specs=in_specs,
        out_specs=pl.BlockSpec((SEQ, tn), lambda b, j: (b, j)),
        out_shape=jax.ShapeDtypeStruct((batch * SEQ, w), BF16),
        scratch_shapes=[pltpu.VMEM((SEQ, tn), F32)],
        compiler_params=_cparams(("arbitrary", "arbitrary")),
        name="long_conv_inv",
    )(*args)


def _rope_tables():
    pos = np.arange(SEQ)
    row = (pos // GRID_W).astype(np.float32)
    col = (pos % GRID_W).astype(np.float32)
    half = HEAD_DIM // 2
    inv = (ROPE_BASE ** (-np.arange(0, half, 2, dtype=np.float32) / half)).astype(np.float32)
    ar = row[:, None] * inv[None]
    ac = col[:, None] * inv[None]
    cos = np.concatenate([np.cos(ar), np.cos(ar), np.cos(ac), np.cos(ac)], axis=1)
    sin = np.concatenate([-np.sin(ar), np.sin(ar), -np.sin(ac), np.sin(ac)], axis=1)
    return jnp.asarray(cos, F32), jnp.asarray(sin, F32)


def _tile_types(tn, kinds):
    per_tile = tn // LANES
    return [kinds[s0:s0 + per_tile] for s0 in range(0, len(kinds), per_tile)]


def kernel(x, c, ctx, c_ctx, norm_g, ada_w, ada_b, attn_w_in, attn_w_out, attn_sink, hy_w_in,
           hy_conv_w, hy_conv_b, hy_w1, hy_b1, hy_w2, hy_b2, hy_w3, hy_b3, hy_freq, hy_bias_d,
           hy_w_out, final_g):
    batch, seq, d = x.shape
    assert (seq, d) == (SEQ, D_MODEL) and ctx.shape[1] == CTX_LEN
    assert norm_g.shape[0] == 2 and attn_w_in.shape[0] == 1 and hy_w_in.shape[0] == 1
    w = HYENA_WIDTH

    cc = jnp.concatenate([c, c_ctx[None], jnp.zeros((8 - batch - 1, d), F32)], axis=0)
    mods = _mods(cc, ada_w, ada_b)
    part = lambda layer, r0, r1, k: mods[layer, r0:r1, None, k * d:(k + 1) * d]

    x2 = x.reshape(batch * seq, d)
    ctx2 = ctx.reshape(batch * CTX_LEN, d)

    tn = 1024
    kinds0 = (["q"] * N_HEADS + ["k"] * N_KV_HEADS + [None] * N_KV_HEADS
              + ["g"] * (ATTN_WIDTH // LANES))
    hx = _norm_mod(x2, norm_g[0], part(0, 0, batch, 0), part(0, 0, batch, 1), tm=512,
                   rows_per_mod=seq)
    hc = _norm_mod(ctx2, norm_g[0], part(0, batch, batch + 1, 0), part(0, batch, batch + 1, 1),
                   tm=512, rows_per_mod=CTX_LEN)
    px = _proj(hx, attn_w_in[0], 0, ATTN_IN, tm=1024, tn=tn, tile_types=_tile_types(tn, kinds0),
               rope=_rope_tables(), q_scale=HEAD_DIM ** -0.5 * LOG2E)
    ckv = _proj(hc, attn_w_in[0], ATTN_WIDTH // tn, 2 * KV_WIDTH, tm=1024, tn=tn)
    og = _attention(px, ckv, attn_sink[0], batch)
    x2, hx = _out_proj(og, attn_w_out[0].astype(BF16), x2, part(0, 0, batch, 2),
                       next_norm=(norm_g[1], part(1, 0, batch, 0), part(1, 0, batch, 1)))

    kinds1 = [None] * (3 * w // LANES) + ["g"] * (w // LANES)
    p = _proj(hx, hy_w_in[0], 0, 4 * w, tm=1024, tn=tn, tile_types=_tile_types(tn, kinds1))
    conv_w = hy_conv_w[0]
    conv_b = hy_conv_b[0].reshape(1, 3 * w)
    fwd, inv, anti, phase = _fold_tables()
    sd = _filters_sd(hy_w1[0], hy_b1[0], hy_w2[0], hy_b2[0], hy_w3[0], hy_b3[0], hy_freq[0])
    kspec = _filter_spectra2(sd, fwd, anti, phase)
    y1, v = _conv_fwd2(p, 2, kspec, 0, fwd, anti, batch, conv=(conv_w, conv_b, 2))
    z = _conv_inv2(y1, inv, anti, v, 0, hy_bias_d[0, 0], p, 0, conv_w, conv_b)
    y2, = _conv_fwd2(z, 0, kspec, 1, fwd, anti, batch)
    yg = _conv_inv2(y2, inv, anti, z, 0, hy_bias_d[0, 1], p, 1, conv_w, conv_b, g_blk0=3)
    out, = _out_proj(yg, hy_w_out[0].astype(BF16), x2, part(1, 0, batch, 2), final_g=final_g)
    return out.reshape(batch, seq, d)
```

```python
import functools
import math

import jax
import jax.numpy as jnp
import numpy as np
from jax import lax
from jax.experimental import pallas as pl
from jax.experimental.pallas import tpu as pltpu

F32 = jnp.float32
BF16 = jnp.bfloat16

D_MODEL = 2048
SEQ = 2048
CTX_LEN = 256
GRID_W = 64
HEAD_DIM = 128
N_HEADS = 16
N_KV_HEADS = 4
GQA_GROUP = 4
ATTN_WIDTH = 2048
KV_WIDTH = 512
ATTN_IN = 2 * ATTN_WIDTH + 2 * KV_WIDTH
WINDOW = 128
BLOCK = 128
ROPE_BASE = 10000.0
HYENA_WIDTH = 2048
FILTER_EMB = 33
FILTER_HIDDEN = 64
DECAY_FAST = 0.3
DECAY_SLOW = 1.5
DECAY_TARGET = 1e-2
WINDOW_SHIFT = 0.05
NORM_EPS = 1e-6
NEG_INF = -1e30

LANES = 128
MXU_COLS = 256
ONES_ROWS = 16
LOG2E = math.log2(math.e)
FFT_N = 2 * SEQ
VMEM_LIMIT = 56 * 1024 * 1024


def _cparams(sem):
    return pltpu.CompilerParams(dimension_semantics=sem, vmem_limit_bytes=VMEM_LIMIT)


def _mods_kernel(c_ref, w_ref, b_ref, o_ref):
    c = c_ref[...]
    s = c * jax.nn.sigmoid(c)
    s_hi = s.astype(BF16)
    s_lo = (s - s_hi.astype(F32)).astype(BF16)
    lhs = jnp.concatenate([s_hi, s_lo], axis=0)
    r = jnp.dot(lhs, w_ref[0].astype(BF16), preferred_element_type=F32)
    o_ref[0] = r[:8] + r[8:] + b_ref[0]


def _mods(cc, ada_w, ada_b):
    depth, d, n = ada_w.shape
    tn = 1024
    return pl.pallas_call(
        _mods_kernel,
        grid=(depth, n // tn),
        in_specs=[
            pl.BlockSpec((8, d), lambda l, j: (0, 0)),
            pl.BlockSpec((1, d, tn), lambda l, j: (l, 0, j)),
            pl.BlockSpec((1, 1, tn), lambda l, j: (l, 0, j)),
        ],
        out_specs=pl.BlockSpec((1, 8, tn), lambda l, j: (l, 0, j)),
        out_shape=jax.ShapeDtypeStruct((depth, 8, n), F32),
        compiler_params=_cparams(("arbitrary", "arbitrary")),
        name="adaln_mods",
    )(cc, ada_w, ada_b.reshape(depth, 1, n))


def _rope_slab(t, cos, sin):
    lane = lax.broadcasted_iota(jnp.int32, t.shape, 1)
    first = (lane % 64) < 32
    partner = jnp.where(first, pltpu.roll(t, 96, 1), pltpu.roll(t, 32, 1))
    return t * cos + partner * sin


def _norm_mod_rows(x, g, mul, add):
    ms = jnp.mean(x * x, axis=-1, keepdims=True)
    return x * lax.rsqrt(ms + NORM_EPS) * g * mul + add


def _norm_mod_kernel(x_ref, g_ref, sh_ref, sc_ref, o_ref, *, row_chunk):
    g = g_ref[...]
    mul = 1.0 + sc_ref[0]
    add = sh_ref[0]
    for r in range(0, x_ref.shape[0], row_chunk):
        o_ref[r:r + row_chunk, :] = _norm_mod_rows(x_ref[r:r + row_chunk, :], g, mul, add).astype(o_ref.dtype)


def _norm_mod(x, g, shift, scale, *, tm, rows_per_mod):
    m, d = x.shape
    if shift.shape[0] > 1:
        mod_map = lambda i: ((i * tm) // rows_per_mod, 0, 0)
    else:
        mod_map = lambda i: (0, 0, 0)
    return pl.pallas_call(
        functools.partial(_norm_mod_kernel, row_chunk=min(tm, 256)),
        grid=(m // tm,),
        in_specs=[pl.BlockSpec((tm, d), lambda i: (i, 0)), pl.BlockSpec((1, d), lambda i: (0, 0)),
                  pl.BlockSpec((1, 1, d), mod_map), pl.BlockSpec((1, 1, d), mod_map)],
        out_specs=pl.BlockSpec((tm, d), lambda i: (i, 0)),
        out_shape=jax.ShapeDtypeStruct((m, d), BF16),
        compiler_params=_cparams(("arbitrary",)),
        name="norm_mod",
    )(x, g.reshape(1, d), shift, scale)


def _proj_kernel(*refs, tile_types, has_rope, q_scale):
    if has_rope:
        hx_ref, w_ref, cos_ref, sin_ref, o_ref, wb_ref = refs
    else:
        hx_ref, w_ref, o_ref, wb_ref = refs
    j = pl.program_id(0)
    tn = o_ref.shape[1]

    @pl.when(pl.program_id(1) == 0)
    def _():
        wb_ref[...] = w_ref[...].astype(BF16)

    w_ref = wb_ref

    def plain():
        o_ref[...] = jnp.dot(hx_ref[...], w_ref[...], preferred_element_type=F32).astype(o_ref.dtype)

    def gated():
        for c0 in range(0, tn, MXU_COLS):
            acc = jnp.dot(hx_ref[...], w_ref[:, c0:c0 + MXU_COLS], preferred_element_type=F32)
            o_ref[:, c0:c0 + MXU_COLS] = (acc * jax.nn.sigmoid(acc)).astype(o_ref.dtype)

    def roped(types):
        cos = cos_ref[...]
        sin = sin_ref[...]
        for c0 in range(0, tn, MXU_COLS):
            acc = jnp.dot(hx_ref[...], w_ref[:, c0:c0 + MXU_COLS], preferred_element_type=F32)
            for h in range(MXU_COLS // LANES):
                ty = types[c0 // LANES + h]
                slab = acc[:, h * LANES:(h + 1) * LANES]
                if ty is not None:
                    slab = _rope_slab(slab, cos, sin)
                    if ty == "q":
                        slab = slab * q_scale
                lo = c0 + h * LANES
                o_ref[:, lo:lo + LANES] = slab.astype(o_ref.dtype)

    if tile_types is None:
        plain()
        return

    groups = {}
    for t, types in enumerate(tile_types):
        groups.setdefault(tuple(types), []).append(t)
    for types, tiles in groups.items():
        cond = functools.reduce(jnp.logical_or, [j == t for t in tiles])
        if all(ty is None for ty in types):
            pl.when(cond)(plain)
        elif all(ty == "g" for ty in types):
            pl.when(cond)(gated)
        else:
            assert "g" not in types
            pl.when(cond)(functools.partial(roped, types))


def _proj(hx, w, col_blk0, n, *, tm, tn, tile_types=None, rope=None, q_scale=1.0):
    m, d = hx.shape
    in_specs = [
        pl.BlockSpec((tm, d), lambda j, i: (i, 0)),
        pl.BlockSpec((d, tn), lambda j, i: (0, col_blk0 + j)),
    ]
    args = [hx, w]
    if rope is not None:
        cos, sin = rope
        seq_tiles = cos.shape[0] // tm
        in_specs += [pl.BlockSpec((tm, LANES), lambda j, i: (i % seq_tiles, 0))] * 2
        args += [cos, sin]
    return pl.pallas_call(
        functools.partial(_proj_kernel, tile_types=tile_types, has_rope=rope is not None,
                          q_scale=q_scale),
        grid=(n // tn, m // tm),
        in_specs=in_specs,
        out_specs=pl.BlockSpec((tm, tn), lambda j, i: (i, j)),
        out_shape=jax.ShapeDtypeStruct((m, n), BF16),
        scratch_shapes=[pltpu.VMEM((d, tn), BF16)],
        compiler_params=_cparams(("arbitrary", "arbitrary")),
        name="proj",
    )(*args)


def _attn_kernel(sink_ref, q_ref, k_ref, v_ref, g_ref, kc_ref, vc_ref, o_ref,
                 vt_ref, vct_ref, bias_ref):
    kh = pl.program_id(1)
    band = 3 * BLOCK
    cols = GQA_GROUP * BLOCK
    n_blocks = SEQ // BLOCK

    vt_ref[:HEAD_DIM, :] = v_ref[...].T
    vt_ref[HEAD_DIM:, :] = jnp.ones((ONES_ROWS, SEQ), BF16)
    vct_ref[:HEAD_DIM, :] = vc_ref[...].T
    vct_ref[HEAD_DIM:, :] = jnp.ones((ONES_ROWS, CTX_LEN), BF16)
    krow = lax.broadcasted_iota(jnp.int32, (band, cols), 0)
    qcol = lax.broadcasted_iota(jnp.int32, (band, cols), 1) % BLOCK
    for idx, off in enumerate((0, -BLOCK, -2 * BLOCK)):
        bias_ref[idx] = jnp.where(jnp.abs(krow - qcol + off) <= WINDOW, 0.0, NEG_INF)

    kc = kc_ref[...]
    vct = vct_ref[...]
    sink_row = jnp.concatenate(
        [jnp.full((1, BLOCK), sink_ref[kh * GQA_GROUP + h] * LOG2E, F32) for h in range(GQA_GROUP)],
        axis=1)

    def body(n, carry):
        q0 = pl.multiple_of(n * BLOCK, BLOCK)
        ks = pl.multiple_of(jnp.clip((n - 1) * BLOCK, 0, SEQ - band), BLOCK)
        bidx = jnp.where(n == 0, 0, jnp.where(n == n_blocks - 1, 2, 1))
        qs = q_ref[pl.ds(q0, BLOCK), :]
        q4t = jnp.concatenate([qs[:, h * LANES:(h + 1) * LANES].T for h in range(GQA_GROUP)],
                              axis=1)
        kb = k_ref[pl.ds(ks, band), :]
        s_loc = jnp.dot(kb, q4t, preferred_element_type=F32) + bias_ref[bidx]
        s_ctx = jnp.dot(kc, q4t, preferred_element_type=F32)
        m = jnp.maximum(jnp.maximum(jnp.max(s_loc, axis=0, keepdims=True),
                                    jnp.max(s_ctx, axis=0, keepdims=True)), sink_row)
        p_loc = jnp.exp2(s_loc - m).astype(BF16)
        p_ctx = jnp.exp2(s_ctx - m).astype(BF16)
        ox = (jnp.dot(vt_ref[:, pl.ds(ks, band)], p_loc, preferred_element_type=F32)
              + jnp.dot(vct, p_ctx, preferred_element_type=F32))
        den = ox[HEAD_DIM:HEAD_DIM + 1, :] + jnp.exp2(sink_row - m)
        ot = ox[:HEAD_DIM, :] * (1.0 / den)
        gs = g_ref[pl.ds(q0, BLOCK), :].astype(F32)
        for h in range(GQA_GROUP):
            oh = ot[:, h * LANES:(h + 1) * LANES].T * gs[:, h * LANES:(h + 1) * LANES]
            o_ref[pl.ds(q0, BLOCK), h * LANES:(h + 1) * LANES] = oh.astype(o_ref.dtype)
        return carry

    lax.fori_loop(0, n_blocks, body, 0, unroll=4)


def _attention(px, ckv, sink, batch):
    gw = GQA_GROUP * HEAD_DIM
    k_blk0 = ATTN_WIDTH // HEAD_DIM
    v_blk0 = (ATTN_WIDTH + KV_WIDTH) // HEAD_DIM
    g_blk0 = (ATTN_WIDTH + 2 * KV_WIDTH) // gw
    return pl.pallas_call(
        _attn_kernel,
        grid=(batch, N_KV_HEADS),
        in_specs=[
            pl.BlockSpec(memory_space=pltpu.SMEM),
            pl.BlockSpec((SEQ, gw), lambda b, h: (b, h)),
            pl.BlockSpec((SEQ, HEAD_DIM), lambda b, h: (b, k_blk0 + h)),
            pl.BlockSpec((SEQ, HEAD_DIM), lambda b, h: (b, v_blk0 + h)),
            pl.BlockSpec((SEQ, gw), lambda b, h: (b, g_blk0 + h)),
            pl.BlockSpec((CTX_LEN, HEAD_DIM), lambda b, h: (b, h)),
            pl.BlockSpec((CTX_LEN, HEAD_DIM), lambda b, h: (b, N_KV_HEADS + h)),
        ],
        out_specs=pl.BlockSpec((SEQ, gw), lambda b, h: (b, h)),
        out_shape=jax.ShapeDtypeStruct((batch * SEQ, ATTN_WIDTH), BF16),
        scratch_shapes=[pltpu.VMEM((HEAD_DIM + ONES_ROWS, SEQ), BF16),
                        pltpu.VMEM((HEAD_DIM + ONES_ROWS, CTX_LEN), BF16),
                        pltpu.VMEM((3, 3 * BLOCK, GQA_GROUP * BLOCK), F32)],
        compiler_params=_cparams(("arbitrary", "arbitrary")),
        name="banded_attention",
    )(sink, px, px, px, px, ckv, ckv)


def _out_proj_kernel(*refs, final):
    if final:
        a_ref, w_ref, x_ref, gate_ref, fg_ref, o_ref = refs
    else:
        a_ref, w_ref, x_ref, gate_ref, ng_ref, sh_ref, sc_ref, o_ref, hx_ref = refs
    acc = jnp.dot(a_ref[...], w_ref[...], preferred_element_type=F32)
    y = x_ref[...] + gate_ref[0] * acc
    if final:
        ms = jnp.mean(y * y, axis=-1, keepdims=True)
        y = y * lax.rsqrt(ms + NORM_EPS) * fg_ref[...]
    else:
        hx_ref[...] = _norm_mod_rows(y, ng_ref[...], 1.0 + sc_ref[0], sh_ref[0]).astype(hx_ref.dtype)
    o_ref[...] = y


def _out_proj(a, w, x, gate, *, final_g=None, next_norm=None, tm=512):
    m, d = x.shape
    kdim = a.shape[1]
    final = final_g is not None
    per_batch = lambda i: ((i * tm) // SEQ, 0, 0)
    row_tile = pl.BlockSpec((tm, d), lambda i: (i, 0))
    vec = pl.BlockSpec((1, d), lambda i: (0, 0))
    in_specs = [
        pl.BlockSpec((tm, kdim), lambda i: (i, 0)),
        pl.BlockSpec((kdim, d), lambda i: (0, 0)),
        row_tile,
        pl.BlockSpec((1, 1, d), per_batch),
    ]
    args = [a, w, x, gate]
    out_specs = [row_tile]
    out_shape = [jax.ShapeDtypeStruct((m, d), F32)]
    if final:
        in_specs.append(vec)
        args.append(final_g.reshape(1, d))
    else:
        ng, shift, scale = next_norm
        in_specs += [vec, pl.BlockSpec((1, 1, d), per_batch), pl.BlockSpec((1, 1, d), per_batch)]
        args += [ng.reshape(1, d), shift, scale]
        out_specs.append(row_tile)
        out_shape.append(jax.ShapeDtypeStruct((m, d), BF16))
    return pl.pallas_call(
        functools.partial(_out_proj_kernel, final=final),
        grid=(m // tm,),
        in_specs=in_specs,
        out_specs=out_specs,
        out_shape=out_shape,
        compiler_params=_cparams(("arbitrary",)),
        name="out_proj",
    )(*args)


def _short_conv_kernel(p_ref, w_ref, b_ref, o_ref):
    length = p_ref.shape[0]
    row = lax.broadcasted_iota(jnp.int32, (length, LANES), 0)
    for c in range(0, p_ref.shape[1], LANES):
        u = p_ref[:, c:c + LANES].astype(F32)
        w = w_ref[:, c:c + LANES]
        prev = jnp.where(row == 0, 0.0, pltpu.roll(u, 1, 0))
        nxt = jnp.where(row == length - 1, 0.0, pltpu.roll(u, length - 1, 0))
        y = prev * w[0:1] + u * w[1:2] + nxt * w[2:3] + b_ref[:, c:c + LANES]
        o_ref[:, c:c + LANES] = y.astype(o_ref.dtype)


def _short_conv(p, conv_w, conv_b, batch, *, tn=512):
    n = conv_w.shape[1]
    return pl.pallas_call(
        _short_conv_kernel,
        grid=(batch, n // tn),
        in_specs=[
            pl.BlockSpec((SEQ, tn), lambda b, j: (b, j)),
            pl.BlockSpec((3, tn), lambda b, j: (0, j)),
            pl.BlockSpec((1, tn), lambda b, j: (0, j)),
        ],
        out_specs=pl.BlockSpec((SEQ, tn), lambda b, j: (b, j)),
        out_shape=jax.ShapeDtypeStruct((batch * SEQ, n), BF16),
        compiler_params=_cparams(("arbitrary", "arbitrary")),
        name="short_conv",
    )(p, conv_w, conv_b.reshape(1, n))


def _filter_kernel(w1t_ref, w1c_ref, w1s_ref, b1_ref, w2_ref, b2_ref, fq_ref, w3_ref, b3_ref,
                   o_ref, hid_ref):
    j = pl.program_id(0)
    length = o_ref.shape[0]
    tn = o_ref.shape[1]
    hp = lax.Precision.HIGHEST

    @pl.when(j == 0)
    def _():
        n = lax.broadcasted_iota(jnp.int32, (length, LANES), 0).astype(F32)
        lane = lax.broadcasted_iota(jnp.int32, (length, LANES), 1)
        bands = (FILTER_EMB - 1) // 2
        fr_step = (bands - 1 - 1e-4) / (bands - 1)
        fr = jnp.where(lane < bands, 1e-4 + lane.astype(F32) * fr_step, 0.0)
        ang = (2.0 * math.pi * n / length) * fr
        t = n / (length - 1)
        fq = fq_ref[...]
        pre = (t * w1t_ref[...]
               + jnp.dot(jnp.cos(ang), w1c_ref[...], precision=hp, preferred_element_type=F32)
               + jnp.dot(-jnp.sin(ang), w1s_ref[...], precision=hp, preferred_element_type=F32)
               + b1_ref[...])
        hid = jnp.sin(fq * pre)
        hid = jnp.sin(fq * (jnp.dot(hid, w2_ref[...], precision=hp, preferred_element_type=F32)
                            + b2_ref[...]))
        hid_ref[...] = hid

    hf = jnp.dot(hid_ref[...], w3_ref[...], precision=hp, preferred_element_type=F32) + b3_ref[...]
    row = lax.broadcasted_iota(jnp.int32, (length, tn), 0)
    col = lax.broadcasted_iota(jnp.int32, (length, tn), 1) + j * tn
    chan = (col % HYENA_WIDTH).astype(F32)
    min_decay = math.log(DECAY_TARGET) / DECAY_SLOW
    max_decay = math.log(DECAY_TARGET) / DECAY_FAST
    delta = min_decay + chan * ((max_decay - min_decay) / (HYENA_WIDTH - 1))
    t = row.astype(F32) / (length - 1)
    filt = hf * (jnp.exp(-t * jnp.abs(delta)) + WINDOW_SHIFT)
    backward = (col // HYENA_WIDTH) % 2 == 1
    filt = jnp.where(backward & (row == 0), 0.0, filt)
    o_ref[...] = filt.astype(o_ref.dtype)


def _filters(w1, b1, w2, b2, w3, b3, freq, *, tn=1024):
    hpad = LANES - FILTER_HIDDEN
    bands = (FILTER_EMB - 1) // 2
    n = w3.shape[1]
    w1t = jnp.pad(w1[0:1], ((0, 0), (0, hpad)))
    w1c = jnp.pad(w1[1:1 + bands], ((0, LANES - bands), (0, hpad)))
    w1s = jnp.pad(w1[1 + bands:], ((0, LANES - bands), (0, hpad)))
    b1p = jnp.pad(b1.reshape(1, -1), ((0, 0), (0, hpad)))
    w2p = jnp.pad(w2, ((0, hpad), (0, hpad)))
    b2p = jnp.pad(b2.reshape(1, -1), ((0, 0), (0, hpad)))
    fqp = jnp.pad(freq.reshape(1, -1), ((0, 0), (0, hpad)))
    w3p = jnp.pad(w3, ((0, hpad), (0, 0)))
    small = lambda shape: pl.BlockSpec(shape, lambda j: (0, 0))
    return pl.pallas_call(
        _filter_kernel,
        grid=(n // tn,),
        in_specs=[small((1, LANES)), small((LANES, LANES)), small((LANES, LANES)), small((1, LANES)),
                  small((LANES, LANES)), small((1, LANES)), small((1, LANES)),
                  pl.BlockSpec((LANES, tn), lambda j: (0, j)),
                  pl.BlockSpec((1, tn), lambda j: (0, j))],
        out_specs=pl.BlockSpec((SEQ, tn), lambda j: (0, j)),
        out_shape=jax.ShapeDtypeStruct((SEQ, n), BF16),
        scratch_shapes=[pltpu.VMEM((SEQ, LANES), F32)],
        compiler_params=_cparams(("arbitrary",)),
        name="hyena_filters",
    )(w1t, w1c, w1s, b1p, w2p, b2p, fqp, w3p, b3.reshape(1, n))


def _dft_tables():
    idx = np.arange(SEQ, dtype=np.int64)
    ang = ((idx[:, None] * idx[None, :]) % FFT_N).astype(np.float64) * (2.0 * math.pi / FFT_N)
    cm = np.cos(ang)
    sm = np.sin(ang)
    sm[0, :] = np.where(idx % 2 == 0, 1.0, -1.0)
    as_bf16 = lambda a: jnp.asarray(a.astype(np.float32)).astype(BF16)
    return as_bf16(cm), as_bf16(sm), as_bf16(np.ascontiguousarray(sm.T))


def _spec_scale(i, tf, shape):
    row = lax.broadcasted_iota(jnp.int32, shape, 0) + i * tf
    is0 = row == 0
    return is0, jnp.where(is0, 1.0 / FFT_N, 2.0 / FFT_N)


def _filter_spec_kernel(hf_ref, hb_ref, cm_ref, sm_ref, ka_ref, kb_ref):
    i = pl.program_id(2)
    tf = cm_ref.shape[0]
    cm = cm_ref[...]
    sm = sm_ref[...]
    hf = hf_ref[...]
    hb = hb_ref[...]
    a_f = jnp.dot(cm, hf, preferred_element_type=F32)
    b_f = jnp.dot(sm, hf, preferred_element_type=F32)
    a_b = jnp.dot(cm, hb, preferred_element_type=F32)
    b_b = jnp.dot(sm, hb, preferred_element_type=F32)
    is0, scale = _spec_scale(i, tf, a_f.shape)
    ka_ref[0] = ((a_f + a_b) * scale).astype(ka_ref.dtype)
    kb_ref[0] = (jnp.where(is0, b_f + b_b, b_f - b_b) * scale).astype(kb_ref.dtype)


def _filter_spectra(filt, cm, sm, *, tf=512, tn=512):
    w = HYENA_WIDTH
    ct = w // tn
    out = jax.ShapeDtypeStruct((2, SEQ, w), BF16)
    return pl.pallas_call(
        _filter_spec_kernel,
        grid=(2, ct, SEQ // tf),
        in_specs=[
            pl.BlockSpec((SEQ, tn), lambda o, j, i: (0, 2 * o * ct + j)),
            pl.BlockSpec((SEQ, tn), lambda o, j, i: (0, (2 * o + 1) * ct + j)),
            pl.BlockSpec((tf, SEQ), lambda o, j, i: (i, 0)),
            pl.BlockSpec((tf, SEQ), lambda o, j, i: (i, 0)),
        ],
        out_specs=[pl.BlockSpec((1, tf, tn), lambda o, j, i: (o, i, j))] * 2,
        out_shape=[out, out],
        compiler_params=_cparams(("arbitrary", "arbitrary", "arbitrary")),
        name="filter_spectra",
    )(filt, filt, cm, sm)


def _conv_fwd_kernel(u_ref, cm_ref, sm_ref, ka_ref, kb_ref, ya_ref, yb_ref):
    i = pl.program_id(2)
    tf = cm_ref.shape[0]
    u = u_ref[...]
    a = jnp.dot(cm_ref[...], u, preferred_element_type=F32)
    b = jnp.dot(sm_ref[...], u, preferred_element_type=F32)
    ka = ka_ref[0].astype(F32)
    kb = kb_ref[0].astype(F32)
    is0, _ = _spec_scale(i, tf, a.shape)
    bkb = b * kb
    ya_ref[0] = (a * ka - jnp.where(is0, 0.0, bkb)).astype(ya_ref.dtype)
    yb_ref[0] = jnp.where(is0, bkb, a * kb + b * ka).astype(yb_ref.dtype)


def _conv_fwd(u, col_blk0, ka, kb, order, cm, sm, batch, *, tf=512, tn=1024):
    w = HYENA_WIDTH
    out = jax.ShapeDtypeStruct((batch, SEQ, w), BF16)
    c0 = col_blk0 * (w // tn)
    return pl.pallas_call(
        _conv_fwd_kernel,
        grid=(batch, w // tn, SEQ // tf),
        in_specs=[
            pl.BlockSpec((SEQ, tn), lambda b, j, i: (b, c0 + j)),
            pl.BlockSpec((tf, SEQ), lambda b, j, i: (i, 0)),
            pl.BlockSpec((tf, SEQ), lambda b, j, i: (i, 0)),
            pl.BlockSpec((1, tf, tn), lambda b, j, i: (order, i, j)),
            pl.BlockSpec((1, tf, tn), lambda b, j, i: (order, i, j)),
        ],
        out_specs=[pl.BlockSpec((1, tf, tn), lambda b, j, i: (b, i, j))] * 2,
        out_shape=[out, out],
        compiler_params=_cparams(("arbitrary", "arbitrary", "arbitrary")),
        name="long_conv_fwd",
    )(u, cm, sm, ka, kb)


def _conv_inv_kernel(*refs, gated):
    if gated:
        ya_ref, yb_ref, cm_ref, st_ref, u_ref, d_ref, x_ref, g_ref, o_ref = refs
    else:
        ya_ref, yb_ref, cm_ref, st_ref, u_ref, d_ref, x_ref, o_ref = refs
    y = (jnp.dot(cm_ref[...], ya_ref[0], preferred_element_type=F32)
         + jnp.dot(st_ref[...], yb_ref[0], preferred_element_type=F32))
    y = y + u_ref[...].astype(F32) * d_ref[...]
    y = y * x_ref[...].astype(F32)
    if gated:
        y = y * g_ref[...].astype(F32)
    o_ref[...] = y.astype(o_ref.dtype)


def _conv_inv(ya, yb, cm, st, u, u_blk0, d, x, x_blk0, g=None, g_blk0=0, *, tt=512, tn=1024):
    batch = ya.shape[0]
    w = HYENA_WIDTH
    ct = w // tn
    tiles = SEQ // tt
    gated = g is not None
    in_specs = [
        pl.BlockSpec((1, SEQ, tn), lambda b, j, i: (b, 0, j)),
        pl.BlockSpec((1, SEQ, tn), lambda b, j, i: (b, 0, j)),
        pl.BlockSpec((tt, SEQ), lambda b, j, i: (i, 0)),
        pl.BlockSpec((tt, SEQ), lambda b, j, i: (i, 0)),
        pl.BlockSpec((tt, tn), lambda b, j, i: (b * tiles + i, u_blk0 * ct + j)),
        pl.BlockSpec((1, tn), lambda b, j, i: (0, j)),
        pl.BlockSpec((tt, tn), lambda b, j, i: (b * tiles + i, x_blk0 * ct + j)),
    ]
    args = [ya, yb, cm, st, u, d.reshape(1, w), x]
    if gated:
        in_specs.append(pl.BlockSpec((tt, tn), lambda b, j, i: (b * tiles + i, g_blk0 * ct + j)))
        args.append(g)
    return pl.pallas_call(
        functools.partial(_conv_inv_kernel, gated=gated),
        grid=(batch, ct, tiles),
        in_specs=in_specs,
        out_specs=pl.BlockSpec((tt, tn), lambda b, j, i: (b * tiles + i, j)),
        out_shape=jax.ShapeDtypeStruct((batch * SEQ, w), BF16),
        compiler_params=_cparams(("arbitrary", "arbitrary", "arbitrary")),
        name="long_conv_inv",
    )(*args)


HALF = SEQ // 2
FLIP_BLOCK = 256
ROW_TILE = 512


def _fold_tables():
    t2 = 2 * np.arange(HALF, dtype=np.int64) + 1
    g = np.arange(HALF, dtype=np.int64)

    def tab(f):
        ang = ((f[:, None] * t2[None, :]) % (2 * FFT_N)).astype(np.float64) * (math.pi / FFT_N)
        return np.cos(ang), np.sin(ang)

    ce, se = tab(2 * g)
    co, so = tab(2 * g + 1)
    se[0, :] = np.where(np.arange(HALF) % 2 == 0, 1.0, -1.0)
    const = lambda a, dt: jnp.asarray(np.ascontiguousarray(a).astype(np.float32)).astype(dt)
    fwd = const(np.stack([ce, se, co, so]), BF16)
    inv = const(np.stack([np.hstack([ce.T, so.T]), np.hstack([se.T, co.T])]), BF16)
    anti = const(np.eye(FLIP_BLOCK)[::-1], BF16)
    theta = np.stack([2 * g, 2 * g + 1]).astype(np.float64) * (math.pi / FFT_N)
    phase = const(np.stack([np.cos(theta), np.sin(theta)])[..., None], F32)
    return fwd, inv, anti, phase


def _flip_rows(h, anti):
    nb = h.shape[0] // FLIP_BLOCK
    return jnp.concatenate(
        [jnp.dot(anti[...], h[(nb - 1 - a) * FLIP_BLOCK:(nb - a) * FLIP_BLOCK, :], preferred_element_type=F32)
         for a in range(nb)], axis=0)


def _fold(x_ref, cols, anti):
    x0 = x_ref[:HALF, cols].astype(F32)
    xr = _flip_rows(x_ref[HALF:, cols], anti)
    return (x0 + xr).astype(BF16), (x0 - xr).astype(BF16)


def _staggered(n, matmuls, finish):
    matmuls(0, 0)
    for t in range(1, n):
        matmuls(t, t % 2)
        finish(t - 1, (t - 1) % 2)
    finish(n - 1, (n - 1) % 2)


def _col_chunks(n):
    return [slice(c, c + MXU_COLS) for c in range(0, n, MXU_COLS)]


def _half_spectrum(tab_ref, xs, xa):
    dot = lambda k, v: jnp.dot(tab_ref[k], v, preferred_element_type=F32)
    return dot(0, xs), dot(1, xa), dot(2, xa), dot(3, xs)


def _short_conv_into(p_ref, w_ref, b_ref, dst_ref, cols):
    length = p_ref.shape[0]
    row = lax.broadcasted_iota(jnp.int32, (length, LANES), 0)
    for c in range(cols.start, cols.stop, LANES):
        u = p_ref[:, c:c + LANES].astype(F32)
        w = w_ref[:, c:c + LANES]
        prev = jnp.where(row == 0, 0.0, pltpu.roll(u, 1, 0))
        nxt = jnp.where(row == length - 1, 0.0, pltpu.roll(u, length - 1, 0))
        y = prev * w[0:1] + u * w[1:2] + nxt * w[2:3] + b_ref[:, c:c + LANES]
        dst_ref[:, c:c + LANES] = y.astype(dst_ref.dtype)


def _resident(shape):
    zeros = (0,) * len(shape)
    return pl.BlockSpec(shape, lambda *_: zeros, pipeline_mode=pl.Buffered(1))


def _filter_sd_kernel(w1t_ref, w1c_ref, w1s_ref, b1_ref, w2_ref, b2_ref, fq_ref,
                      w3f_ref, b3f_ref, w3b_ref, b3b_ref, o_ref, hid_ref):
    first = (pl.program_id(0) == 0) & (pl.program_id(1) == 0)
    j = pl.program_id(1)
    length = o_ref.shape[1]
    tn = o_ref.shape[2]
    hp = lax.Precision.HIGHEST

    @pl.when(first)
    def _():
        n = lax.broadcasted_iota(jnp.int32, (length, LANES), 0).astype(F32)
        lane = lax.broadcasted_iota(jnp.int32, (length, LANES), 1)
        bands = (FILTER_EMB - 1) // 2
        fr_step = (bands - 1 - 1e-4) / (bands - 1)
        fr = jnp.where(lane < bands, 1e-4 + lane.astype(F32) * fr_step, 0.0)
        ang = (2.0 * math.pi * n / length) * fr
        t = n / (length - 1)
        fq = fq_ref[...]
        pre = (t * w1t_ref[...]
               + jnp.dot(jnp.cos(ang), w1c_ref[...], precision=hp, preferred_element_type=F32)
               + jnp.dot(-jnp.sin(ang), w1s_ref[...], precision=hp, preferred_element_type=F32)
               + b1_ref[...])
        hid = jnp.sin(fq * pre)
        hid = jnp.sin(fq * (jnp.dot(hid, w2_ref[...], precision=hp, preferred_element_type=F32)
                            + b2_ref[...]))
        hid_hi = hid.astype(BF16)
        hid_ref[0] = hid_hi
        hid_ref[1] = (hid - hid_hi.astype(F32)).astype(BF16)

    def dot3(w_ref):
        wf = w_ref[...]
        w_hi = wf.astype(BF16)
        w_lo = (wf - w_hi.astype(F32)).astype(BF16)
        return (jnp.dot(hid_ref[0], w_hi, preferred_element_type=F32)
                + jnp.dot(hid_ref[1], w_hi, preferred_element_type=F32)
                + jnp.dot(hid_ref[0], w_lo, preferred_element_type=F32))

    row = lax.broadcasted_iota(jnp.int32, (length, tn), 0)
    chan = (lax.broadcasted_iota(jnp.int32, (length, tn), 1) + j * tn).astype(F32)
    min_decay = math.log(DECAY_TARGET) / DECAY_SLOW
    max_decay = math.log(DECAY_TARGET) / DECAY_FAST
    delta = min_decay + chan * ((max_decay - min_decay) / (HYENA_WIDTH - 1))
    t = row.astype(F32) / (length - 1)
    window = jnp.exp(-t * jnp.abs(delta)) + WINDOW_SHIFT
    hf = (dot3(w3f_ref) + b3f_ref[...]) * window
    hb = (dot3(w3b_ref) + b3b_ref[...]) * window
    hb = jnp.where(row == 0, 0.0, hb)
    o_ref[0] = (hf + hb).astype(o_ref.dtype)
    o_ref[1] = (hf - hb).astype(o_ref.dtype)


def _filters_sd(w1, b1, w2, b2, w3, b3, freq, *, tn=512):
    hpad = LANES - FILTER_HIDDEN
    bands = (FILTER_EMB - 1) // 2
    w = HYENA_WIDTH
    ct = w // tn
    n = w3.shape[1]
    w1t = jnp.pad(w1[0:1], ((0, 0), (0, hpad)))
    w1c = jnp.pad(w1[1:1 + bands], ((0, LANES - bands), (0, hpad)))
    w1s = jnp.pad(w1[1 + bands:], ((0, LANES - bands), (0, hpad)))
    b1p = jnp.pad(b1.reshape(1, -1), ((0, 0), (0, hpad)))
    w2p = jnp.pad(w2, ((0, hpad), (0, hpad)))
    b2p = jnp.pad(b2.reshape(1, -1), ((0, 0), (0, hpad)))
    fqp = jnp.pad(freq.reshape(1, -1), ((0, 0), (0, hpad)))
    w3p = jnp.pad(w3, ((0, hpad), (0, 0)))
    b3r = b3.reshape(1, n)
    small = lambda shape: pl.BlockSpec(shape, lambda o, j: (0, 0))
    fwd_cols = lambda o, j: (0, 2 * o * ct + j)
    bwd_cols = lambda o, j: (0, (2 * o + 1) * ct + j)
    return pl.pallas_call(
        _filter_sd_kernel,
        grid=(2, ct),
        in_specs=[small((1, LANES)), small((LANES, LANES)), small((LANES, LANES)), small((1, LANES)),
                  small((LANES, LANES)), small((1, LANES)), small((1, LANES)),
                  pl.BlockSpec((LANES, tn), fwd_cols), pl.BlockSpec((1, tn), fwd_cols),
                  pl.BlockSpec((LANES, tn), bwd_cols), pl.BlockSpec((1, tn), bwd_cols)],
        out_specs=pl.BlockSpec((2, SEQ, tn), lambda o, j: (0, 0, o * ct + j)),
        out_shape=jax.ShapeDtypeStruct((2, SEQ, 2 * w), BF16),
        scratch_shapes=[pltpu.VMEM((2, SEQ, LANES), BF16)],
        compiler_params=_cparams(("arbitrary", "arbitrary")),
        name="hyena_filters",
    )(w1t, w1c, w1s, b1p, w2p, b2p, fqp, w3p, b3r, w3p, b3r)


def _filter_spec_kernel2(s_ref, d_ref, tab_ref, anti_ref, ph_ref, k_ref):
    anti = anti_ref
    is0 = lax.broadcasted_iota(jnp.int32, (HALF, MXU_COLS), 0) == 0
    scale_e = jnp.where(is0, 1.0 / FFT_N, 2.0 / FFT_N)
    cos_e, sin_e = ph_ref[0, 0], ph_ref[1, 0]
    cos_o, sin_o = ph_ref[0, 1], ph_ref[1, 1]
    for cols in _col_chunks(k_ref.shape[3]):
        sa_e, sb_e, sa_o, sb_o = _half_spectrum(tab_ref, *_fold(s_ref.at[0], cols, anti))
        da_e, db_e, da_o, db_o = _half_spectrum(tab_ref, *_fold(d_ref.at[0], cols, anti))
        k_ref[0, 0, :, cols] = ((sa_e * cos_e + sb_e * sin_e) * scale_e).astype(k_ref.dtype)
        k_ref[0, 1, :, cols] = (jnp.where(is0, sb_e, db_e * cos_e - da_e * sin_e)
                                * scale_e).astype(k_ref.dtype)
        k_ref[0, 2, :, cols] = ((sa_o * cos_o + sb_o * sin_o) * (2.0 / FFT_N)).astype(k_ref.dtype)
        k_ref[0, 3, :, cols] = ((db_o * cos_o - da_o * sin_o) * (2.0 / FFT_N)).astype(k_ref.dtype)


def _filter_spectra2(sd, fwd, anti, phase, *, tn=512):
    w = HYENA_WIDTH
    ct = w // tn
    return pl.pallas_call(
        _filter_spec_kernel2,
        grid=(2, ct),
        in_specs=[
            pl.BlockSpec((1, SEQ, tn), lambda o, j: (0, 0, o * ct + j)),
            pl.BlockSpec((1, SEQ, tn), lambda o, j: (1, 0, o * ct + j)),
            _resident(fwd.shape), _resident(anti.shape), _resident(phase.shape),
        ],
        out_specs=pl.BlockSpec((1, 4, HALF, tn), lambda o, j: (o, 0, 0, j)),
        out_shape=jax.ShapeDtypeStruct((2, 4, HALF, w), BF16),
        compiler_params=_cparams(("arbitrary", "arbitrary")),
        name="filter_spectra",
    )(sd, sd, fwd, anti, phase)


def _conv_fwd_kernel2(*refs, with_conv):
    if with_conv:
        p_ref, cw_ref, cb_ref, tab_ref, anti_ref, k_ref, y_ref, v_ref = refs
        x_ref = v_ref
    else:
        x_ref, tab_ref, anti_ref, k_ref, y_ref = refs
    anti = anti_ref
    is0 = lax.broadcasted_iota(jnp.int32, (HALF, MXU_COLS), 0) == 0
    for cols in _col_chunks(y_ref.shape[3]):
        if with_conv:
            _short_conv_into(p_ref, cw_ref, cb_ref, v_ref, cols)
        a_e, b_e, a_o, b_o = _half_spectrum(tab_ref, *_fold(x_ref, cols, anti))
        ka_e, kb_e, ka_o, kb_o = [k_ref[0, plane, :, cols].astype(F32) for plane in range(4)]
        bkb = b_e * kb_e
        y_ref[0, 0, :HALF, cols] = (a_e * ka_e - jnp.where(is0, 0.0, bkb)).astype(y_ref.dtype)
        y_ref[0, 1, :HALF, cols] = jnp.where(is0, bkb, a_e * kb_e + b_e * ka_e).astype(y_ref.dtype)
        y_ref[0, 1, HALF:, cols] = (a_o * ka_o - b_o * kb_o).astype(y_ref.dtype)
        y_ref[0, 0, HALF:, cols] = (a_o * kb_o + b_o * ka_o).astype(y_ref.dtype)


def _conv_fwd2(x, x_blk0, kspec, order, fwd, anti, batch, conv=None, *, tn=512):
    w = HYENA_WIDTH
    ct = w // tn
    in_specs = [pl.BlockSpec((SEQ, tn), lambda b, j: (b, x_blk0 * ct + j))]
    args = [x]
    out_specs = [pl.BlockSpec((1, 2, SEQ, tn), lambda b, j: (b, 0, 0, j))]
    out_shape = [jax.ShapeDtypeStruct((batch, 2, SEQ, w), BF16)]
    if conv is not None:
        cw, cb, c_blk0 = conv
        in_specs += [pl.BlockSpec((3, tn), lambda b, j: (0, c_blk0 * ct + j)),
                     pl.BlockSpec((1, tn), lambda b, j: (0, c_blk0 * ct + j))]
        args += [cw, cb]
        out_specs.append(pl.BlockSpec((SEQ, tn), lambda b, j: (b, j)))
        out_shape.append(jax.ShapeDtypeStruct((batch * SEQ, w), BF16))
    in_specs += [_resident(fwd.shape), _resident(anti.shape),
                 pl.BlockSpec((1, 4, HALF, tn), lambda b, j: (order, 0, 0, j))]
    args += [fwd, anti, kspec]
    return pl.pallas_call(
        functools.partial(_conv_fwd_kernel2, with_conv=conv is not None),
        grid=(batch, ct),
        in_specs=in_specs,
        out_specs=out_specs,
        out_shape=out_shape,
        compiler_params=_cparams(("arbitrary", "arbitrary")),
        name="long_conv_fwd",
    )(*args)


def _conv_inv_kernel2(*refs, gated):
    if gated:
        (y_ref, tab_ref, anti_ref, u_ref, d_ref, p_ref, cw_ref, cb_ref, g_ref, o_ref,
         xm_ref, acc_ref) = refs
    else:
        y_ref, tab_ref, anti_ref, u_ref, d_ref, p_ref, cw_ref, cb_ref, o_ref, xm_ref, acc_ref = refs
    anti = anti_ref
    tiles = [(cols, r0) for cols in _col_chunks(o_ref.shape[1]) for r0 in range(0, HALF, ROW_TILE)]

    def matmuls(t, slot):
        cols, r0 = tiles[t]
        for k in range(2):
            acc_ref[slot, k] = jnp.dot(tab_ref[k, r0:r0 + ROW_TILE, :], y_ref[0, k, :, cols],
                                       preferred_element_type=F32)

    def finish(t, slot):
        cols, r0 = tiles[t]
        if r0 == 0:
            _short_conv_into(p_ref, cw_ref, cb_ref, xm_ref, cols)
        p1 = acc_ref[slot, 0]
        p2 = acc_ref[slot, 1]
        d = d_ref[:, cols]
        direct = slice(r0, r0 + ROW_TILE)
        mirrored = slice(SEQ - r0 - ROW_TILE, SEQ - r0)
        for rows, y in ((direct, p1 + p2), (mirrored, _flip_rows((p1 - p2).astype(BF16), anti))):
            out = (y + u_ref[rows, cols].astype(F32) * d) * xm_ref[rows, cols]
            if gated:
                out = out * g_ref[rows, cols].astype(F32)
            o_ref[rows, cols] = out.astype(o_ref.dtype)

    _staggered(len(tiles), matmuls, finish)


def _conv_inv2(y, inv, anti, u, u_blk0, d, p, p_blk0, conv_w, conv_b, g_blk0=None, *, tn=512):
    batch = y.shape[0]
    w = HYENA_WIDTH
    ct = w // tn
    gated = g_blk0 is not None
    col = lambda blk0: (lambda b, j: (b, blk0 * ct + j))
    in_specs = [
        pl.BlockSpec((1, 2, SEQ, tn), lambda b, j: (b, 0, 0, j)),
        _resident(inv.shape), _resident(anti.shape),
        pl.BlockSpec((SEQ, tn), col(u_blk0)),
        pl.BlockSpec((1, tn), lambda b, j: (0, j)),
        pl.BlockSpec((SEQ, tn), col(p_blk0)),
        pl.BlockSpec((3, tn), lambda b, j: (0, p_blk0 * ct + j)),
        pl.BlockSpec((1, tn), lambda b, j: (0, p_blk0 * ct + j)),
    ]
    args = [y, inv, anti, u, d.reshape(1, w), p, conv_w, conv_b]
    if gated:
        in_specs.append(pl.BlockSpec((SEQ, tn), col(g_blk0)))
        args.append(p)
    return pl.pallas_call(
        functools.partial(_conv_inv_kernel2, gated=gated),
        grid=(batch, ct),
        in_specs=in_specs,
        out_specs=pl.BlockSpec((SEQ, tn), lambda b, j: (b, j)),
        out_shape=jax.ShapeDtypeStruct((batch * SEQ, w), BF16),
        scratch_shapes=[pltpu.VMEM((SEQ, tn), F32),
                        pltpu.VMEM((2, 2, ROW_TILE, MXU_COLS), F32)],
        compiler_params=_cparams(("arbitrary", "arbitrary")),
        name="long_conv_inv",
    )(*args)


def _rope_tables():
    pos = np.arange(SEQ)
    row = (pos // GRID_W).astype(np.float32)
    col = (pos % GRID_W).astype(np.float32)
    half = HEAD_DIM // 2
    inv = (ROPE_BASE ** (-np.arange(0, half, 2, dtype=np.float32) / half)).astype(np.float32)
    ar = row[:, None] * inv[None]
    ac = col[:, None] * inv[None]
    cos = np.concatenate([np.cos(ar), np.cos(ar), np.cos(ac), np.cos(ac)], axis=1)
    sin = np.concatenate([-np.sin(ar), np.sin(ar), -np.sin(ac), np.sin(ac)], axis=1)
    return jnp.asarray(cos, F32), jnp.asarray(sin, F32)


def _tile_types(tn, kinds):
    per_tile = tn // LANES
    return [kinds[s0:s0 + per_tile] for s0 in range(0, len(kinds), per_tile)]


def kernel(x, c, ctx, c_ctx, norm_g, ada_w, ada_b, attn_w_in, attn_w_out, attn_sink, hy_w_in,
           hy_conv_w, hy_conv_b, hy_w1, hy_b1, hy_w2, hy_b2, hy_w3, hy_b3, hy_freq, hy_bias_d,
           hy_w_out, final_g):
    batch, seq, d = x.shape
    assert (seq, d) == (SEQ, D_MODEL) and ctx.shape[1] == CTX_LEN
    assert norm_g.shape[0] == 2 and attn_w_in.shape[0] == 1 and hy_w_in.shape[0] == 1
    w = HYENA_WIDTH

    cc = jnp.concatenate([c, c_ctx[None], jnp.zeros((8 - batch - 1, d), F32)], axis=0)
    mods = _mods(cc, ada_w, ada_b)
    part = lambda layer, r0, r1, k: mods[layer, r0:r1, None, k * d:(k + 1) * d]

    x2 = x.reshape(batch * seq, d)
    ctx2 = ctx.reshape(batch * CTX_LEN, d)

    tn = 1024
    kinds0 = (["q"] * N_HEADS + ["k"] * N_KV_HEADS + [None] * N_KV_HEADS
              + ["g"] * (ATTN_WIDTH // LANES))
    hx = _norm_mod(x2, norm_g[0], part(0, 0, batch, 0), part(0, 0, batch, 1), tm=1024,
                   rows_per_mod=seq)
    hc = _norm_mod(ctx2, norm_g[0], part(0, batch, batch + 1, 0), part(0, batch, batch + 1, 1),
                   tm=512, rows_per_mod=CTX_LEN)
    px = _proj(hx, attn_w_in[0], 0, ATTN_IN, tm=1024, tn=tn, tile_types=_tile_types(tn, kinds0),
               rope=_rope_tables(), q_scale=HEAD_DIM ** -0.5 * LOG2E)
    ckv = _proj(hc, attn_w_in[0], ATTN_WIDTH // tn, 2 * KV_WIDTH, tm=1024, tn=tn)
    og = _attention(px, ckv, attn_sink[0], batch)
    x2, hx = _out_proj(og, attn_w_out[0].astype(BF16), x2, part(0, 0, batch, 2),
                       next_norm=(norm_g[1], part(1, 0, batch, 0), part(1, 0, batch, 1)))

    kinds1 = [None] * (3 * w // LANES) + ["g"] * (w // LANES)
    p = _proj(hx, hy_w_in[0], 0, 4 * w, tm=1024, tn=tn, tile_types=_tile_types(tn, kinds1))
    conv_w = hy_conv_w[0]
    conv_b = hy_conv_b[0].reshape(1, 3 * w)
    fwd, inv, anti, phase = _fold_tables()
    sd = _filters_sd(hy_w1[0], hy_b1[0], hy_w2[0], hy_b2[0], hy_w3[0], hy_b3[0], hy_freq[0])
    kspec = _filter_spectra2(sd, fwd, anti, phase)
    y1, v = _conv_fwd2(p, 2, kspec, 0, fwd, anti, batch, conv=(conv_w, conv_b, 2))
    z = _conv_inv2(y1, inv, anti, v, 0, hy_bias_d[0, 0], p, 0, conv_w, conv_b)
    y2, = _conv_fwd2(z, 0, kspec, 1, fwd, anti, batch)
    yg = _conv_inv2(y2, inv, anti, z, 0, hy_bias_d[0, 1], p, 1, conv_w, conv_b, g_blk0=3)
    out, = _out_proj(yg, hy_w_out[0].astype(BF16), x2, part(1, 0, batch, 2), final_g=final_g)
    return out.reshape(batch, seq, d)
```

```python
import functools
import math

import jax
import jax.numpy as jnp
import numpy as np
from jax import lax
from jax.experimental import pallas as pl
from jax.experimental.pallas import tpu as pltpu

F32 = jnp.float32
BF16 = jnp.bfloat16

D_MODEL = 2048
SEQ = 2048
CTX_LEN = 256
GRID_W = 64
HEAD_DIM = 128
N_HEADS = 16
N_KV_HEADS = 4
GQA_GROUP = 4
ATTN_WIDTH = 2048
KV_WIDTH = 512
ATTN_IN = 2 * ATTN_WIDTH + 2 * KV_WIDTH
WINDOW = 128
BLOCK = 128
ROPE_BASE = 10000.0
HYENA_WIDTH = 2048
FILTER_EMB = 33
FILTER_HIDDEN = 64
DECAY_FAST = 0.3
DECAY_SLOW = 1.5
DECAY_TARGET = 1e-2
WINDOW_SHIFT = 0.05
NORM_EPS = 1e-6
NEG_INF = -1e30

LANES = 128
MXU_COLS = 256
ONES_ROWS = 16
LOG2E = math.log2(math.e)
FFT_N = 2 * SEQ
VMEM_LIMIT = 56 * 1024 * 1024


def _cparams(sem):
    return pltpu.CompilerParams(dimension_semantics=sem, vmem_limit_bytes=VMEM_LIMIT)


def _mods_kernel(c_ref, w_ref, b_ref, o_ref):
    c = c_ref[...]
    s = c * jax.nn.sigmoid(c)
    s_hi = s.astype(BF16)
    s_lo = (s - s_hi.astype(F32)).astype(BF16)
    lhs = jnp.concatenate([s_hi, s_lo], axis=0)
    r = jnp.dot(lhs, w_ref[0].astype(BF16), preferred_element_type=F32)
    o_ref[0] = r[:8] + r[8:] + b_ref[0]


def _mods(cc, ada_w, ada_b):
    depth, d, n = ada_w.shape
    tn = 1024
    return pl.pallas_call(
        _mods_kernel,
        grid=(depth, n // tn),
        in_specs=[
            pl.BlockSpec((8, d), lambda l, j: (0, 0)),
            pl.BlockSpec((1, d, tn), lambda l, j: (l, 0, j)),
            pl.BlockSpec((1, 1, tn), lambda l, j: (l, 0, j)),
        ],
        out_specs=pl.BlockSpec((1, 8, tn), lambda l, j: (l, 0, j)),
        out_shape=jax.ShapeDtypeStruct((depth, 8, n), F32),
        compiler_params=_cparams(("arbitrary", "arbitrary")),
        name="adaln_mods",
    )(cc, ada_w, ada_b.reshape(depth, 1, n))


def _rope_slab(t, cos, sin):
    lane = lax.broadcasted_iota(jnp.int32, t.shape, 1)
    first = (lane % 64) < 32
    partner = jnp.where(first, pltpu.roll(t, 96, 1), pltpu.roll(t, 32, 1))
    return t * cos + partner * sin


def _norm_mod_rows(x, g, mul, add):
    ms = jnp.mean(x * x, axis=-1, keepdims=True)
    return x * lax.rsqrt(ms + NORM_EPS) * g * mul + add


def _norm_mod_kernel(x_ref, g_ref, sh_ref, sc_ref, o_ref, *, row_chunk):
    g = g_ref[...]
    mul = 1.0 + sc_ref[0]
    add = sh_ref[0]
    for r in range(0, x_ref.shape[0], row_chunk):
        o_ref[r:r + row_chunk, :] = _norm_mod_rows(x_ref[r:r + row_chunk, :], g, mul, add).astype(o_ref.dtype)


def _norm_mod(x, g, shift, scale, *, tm, rows_per_mod):
    m, d = x.shape
    if shift.shape[0] > 1:
        mod_map = lambda i: ((i * tm) // rows_per_mod, 0, 0)
    else:
        mod_map = lambda i: (0, 0, 0)
    return pl.pallas_call(
        functools.partial(_norm_mod_kernel, row_chunk=min(tm, 256)),
        grid=(m // tm,),
        in_specs=[pl.BlockSpec((tm, d), lambda i: (i, 0)), pl.BlockSpec((1, d), lambda i: (0, 0)),
                  pl.BlockSpec((1, 1, d), mod_map), pl.BlockSpec((1, 1, d), mod_map)],
        out_specs=pl.BlockSpec((tm, d), lambda i: (i, 0)),
        out_shape=jax.ShapeDtypeStruct((m, d), BF16),
        compiler_params=_cparams(("arbitrary",)),
        name="norm_mod",
    )(x, g.reshape(1, d), shift, scale)


def _proj_kernel(*refs, tile_types, has_rope, has_conv, q_scale):
    if has_rope:
        hx_ref, w_ref, cos_ref, sin_ref, o_ref, wb_ref = refs
    elif has_conv:
        hx_ref, w_ref, cw_ref, cb_ref, o_ref, wb_ref = refs
    else:
        hx_ref, w_ref, o_ref, wb_ref = refs
    j = pl.program_id(0)
    tm, tn = o_ref.shape

    @pl.when(pl.program_id(1) == 0)
    def _():
        wb_ref[...] = w_ref[...].astype(BF16)

    w_ref = wb_ref

    def plain():
        o_ref[...] = jnp.dot(hx_ref[...], w_ref[...], preferred_element_type=F32).astype(o_ref.dtype)

    def gated():
        for c0 in range(0, tn, MXU_COLS):
            acc = jnp.dot(hx_ref[...], w_ref[:, c0:c0 + MXU_COLS], preferred_element_type=F32)
            o_ref[:, c0:c0 + MXU_COLS] = (acc * jax.nn.sigmoid(acc)).astype(o_ref.dtype)

    def short_conv():
        zero_row = jnp.zeros((1, LANES), F32)
        for c0 in range(0, tn, MXU_COLS):
            acc = jnp.dot(hx_ref[...], w_ref[:, c0:c0 + MXU_COLS], preferred_element_type=F32)
            for h in range(MXU_COLS // LANES):
                lo = c0 + h * LANES
                u = acc[:, h * LANES:(h + 1) * LANES]
                w3 = cw_ref[:, lo:lo + LANES]
                prev = jnp.concatenate([zero_row, u[:-1]], axis=0)
                nxt = jnp.concatenate([u[1:], zero_row], axis=0)
                y = prev * w3[0:1] + u * w3[1:2] + nxt * w3[2:3] + cb_ref[:, lo:lo + LANES]
                o_ref[:, lo:lo + LANES] = y.astype(o_ref.dtype)

    def roped(types):
        cos = cos_ref[...]
        sin = sin_ref[...]
        for c0 in range(0, tn, MXU_COLS):
            acc = jnp.dot(hx_ref[...], w_ref[:, c0:c0 + MXU_COLS], preferred_element_type=F32)
            for h in range(MXU_COLS // LANES):
                ty = types[c0 // LANES + h]
                slab = acc[:, h * LANES:(h + 1) * LANES]
                if ty is not None:
                    slab = _rope_slab(slab, cos, sin)
                    if ty == "q":
                        slab = slab * q_scale
                lo = c0 + h * LANES
                o_ref[:, lo:lo + LANES] = slab.astype(o_ref.dtype)

    if tile_types is None:
        plain()
        return

    groups = {}
    for t, types in enumerate(tile_types):
        groups.setdefault(tuple(types), []).append(t)
    for types, tiles in groups.items():
        cond = functools.reduce(jnp.logical_or, [j == t for t in tiles])
        if all(ty is None for ty in types):
            pl.when(cond)(plain)
        elif all(ty == "g" for ty in types):
            pl.when(cond)(gated)
        elif all(ty == "c" for ty in types):
            pl.when(cond)(short_conv)
        else:
            assert "g" not in types and "c" not in types
            pl.when(cond)(functools.partial(roped, types))


def _proj(hx, w, col_blk0, n, *, tm, tn, tile_types=None, rope=None, conv=None, q_scale=1.0):
    m, d = hx.shape
    in_specs = [
        pl.BlockSpec((tm, d), lambda j, i: (i, 0)),
        pl.BlockSpec((d, tn), lambda j, i: (0, col_blk0 + j)),
    ]
    args = [hx, w]
    if rope is not None:
        cos, sin = rope
        seq_tiles = cos.shape[0] // tm
        in_specs += [pl.BlockSpec((tm, LANES), lambda j, i: (i % seq_tiles, 0))] * 2
        args += [cos, sin]
    if conv is not None:
        assert rope is None and tm == SEQ
        cw, cb = conv
        last = cw.shape[1] // tn - 1
        in_specs += [pl.BlockSpec((cw.shape[0], tn), lambda j, i: (0, jnp.minimum(j, last))),
                     pl.BlockSpec((1, tn), lambda j, i: (0, jnp.minimum(j, last)))]
        args += [cw, cb]
    return pl.pallas_call(
        functools.partial(_proj_kernel, tile_types=tile_types, has_rope=rope is not None,
                          has_conv=conv is not None, q_scale=q_scale),
        grid=(n // tn, m // tm),
        in_specs=in_specs,
        out_specs=pl.BlockSpec((tm, tn), lambda j, i: (i, j)),
        out_shape=jax.ShapeDtypeStruct((m, n), BF16),
        scratch_shapes=[pltpu.VMEM((d, tn), BF16)],
        compiler_params=_cparams(("arbitrary", "arbitrary")),
        name="proj",
    )(*args)


def _attn_kernel(sink_ref, q_ref, k_ref, v_ref, g_ref, kc_ref, vc_ref, o_ref,
                 vt_ref, vct_ref, bias_ref):
    kh = pl.program_id(1)
    band = 3 * BLOCK
    cols = GQA_GROUP * BLOCK
    n_blocks = SEQ // BLOCK

    vt_ref[:HEAD_DIM, :] = v_ref[...].T
    vt_ref[HEAD_DIM:, :] = jnp.ones((ONES_ROWS, SEQ), BF16)
    vct_ref[:HEAD_DIM, :] = vc_ref[...].T
    vct_ref[HEAD_DIM:, :] = jnp.ones((ONES_ROWS, CTX_LEN), BF16)
    krow = lax.broadcasted_iota(jnp.int32, (band, cols), 0)
    qcol = lax.broadcasted_iota(jnp.int32, (band, cols), 1) % BLOCK
    for idx, off in enumerate((0, -BLOCK, -2 * BLOCK)):
        bias_ref[idx] = jnp.where(jnp.abs(krow - qcol + off) <= WINDOW, 0.0, NEG_INF)

    kc = kc_ref[...]
    vct = vct_ref[...]
    sink_row = jnp.concatenate(
        [jnp.full((1, BLOCK), sink_ref[kh * GQA_GROUP + h] * LOG2E, F32) for h in range(GQA_GROUP)],
        axis=1)

    def body(n, carry):
        q0 = pl.multiple_of(n * BLOCK, BLOCK)
        ks = pl.multiple_of(jnp.clip((n - 1) * BLOCK, 0, SEQ - band), BLOCK)
        bidx = jnp.where(n == 0, 0, jnp.where(n == n_blocks - 1, 2, 1))
        qs = q_ref[pl.ds(q0, BLOCK), :]
        q4t = jnp.concatenate([qs[:, h * LANES:(h + 1) * LANES].T for h in range(GQA_GROUP)],
                              axis=1)
        kb = k_ref[pl.ds(ks, band), :]
        s_loc = jnp.dot(kb, q4t, preferred_element_type=F32) + bias_ref[bidx]
        s_ctx = jnp.dot(kc, q4t, preferred_element_type=F32)
        m = jnp.maximum(jnp.maximum(jnp.max(s_loc, axis=0, keepdims=True),
                                    jnp.max(s_ctx, axis=0, keepdims=True)), sink_row)
        p_loc = jnp.exp2(s_loc - m).astype(BF16)
        p_ctx = jnp.exp2(s_ctx - m).astype(BF16)
        ox = (jnp.dot(vt_ref[:, pl.ds(ks, band)], p_loc, preferred_element_type=F32)
              + jnp.dot(vct, p_ctx, preferred_element_type=F32))
        den = ox[HEAD_DIM:HEAD_DIM + 1, :] + jnp.exp2(sink_row - m)
        ot = ox[:HEAD_DIM, :] * (1.0 / den)
        gs = g_ref[pl.ds(q0, BLOCK), :].astype(F32)
        for h in range(GQA_GROUP):
            oh = ot[:, h * LANES:(h + 1) * LANES].T * gs[:, h * LANES:(h + 1) * LANES]
            o_ref[pl.ds(q0, BLOCK), h * LANES:(h + 1) * LANES] = oh.astype(o_ref.dtype)
        return carry

    lax.fori_loop(0, n_blocks, body, 0, unroll=8)


def _attention(px, ckv, sink, batch):
    gw = GQA_GROUP * HEAD_DIM
    k_blk0 = ATTN_WIDTH // HEAD_DIM
    v_blk0 = (ATTN_WIDTH + KV_WIDTH) // HEAD_DIM
    g_blk0 = (ATTN_WIDTH + 2 * KV_WIDTH) // gw
    return pl.pallas_call(
        _attn_kernel,
        grid=(batch, N_KV_HEADS),
        in_specs=[
            pl.BlockSpec(memory_space=pltpu.SMEM),
            pl.BlockSpec((SEQ, gw), lambda b, h: (b, h)),
            pl.BlockSpec((SEQ, HEAD_DIM), lambda b, h: (b, k_blk0 + h)),
            pl.BlockSpec((SEQ, HEAD_DIM), lambda b, h: (b, v_blk0 + h)),
            pl.BlockSpec((SEQ, gw), lambda b, h: (b, g_blk0 + h)),
            pl.BlockSpec((CTX_LEN, HEAD_DIM), lambda b, h: (b, h)),
            pl.BlockSpec((CTX_LEN, HEAD_DIM), lambda b, h: (b, N_KV_HEADS + h)),
        ],
        out_specs=pl.BlockSpec((SEQ, gw), lambda b, h: (b, h)),
        out_shape=jax.ShapeDtypeStruct((batch * SEQ, ATTN_WIDTH), BF16),
        scratch_shapes=[pltpu.VMEM((HEAD_DIM + ONES_ROWS, SEQ), BF16),
                        pltpu.VMEM((HEAD_DIM + ONES_ROWS, CTX_LEN), BF16),
                        pltpu.VMEM((3, 3 * BLOCK, GQA_GROUP * BLOCK), F32)],
        compiler_params=_cparams(("arbitrary", "arbitrary")),
        name="banded_attention",
    )(sink, px, px, px, px, ckv, ckv)


def _out_proj_kernel(*refs, final):
    if final:
        a_ref, w_ref, x_ref, gate_ref, fg_ref, o_ref = refs
    else:
        a_ref, w_ref, x_ref, gate_ref, ng_ref, sh_ref, sc_ref, o_ref, hx_ref = refs
    acc = jnp.dot(a_ref[...], w_ref[...], preferred_element_type=F32)
    y = x_ref[...] + gate_ref[0] * acc
    if final:
        ms = jnp.mean(y * y, axis=-1, keepdims=True)
        y = y * lax.rsqrt(ms + NORM_EPS) * fg_ref[...]
    else:
        hx_ref[...] = _norm_mod_rows(y, ng_ref[...], 1.0 + sc_ref[0], sh_ref[0]).astype(hx_ref.dtype)
    o_ref[...] = y


def _out_proj(a, w, x, gate, *, final_g=None, next_norm=None, tm=512):
    m, d = x.shape
    kdim = a.shape[1]
    final = final_g is not None
    per_batch = lambda i: ((i * tm) // SEQ, 0, 0)
    row_tile = pl.BlockSpec((tm, d), lambda i: (i, 0))
    vec = pl.BlockSpec((1, d), lambda i: (0, 0))
    in_specs = [
        pl.BlockSpec((tm, kdim), lambda i: (i, 0)),
        pl.BlockSpec((kdim, d), lambda i: (0, 0)),
        row_tile,
        pl.BlockSpec((1, 1, d), per_batch),
    ]
    args = [a, w, x, gate]
    out_specs = [row_tile]
    out_shape = [jax.ShapeDtypeStruct((m, d), F32)]
    if final:
        in_specs.append(vec)
        args.append(final_g.reshape(1, d))
    else:
        ng, shift, scale = next_norm
        in_specs += [vec, pl.BlockSpec((1, 1, d), per_batch), pl.BlockSpec((1, 1, d), per_batch)]
        args += [ng.reshape(1, d), shift, scale]
        out_specs.append(row_tile)
        out_shape.append(jax.ShapeDtypeStruct((m, d), BF16))
    return pl.pallas_call(
        functools.partial(_out_proj_kernel, final=final),
        grid=(m // tm,),
        in_specs=in_specs,
        out_specs=out_specs,
        out_shape=out_shape,
        compiler_params=_cparams(("arbitrary",)),
        name="out_proj",
    )(*args)


def _short_conv_kernel(p_ref, w_ref, b_ref, o_ref):
    length = p_ref.shape[0]
    row = lax.broadcasted_iota(jnp.int32, (length, LANES), 0)
    for c in range(0, p_ref.shape[1], LANES):
        u = p_ref[:, c:c + LANES].astype(F32)
        w = w_ref[:, c:c + LANES]
        prev = jnp.where(row == 0, 0.0, pltpu.roll(u, 1, 0))
        nxt = jnp.where(row == length - 1, 0.0, pltpu.roll(u, length - 1, 0))
        y = prev * w[0:1] + u * w[1:2] + nxt * w[2:3] + b_ref[:, c:c + LANES]
        o_ref[:, c:c + LANES] = y.astype(o_ref.dtype)


def _short_conv(p, conv_w, conv_b, batch, *, tn=512):
    n = conv_w.shape[1]
    return pl.pallas_call(
        _short_conv_kernel,
        grid=(batch, n // tn),
        in_specs=[
            pl.BlockSpec((SEQ, tn), lambda b, j: (b, j)),
            pl.BlockSpec((3, tn), lambda b, j: (0, j)),
            pl.BlockSpec((1, tn), lambda b, j: (0, j)),
        ],
        out_specs=pl.BlockSpec((SEQ, tn), lambda b, j: (b, j)),
        out_shape=jax.ShapeDtypeStruct((batch * SEQ, n), BF16),
        compiler_params=_cparams(("arbitrary", "arbitrary")),
        name="short_conv",
    )(p, conv_w, conv_b.reshape(1, n))


def _filter_kernel(w1t_ref, w1c_ref, w1s_ref, b1_ref, w2_ref, b2_ref, fq_ref, w3_ref, b3_ref,
                   o_ref, hid_ref):
    j = pl.program_id(0)
    length = o_ref.shape[0]
    tn = o_ref.shape[1]
    hp = lax.Precision.HIGHEST

    @pl.when(j == 0)
    def _():
        n = lax.broadcasted_iota(jnp.int32, (length, LANES), 0).astype(F32)
        lane = lax.broadcasted_iota(jnp.int32, (length, LANES), 1)
        bands = (FILTER_EMB - 1) // 2
        fr_step = (bands - 1 - 1e-4) / (bands - 1)
        fr = jnp.where(lane < bands, 1e-4 + lane.astype(F32) * fr_step, 0.0)
        ang = (2.0 * math.pi * n / length) * fr
        t = n / (length - 1)
        fq = fq_ref[...]
        pre = (t * w1t_ref[...]
               + jnp.dot(jnp.cos(ang), w1c_ref[...], precision=hp, preferred_element_type=F32)
               + jnp.dot(-jnp.sin(ang), w1s_ref[...], precision=hp, preferred_element_type=F32)
               + b1_ref[...])
        hid = jnp.sin(fq * pre)
        hid = jnp.sin(fq * (jnp.dot(hid, w2_ref[...], precision=hp, preferred_element_type=F32)
                            + b2_ref[...]))
        hid_ref[...] = hid

    hf = jnp.dot(hid_ref[...], w3_ref[...], precision=hp, preferred_element_type=F32) + b3_ref[...]
    row = lax.broadcasted_iota(jnp.int32, (length, tn), 0)
    col = lax.broadcasted_iota(jnp.int32, (length, tn), 1) + j * tn
    chan = (col % HYENA_WIDTH).astype(F32)
    min_decay = math.log(DECAY_TARGET) / DECAY_SLOW
    max_decay = math.log(DECAY_TARGET) / DECAY_FAST
    delta = min_decay + chan * ((max_decay - min_decay) / (HYENA_WIDTH - 1))
    t = row.astype(F32) / (length - 1)
    filt = hf * (jnp.exp(-t * jnp.abs(delta)) + WINDOW_SHIFT)
    backward = (col // HYENA_WIDTH) % 2 == 1
    filt = jnp.where(backward & (row == 0), 0.0, filt)
    o_ref[...] = filt.astype(o_ref.dtype)


def _filters(w1, b1, w2, b2, w3, b3, freq, *, tn=1024):
    hpad = LANES - FILTER_HIDDEN
    bands = (FILTER_EMB - 1) // 2
    n = w3.shape[1]
    w1t = jnp.pad(w1[0:1], ((0, 0), (0, hpad)))
    w1c = jnp.pad(w1[1:1 + bands], ((0, LANES - bands), (0, hpad)))
    w1s = jnp.pad(w1[1 + bands:], ((0, LANES - bands), (0, hpad)))
    b1p = jnp.pad(b1.reshape(1, -1), ((0, 0), (0, hpad)))
    w2p = jnp.pad(w2, ((0, hpad), (0, hpad)))
    b2p = jnp.pad(b2.reshape(1, -1), ((0, 0), (0, hpad)))
    fqp = jnp.pad(freq.reshape(1, -1), ((0, 0), (0, hpad)))
    w3p = jnp.pad(w3, ((0, hpad), (0, 0)))
    small = lambda shape: pl.BlockSpec(shape, lambda j: (0, 0))
    return pl.pallas_call(
        _filter_kernel,
        grid=(n // tn,),
        in_specs=[small((1, LANES)), small((LANES, LANES)), small((LANES, LANES)), small((1, LANES)),
                  small((LANES, LANES)), small((1, LANES)), small((1, LANES)),
                  pl.BlockSpec((LANES, tn), lambda j: (0, j)),
                  pl.BlockSpec((1, tn), lambda j: (0, j))],
        out_specs=pl.BlockSpec((SEQ, tn), lambda j: (0, j)),
        out_shape=jax.ShapeDtypeStruct((SEQ, n), BF16),
        scratch_shapes=[pltpu.VMEM((SEQ, LANES), F32)],
        compiler_params=_cparams(("arbitrary",)),
        name="hyena_filters",
    )(w1t, w1c, w1s, b1p, w2p, b2p, fqp, w3p, b3.reshape(1, n))


def _dft_tables():
    idx = np.arange(SEQ, dtype=np.int64)
    ang = ((idx[:, None] * idx[None, :]) % FFT_N).astype(np.float64) * (2.0 * math.pi / FFT_N)
    cm = np.cos(ang)
    sm = np.sin(ang)
    sm[0, :] = np.where(idx % 2 == 0, 1.0, -1.0)
    as_bf16 = lambda a: jnp.asarray(a.astype(np.float32)).astype(BF16)
    return as_bf16(cm), as_bf16(sm), as_bf16(np.ascontiguousarray(sm.T))


def _spec_scale(i, tf, shape):
    row = lax.broadcasted_iota(jnp.int32, shape, 0) + i * tf
    is0 = row == 0
    return is0, jnp.where(is0, 1.0 / FFT_N, 2.0 / FFT_N)


def _filter_spec_kernel(hf_ref, hb_ref, cm_ref, sm_ref, ka_ref, kb_ref):
    i = pl.program_id(2)
    tf = cm_ref.shape[0]
    cm = cm_ref[...]
    sm = sm_ref[...]
    hf = hf_ref[...]
    hb = hb_ref[...]
    a_f = jnp.dot(cm, hf, preferred_element_type=F32)
    b_f = jnp.dot(sm, hf, preferred_element_type=F32)
    a_b = jnp.dot(cm, hb, preferred_element_type=F32)
    b_b = jnp.dot(sm, hb, preferred_element_type=F32)
    is0, scale = _spec_scale(i, tf, a_f.shape)
    ka_ref[0] = ((a_f + a_b) * scale).astype(ka_ref.dtype)
    kb_ref[0] = (jnp.where(is0, b_f + b_b, b_f - b_b) * scale).astype(kb_ref.dtype)


def _filter_spectra(filt, cm, sm, *, tf=512, tn=512):
    w = HYENA_WIDTH
    ct = w // tn
    out = jax.ShapeDtypeStruct((2, SEQ, w), BF16)
    return pl.pallas_call(
        _filter_spec_kernel,
        grid=(2, ct, SEQ // tf),
        in_specs=[
            pl.BlockSpec((SEQ, tn), lambda o, j, i: (0, 2 * o * ct + j)),
            pl.BlockSpec((SEQ, tn), lambda o, j, i: (0, (2 * o + 1) * ct + j)),
            pl.BlockSpec((tf, SEQ), lambda o, j, i: (i, 0)),
            pl.BlockSpec((tf, SEQ), lambda o, j, i: (i, 0)),
        ],
        out_specs=[pl.BlockSpec((1, tf, tn), lambda o, j, i: (o, i, j))] * 2,
        out_shape=[out, out],
        compiler_params=_cparams(("arbitrary", "arbitrary", "arbitrary")),
        name="filter_spectra",
    )(filt, filt, cm, sm)


def _conv_fwd_kernel(u_ref, cm_ref, sm_ref, ka_ref, kb_ref, ya_ref, yb_ref):
    i = pl.program_id(2)
    tf = cm_ref.shape[0]
    u = u_ref[...]
    a = jnp.dot(cm_ref[...], u, preferred_element_type=F32)
    b = jnp.dot(sm_ref[...], u, preferred_element_type=F32)
    ka = ka_ref[0].astype(F32)
    kb = kb_ref[0].astype(F32)
    is0, _ = _spec_scale(i, tf, a.shape)
    bkb = b * kb
    ya_ref[0] = (a * ka - jnp.where(is0, 0.0, bkb)).astype(ya_ref.dtype)
    yb_ref[0] = jnp.where(is0, bkb, a * kb + b * ka).astype(yb_ref.dtype)


def _conv_fwd(u, col_blk0, ka, kb, order, cm, sm, batch, *, tf=512, tn=1024):
    w = HYENA_WIDTH
    out = jax.ShapeDtypeStruct((batch, SEQ, w), BF16)
    c0 = col_blk0 * (w // tn)
    return pl.pallas_call(
        _conv_fwd_kernel,
        grid=(batch, w // tn, SEQ // tf),
        in_specs=[
            pl.BlockSpec((SEQ, tn), lambda b, j, i: (b, c0 + j)),
            pl.BlockSpec((tf, SEQ), lambda b, j, i: (i, 0)),
            pl.BlockSpec((tf, SEQ), lambda b, j, i: (i, 0)),
            pl.BlockSpec((1, tf, tn), lambda b, j, i: (order, i, j)),
            pl.BlockSpec((1, tf, tn), lambda b, j, i: (order, i, j)),
        ],
        out_specs=[pl.BlockSpec((1, tf, tn), lambda b, j, i: (b, i, j))] * 2,
        out_shape=[out, out],
        compiler_params=_cparams(("arbitrary", "arbitrary", "arbitrary")),
        name="long_conv_fwd",
    )(u, cm, sm, ka, kb)


def _conv_inv_kernel(*refs, gated):
    if gated:
        ya_ref, yb_ref, cm_ref, st_ref, u_ref, d_ref, x_ref, g_ref, o_ref = refs
    else:
        ya_ref, yb_ref, cm_ref, st_ref, u_ref, d_ref, x_ref, o_ref = refs
    y = (jnp.dot(cm_ref[...], ya_ref[0], preferred_element_type=F32)
         + jnp.dot(st_ref[...], yb_ref[0], preferred_element_type=F32))
    y = y + u_ref[...].astype(F32) * d_ref[...]
    y = y * x_ref[...].astype(F32)
    if gated:
        y = y * g_ref[...].astype(F32)
    o_ref[...] = y.astype(o_ref.dtype)


def _conv_inv(ya, yb, cm, st, u, u_blk0, d, x, x_blk0, g=None, g_blk0=0, *, tt=512, tn=1024):
    batch = ya.shape[0]
    w = HYENA_WIDTH
    ct = w // tn
    tiles = SEQ // tt
    gated = g is not None
    in_specs = [
        pl.BlockSpec((1, SEQ, tn), lambda b, j, i: (b, 0, j)),
        pl.BlockSpec((1, SEQ, tn), lambda b, j, i: (b, 0, j)),
        pl.BlockSpec((tt, SEQ), lambda b, j, i: (i, 0)),
        pl.BlockSpec((tt, SEQ), lambda b, j, i: (i, 0)),
        pl.BlockSpec((tt, tn), lambda b, j, i: (b * tiles + i, u_blk0 * ct + j)),
        pl.BlockSpec((1, tn), lambda b, j, i: (0, j)),
        pl.BlockSpec((tt, tn), lambda b, j, i: (b * tiles + i, x_blk0 * ct + j)),
    ]
    args = [ya, yb, cm, st, u, d.reshape(1, w), x]
    if gated:
        in_specs.append(pl.BlockSpec((tt, tn), lambda b, j, i: (b * tiles + i, g_blk0 * ct + j)))
        args.append(g)
    return pl.pallas_call(
        functools.partial(_conv_inv_kernel, gated=gated),
        grid=(batch, ct, tiles),
        in_specs=in_specs,
        out_specs=pl.BlockSpec((tt, tn), lambda b, j, i: (b * tiles + i, j)),
        out_shape=jax.ShapeDtypeStruct((batch * SEQ, w), BF16),
        compiler_params=_cparams(("arbitrary", "arbitrary", "arbitrary")),
        name="long_conv_inv",
    )(*args)


HALF = SEQ // 2
FLIP_BLOCK = 256
ROW_TILE = 512


def _fold_tables():
    t2 = 2 * np.arange(HALF, dtype=np.int64) + 1
    g = np.arange(HALF, dtype=np.int64)

    def tab(f):
        ang = ((f[:, None] * t2[None, :]) % (2 * FFT_N)).astype(np.float64) * (math.pi / FFT_N)
        return np.cos(ang), np.sin(ang)

    ce, se = tab(2 * g)
    co, so = tab(2 * g + 1)
    se[0, :] = np.where(np.arange(HALF) % 2 == 0, 1.0, -1.0)
    const = lambda a, dt: jnp.asarray(np.ascontiguousarray(a).astype(np.float32)).astype(dt)
    fwd = const(np.stack([ce, se, co, so]), BF16)
    inv = const(np.stack([np.hstack([ce.T, so.T]), np.hstack([se.T, co.T])]), BF16)
    anti = const(np.eye(FLIP_BLOCK)[::-1], BF16)
    theta = np.stack([2 * g, 2 * g + 1]).astype(np.float64) * (math.pi / FFT_N)
    phase = const(np.stack([np.cos(theta), np.sin(theta)])[..., None], F32)
    return fwd, inv, anti, phase


def _flip_rows(h, anti):
    nb = h.shape[0] // FLIP_BLOCK
    return jnp.concatenate(
        [jnp.dot(anti[...], h[(nb - 1 - a) * FLIP_BLOCK:(nb - a) * FLIP_BLOCK, :], preferred_element_type=F32)
         for a in range(nb)], axis=0)


def _fold(x_ref, cols, anti):
    x0 = x_ref[:HALF, cols].astype(F32)
    xr = _flip_rows(x_ref[HALF:, cols], anti)
    return (x0 + xr).astype(BF16), (x0 - xr).astype(BF16)


def _staggered(n, matmuls, finish):
    matmuls(0, 0)
    for t in range(1, n):
        matmuls(t, t % 2)
        finish(t - 1, (t - 1) % 2)
    finish(n - 1, (n - 1) % 2)


def _col_chunks(n):
    return [slice(c, c + MXU_COLS) for c in range(0, n, MXU_COLS)]


def _half_spectrum(tab_ref, xs, xa):
    dot = lambda k, v: jnp.dot(tab_ref[k], v, preferred_element_type=F32)
    return dot(0, xs), dot(1, xa), dot(2, xa), dot(3, xs)


def _short_conv_into(p_ref, w_ref, b_ref, dst_ref, cols):
    length = p_ref.shape[0]
    row = lax.broadcasted_iota(jnp.int32, (length, LANES), 0)
    for c in range(cols.start, cols.stop, LANES):
        u = p_ref[:, c:c + LANES].astype(F32)
        w = w_ref[:, c:c + LANES]
        prev = jnp.where(row == 0, 0.0, pltpu.roll(u, 1, 0))
        nxt = jnp.where(row == length - 1, 0.0, pltpu.roll(u, length - 1, 0))
        y = prev * w[0:1] + u * w[1:2] + nxt * w[2:3] + b_ref[:, c:c + LANES]
        dst_ref[:, c:c + LANES] = y.astype(dst_ref.dtype)


def _resident(shape):
    zeros = (0,) * len(shape)
    return pl.BlockSpec(shape, lambda *_: zeros, pipeline_mode=pl.Buffered(1))


def _filter_sd_kernel(w1t_ref, w1c_ref, w1s_ref, b1_ref, w2_ref, b2_ref, fq_ref,
                      w3f_ref, b3f_ref, w3b_ref, b3b_ref, d_ref, o_ref, hid_ref):
    first = (pl.program_id(0) == 0) & (pl.program_id(1) == 0)
    j = pl.program_id(1)
    length = o_ref.shape[1]
    tn = o_ref.shape[2]
    hp = lax.Precision.HIGHEST

    @pl.when(first)
    def _():
        n = lax.broadcasted_iota(jnp.int32, (length, LANES), 0).astype(F32)
        lane = lax.broadcasted_iota(jnp.int32, (length, LANES), 1)
        bands = (FILTER_EMB - 1) // 2
        fr_step = (bands - 1 - 1e-4) / (bands - 1)
        fr = jnp.where(lane < bands, 1e-4 + lane.astype(F32) * fr_step, 0.0)
        ang = (2.0 * math.pi * n / length) * fr
        t = n / (length - 1)
        fq = fq_ref[...]
        pre = (t * w1t_ref[...]
               + jnp.dot(jnp.cos(ang), w1c_ref[...], precision=hp, preferred_element_type=F32)
               + jnp.dot(-jnp.sin(ang), w1s_ref[...], precision=hp, preferred_element_type=F32)
               + b1_ref[...])
        hid = jnp.sin(fq * pre)
        hid = jnp.sin(fq * (jnp.dot(hid, w2_ref[...], precision=hp, preferred_element_type=F32)
                            + b2_ref[...]))
        hid_hi = hid.astype(BF16)
        hid_ref[0] = hid_hi
        hid_ref[1] = (hid - hid_hi.astype(F32)).astype(BF16)

    def dot3(w_ref):
        wf = w_ref[...]
        w_hi = wf.astype(BF16)
        w_lo = (wf - w_hi.astype(F32)).astype(BF16)
        return (jnp.dot(hid_ref[0], w_hi, preferred_element_type=F32)
                + jnp.dot(hid_ref[1], w_hi, preferred_element_type=F32)
                + jnp.dot(hid_ref[0], w_lo, preferred_element_type=F32))

    row = lax.broadcasted_iota(jnp.int32, (length, tn), 0)
    chan = (lax.broadcasted_iota(jnp.int32, (length, tn), 1) + j * tn).astype(F32)
    min_decay = math.log(DECAY_TARGET) / DECAY_SLOW
    max_decay = math.log(DECAY_TARGET) / DECAY_FAST
    delta = min_decay + chan * ((max_decay - min_decay) / (HYENA_WIDTH - 1))
    t = row.astype(F32) / (length - 1)
    window = jnp.exp(-t * jnp.abs(delta)) + WINDOW_SHIFT
    hf = (dot3(w3f_ref) + b3f_ref[...]) * window
    hb = (dot3(w3b_ref) + b3b_ref[...]) * window
    hf = jnp.where(row == 0, hf + d_ref[0], hf)
    hb = jnp.where(row == 0, 0.0, hb)
    o_ref[0] = (hf + hb).astype(o_ref.dtype)
    o_ref[1] = (hf - hb).astype(o_ref.dtype)


def _filters_sd(w1, b1, w2, b2, w3, b3, freq, bias_d, *, tn=512):
    hpad = LANES - FILTER_HIDDEN
    bands = (FILTER_EMB - 1) // 2
    w = HYENA_WIDTH
    ct = w // tn
    n = w3.shape[1]
    w1t = jnp.pad(w1[0:1], ((0, 0), (0, hpad)))
    w1c = jnp.pad(w1[1:1 + bands], ((0, LANES - bands), (0, hpad)))
    w1s = jnp.pad(w1[1 + bands:], ((0, LANES - bands), (0, hpad)))
    b1p = jnp.pad(b1.reshape(1, -1), ((0, 0), (0, hpad)))
    w2p = jnp.pad(w2, ((0, hpad), (0, hpad)))
    b2p = jnp.pad(b2.reshape(1, -1), ((0, 0), (0, hpad)))
    fqp = jnp.pad(freq.reshape(1, -1), ((0, 0), (0, hpad)))
    w3p = jnp.pad(w3, ((0, hpad), (0, 0)))
    b3r = b3.reshape(1, n)
    small = lambda shape: pl.BlockSpec(shape, lambda o, j: (0, 0))
    fwd_cols = lambda o, j: (0, 2 * o * ct + j)
    bwd_cols = lambda o, j: (0, (2 * o + 1) * ct + j)
    return pl.pallas_call(
        _filter_sd_kernel,
        grid=(2, ct),
        in_specs=[small((1, LANES)), small((LANES, LANES)), small((LANES, LANES)), small((1, LANES)),
                  small((LANES, LANES)), small((1, LANES)), small((1, LANES)),
                  pl.BlockSpec((LANES, tn), fwd_cols), pl.BlockSpec((1, tn), fwd_cols),
                  pl.BlockSpec((LANES, tn), bwd_cols), pl.BlockSpec((1, tn), bwd_cols),
                  pl.BlockSpec((1, 1, tn), lambda o, j: (o, 0, j))],
        out_specs=pl.BlockSpec((2, SEQ, tn), lambda o, j: (0, 0, o * ct + j)),
        out_shape=jax.ShapeDtypeStruct((2, SEQ, 2 * w), BF16),
        scratch_shapes=[pltpu.VMEM((2, SEQ, LANES), BF16)],
        compiler_params=_cparams(("arbitrary", "arbitrary")),
        name="hyena_filters",
    )(w1t, w1c, w1s, b1p, w2p, b2p, fqp, w3p, b3r, w3p, b3r, bias_d.reshape(2, 1, w))


def _filter_spec_kernel2(s_ref, d_ref, tab_ref, anti_ref, ph_ref, k_ref):
    anti = anti_ref
    is0 = lax.broadcasted_iota(jnp.int32, (HALF, MXU_COLS), 0) == 0
    scale_e = jnp.where(is0, 1.0 / FFT_N, 2.0 / FFT_N)
    cos_e, sin_e = ph_ref[0, 0], ph_ref[1, 0]
    cos_o, sin_o = ph_ref[0, 1], ph_ref[1, 1]
    for cols in _col_chunks(k_ref.shape[3]):
        sa_e, sb_e, sa_o, sb_o = _half_spectrum(tab_ref, *_fold(s_ref.at[0], cols, anti))
        da_e, db_e, da_o, db_o = _half_spectrum(tab_ref, *_fold(d_ref.at[0], cols, anti))
        k_ref[0, 0, :, cols] = ((sa_e * cos_e + sb_e * sin_e) * scale_e).astype(k_ref.dtype)
        k_ref[0, 1, :, cols] = (jnp.where(is0, sb_e, db_e * cos_e - da_e * sin_e)
                                * scale_e).astype(k_ref.dtype)
        k_ref[0, 2, :, cols] = ((sa_o * cos_o + sb_o * sin_o) * (2.0 / FFT_N)).astype(k_ref.dtype)
        k_ref[0, 3, :, cols] = ((db_o * cos_o - da_o * sin_o) * (2.0 / FFT_N)).astype(k_ref.dtype)


def _filter_spectra2(sd, fwd, anti, phase, *, tn=512):
    w = HYENA_WIDTH
    ct = w // tn
    return pl.pallas_call(
        _filter_spec_kernel2,
        grid=(2, ct),
        in_specs=[
            pl.BlockSpec((1, SEQ, tn), lambda o, j: (0, 0, o * ct + j)),
            pl.BlockSpec((1, SEQ, tn), lambda o, j: (1, 0, o * ct + j)),
            _resident(fwd.shape), _resident(anti.shape), _resident(phase.shape),
        ],
        out_specs=pl.BlockSpec((1, 4, HALF, tn), lambda o, j: (o, 0, 0, j)),
        out_shape=jax.ShapeDtypeStruct((2, 4, HALF, w), BF16),
        compiler_params=_cparams(("arbitrary", "arbitrary")),
        name="filter_spectra",
    )(sd, sd, fwd, anti, phase)


def _conv_fwd_kernel2(x_ref, tab_ref, anti_ref, k_ref, y_ref):
    anti = anti_ref
    is0 = lax.broadcasted_iota(jnp.int32, (HALF, MXU_COLS), 0) == 0
    for cols in _col_chunks(y_ref.shape[3]):
        a_e, b_e, a_o, b_o = _half_spectrum(tab_ref, *_fold(x_ref, cols, anti))
        ka_e, kb_e, ka_o, kb_o = [k_ref[0, plane, :, cols].astype(F32) for plane in range(4)]
        bkb = b_e * kb_e
        y_ref[0, 0, :HALF, cols] = (a_e * ka_e - jnp.where(is0, 0.0, bkb)).astype(y_ref.dtype)
        y_ref[0, 1, :HALF, cols] = jnp.where(is0, bkb, a_e * kb_e + b_e * ka_e).astype(y_ref.dtype)
        y_ref[0, 1, HALF:, cols] = (a_o * ka_o - b_o * kb_o).astype(y_ref.dtype)
        y_ref[0, 0, HALF:, cols] = (a_o * kb_o + b_o * ka_o).astype(y_ref.dtype)


def _conv_fwd2(x, x_blk0, kspec, order, fwd, anti, batch, *, tn=512):
    w = HYENA_WIDTH
    ct = w // tn
    return pl.pallas_call(
        _conv_fwd_kernel2,
        grid=(batch, ct),
        in_specs=[pl.BlockSpec((SEQ, tn), lambda b, j: (b, x_blk0 * ct + j)),
                  _resident(fwd.shape), _resident(anti.shape),
                  pl.BlockSpec((1, 4, HALF, tn), lambda b, j: (order, 0, 0, j))],
        out_specs=pl.BlockSpec((1, 2, SEQ, tn), lambda b, j: (b, 0, 0, j)),
        out_shape=jax.ShapeDtypeStruct((batch, 2, SEQ, w), BF16),
        compiler_params=_cparams(("arbitrary", "arbitrary")),
        name="long_conv_fwd",
    )(x, fwd, anti, kspec)


def _conv_inv_kernel2(*refs, gated):
    if gated:
        y_ref, tab_ref, anti_ref, xm_ref, g_ref, o_ref, acc_ref = refs
    else:
        y_ref, tab_ref, anti_ref, xm_ref, o_ref, acc_ref = refs
    anti = anti_ref
    tiles = [(cols, r0) for cols in _col_chunks(o_ref.shape[1]) for r0 in range(0, HALF, ROW_TILE)]

    def matmuls(t, slot):
        cols, r0 = tiles[t]
        for k in range(2):
            acc_ref[slot, k] = jnp.dot(tab_ref[k, r0:r0 + ROW_TILE, :], y_ref[0, k, :, cols],
                                       preferred_element_type=F32)

    def finish(t, slot):
        cols, r0 = tiles[t]
        p1 = acc_ref[slot, 0]
        p2 = acc_ref[slot, 1]
        direct = slice(r0, r0 + ROW_TILE)
        mirrored = slice(SEQ - r0 - ROW_TILE, SEQ - r0)
        for rows, y in ((direct, p1 + p2), (mirrored, _flip_rows((p1 - p2).astype(BF16), anti))):
            out = y * xm_ref[rows, cols].astype(F32)
            if gated:
                out = out * g_ref[rows, cols].astype(F32)
            o_ref[rows, cols] = out.astype(o_ref.dtype)

    _staggered(len(tiles), matmuls, finish)


def _conv_inv2(y, inv, anti, p, m_blk0, g_blk0=None, *, tn=512):
    batch = y.shape[0]
    w = HYENA_WIDTH
    ct = w // tn
    gated = g_blk0 is not None
    col = lambda blk0: (lambda b, j: (b, blk0 * ct + j))
    in_specs = [
        pl.BlockSpec((1, 2, SEQ, tn), lambda b, j: (b, 0, 0, j)),
        _resident(inv.shape), _resident(anti.shape),
        pl.BlockSpec((SEQ, tn), col(m_blk0)),
    ]
    args = [y, inv, anti, p]
    if gated:
        in_specs.append(pl.BlockSpec((SEQ, tn), col(g_blk0)))
        args.append(p)
    return pl.pallas_call(
        functools.partial(_conv_inv_kernel2, gated=gated),
        grid=(batch, ct),
        in_specs=in_specs,
        out_specs=pl.BlockSpec((SEQ, tn), lambda b, j: (b, j)),
        out_shape=jax.ShapeDtypeStruct((batch * SEQ, w), BF16),
        scratch_shapes=[pltpu.VMEM((2, 2, ROW_TILE, MXU_COLS), F32)],
        compiler_params=_cparams(("arbitrary", "arbitrary")),
        name="long_conv_inv",
    )(*args)


def _rope_tables():
    pos = np.arange(SEQ)
    row = (pos // GRID_W).astype(np.float32)
    col = (pos % GRID_W).astype(np.float32)
    half = HEAD_DIM // 2
    inv = (ROPE_BASE ** (-np.arange(0, half, 2, dtype=np.float32) / half)).astype(np.float32)
    ar = row[:, None] * inv[None]
    ac = col[:, None] * inv[None]
    cos = np.concatenate([np.cos(ar), np.cos(ar), np.cos(ac), np.cos(ac)], axis=1)
    sin = np.concatenate([-np.sin(ar), np.sin(ar), -np.sin(ac), np.sin(ac)], axis=1)
    return jnp.asarray(cos, F32), jnp.asarray(sin, F32)


def _tile_types(tn, kinds):
    per_tile = tn // LANES
    return [kinds[s0:s0 + per_tile] for s0 in range(0, len(kinds), per_tile)]


def kernel(x, c, ctx, c_ctx, norm_g, ada_w, ada_b, attn_w_in, attn_w_out, attn_sink, hy_w_in,
           hy_conv_w, hy_conv_b, hy_w1, hy_b1, hy_w2, hy_b2, hy_w3, hy_b3, hy_freq, hy_bias_d,
           hy_w_out, final_g):
    batch, seq, d = x.shape
    assert (seq, d) == (SEQ, D_MODEL) and ctx.shape[1] == CTX_LEN
    assert norm_g.shape[0] == 2 and attn_w_in.shape[0] == 1 and hy_w_in.shape[0] == 1
    w = HYENA_WIDTH

    cc = jnp.concatenate([c, c_ctx[None], jnp.zeros((8 - batch - 1, d), F32)], axis=0)
    mods = _mods(cc, ada_w, ada_b)
    part = lambda layer, r0, r1, k: mods[layer, r0:r1, None, k * d:(k + 1) * d]

    x2 = x.reshape(batch * seq, d)
    ctx2 = ctx.reshape(batch * CTX_LEN, d)

    tn = 1024
    kinds0 = (["q"] * N_HEADS + ["k"] * N_KV_HEADS + [None] * N_KV_HEADS
              + ["g"] * (ATTN_WIDTH // LANES))
    hx = _norm_mod(x2, norm_g[0], part(0, 0, batch, 0), part(0, 0, batch, 1), tm=1024,
                   rows_per_mod=seq)
    hc = _norm_mod(ctx2, norm_g[0], part(0, batch, batch + 1, 0), part(0, batch, batch + 1, 1),
                   tm=512, rows_per_mod=CTX_LEN)
    px = _proj(hx, attn_w_in[0], 0, ATTN_IN, tm=1024, tn=tn, tile_types=_tile_types(tn, kinds0),
               rope=_rope_tables(), q_scale=HEAD_DIM ** -0.5 * LOG2E)
    ckv = _proj(hc, attn_w_in[0], ATTN_WIDTH // tn, 2 * KV_WIDTH, tm=1024, tn=tn)
    og = _attention(px, ckv, attn_sink[0], batch)
    x2, hx = _out_proj(og, attn_w_out[0].astype(BF16), x2, part(0, 0, batch, 2),
                       next_norm=(norm_g[1], part(1, 0, batch, 0), part(1, 0, batch, 1)))

    kinds1 = ["c"] * (3 * w // LANES) + ["g"] * (w // LANES)
    tn1 = 512
    p = _proj(hx, hy_w_in[0], 0, 4 * w, tm=SEQ, tn=tn1, tile_types=_tile_types(tn1, kinds1),
              conv=(hy_conv_w[0], hy_conv_b[0].reshape(1, 3 * w)))
    fwd, inv, anti, phase = _fold_tables()
    sd = _filters_sd(hy_w1[0], hy_b1[0], hy_w2[0], hy_b2[0], hy_w3[0], hy_b3[0], hy_freq[0],
                     hy_bias_d[0])
    kspec = _filter_spectra2(sd, fwd, anti, phase)
    y1 = _conv_fwd2(p, 2, kspec, 0, fwd, anti, batch)
    z = _conv_inv2(y1, inv, anti, p, 0)
    y2 = _conv_fwd2(z, 0, kspec, 1, fwd, anti, batch)
    yg = _conv_inv2(y2, inv, anti, p, 1, g_blk0=3)
    out, = _out_proj(yg, hy_w_out[0].astype(BF16), x2, part(1, 0, batch, 2), final_g=final_g)
    return out.reshape(batch, seq, d)
```

```python
import functools
import math

import jax
import jax.numpy as jnp
import numpy as np
from jax import lax
from jax.experimental import pallas as pl
from jax.experimental.pallas import tpu as pltpu

F32 = jnp.float32
BF16 = jnp.bfloat16

D_MODEL = 2048
SEQ = 2048
CTX_LEN = 256
GRID_W = 64
HEAD_DIM = 128
N_HEADS = 16
N_KV_HEADS = 4
GQA_GROUP = 4
ATTN_WIDTH = 2048
KV_WIDTH = 512
ATTN_IN = 2 * ATTN_WIDTH + 2 * KV_WIDTH
WINDOW = 128
BLOCK = 128
ROPE_BASE = 10000.0
HYENA_WIDTH = 2048
FILTER_EMB = 33
FILTER_HIDDEN = 64
DECAY_FAST = 0.3
DECAY_SLOW = 1.5
DECAY_TARGET = 1e-2
WINDOW_SHIFT = 0.05
NORM_EPS = 1e-6
NEG_INF = -1e30

LANES = 128
MXU_COLS = 256
ACC_ROWS = 1024
ONES_ROWS = 16
LOG2E = math.log2(math.e)
FFT_N = 2 * SEQ
VMEM_LIMIT = 56 * 1024 * 1024


def _cparams(sem):
    return pltpu.CompilerParams(dimension_semantics=sem, vmem_limit_bytes=VMEM_LIMIT)


def _mods_kernel(c_ref, w_ref, b_ref, o_ref):
    c = c_ref[...]
    s = c * jax.nn.sigmoid(c)
    s_hi = s.astype(BF16)
    s_lo = (s - s_hi.astype(F32)).astype(BF16)
    lhs = jnp.concatenate([s_hi, s_lo], axis=0)
    r = jnp.dot(lhs, w_ref[0].astype(BF16), preferred_element_type=F32)
    o_ref[0] = r[:8] + r[8:] + b_ref[0]


def _mods(cc, ada_w, ada_b):
    depth, d, n = ada_w.shape
    tn = 1024
    return pl.pallas_call(
        _mods_kernel,
        grid=(depth, n // tn),
        in_specs=[
            pl.BlockSpec((8, d), lambda l, j: (0, 0)),
            pl.BlockSpec((1, d, tn), lambda l, j: (l, 0, j)),
            pl.BlockSpec((1, 1, tn), lambda l, j: (l, 0, j)),
        ],
        out_specs=pl.BlockSpec((1, 8, tn), lambda l, j: (l, 0, j)),
        out_shape=jax.ShapeDtypeStruct((depth, 8, n), F32),
        compiler_params=_cparams(("arbitrary", "arbitrary")),
        name="adaln_mods",
    )(cc, ada_w, ada_b.reshape(depth, 1, n))


def _rope_slab(t, cos, sin):
    lane = lax.broadcasted_iota(jnp.int32, t.shape, 1)
    first = (lane % 64) < 32
    partner = jnp.where(first, pltpu.roll(t, 96, 1), pltpu.roll(t, 32, 1))
    return t * cos + partner * sin


def _norm_mod_rows(x, g, mul, add):
    ms = jnp.mean(x * x, axis=-1, keepdims=True)
    return x * lax.rsqrt(ms + NORM_EPS) * g * mul + add


def _norm_mod_kernel(x_ref, g_ref, sh_ref, sc_ref, o_ref, *, row_chunk):
    g = g_ref[...]
    mul = 1.0 + sc_ref[0]
    add = sh_ref[0]
    for r in range(0, x_ref.shape[0], row_chunk):
        o_ref[r:r + row_chunk, :] = _norm_mod_rows(x_ref[r:r + row_chunk, :], g, mul, add).astype(o_ref.dtype)


def _norm_mod(x, g, shift, scale, *, tm, rows_per_mod):
    m, d = x.shape
    if shift.shape[0] > 1:
        mod_map = lambda i: ((i * tm) // rows_per_mod, 0, 0)
    else:
        mod_map = lambda i: (0, 0, 0)
    return pl.pallas_call(
        functools.partial(_norm_mod_kernel, row_chunk=min(tm, 256)),
        grid=(m // tm,),
        in_specs=[pl.BlockSpec((tm, d), lambda i: (i, 0)), pl.BlockSpec((1, d), lambda i: (0, 0)),
                  pl.BlockSpec((1, 1, d), mod_map), pl.BlockSpec((1, 1, d), mod_map)],
        out_specs=pl.BlockSpec((tm, d), lambda i: (i, 0)),
        out_shape=jax.ShapeDtypeStruct((m, d), BF16),
        compiler_params=_cparams(("arbitrary",)),
        name="norm_mod",
    )(x, g.reshape(1, d), shift, scale)


def _proj_kernel(*refs, tile_types, has_rope, has_conv, q_scale):
    if has_rope:
        hx_ref, w_ref, cos_ref, sin_ref, o_ref, wb_ref = refs
    elif has_conv:
        hx_ref, w_ref, cw_ref, cb_ref, o_ref, wb_ref = refs
    else:
        hx_ref, w_ref, o_ref, wb_ref = refs
    j = pl.program_id(0)
    tm, tn = o_ref.shape

    @pl.when(pl.program_id(1) == 0)
    def _():
        wb_ref[...] = w_ref[...].astype(BF16)

    w_ref = wb_ref

    def plain():
        o_ref[...] = jnp.dot(hx_ref[...], w_ref[...], preferred_element_type=F32).astype(o_ref.dtype)

    def chunk_dot(c0):
        pieces = [jnp.dot(hx_ref[r:r + ACC_ROWS, :], w_ref[:, c0:c0 + MXU_COLS],
                          preferred_element_type=F32) for r in range(0, tm, ACC_ROWS)]
        return pieces[0] if len(pieces) == 1 else jnp.concatenate(pieces, axis=0)

    def gated():
        for c0 in range(0, tn, MXU_COLS):
            half = 0.5 * chunk_dot(c0)
            o_ref[:, c0:c0 + MXU_COLS] = (half + half * jnp.tanh(half)).astype(o_ref.dtype)

    def short_conv():
        for c0 in range(0, tn, MXU_COLS):
            acc = chunk_dot(c0)
            for h in range(MXU_COLS // LANES):
                lo = c0 + h * LANES
                u = acc[:, h * LANES:(h + 1) * LANES]
                w3 = cw_ref[:, lo:lo + LANES]
                b = cb_ref[:, lo:lo + LANES]
                y = pltpu.roll(u, 1, 0) * w3[0:1] + u * w3[1:2] + pltpu.roll(u, tm - 1, 0) * w3[2:3] + b
                o_ref[:, lo:lo + LANES] = y.astype(o_ref.dtype)
                first = u[0:1] * w3[1:2] + u[1:2] * w3[2:3] + b
                last = u[tm - 2:tm - 1] * w3[0:1] + u[tm - 1:tm] * w3[1:2] + b
                o_ref[0:1, lo:lo + LANES] = first.astype(o_ref.dtype)
                o_ref[tm - 1:tm, lo:lo + LANES] = last.astype(o_ref.dtype)

    def roped(types):
        cos = cos_ref[...]
        sin = sin_ref[...]
        for c0 in range(0, tn, MXU_COLS):
            acc = jnp.dot(hx_ref[...], w_ref[:, c0:c0 + MXU_COLS], preferred_element_type=F32)
            for h in range(MXU_COLS // LANES):
                ty = types[c0 // LANES + h]
                slab = acc[:, h * LANES:(h + 1) * LANES]
                if ty is not None:
                    slab = _rope_slab(slab, cos, sin)
                    if ty == "q":
                        slab = slab * q_scale
                lo = c0 + h * LANES
                o_ref[:, lo:lo + LANES] = slab.astype(o_ref.dtype)

    if tile_types is None:
        plain()
        return

    groups = {}
    for t, types in enumerate(tile_types):
        groups.setdefault(tuple(types), []).append(t)
    for types, tiles in groups.items():
        cond = functools.reduce(jnp.logical_or, [j == t for t in tiles])
        if all(ty is None for ty in types):
            pl.when(cond)(plain)
        elif all(ty == "g" for ty in types):
            pl.when(cond)(gated)
        elif all(ty == "c" for ty in types):
            pl.when(cond)(short_conv)
        else:
            assert "g" not in types and "c" not in types
            pl.when(cond)(functools.partial(roped, types))


def _proj(hx, w, col_blk0, n, *, tm, tn, tile_types=None, rope=None, conv=None, q_scale=1.0):
    m, d = hx.shape
    in_specs = [
        pl.BlockSpec((tm, d), lambda j, i: (i, 0)),
        pl.BlockSpec((d, tn), lambda j, i: (0, col_blk0 + j)),
    ]
    args = [hx, w]
    scratch = [pltpu.VMEM((d, tn), BF16)]
    if rope is not None:
        cos, sin = rope
        seq_tiles = cos.shape[0] // tm
        in_specs += [pl.BlockSpec((tm, LANES), lambda j, i: (i % seq_tiles, 0))] * 2
        args += [cos, sin]
    if conv is not None:
        assert rope is None and tm == SEQ
        cw, cb = conv
        last = cw.shape[1] // tn - 1
        in_specs += [pl.BlockSpec((cw.shape[0], tn), lambda j, i: (0, jnp.minimum(j, last))),
                     pl.BlockSpec((1, tn), lambda j, i: (0, jnp.minimum(j, last)))]
        args += [cw, cb]
    return pl.pallas_call(
        functools.partial(_proj_kernel, tile_types=tile_types, has_rope=rope is not None,
                          has_conv=conv is not None, q_scale=q_scale),
        grid=(n // tn, m // tm),
        in_specs=in_specs,
        out_specs=pl.BlockSpec((tm, tn), lambda j, i: (i, j)),
        out_shape=jax.ShapeDtypeStruct((m, n), BF16),
        scratch_shapes=scratch,
        compiler_params=_cparams(("arbitrary", "arbitrary")),
        name="proj",
    )(*args)


def _attn_kernel(sink_ref, q_ref, k_ref, v_ref, g_ref, kc_ref, vc_ref, o_ref,
                 vt_ref, vct_ref, bias_ref):
    kh = pl.program_id(1)
    band = 3 * BLOCK
    cols = GQA_GROUP * BLOCK
    n_blocks = SEQ // BLOCK

    vt_ref[:HEAD_DIM, :] = v_ref[...].T
    vt_ref[HEAD_DIM:, :] = jnp.ones((ONES_ROWS, SEQ), BF16)
    vct_ref[:HEAD_DIM, :] = vc_ref[...].T
    vct_ref[HEAD_DIM:, :] = jnp.ones((ONES_ROWS, CTX_LEN), BF16)
    krow = lax.broadcasted_iota(jnp.int32, (band, cols), 0)
    qcol = lax.broadcasted_iota(jnp.int32, (band, cols), 1) % BLOCK
    for idx, off in enumerate((0, -BLOCK, -2 * BLOCK)):
        bias_ref[idx] = jnp.where(jnp.abs(krow - qcol + off) <= WINDOW, 0.0, NEG_INF)

    kc = kc_ref[...]
    vct = vct_ref[...]
    sink_row = jnp.concatenate(
        [jnp.full((1, BLOCK), sink_ref[kh * GQA_GROUP + h] * LOG2E, F32) for h in range(GQA_GROUP)],
        axis=1)

    def body(n, carry):
        q0 = pl.multiple_of(n * BLOCK, BLOCK)
        ks = pl.multiple_of(jnp.clip((n - 1) * BLOCK, 0, SEQ - band), BLOCK)
        bidx = jnp.where(n == 0, 0, jnp.where(n == n_blocks - 1, 2, 1))
        qs = q_ref[pl.ds(q0, BLOCK), :]
        q4t = jnp.concatenate([qs[:, h * LANES:(h + 1) * LANES].T for h in range(GQA_GROUP)],
                              axis=1)
        kb = k_ref[pl.ds(ks, band), :]
        s_loc = jnp.dot(kb, q4t, preferred_element_type=F32) + bias_ref[bidx]
        s_ctx = jnp.dot(kc, q4t, preferred_element_type=F32)
        m = jnp.maximum(jnp.maximum(jnp.max(s_loc, axis=0, keepdims=True),
                                    jnp.max(s_ctx, axis=0, keepdims=True)), sink_row)
        p_loc = jnp.exp2(s_loc - m).astype(BF16)
        p_ctx = jnp.exp2(s_ctx - m).astype(BF16)
        ox = (jnp.dot(vt_ref[:, pl.ds(ks, band)], p_loc, preferred_element_type=F32)
              + jnp.dot(vct, p_ctx, preferred_element_type=F32))
        den = ox[HEAD_DIM:HEAD_DIM + 1, :] + jnp.exp2(sink_row - m)
        ot = ox[:HEAD_DIM, :] * (1.0 / den)
        gs = g_ref[pl.ds(q0, BLOCK), :].astype(F32)
        for h in range(GQA_GROUP):
            oh = ot[:, h * LANES:(h + 1) * LANES].T * gs[:, h * LANES:(h + 1) * LANES]
            o_ref[pl.ds(q0, BLOCK), h * LANES:(h + 1) * LANES] = oh.astype(o_ref.dtype)
        return carry

    lax.fori_loop(0, n_blocks, body, 0, unroll=8)


def _attention(px, ckv, sink, batch):
    gw = GQA_GROUP * HEAD_DIM
    k_blk0 = ATTN_WIDTH // HEAD_DIM
    v_blk0 = (ATTN_WIDTH + KV_WIDTH) // HEAD_DIM
    g_blk0 = (ATTN_WIDTH + 2 * KV_WIDTH) // gw
    return pl.pallas_call(
        _attn_kernel,
        grid=(batch, N_KV_HEADS),
        in_specs=[
            pl.BlockSpec(memory_space=pltpu.SMEM),
            pl.BlockSpec((SEQ, gw), lambda b, h: (b, h)),
            pl.BlockSpec((SEQ, HEAD_DIM), lambda b, h: (b, k_blk0 + h)),
            pl.BlockSpec((SEQ, HEAD_DIM), lambda b, h: (b, v_blk0 + h)),
            pl.BlockSpec((SEQ, gw), lambda b, h: (b, g_blk0 + h)),
            pl.BlockSpec((CTX_LEN, HEAD_DIM), lambda b, h: (b, h)),
            pl.BlockSpec((CTX_LEN, HEAD_DIM), lambda b, h: (b, N_KV_HEADS + h)),
        ],
        out_specs=pl.BlockSpec((SEQ, gw), lambda b, h: (b, h)),
        out_shape=jax.ShapeDtypeStruct((batch * SEQ, ATTN_WIDTH), BF16),
        scratch_shapes=[pltpu.VMEM((HEAD_DIM + ONES_ROWS, SEQ), BF16),
                        pltpu.VMEM((HEAD_DIM + ONES_ROWS, CTX_LEN), BF16),
                        pltpu.VMEM((3, 3 * BLOCK, GQA_GROUP * BLOCK), F32)],
        compiler_params=_cparams(("arbitrary", "arbitrary")),
        name="banded_attention",
    )(sink, px, px, px, px, ckv, ckv)


def _out_proj_kernel(*refs, final):
    if final:
        a_ref, w_ref, x_ref, gate_ref, fg_ref, o_ref = refs
    else:
        a_ref, w_ref, x_ref, gate_ref, ng_ref, sh_ref, sc_ref, o_ref, hx_ref = refs
    acc = jnp.dot(a_ref[...], w_ref[...], preferred_element_type=F32)
    y = x_ref[...] + gate_ref[0] * acc
    if final:
        ms = jnp.mean(y * y, axis=-1, keepdims=True)
        y = y * lax.rsqrt(ms + NORM_EPS) * fg_ref[...]
    else:
        hx_ref[...] = _norm_mod_rows(y, ng_ref[...], 1.0 + sc_ref[0], sh_ref[0]).astype(hx_ref.dtype)
    o_ref[...] = y


def _out_proj(a, w, x, gate, *, final_g=None, next_norm=None, tm=512):
    m, d = x.shape
    kdim = a.shape[1]
    final = final_g is not None
    per_batch = lambda i: ((i * tm) // SEQ, 0, 0)
    row_tile = pl.BlockSpec((tm, d), lambda i: (i, 0))
    vec = pl.BlockSpec((1, d), lambda i: (0, 0))
    in_specs = [
        pl.BlockSpec((tm, kdim), lambda i: (i, 0)),
        pl.BlockSpec((kdim, d), lambda i: (0, 0)),
        row_tile,
        pl.BlockSpec((1, 1, d), per_batch),
    ]
    args = [a, w, x, gate]
    out_specs = [row_tile]
    out_shape = [jax.ShapeDtypeStruct((m, d), F32)]
    if final:
        in_specs.append(vec)
        args.append(final_g.reshape(1, d))
    else:
        ng, shift, scale = next_norm
        in_specs += [vec, pl.BlockSpec((1, 1, d), per_batch), pl.BlockSpec((1, 1, d), per_batch)]
        args += [ng.reshape(1, d), shift, scale]
        out_specs.append(row_tile)
        out_shape.append(jax.ShapeDtypeStruct((m, d), BF16))
    return pl.pallas_call(
        functools.partial(_out_proj_kernel, final=final),
        grid=(m // tm,),
        in_specs=in_specs,
        out_specs=out_specs,
        out_shape=out_shape,
        compiler_params=_cparams(("arbitrary",)),
        name="out_proj",
    )(*args)


def _short_conv_kernel(p_ref, w_ref, b_ref, o_ref):
    length = p_ref.shape[0]
    row = lax.broadcasted_iota(jnp.int32, (length, LANES), 0)
    for c in range(0, p_ref.shape[1], LANES):
        u = p_ref[:, c:c + LANES].astype(F32)
        w = w_ref[:, c:c + LANES]
        prev = jnp.where(row == 0, 0.0, pltpu.roll(u, 1, 0))
        nxt = jnp.where(row == length - 1, 0.0, pltpu.roll(u, length - 1, 0))
        y = prev * w[0:1] + u * w[1:2] + nxt * w[2:3] + b_ref[:, c:c + LANES]
        o_ref[:, c:c + LANES] = y.astype(o_ref.dtype)


def _short_conv(p, conv_w, conv_b, batch, *, tn=512):
    n = conv_w.shape[1]
    return pl.pallas_call(
        _short_conv_kernel,
        grid=(batch, n // tn),
        in_specs=[
            pl.BlockSpec((SEQ, tn), lambda b, j: (b, j)),
            pl.BlockSpec((3, tn), lambda b, j: (0, j)),
            pl.BlockSpec((1, tn), lambda b, j: (0, j)),
        ],
        out_specs=pl.BlockSpec((SEQ, tn), lambda b, j: (b, j)),
        out_shape=jax.ShapeDtypeStruct((batch * SEQ, n), BF16),
        compiler_params=_cparams(("arbitrary", "arbitrary")),
        name="short_conv",
    )(p, conv_w, conv_b.reshape(1, n))


def _filter_kernel(w1t_ref, w1c_ref, w1s_ref, b1_ref, w2_ref, b2_ref, fq_ref, w3_ref, b3_ref,
                   o_ref, hid_ref):
    j = pl.program_id(0)
    length = o_ref.shape[0]
    tn = o_ref.shape[1]
    hp = lax.Precision.HIGHEST

    @pl.when(j == 0)
    def _():
        n = lax.broadcasted_iota(jnp.int32, (length, LANES), 0).astype(F32)
        lane = lax.broadcasted_iota(jnp.int32, (length, LANES), 1)
        bands = (FILTER_EMB - 1) // 2
        fr_step = (bands - 1 - 1e-4) / (bands - 1)
        fr = jnp.where(lane < bands, 1e-4 + lane.astype(F32) * fr_step, 0.0)
        ang = (2.0 * math.pi * n / length) * fr
        t = n / (length - 1)
        fq = fq_ref[...]
        pre = (t * w1t_ref[...]
               + jnp.dot(jnp.cos(ang), w1c_ref[...], precision=hp, preferred_element_type=F32)
               + jnp.dot(-jnp.sin(ang), w1s_ref[...], precision=hp, preferred_element_type=F32)
               + b1_ref[...])
        hid = jnp.sin(fq * pre)
        hid = jnp.sin(fq * (jnp.dot(hid, w2_ref[...], precision=hp, preferred_element_type=F32)
                            + b2_ref[...]))
        hid_ref[...] = hid

    hf = jnp.dot(hid_ref[...], w3_ref[...], precision=hp, preferred_element_type=F32) + b3_ref[...]
    row = lax.broadcasted_iota(jnp.int32, (length, tn), 0)
    col = lax.broadcasted_iota(jnp.int32, (length, tn), 1) + j * tn
    chan = (col % HYENA_WIDTH).astype(F32)
    min_decay = math.log(DECAY_TARGET) / DECAY_SLOW
    max_decay = math.log(DECAY_TARGET) / DECAY_FAST
    delta = min_decay + chan * ((max_decay - min_decay) / (HYENA_WIDTH - 1))
    t = row.astype(F32) / (length - 1)
    filt = hf * (jnp.exp(-t * jnp.abs(delta)) + WINDOW_SHIFT)
    backward = (col // HYENA_WIDTH) % 2 == 1
    filt = jnp.where(backward & (row == 0), 0.0, filt)
    o_ref[...] = filt.astype(o_ref.dtype)


def _filters(w1, b1, w2, b2, w3, b3, freq, *, tn=1024):
    hpad = LANES - FILTER_HIDDEN
    bands = (FILTER_EMB - 1) // 2
    n = w3.shape[1]
    w1t = jnp.pad(w1[0:1], ((0, 0), (0, hpad)))
    w1c = jnp.pad(w1[1:1 + bands], ((0, LANES - bands), (0, hpad)))
    w1s = jnp.pad(w1[1 + bands:], ((0, LANES - bands), (0, hpad)))
    b1p = jnp.pad(b1.reshape(1, -1), ((0, 0), (0, hpad)))
    w2p = jnp.pad(w2, ((0, hpad), (0, hpad)))
    b2p = jnp.pad(b2.reshape(1, -1), ((0, 0), (0, hpad)))
    fqp = jnp.pad(freq.reshape(1, -1), ((0, 0), (0, hpad)))
    w3p = jnp.pad(w3, ((0, hpad), (0, 0)))
    small = lambda shape: pl.BlockSpec(shape, lambda j: (0, 0))
    return pl.pallas_call(
        _filter_kernel,
        grid=(n // tn,),
        in_specs=[small((1, LANES)), small((LANES, LANES)), small((LANES, LANES)), small((1, LANES)),
                  small((LANES, LANES)), small((1, LANES)), small((1, LANES)),
                  pl.BlockSpec((LANES, tn), lambda j: (0, j)),
                  pl.BlockSpec((1, tn), lambda j: (0, j))],
        out_specs=pl.BlockSpec((SEQ, tn), lambda j: (0, j)),
        out_shape=jax.ShapeDtypeStruct((SEQ, n), BF16),
        scratch_shapes=[pltpu.VMEM((SEQ, LANES), F32)],
        compiler_params=_cparams(("arbitrary",)),
        name="hyena_filters",
    )(w1t, w1c, w1s, b1p, w2p, b2p, fqp, w3p, b3.reshape(1, n))


def _dft_tables():
    idx = np.arange(SEQ, dtype=np.int64)
    ang = ((idx[:, None] * idx[None, :]) % FFT_N).astype(np.float64) * (2.0 * math.pi / FFT_N)
    cm = np.cos(ang)
    sm = np.sin(ang)
    sm[0, :] = np.where(idx % 2 == 0, 1.0, -1.0)
    as_bf16 = lambda a: jnp.asarray(a.astype(np.float32)).astype(BF16)
    return as_bf16(cm), as_bf16(sm), as_bf16(np.ascontiguousarray(sm.T))


def _spec_scale(i, tf, shape):
    row = lax.broadcasted_iota(jnp.int32, shape, 0) + i * tf
    is0 = row == 0
    return is0, jnp.where(is0, 1.0 / FFT_N, 2.0 / FFT_N)


def _filter_spec_kernel(hf_ref, hb_ref, cm_ref, sm_ref, ka_ref, kb_ref):
    i = pl.program_id(2)
    tf = cm_ref.shape[0]
    cm = cm_ref[...]
    sm = sm_ref[...]
    hf = hf_ref[...]
    hb = hb_ref[...]
    a_f = jnp.dot(cm, hf, preferred_element_type=F32)
    b_f = jnp.dot(sm, hf, preferred_element_type=F32)
    a_b = jnp.dot(cm, hb, preferred_element_type=F32)
    b_b = jnp.dot(sm, hb, preferred_element_type=F32)
    is0, scale = _spec_scale(i, tf, a_f.shape)
    ka_ref[0] = ((a_f + a_b) * scale).astype(ka_ref.dtype)
    kb_ref[0] = (jnp.where(is0, b_f + b_b, b_f - b_b) * scale).astype(kb_ref.dtype)


def _filter_spectra(filt, cm, sm, *, tf=512, tn=512):
    w = HYENA_WIDTH
    ct = w // tn
    out = jax.ShapeDtypeStruct((2, SEQ, w), BF16)
    return pl.pallas_call(
        _filter_spec_kernel,
        grid=(2, ct, SEQ // tf),
        in_specs=[
            pl.BlockSpec((SEQ, tn), lambda o, j, i: (0, 2 * o * ct + j)),
            pl.BlockSpec((SEQ, tn), lambda o, j, i: (0, (2 * o + 1) * ct + j)),
            pl.BlockSpec((tf, SEQ), lambda o, j, i: (i, 0)),
            pl.BlockSpec((tf, SEQ), lambda o, j, i: (i, 0)),
        ],
        out_specs=[pl.BlockSpec((1, tf, tn), lambda o, j, i: (o, i, j))] * 2,
        out_shape=[out, out],
        compiler_params=_cparams(("arbitrary", "arbitrary", "arbitrary")),
        name="filter_spectra",
    )(filt, filt, cm, sm)


def _conv_fwd_kernel(u_ref, cm_ref, sm_ref, ka_ref, kb_ref, ya_ref, yb_ref):
    i = pl.program_id(2)
    tf = cm_ref.shape[0]
    u = u_ref[...]
    a = jnp.dot(cm_ref[...], u, preferred_element_type=F32)
    b = jnp.dot(sm_ref[...], u, preferred_element_type=F32)
    ka = ka_ref[0].astype(F32)
    kb = kb_ref[0].astype(F32)
    is0, _ = _spec_scale(i, tf, a.shape)
    bkb = b * kb
    ya_ref[0] = (a * ka - jnp.where(is0, 0.0, bkb)).astype(ya_ref.dtype)
    yb_ref[0] = jnp.where(is0, bkb, a * kb + b * ka).astype(yb_ref.dtype)


def _conv_fwd(u, col_blk0, ka, kb, order, cm, sm, batch, *, tf=512, tn=1024):
    w = HYENA_WIDTH
    out = jax.ShapeDtypeStruct((batch, SEQ, w), BF16)
    c0 = col_blk0 * (w // tn)
    return pl.pallas_call(
        _conv_fwd_kernel,
        grid=(batch, w // tn, SEQ // tf),
        in_specs=[
            pl.BlockSpec((SEQ, tn), lambda b, j, i: (b, c0 + j)),
            pl.BlockSpec((tf, SEQ), lambda b, j, i: (i, 0)),
            pl.BlockSpec((tf, SEQ), lambda b, j, i: (i, 0)),
            pl.BlockSpec((1, tf, tn), lambda b, j, i: (order, i, j)),
            pl.BlockSpec((1, tf, tn), lambda b, j, i: (order, i, j)),
        ],
        out_specs=[pl.BlockSpec((1, tf, tn), lambda b, j, i: (b, i, j))] * 2,
        out_shape=[out, out],
        compiler_params=_cparams(("arbitrary", "arbitrary", "arbitrary")),
        name="long_conv_fwd",
    )(u, cm, sm, ka, kb)


def _conv_inv_kernel(*refs, gated):
    if gated:
        ya_ref, yb_ref, cm_ref, st_ref, u_ref, d_ref, x_ref, g_ref, o_ref = refs
    else:
        ya_ref, yb_ref, cm_ref, st_ref, u_ref, d_ref, x_ref, o_ref = refs
    y = (jnp.dot(cm_ref[...], ya_ref[0], preferred_element_type=F32)
         + jnp.dot(st_ref[...], yb_ref[0], preferred_element_type=F32))
    y = y + u_ref[...].astype(F32) * d_ref[...]
    y = y * x_ref[...].astype(F32)
    if gated:
        y = y * g_ref[...].astype(F32)
    o_ref[...] = y.astype(o_ref.dtype)


def _conv_inv(ya, yb, cm, st, u, u_blk0, d, x, x_blk0, g=None, g_blk0=0, *, tt=512, tn=1024):
    batch = ya.shape[0]
    w = HYENA_WIDTH
    ct = w // tn
    tiles = SEQ // tt
    gated = g is not None
    in_specs = [
        pl.BlockSpec((1, SEQ, tn), lambda b, j, i: (b, 0, j)),
        pl.BlockSpec((1, SEQ, tn), lambda b, j, i: (b, 0, j)),
        pl.BlockSpec((tt, SEQ), lambda b, j, i: (i, 0)),
        pl.BlockSpec((tt, SEQ), lambda b, j, i: (i, 0)),
        pl.BlockSpec((tt, tn), lambda b, j, i: (b * tiles + i, u_blk0 * ct + j)),
        pl.BlockSpec((1, tn), lambda b, j, i: (0, j)),
        pl.BlockSpec((tt, tn), lambda b, j, i: (b * tiles + i, x_blk0 * ct + j)),
    ]
    args = [ya, yb, cm, st, u, d.reshape(1, w), x]
    if gated:
        in_specs.append(pl.BlockSpec((tt, tn), lambda b, j, i: (b * tiles + i, g_blk0 * ct + j)))
        args.append(g)
    return pl.pallas_call(
        functools.partial(_conv_inv_kernel, gated=gated),
        grid=(batch, ct, tiles),
        in_specs=in_specs,
        out_specs=pl.BlockSpec((tt, tn), lambda b, j, i: (b * tiles + i, j)),
        out_shape=jax.ShapeDtypeStruct((batch * SEQ, w), BF16),
        compiler_params=_cparams(("arbitrary", "arbitrary", "arbitrary")),
        name="long_conv_inv",
    )(*args)


HALF = SEQ // 2
FLIP_BLOCK = 256
ROW_TILE = 512


def _fold_tables():
    t2 = 2 * np.arange(HALF, dtype=np.int64) + 1
    g = np.arange(HALF, dtype=np.int64)

    def tab(f):
        ang = ((f[:, None] * t2[None, :]) % (2 * FFT_N)).astype(np.float64) * (math.pi / FFT_N)
        return np.cos(ang), np.sin(ang)

    ce, se = tab(2 * g)
    co, so = tab(2 * g + 1)
    se[0, :] = np.where(np.arange(HALF) % 2 == 0, 1.0, -1.0)
    const = lambda a, dt: jnp.asarray(np.ascontiguousarray(a).astype(np.float32)).astype(dt)
    fwd = const(np.stack([ce, se, co, so]), BF16)
    inv = const(np.stack([np.hstack([ce.T, so.T]), np.hstack([se.T, co.T])]), BF16)
    anti = const(np.eye(FLIP_BLOCK)[::-1], BF16)
    theta = np.stack([2 * g, 2 * g + 1]).astype(np.float64) * (math.pi / FFT_N)
    phase = const(np.stack([np.cos(theta), np.sin(theta)])[..., None], F32)
    return fwd, inv, anti, phase


def _flip_rows(h, anti):
    nb = h.shape[0] // FLIP_BLOCK
    return jnp.concatenate(
        [jnp.dot(anti[...], h[(nb - 1 - a) * FLIP_BLOCK:(nb - a) * FLIP_BLOCK, :], preferred_element_type=F32)
         for a in range(nb)], axis=0)


def _fold(x_ref, cols, anti):
    x0 = x_ref[:HALF, cols].astype(F32)
    xr = _flip_rows(x_ref[HALF:, cols], anti)
    return (x0 + xr).astype(BF16), (x0 - xr).astype(BF16)


def _staggered(n, matmuls, finish):
    matmuls(0, 0)
    for t in range(1, n):
        matmuls(t, t % 2)
        finish(t - 1, (t - 1) % 2)
    finish(n - 1, (n - 1) % 2)


def _col_chunks(n):
    return [slice(c, c + MXU_COLS) for c in range(0, n, MXU_COLS)]


def _half_spectrum(tab_ref, xs, xa):
    dot = lambda k, v: jnp.dot(tab_ref[k], v, preferred_element_type=F32)
    return dot(0, xs), dot(1, xa), dot(2, xa), dot(3, xs)


def _short_conv_into(p_ref, w_ref, b_ref, dst_ref, cols):
    length = p_ref.shape[0]
    row = lax.broadcasted_iota(jnp.int32, (length, LANES), 0)
    for c in range(cols.start, cols.stop, LANES):
        u = p_ref[:, c:c + LANES].astype(F32)
        w = w_ref[:, c:c + LANES]
        prev = jnp.where(row == 0, 0.0, pltpu.roll(u, 1, 0))
        nxt = jnp.where(row == length - 1, 0.0, pltpu.roll(u, length - 1, 0))
        y = prev * w[0:1] + u * w[1:2] + nxt * w[2:3] + b_ref[:, c:c + LANES]
        dst_ref[:, c:c + LANES] = y.astype(dst_ref.dtype)


def _resident(shape):
    zeros = (0,) * len(shape)
    return pl.BlockSpec(shape, lambda *_: zeros, pipeline_mode=pl.Buffered(1))


def _filter_sd_kernel(w1t_ref, w1c_ref, w1s_ref, b1_ref, w2_ref, b2_ref, fq_ref,
                      w3f_ref, b3f_ref, w3b_ref, b3b_ref, d_ref, o_ref, hid_ref):
    first = (pl.program_id(0) == 0) & (pl.program_id(1) == 0)
    j = pl.program_id(1)
    length = o_ref.shape[1]
    tn = o_ref.shape[2]
    hp = lax.Precision.HIGHEST

    @pl.when(first)
    def _():
        n = lax.broadcasted_iota(jnp.int32, (length, LANES), 0).astype(F32)
        lane = lax.broadcasted_iota(jnp.int32, (length, LANES), 1)
        bands = (FILTER_EMB - 1) // 2
        fr_step = (bands - 1 - 1e-4) / (bands - 1)
        fr = jnp.where(lane < bands, 1e-4 + lane.astype(F32) * fr_step, 0.0)
        ang = (2.0 * math.pi * n / length) * fr
        t = n / (length - 1)
        fq = fq_ref[...]
        pre = (t * w1t_ref[...]
               + jnp.dot(jnp.cos(ang), w1c_ref[...], precision=hp, preferred_element_type=F32)
               + jnp.dot(-jnp.sin(ang), w1s_ref[...], precision=hp, preferred_element_type=F32)
               + b1_ref[...])
        hid = jnp.sin(fq * pre)
        hid = jnp.sin(fq * (jnp.dot(hid, w2_ref[...], precision=hp, preferred_element_type=F32)
                            + b2_ref[...]))
        hid_hi = hid.astype(BF16)
        hid_ref[0] = hid_hi
        hid_ref[1] = (hid - hid_hi.astype(F32)).astype(BF16)

    def dot3(w_ref):
        wf = w_ref[...]
        w_hi = wf.astype(BF16)
        w_lo = (wf - w_hi.astype(F32)).astype(BF16)
        return (jnp.dot(hid_ref[0], w_hi, preferred_element_type=F32)
                + jnp.dot(hid_ref[1], w_hi, preferred_element_type=F32)
                + jnp.dot(hid_ref[0], w_lo, preferred_element_type=F32))

    row = lax.broadcasted_iota(jnp.int32, (length, tn), 0)
    chan = (lax.broadcasted_iota(jnp.int32, (length, tn), 1) + j * tn).astype(F32)
    min_decay = math.log(DECAY_TARGET) / DECAY_SLOW
    max_decay = math.log(DECAY_TARGET) / DECAY_FAST
    delta = min_decay + chan * ((max_decay - min_decay) / (HYENA_WIDTH - 1))
    t = row.astype(F32) / (length - 1)
    window = jnp.exp(-t * jnp.abs(delta)) + WINDOW_SHIFT
    hf = (dot3(w3f_ref) + b3f_ref[...]) * window
    hb = (dot3(w3b_ref) + b3b_ref[...]) * window
    hf = jnp.where(row == 0, hf + d_ref[0], hf)
    hb = jnp.where(row == 0, 0.0, hb)
    o_ref[0] = (hf + hb).astype(o_ref.dtype)
    o_ref[1] = (hf - hb).astype(o_ref.dtype)


def _filters_sd(w1, b1, w2, b2, w3, b3, freq, bias_d, *, tn=512):
    hpad = LANES - FILTER_HIDDEN
    bands = (FILTER_EMB - 1) // 2
    w = HYENA_WIDTH
    ct = w // tn
    n = w3.shape[1]
    w1t = jnp.pad(w1[0:1], ((0, 0), (0, hpad)))
    w1c = jnp.pad(w1[1:1 + bands], ((0, LANES - bands), (0, hpad)))
    w1s = jnp.pad(w1[1 + bands:], ((0, LANES - bands), (0, hpad)))
    b1p = jnp.pad(b1.reshape(1, -1), ((0, 0), (0, hpad)))
    w2p = jnp.pad(w2, ((0, hpad), (0, hpad)))
    b2p = jnp.pad(b2.reshape(1, -1), ((0, 0), (0, hpad)))
    fqp = jnp.pad(freq.reshape(1, -1), ((0, 0), (0, hpad)))
    w3p = jnp.pad(w3, ((0, hpad), (0, 0)))
    b3r = b3.reshape(1, n)
    small = lambda shape: pl.BlockSpec(shape, lambda o, j: (0, 0))
    fwd_cols = lambda o, j: (0, 2 * o * ct + j)
    bwd_cols = lambda o, j: (0, (2 * o + 1) * ct + j)
    return pl.pallas_call(
        _filter_sd_kernel,
        grid=(2, ct),
        in_specs=[small((1, LANES)), small((LANES, LANES)), small((LANES, LANES)), small((1, LANES)),
                  small((LANES, LANES)), small((1, LANES)), small((1, LANES)),
                  pl.BlockSpec((LANES, tn), fwd_cols), pl.BlockSpec((1, tn), fwd_cols),
                  pl.BlockSpec((LANES, tn), bwd_cols), pl.BlockSpec((1, tn), bwd_cols),
                  pl.BlockSpec((1, 1, tn), lambda o, j: (o, 0, j))],
        out_specs=pl.BlockSpec((2, SEQ, tn), lambda o, j: (0, 0, o * ct + j)),
        out_shape=jax.ShapeDtypeStruct((2, SEQ, 2 * w), BF16),
        scratch_shapes=[pltpu.VMEM((2, SEQ, LANES), BF16)],
        compiler_params=_cparams(("arbitrary", "arbitrary")),
        name="hyena_filters",
    )(w1t, w1c, w1s, b1p, w2p, b2p, fqp, w3p, b3r, w3p, b3r, bias_d.reshape(2, 1, w))


def _filter_spec_kernel2(s_ref, d_ref, tab_ref, anti_ref, ph_ref, k_ref):
    anti = anti_ref
    is0 = lax.broadcasted_iota(jnp.int32, (HALF, MXU_COLS), 0) == 0
    scale_e = jnp.where(is0, 1.0 / FFT_N, 2.0 / FFT_N)
    cos_e, sin_e = ph_ref[0, 0], ph_ref[1, 0]
    cos_o, sin_o = ph_ref[0, 1], ph_ref[1, 1]
    for cols in _col_chunks(k_ref.shape[3]):
        sa_e, sb_e, sa_o, sb_o = _half_spectrum(tab_ref, *_fold(s_ref.at[0], cols, anti))
        da_e, db_e, da_o, db_o = _half_spectrum(tab_ref, *_fold(d_ref.at[0], cols, anti))
        k_ref[0, 0, :, cols] = ((sa_e * cos_e + sb_e * sin_e) * scale_e).astype(k_ref.dtype)
        k_ref[0, 1, :, cols] = (jnp.where(is0, sb_e, db_e * cos_e - da_e * sin_e)
                                * scale_e).astype(k_ref.dtype)
        k_ref[0, 2, :, cols] = ((sa_o * cos_o + sb_o * sin_o) * (2.0 / FFT_N)).astype(k_ref.dtype)
        k_ref[0, 3, :, cols] = ((db_o * cos_o - da_o * sin_o) * (2.0 / FFT_N)).astype(k_ref.dtype)


def _filter_spectra2(sd, fwd, anti, phase, *, tn=512):
    w = HYENA_WIDTH
    ct = w // tn
    return pl.pallas_call(
        _filter_spec_kernel2,
        grid=(2, ct),
        in_specs=[
            pl.BlockSpec((1, SEQ, tn), lambda o, j: (0, 0, o * ct + j)),
            pl.BlockSpec((1, SEQ, tn), lambda o, j: (1, 0, o * ct + j)),
            _resident(fwd.shape), _resident(anti.shape), _resident(phase.shape),
        ],
        out_specs=pl.BlockSpec((1, 4, HALF, tn), lambda o, j: (o, 0, 0, j)),
        out_shape=jax.ShapeDtypeStruct((2, 4, HALF, w), BF16),
        compiler_params=_cparams(("arbitrary", "arbitrary")),
        name="filter_spectra",
    )(sd, sd, fwd, anti, phase)


def _conv_fwd_kernel2(x_ref, tab_ref, anti_ref, k_ref, y_ref):
    anti = anti_ref
    is0 = lax.broadcasted_iota(jnp.int32, (HALF, MXU_COLS), 0) == 0
    for cols in _col_chunks(y_ref.shape[3]):
        a_e, b_e, a_o, b_o = _half_spectrum(tab_ref, *_fold(x_ref, cols, anti))
        ka_e, kb_e, ka_o, kb_o = [k_ref[0, plane, :, cols].astype(F32) for plane in range(4)]
        bkb = b_e * kb_e
        y_ref[0, 0, :HALF, cols] = (a_e * ka_e - jnp.where(is0, 0.0, bkb)).astype(y_ref.dtype)
        y_ref[0, 1, :HALF, cols] = jnp.where(is0, bkb, a_e * kb_e + b_e * ka_e).astype(y_ref.dtype)
        y_ref[0, 1, HALF:, cols] = (a_o * ka_o - b_o * kb_o).astype(y_ref.dtype)
        y_ref[0, 0, HALF:, cols] = (a_o * kb_o + b_o * ka_o).astype(y_ref.dtype)


def _conv_fwd2(x, x_blk0, kspec, order, fwd, anti, batch, *, tn=512):
    w = HYENA_WIDTH
    ct = w // tn
    return pl.pallas_call(
        _conv_fwd_kernel2,
        grid=(batch, ct),
        in_specs=[pl.BlockSpec((SEQ, tn), lambda b, j: (b, x_blk0 * ct + j)),
                  _resident(fwd.shape), _resident(anti.shape),
                  pl.BlockSpec((1, 4, HALF, tn), lambda b, j: (order, 0, 0, j))],
        out_specs=pl.BlockSpec((1, 2, SEQ, tn), lambda b, j: (b, 0, 0, j)),
        out_shape=jax.ShapeDtypeStruct((batch, 2, SEQ, w), BF16),
        compiler_params=_cparams(("arbitrary", "arbitrary")),
        name="long_conv_fwd",
    )(x, fwd, anti, kspec)


def _conv_inv_kernel2(*refs, gated):
    if gated:
        y_ref, tab_ref, anti_ref, xm_ref, g_ref, o_ref, acc_ref = refs
    else:
        y_ref, tab_ref, anti_ref, xm_ref, o_ref, acc_ref = refs
    anti = anti_ref
    tiles = [(cols, r0) for cols in _col_chunks(o_ref.shape[1]) for r0 in range(0, HALF, ROW_TILE)]

    def matmuls(t, slot):
        cols, r0 = tiles[t]
        for k in range(2):
            acc_ref[slot, k] = jnp.dot(tab_ref[k, r0:r0 + ROW_TILE, :], y_ref[0, k, :, cols],
                                       preferred_element_type=F32)

    def finish(t, slot):
        cols, r0 = tiles[t]
        p1 = acc_ref[slot, 0]
        p2 = acc_ref[slot, 1]
        direct = slice(r0, r0 + ROW_TILE)
        mirrored = slice(SEQ - r0 - ROW_TILE, SEQ - r0)
        for rows, y in ((direct, p1 + p2), (mirrored, _flip_rows((p1 - p2).astype(BF16), anti))):
            out = y * xm_ref[rows, cols].astype(F32)
            if gated:
                out = out * g_ref[rows, cols].astype(F32)
            o_ref[rows, cols] = out.astype(o_ref.dtype)

    _staggered(len(tiles), matmuls, finish)


def _conv_inv2(y, inv, anti, p, m_blk0, g_blk0=None, *, tn=512):
    batch = y.shape[0]
    w = HYENA_WIDTH
    ct = w // tn
    gated = g_blk0 is not None
    col = lambda blk0: (lambda b, j: (b, blk0 * ct + j))
    in_specs = [
        pl.BlockSpec((1, 2, SEQ, tn), lambda b, j: (b, 0, 0, j)),
        _resident(inv.shape), _resident(anti.shape),
        pl.BlockSpec((SEQ, tn), col(m_blk0)),
    ]
    args = [y, inv, anti, p]
    if gated:
        in_specs.append(pl.BlockSpec((SEQ, tn), col(g_blk0)))
        args.append(p)
    return pl.pallas_call(
        functools.partial(_conv_inv_kernel2, gated=gated),
        grid=(batch, ct),
        in_specs=in_specs,
        out_specs=pl.BlockSpec((SEQ, tn), lambda b, j: (b, j)),
        out_shape=jax.ShapeDtypeStruct((batch * SEQ, w), BF16),
        scratch_shapes=[pltpu.VMEM((2, 2, ROW_TILE, MXU_COLS), F32)],
        compiler_params=_cparams(("arbitrary", "arbitrary")),
        name="long_conv_inv",
    )(*args)


def _rope_tables():
    pos = np.arange(SEQ)
    row = (pos // GRID_W).astype(np.float32)
    col = (pos % GRID_W).astype(np.float32)
    half = HEAD_DIM // 2
    inv = (ROPE_BASE ** (-np.arange(0, half, 2, dtype=np.float32) / half)).astype(np.float32)
    ar = row[:, None] * inv[None]
    ac = col[:, None] * inv[None]
    cos = np.concatenate([np.cos(ar), np.cos(ar), np.cos(ac), np.cos(ac)], axis=1)
    sin = np.concatenate([-np.sin(ar), np.sin(ar), -np.sin(ac), np.sin(ac)], axis=1)
    return jnp.asarray(cos, F32), jnp.asarray(sin, F32)


def _tile_types(tn, kinds):
    per_tile = tn // LANES
    return [kinds[s0:s0 + per_tile] for s0 in range(0, len(kinds), per_tile)]


def kernel(x, c, ctx, c_ctx, norm_g, ada_w, ada_b, attn_w_in, attn_w_out, attn_sink, hy_w_in,
           hy_conv_w, hy_conv_b, hy_w1, hy_b1, hy_w2, hy_b2, hy_w3, hy_b3, hy_freq, hy_bias_d,
           hy_w_out, final_g):
    batch, seq, d = x.shape
    assert (seq, d) == (SEQ, D_MODEL) and ctx.shape[1] == CTX_LEN
    assert norm_g.shape[0] == 2 and attn_w_in.shape[0] == 1 and hy_w_in.shape[0] == 1
    w = HYENA_WIDTH

    cc = jnp.concatenate([c, c_ctx[None], jnp.zeros((8 - batch - 1, d), F32)], axis=0)
    mods = _mods(cc, ada_w, ada_b)
    part = lambda layer, r0, r1, k: mods[layer, r0:r1, None, k * d:(k + 1) * d]

    x2 = x.reshape(batch * seq, d)
    ctx2 = ctx.reshape(batch * CTX_LEN, d)

    tn = 1024
    kinds0 = (["q"] * N_HEADS + ["k"] * N_KV_HEADS + [None] * N_KV_HEADS
              + ["g"] * (ATTN_WIDTH // LANES))
    hx = _norm_mod(x2, norm_g[0], part(0, 0, batch, 0), part(0, 0, batch, 1), tm=1024,
                   rows_per_mod=seq)
    hc = _norm_mod(ctx2, norm_g[0], part(0, batch, batch + 1, 0), part(0, batch, batch + 1, 1),
                   tm=512, rows_per_mod=CTX_LEN)
    px = _proj(hx, attn_w_in[0], 0, ATTN_IN, tm=1024, tn=tn, tile_types=_tile_types(tn, kinds0),
               rope=_rope_tables(), q_scale=HEAD_DIM ** -0.5 * LOG2E)
    ckv = _proj(hc, attn_w_in[0], ATTN_WIDTH // tn, 2 * KV_WIDTH, tm=1024, tn=tn)
    og = _attention(px, ckv, attn_sink[0], batch)
    x2, hx = _out_proj(og, attn_w_out[0].astype(BF16), x2, part(0, 0, batch, 2),
                       next_norm=(norm_g[1], part(1, 0, batch, 0), part(1, 0, batch, 1)))

    kinds1 = ["c"] * (3 * w // LANES) + ["g"] * (w // LANES)
    tn1 = 512
    p = _proj(hx, hy_w_in[0], 0, 4 * w, tm=SEQ, tn=tn1, tile_types=_tile_types(tn1, kinds1),
              conv=(hy_conv_w[0], hy_conv_b[0].reshape(1, 3 * w)))
    fwd, inv, anti, phase = _fold_tables()
    sd = _filters_sd(hy_w1[0], hy_b1[0], hy_w2[0], hy_b2[0], hy_w3[0], hy_b3[0], hy_freq[0],
                     hy_bias_d[0])
    kspec = _filter_spectra2(sd, fwd, anti, phase)
    y1 = _conv_fwd2(p, 2, kspec, 0, fwd, anti, batch)
    z = _conv_inv2(y1, inv, anti, p, 0)
    y2 = _conv_fwd2(z, 0, kspec, 1, fwd, anti, batch)
    yg = _conv_inv2(y2, inv, anti, p, 1, g_blk0=3)
    out, = _out_proj(yg, hy_w_out[0].astype(BF16), x2, part(1, 0, batch, 2), final_g=final_g)
    return out.reshape(batch, seq, d)
```

```python
import functools
import math

import jax
import jax.numpy as jnp
import numpy as np
from jax import lax
from jax.experimental import pallas as pl
from jax.experimental.pallas import tpu as pltpu

F32 = jnp.float32
BF16 = jnp.bfloat16

D_MODEL = 2048
SEQ = 2048
CTX_LEN = 256
GRID_W = 64
HEAD_DIM = 128
N_HEADS = 16
N_KV_HEADS = 4
GQA_GROUP = 4
ATTN_WIDTH = 2048
KV_WIDTH = 512
ATTN_IN = 2 * ATTN_WIDTH + 2 * KV_WIDTH
WINDOW = 128
BLOCK = 128
ROPE_BASE = 10000.0
HYENA_WIDTH = 2048
FILTER_EMB = 33
FILTER_HIDDEN = 64
DECAY_FAST = 0.3
DECAY_SLOW = 1.5
DECAY_TARGET = 1e-2
WINDOW_SHIFT = 0.05
NORM_EPS = 1e-6
NEG_INF = -1e30

LANES = 128
MXU_COLS = 256
ACC_ROWS = 1024
ONES_ROWS = 16
LOG2E = math.log2(math.e)
FFT_N = 2 * SEQ
VMEM_LIMIT = 56 * 1024 * 1024


def _cparams(sem):
    return pltpu.CompilerParams(dimension_semantics=sem, vmem_limit_bytes=VMEM_LIMIT)


def _mods_kernel(c_ref, w_ref, b_ref, o_ref):
    c = c_ref[...]
    s = c * jax.nn.sigmoid(c)
    s_hi = s.astype(BF16)
    s_lo = (s - s_hi.astype(F32)).astype(BF16)
    lhs = jnp.concatenate([s_hi, s_lo], axis=0)
    r = jnp.dot(lhs, w_ref[0].astype(BF16), preferred_element_type=F32)
    o_ref[0] = r[:8] + r[8:] + b_ref[0]


def _rope_slab(t, cos, sin):
    lane = lax.broadcasted_iota(jnp.int32, t.shape, 1)
    first = (lane % 64) < 32
    partner = jnp.where(first, pltpu.roll(t, 96, 1), pltpu.roll(t, 32, 1))
    return t * cos + partner * sin


def _norm_mod_rows(x, g, mul, add):
    ms = jnp.mean(x * x, axis=-1, keepdims=True)
    return x * lax.rsqrt(ms + NORM_EPS) * g * mul + add


def _norm_mod_kernel(x_ref, g_ref, sh_ref, sc_ref, o_ref, *, row_chunk):
    g = g_ref[...]
    mul = 1.0 + sc_ref[0]
    add = sh_ref[0]
    for r in range(0, x_ref.shape[0], row_chunk):
        o_ref[r:r + row_chunk, :] = _norm_mod_rows(x_ref[r:r + row_chunk, :], g, mul, add).astype(o_ref.dtype)


def _norm_mod(x, g, shift, scale, *, tm, rows_per_mod):
    m, d = x.shape
    if shift.shape[0] > 1:
        mod_map = lambda i: ((i * tm) // rows_per_mod, 0, 0)
    else:
        mod_map = lambda i: (0, 0, 0)
    return pl.pallas_call(
        functools.partial(_norm_mod_kernel, row_chunk=min(tm, 256)),
        grid=(m // tm,),
        in_specs=[pl.BlockSpec((tm, d), lambda i: (i, 0)), pl.BlockSpec((1, d), lambda i: (0, 0)),
                  pl.BlockSpec((1, 1, d), mod_map), pl.BlockSpec((1, 1, d), mod_map)],
        out_specs=pl.BlockSpec((tm, d), lambda i: (i, 0)),
        out_shape=jax.ShapeDtypeStruct((m, d), BF16),
        compiler_params=_cparams(("arbitrary",)),
        name="norm_mod",
    )(x, g.reshape(1, d), shift, scale)


def _proj_kernel(*refs, tile_types, has_rope, has_conv, q_scale):
    if has_rope:
        hx_ref, w_ref, cos_ref, sin_ref, o_ref, wb_ref = refs
    elif has_conv:
        hx_ref, w_ref, cw_ref, cb_ref, o_ref, wb_ref = refs
    else:
        hx_ref, w_ref, o_ref, wb_ref = refs
    j = pl.program_id(0)
    tm, tn = o_ref.shape

    @pl.when(pl.program_id(1) == 0)
    def _():
        wb_ref[...] = w_ref[...].astype(BF16)

    w_ref = wb_ref

    def plain():
        o_ref[...] = jnp.dot(hx_ref[...], w_ref[...], preferred_element_type=F32).astype(o_ref.dtype)

    def chunk_dot(c0):
        pieces = [jnp.dot(hx_ref[r:r + ACC_ROWS, :], w_ref[:, c0:c0 + MXU_COLS],
                          preferred_element_type=F32) for r in range(0, tm, ACC_ROWS)]
        return pieces[0] if len(pieces) == 1 else jnp.concatenate(pieces, axis=0)

    def gated():
        for c0 in range(0, tn, MXU_COLS):
            half = 0.5 * chunk_dot(c0)
            o_ref[:, c0:c0 + MXU_COLS] = (half + half * jnp.tanh(half)).astype(o_ref.dtype)

    def short_conv():
        for c0 in range(0, tn, MXU_COLS):
            acc = chunk_dot(c0)
            for h in range(MXU_COLS // LANES):
                lo = c0 + h * LANES
                u = acc[:, h * LANES:(h + 1) * LANES]
                w3 = cw_ref[:, lo:lo + LANES]
                b = cb_ref[:, lo:lo + LANES]
                y = pltpu.roll(u, 1, 0) * w3[0:1] + u * w3[1:2] + pltpu.roll(u, tm - 1, 0) * w3[2:3] + b
                o_ref[:, lo:lo + LANES] = y.astype(o_ref.dtype)
                first = u[0:1] * w3[1:2] + u[1:2] * w3[2:3] + b
                last = u[tm - 2:tm - 1] * w3[0:1] + u[tm - 1:tm] * w3[1:2] + b
                o_ref[0:1, lo:lo + LANES] = first.astype(o_ref.dtype)
                o_ref[tm - 1:tm, lo:lo + LANES] = last.astype(o_ref.dtype)

    def roped(types):
        cos = cos_ref[...]
        sin = sin_ref[...]
        for c0 in range(0, tn, MXU_COLS):
            acc = jnp.dot(hx_ref[...], w_ref[:, c0:c0 + MXU_COLS], preferred_element_type=F32)
            for h in range(MXU_COLS // LANES):
                ty = types[c0 // LANES + h]
                slab = acc[:, h * LANES:(h + 1) * LANES]
                if ty is not None:
                    slab = _rope_slab(slab, cos, sin)
                    if ty == "q":
                        slab = slab * q_scale
                lo = c0 + h * LANES
                o_ref[:, lo:lo + LANES] = slab.astype(o_ref.dtype)

    if tile_types is None:
        plain()
        return

    groups = {}
    for t, types in enumerate(tile_types):
        groups.setdefault(tuple(types), []).append(t)
    for types, tiles in groups.items():
        cond = functools.reduce(jnp.logical_or, [j == t for t in tiles])
        if all(ty is None for ty in types):
            pl.when(cond)(plain)
        elif all(ty == "g" for ty in types):
            pl.when(cond)(gated)
        elif all(ty == "c" for ty in types):
            pl.when(cond)(short_conv)
        else:
            assert "g" not in types and "c" not in types
            pl.when(cond)(functools.partial(roped, types))


def _proj(hx, w, col_blk0, n, *, tm, tn, tile_types=None, rope=None, conv=None, q_scale=1.0):
    m, d = hx.shape
    in_specs = [
        pl.BlockSpec((tm, d), lambda j, i: (i, 0)),
        pl.BlockSpec((d, tn), lambda j, i: (0, col_blk0 + j)),
    ]
    args = [hx, w]
    scratch = [pltpu.VMEM((d, tn), BF16)]
    if rope is not None:
        cos, sin = rope
        seq_tiles = cos.shape[0] // tm
        in_specs += [pl.BlockSpec((tm, LANES), lambda j, i: (i % seq_tiles, 0))] * 2
        args += [cos, sin]
    if conv is not None:
        assert rope is None and tm == SEQ
        cw, cb = conv
        last = cw.shape[1] // tn - 1
        in_specs += [pl.BlockSpec((cw.shape[0], tn), lambda j, i: (0, jnp.minimum(j, last))),
                     pl.BlockSpec((1, tn), lambda j, i: (0, jnp.minimum(j, last)))]
        args += [cw, cb]
    return pl.pallas_call(
        functools.partial(_proj_kernel, tile_types=tile_types, has_rope=rope is not None,
                          has_conv=conv is not None, q_scale=q_scale),
        grid=(n // tn, m // tm),
        in_specs=in_specs,
        out_specs=pl.BlockSpec((tm, tn), lambda j, i: (i, j)),
        out_shape=jax.ShapeDtypeStruct((m, n), BF16),
        scratch_shapes=scratch,
        compiler_params=_cparams(("arbitrary", "arbitrary")),
        name="proj",
    )(*args)


def _attn_kernel(sink_ref, q_ref, k_ref, v_ref, g_ref, kc_ref, vc_ref, o_ref,
                 vt_ref, vct_ref, bias_ref):
    kh = pl.program_id(1)
    band = 3 * BLOCK
    cols = GQA_GROUP * BLOCK
    n_blocks = SEQ // BLOCK

    vt_ref[:HEAD_DIM, :] = v_ref[...].T
    vct_ref[:HEAD_DIM, :] = vc_ref[...].T

    @pl.when((pl.program_id(0) == 0) & (kh == 0))
    def _():
        vt_ref[HEAD_DIM:, :] = jnp.ones((ONES_ROWS, SEQ), BF16)
        vct_ref[HEAD_DIM:, :] = jnp.ones((ONES_ROWS, CTX_LEN), BF16)
        krow = lax.broadcasted_iota(jnp.int32, (band, cols), 0)
        qcol = lax.broadcasted_iota(jnp.int32, (band, cols), 1) % BLOCK
        for idx, off in enumerate((0, -BLOCK, -2 * BLOCK)):
            bias_ref[idx] = jnp.where(jnp.abs(krow - qcol + off) <= WINDOW, 0.0, NEG_INF)

    kc = kc_ref[...]
    vct = vct_ref[...]
    sink_row = jnp.concatenate(
        [jnp.full((1, BLOCK), sink_ref[kh * GQA_GROUP + h] * LOG2E, F32) for h in range(GQA_GROUP)],
        axis=1)

    def body(n, carry):
        q0 = pl.multiple_of(n * BLOCK, BLOCK)
        ks = pl.multiple_of(jnp.clip((n - 1) * BLOCK, 0, SEQ - band), BLOCK)
        bidx = jnp.where(n == 0, 0, jnp.where(n == n_blocks - 1, 2, 1))
        qs = q_ref[pl.ds(q0, BLOCK), :]
        q4t = jnp.concatenate([qs[:, h * LANES:(h + 1) * LANES].T for h in range(GQA_GROUP)],
                              axis=1)
        kb = k_ref[pl.ds(ks, band), :]
        s_loc = jnp.dot(kb, q4t, preferred_element_type=F32) + bias_ref[bidx]
        s_ctx = jnp.dot(kc, q4t, preferred_element_type=F32)
        m = jnp.maximum(jnp.maximum(jnp.max(s_loc, axis=0, keepdims=True),
                                    jnp.max(s_ctx, axis=0, keepdims=True)), sink_row)
        p_loc = jnp.exp2(s_loc - m).astype(BF16)
        p_ctx = jnp.exp2(s_ctx - m).astype(BF16)
        ox = (jnp.dot(vt_ref[:, pl.ds(ks, band)], p_loc, preferred_element_type=F32)
              + jnp.dot(vct, p_ctx, preferred_element_type=F32))
        den = ox[HEAD_DIM:HEAD_DIM + 1, :] + jnp.exp2(sink_row - m)
        ot = ox[:HEAD_DIM, :] * (1.0 / den)
        gs = g_ref[pl.ds(q0, BLOCK), :].astype(F32)
        for h in range(GQA_GROUP):
            oh = ot[:, h * LANES:(h + 1) * LANES].T * gs[:, h * LANES:(h + 1) * LANES]
            o_ref[pl.ds(q0, BLOCK), h * LANES:(h + 1) * LANES] = oh.astype(o_ref.dtype)
        return carry

    lax.fori_loop(0, n_blocks, body, 0, unroll=8)


def _attention(px, ckv, sink, batch):
    gw = GQA_GROUP * HEAD_DIM
    k_blk0 = ATTN_WIDTH // HEAD_DIM
    v_blk0 = (ATTN_WIDTH + KV_WIDTH) // HEAD_DIM
    g_blk0 = (ATTN_WIDTH + 2 * KV_WIDTH) // gw
    return pl.pallas_call(
        _attn_kernel,
        grid=(batch, N_KV_HEADS),
        in_specs=[
            pl.BlockSpec(memory_space=pltpu.SMEM),
            pl.BlockSpec((SEQ, gw), lambda b, h: (b, h)),
            pl.BlockSpec((SEQ, HEAD_DIM), lambda b, h: (b, k_blk0 + h)),
            pl.BlockSpec((SEQ, HEAD_DIM), lambda b, h: (b, v_blk0 + h)),
            pl.BlockSpec((SEQ, gw), lambda b, h: (b, g_blk0 + h)),
            pl.BlockSpec((CTX_LEN, HEAD_DIM), lambda b, h: (b, h)),
            pl.BlockSpec((CTX_LEN, HEAD_DIM), lambda b, h: (b, N_KV_HEADS + h)),
        ],
        out_specs=pl.BlockSpec((SEQ, gw), lambda b, h: (b, h)),
        out_shape=jax.ShapeDtypeStruct((batch * SEQ, ATTN_WIDTH), BF16),
        scratch_shapes=[pltpu.VMEM((HEAD_DIM + ONES_ROWS, SEQ), BF16),
                        pltpu.VMEM((HEAD_DIM + ONES_ROWS, CTX_LEN), BF16),
                        pltpu.VMEM((3, 3 * BLOCK, GQA_GROUP * BLOCK), F32)],
        compiler_params=_cparams(("arbitrary", "arbitrary")),
        name="banded_attention",
    )(sink, px, px, px, px, ckv, ckv)


def _out_proj_kernel(*refs, final):
    if final:
        a_ref, w_ref, x_ref, gate_ref, fg_ref, o_ref = refs
    else:
        a_ref, w_ref, x_ref, gate_ref, ng_ref, sh_ref, sc_ref, o_ref, hx_ref = refs
    acc = jnp.dot(a_ref[...], w_ref[...], preferred_element_type=F32)
    y = x_ref[...] + gate_ref[0] * acc
    if final:
        ms = jnp.mean(y * y, axis=-1, keepdims=True)
        y = y * lax.rsqrt(ms + NORM_EPS) * fg_ref[...]
    else:
        hx_ref[...] = _norm_mod_rows(y, ng_ref[...], 1.0 + sc_ref[0], sh_ref[0]).astype(hx_ref.dtype)
    o_ref[...] = y


def _out_proj(a, w, x, gate, *, final_g=None, next_norm=None, tm=512):
    m, d = x.shape
    kdim = a.shape[1]
    final = final_g is not None
    per_batch = lambda i: ((i * tm) // SEQ, 0, 0)
    row_tile = pl.BlockSpec((tm, d), lambda i: (i, 0))
    vec = pl.BlockSpec((1, d), lambda i: (0, 0))
    in_specs = [
        pl.BlockSpec((tm, kdim), lambda i: (i, 0)),
        pl.BlockSpec((kdim, d), lambda i: (0, 0)),
        row_tile,
        pl.BlockSpec((1, 1, d), per_batch),
    ]
    args = [a, w, x, gate]
    out_specs = [row_tile]
    out_shape = [jax.ShapeDtypeStruct((m, d), F32)]
    if final:
        in_specs.append(vec)
        args.append(final_g.reshape(1, d))
    else:
        ng, shift, scale = next_norm
        in_specs += [vec, pl.BlockSpec((1, 1, d), per_batch), pl.BlockSpec((1, 1, d), per_batch)]
        args += [ng.reshape(1, d), shift, scale]
        out_specs.append(row_tile)
        out_shape.append(jax.ShapeDtypeStruct((m, d), BF16))
    return pl.pallas_call(
        functools.partial(_out_proj_kernel, final=final),
        grid=(m // tm,),
        in_specs=in_specs,
        out_specs=out_specs,
        out_shape=out_shape,
        compiler_params=_cparams(("arbitrary",)),
        name="out_proj",
    )(*args)


HALF = SEQ // 2
FLIP_BLOCK = 256
ROW_TILE = 512


def _fold_tables():
    t2 = 2 * np.arange(HALF, dtype=np.int64) + 1
    g = np.arange(HALF, dtype=np.int64)

    def tab(f):
        ang = ((f[:, None] * t2[None, :]) % (2 * FFT_N)).astype(np.float64) * (math.pi / FFT_N)
        return np.cos(ang), np.sin(ang)

    ce, se = tab(2 * g)
    co, so = tab(2 * g + 1)
    se[0, :] = np.where(np.arange(HALF) % 2 == 0, 1.0, -1.0)
    const = lambda a, dt: jnp.asarray(np.ascontiguousarray(a).astype(np.float32)).astype(dt)
    fwd = const(np.stack([ce, se, co, so]), BF16)
    inv = const(np.stack([np.hstack([ce.T, so.T]), np.hstack([se.T, co.T])]), BF16)
    anti = const(np.eye(FLIP_BLOCK)[::-1], BF16)
    theta = np.stack([2 * g, 2 * g + 1]).astype(np.float64) * (math.pi / FFT_N)
    phase = const(np.stack([np.cos(theta), np.sin(theta)])[..., None], F32)
    return fwd, inv, anti, phase


def _flip_rows(h, anti):
    nb = h.shape[0] // FLIP_BLOCK
    return jnp.concatenate(
        [jnp.dot(anti[...], h[(nb - 1 - a) * FLIP_BLOCK:(nb - a) * FLIP_BLOCK, :], preferred_element_type=F32)
         for a in range(nb)], axis=0)


def _fold(x_ref, cols, anti):
    x0 = x_ref[:HALF, cols].astype(F32)
    xr = _flip_rows(x_ref[HALF:, cols], anti)
    return (x0 + xr).astype(BF16), (x0 - xr).astype(BF16)


def _staggered(n, matmuls, finish):
    matmuls(0, 0)
    for t in range(1, n):
        matmuls(t, t % 2)
        finish(t - 1, (t - 1) % 2)
    finish(n - 1, (n - 1) % 2)


def _col_chunks(n):
    return [slice(c, c + MXU_COLS) for c in range(0, n, MXU_COLS)]


def _half_spectrum(tab_ref, xs, xa):
    dot = lambda k, v: jnp.dot(tab_ref[k], v, preferred_element_type=F32)
    return dot(0, xs), dot(1, xa), dot(2, xa), dot(3, xs)


def _resident(shape):
    zeros = (0,) * len(shape)
    return pl.BlockSpec(shape, lambda *_: zeros, pipeline_mode=pl.Buffered(1))


def _filter_sd_body(w1t_ref, w1c_ref, w1s_ref, b1_ref, w2_ref, b2_ref, fq_ref,
                    w3f_ref, b3f_ref, w3b_ref, b3b_ref, d_ref, o_ref, hid_ref, *, first, j):
    length = o_ref.shape[1]
    tn = o_ref.shape[2]
    hp = lax.Precision.HIGHEST

    @pl.when(first)
    def _():
        n = lax.broadcasted_iota(jnp.int32, (length, LANES), 0).astype(F32)
        lane = lax.broadcasted_iota(jnp.int32, (length, LANES), 1)
        bands = (FILTER_EMB - 1) // 2
        fr_step = (bands - 1 - 1e-4) / (bands - 1)
        fr = jnp.where(lane < bands, 1e-4 + lane.astype(F32) * fr_step, 0.0)
        ang = (2.0 * math.pi * n / length) * fr
        t = n / (length - 1)
        fq = fq_ref[...]
        pre = (t * w1t_ref[...]
               + jnp.dot(jnp.cos(ang), w1c_ref[...], precision=hp, preferred_element_type=F32)
               + jnp.dot(-jnp.sin(ang), w1s_ref[...], precision=hp, preferred_element_type=F32)
               + b1_ref[...])
        hid = jnp.sin(fq * pre)
        hid = jnp.sin(fq * (jnp.dot(hid, w2_ref[...], precision=hp, preferred_element_type=F32)
                            + b2_ref[...]))
        hid_hi = hid.astype(BF16)
        hid_ref[0] = hid_hi
        hid_ref[1] = (hid - hid_hi.astype(F32)).astype(BF16)

    def dot3(w_ref):
        wf = w_ref[...]
        w_hi = wf.astype(BF16)
        w_lo = (wf - w_hi.astype(F32)).astype(BF16)
        return (jnp.dot(hid_ref[0], w_hi, preferred_element_type=F32)
                + jnp.dot(hid_ref[1], w_hi, preferred_element_type=F32)
                + jnp.dot(hid_ref[0], w_lo, preferred_element_type=F32))

    row = lax.broadcasted_iota(jnp.int32, (length, tn), 0)
    chan = (lax.broadcasted_iota(jnp.int32, (length, tn), 1) + j * tn).astype(F32)
    min_decay = math.log(DECAY_TARGET) / DECAY_SLOW
    max_decay = math.log(DECAY_TARGET) / DECAY_FAST
    delta = min_decay + chan * ((max_decay - min_decay) / (HYENA_WIDTH - 1))
    t = row.astype(F32) / (length - 1)
    window = jnp.exp(-t * jnp.abs(delta)) + WINDOW_SHIFT
    hf = (dot3(w3f_ref) + b3f_ref[...]) * window
    hb = (dot3(w3b_ref) + b3b_ref[...]) * window
    hf = jnp.where(row == 0, hf + d_ref[0], hf)
    hb = jnp.where(row == 0, 0.0, hb)
    o_ref[0] = (hf + hb).astype(o_ref.dtype)
    o_ref[1] = (hf - hb).astype(o_ref.dtype)


def _prep_kernel(c_ref, aw_ref, ab_ref, *refs, filter_steps, ct):
    mods_ref, sd_ref, hid_ref = refs[-3:]
    s = pl.program_id(0)
    _mods_kernel(c_ref, aw_ref, ab_ref, mods_ref)

    @pl.when(s < filter_steps)
    def _():
        _filter_sd_body(*refs[:-3], sd_ref, hid_ref, first=s == 0, j=s % ct)


def _prep(cc, ada_w, ada_b, w1, b1, w2, b2, w3, b3, freq, bias_d, *, tn_mods=1024, tn=512):
    depth, d, n_mods = ada_w.shape
    mt = n_mods // tn_mods
    hpad = LANES - FILTER_HIDDEN
    bands = (FILTER_EMB - 1) // 2
    w = HYENA_WIDTH
    ct = w // tn
    filter_steps = 2 * ct
    assert filter_steps <= depth * mt
    n = w3.shape[1]
    w1t = jnp.pad(w1[0:1], ((0, 0), (0, hpad)))
    w1c = jnp.pad(w1[1:1 + bands], ((0, LANES - bands), (0, hpad)))
    w1s = jnp.pad(w1[1 + bands:], ((0, LANES - bands), (0, hpad)))
    b1p = jnp.pad(b1.reshape(1, -1), ((0, 0), (0, hpad)))
    w2p = jnp.pad(w2, ((0, hpad), (0, hpad)))
    b2p = jnp.pad(b2.reshape(1, -1), ((0, 0), (0, hpad)))
    fqp = jnp.pad(freq.reshape(1, -1), ((0, 0), (0, hpad)))
    w3p = jnp.pad(w3, ((0, hpad), (0, 0)))
    b3r = b3.reshape(1, n)
    small = lambda shape: pl.BlockSpec(shape, lambda s: (0, 0))
    order = lambda s: jnp.minimum(s, filter_steps - 1) // ct
    chan = lambda s: jnp.minimum(s, filter_steps - 1) % ct
    fwd_cols = lambda s: (0, 2 * order(s) * ct + chan(s))
    bwd_cols = lambda s: (0, (2 * order(s) + 1) * ct + chan(s))
    return pl.pallas_call(
        functools.partial(_prep_kernel, filter_steps=filter_steps, ct=ct),
        grid=(depth * mt,),
        in_specs=[pl.BlockSpec((8, d), lambda s: (0, 0)),
                  pl.BlockSpec((1, d, tn_mods), lambda s: (s // mt, 0, s % mt)),
                  pl.BlockSpec((1, 1, tn_mods), lambda s: (s // mt, 0, s % mt)),
                  small((1, LANES)), small((LANES, LANES)), small((LANES, LANES)), small((1, LANES)),
                  small((LANES, LANES)), small((1, LANES)), small((1, LANES)),
                  pl.BlockSpec((LANES, tn), fwd_cols), pl.BlockSpec((1, tn), fwd_cols),
                  pl.BlockSpec((LANES, tn), bwd_cols), pl.BlockSpec((1, tn), bwd_cols),
                  pl.BlockSpec((1, 1, tn), lambda s: (order(s), 0, chan(s)))],
        out_specs=[pl.BlockSpec((1, 8, tn_mods), lambda s: (s // mt, 0, s % mt)),
                   pl.BlockSpec((2, SEQ, tn), lambda s: (0, 0, order(s) * ct + chan(s)))],
        out_shape=[jax.ShapeDtypeStruct((depth, 8, n_mods), F32),
                   jax.ShapeDtypeStruct((2, SEQ, 2 * w), BF16)],
        scratch_shapes=[pltpu.VMEM((2, SEQ, LANES), BF16)],
        compiler_params=_cparams(("arbitrary",)),
        name="mods_and_filters",
    )(cc, ada_w, ada_b.reshape(depth, 1, n_mods),
      w1t, w1c, w1s, b1p, w2p, b2p, fqp, w3p, b3r, w3p, b3r, bias_d.reshape(2, 1, w))


def _filter_spec_kernel2(s_ref, d_ref, tab_ref, anti_ref, ph_ref, k_ref):
    anti = anti_ref
    is0 = lax.broadcasted_iota(jnp.int32, (HALF, MXU_COLS), 0) == 0
    scale_e = jnp.where(is0, 1.0 / FFT_N, 2.0 / FFT_N)
    cos_e, sin_e = ph_ref[0, 0], ph_ref[1, 0]
    cos_o, sin_o = ph_ref[0, 1], ph_ref[1, 1]
    for cols in _col_chunks(k_ref.shape[3]):
        sa_e, sb_e, sa_o, sb_o = _half_spectrum(tab_ref, *_fold(s_ref.at[0], cols, anti))
        da_e, db_e, da_o, db_o = _half_spectrum(tab_ref, *_fold(d_ref.at[0], cols, anti))
        k_ref[0, 0, :, cols] = ((sa_e * cos_e + sb_e * sin_e) * scale_e).astype(k_ref.dtype)
        k_ref[0, 1, :, cols] = (jnp.where(is0, sb_e, db_e * cos_e - da_e * sin_e)
                                * scale_e).astype(k_ref.dtype)
        k_ref[0, 2, :, cols] = ((sa_o * cos_o + sb_o * sin_o) * (2.0 / FFT_N)).astype(k_ref.dtype)
        k_ref[0, 3, :, cols] = ((db_o * cos_o - da_o * sin_o) * (2.0 / FFT_N)).astype(k_ref.dtype)


def _filter_spectra2(sd, fwd, anti, phase, *, tn=512):
    w = HYENA_WIDTH
    ct = w // tn
    return pl.pallas_call(
        _filter_spec_kernel2,
        grid=(2, ct),
        in_specs=[
            pl.BlockSpec((1, SEQ, tn), lambda o, j: (0, 0, o * ct + j)),
            pl.BlockSpec((1, SEQ, tn), lambda o, j: (1, 0, o * ct + j)),
            _resident(fwd.shape), _resident(anti.shape), _resident(phase.shape),
        ],
        out_specs=pl.BlockSpec((1, 4, HALF, tn), lambda o, j: (o, 0, 0, j)),
        out_shape=jax.ShapeDtypeStruct((2, 4, HALF, w), BF16),
        compiler_params=_cparams(("arbitrary", "arbitrary")),
        name="filter_spectra",
    )(sd, sd, fwd, anti, phase)


def _conv_fwd_kernel2(x_ref, tab_ref, anti_ref, k_ref, y_ref):
    anti = anti_ref
    is0 = lax.broadcasted_iota(jnp.int32, (HALF, MXU_COLS), 0) == 0
    for cols in _col_chunks(y_ref.shape[3]):
        a_e, b_e, a_o, b_o = _half_spectrum(tab_ref, *_fold(x_ref, cols, anti))
        ka_e, kb_e, ka_o, kb_o = [k_ref[0, plane, :, cols].astype(F32) for plane in range(4)]
        bkb = b_e * kb_e
        y_ref[0, 0, :HALF, cols] = (a_e * ka_e - jnp.where(is0, 0.0, bkb)).astype(y_ref.dtype)
        y_ref[0, 1, :HALF, cols] = jnp.where(is0, bkb, a_e * kb_e + b_e * ka_e).astype(y_ref.dtype)
        y_ref[0, 1, HALF:, cols] = (a_o * ka_o - b_o * kb_o).astype(y_ref.dtype)
        y_ref[0, 0, HALF:, cols] = (a_o * kb_o + b_o * ka_o).astype(y_ref.dtype)


def _conv_fwd2(x, x_blk0, kspec, order, fwd, anti, batch, *, tn=512):
    w = HYENA_WIDTH
    ct = w // tn
    return pl.pallas_call(
        _conv_fwd_kernel2,
        grid=(batch, ct),
        in_specs=[pl.BlockSpec((SEQ, tn), lambda b, j: (b, x_blk0 * ct + j)),
                  _resident(fwd.shape), _resident(anti.shape),
                  pl.BlockSpec((1, 4, HALF, tn), lambda b, j: (order, 0, 0, j))],
        out_specs=pl.BlockSpec((1, 2, SEQ, tn), lambda b, j: (b, 0, 0, j)),
        out_shape=jax.ShapeDtypeStruct((batch, 2, SEQ, w), BF16),
        compiler_params=_cparams(("arbitrary", "arbitrary")),
        name="long_conv_fwd",
    )(x, fwd, anti, kspec)


def _conv_inv_kernel2(*refs, gated):
    if gated:
        y_ref, tab_ref, anti_ref, xm_ref, g_ref, o_ref, acc_ref = refs
    else:
        y_ref, tab_ref, anti_ref, xm_ref, o_ref, acc_ref = refs
    anti = anti_ref
    tiles = [(cols, r0) for cols in _col_chunks(o_ref.shape[1]) for r0 in range(0, HALF, ROW_TILE)]

    def matmuls(t, slot):
        cols, r0 = tiles[t]
        for k in range(2):
            acc_ref[slot, k] = jnp.dot(tab_ref[k, r0:r0 + ROW_TILE, :], y_ref[0, k, :, cols],
                                       preferred_element_type=F32)

    def finish(t, slot):
        cols, r0 = tiles[t]
        p1 = acc_ref[slot, 0]
        p2 = acc_ref[slot, 1]
        direct = slice(r0, r0 + ROW_TILE)
        mirrored = slice(SEQ - r0 - ROW_TILE, SEQ - r0)
        for rows, y in ((direct, p1 + p2), (mirrored, _flip_rows((p1 - p2).astype(BF16), anti))):
            out = y * xm_ref[rows, cols].astype(F32)
            if gated:
                out = out * g_ref[rows, cols].astype(F32)
            o_ref[rows, cols] = out.astype(o_ref.dtype)

    _staggered(len(tiles), matmuls, finish)


def _conv_inv2(y, inv, anti, p, m_blk0, g_blk0=None, *, tn=512):
    batch = y.shape[0]
    w = HYENA_WIDTH
    ct = w // tn
    gated = g_blk0 is not None
    col = lambda blk0: (lambda b, j: (b, blk0 * ct + j))
    in_specs = [
        pl.BlockSpec((1, 2, SEQ, tn), lambda b, j: (b, 0, 0, j)),
        _resident(inv.shape), _resident(anti.shape),
        pl.BlockSpec((SEQ, tn), col(m_blk0)),
    ]
    args = [y, inv, anti, p]
    if gated:
        in_specs.append(pl.BlockSpec((SEQ, tn), col(g_blk0)))
        args.append(p)
    return pl.pallas_call(
        functools.partial(_conv_inv_kernel2, gated=gated),
        grid=(batch, ct),
        in_specs=in_specs,
        out_specs=pl.BlockSpec((SEQ, tn), lambda b, j: (b, j)),
        out_shape=jax.ShapeDtypeStruct((batch * SEQ, w), BF16),
        scratch_shapes=[pltpu.VMEM((2, 2, ROW_TILE, MXU_COLS), F32)],
        compiler_params=_cparams(("arbitrary", "arbitrary")),
        name="long_conv_inv",
    )(*args)


def _rope_tables():
    pos = np.arange(SEQ)
    row = (pos // GRID_W).astype(np.float32)
    col = (pos % GRID_W).astype(np.float32)
    half = HEAD_DIM // 2
    inv = (ROPE_BASE ** (-np.arange(0, half, 2, dtype=np.float32) / half)).astype(np.float32)
    ar = row[:, None] * inv[None]
    ac = col[:, None] * inv[None]
    cos = np.concatenate([np.cos(ar), np.cos(ar), np.cos(ac), np.cos(ac)], axis=1)
    sin = np.concatenate([-np.sin(ar), np.sin(ar), -np.sin(ac), np.sin(ac)], axis=1)
    return jnp.asarray(cos, F32), jnp.asarray(sin, F32)


def _tile_types(tn, kinds):
    per_tile = tn // LANES
    return [kinds[s0:s0 + per_tile] for s0 in range(0, len(kinds), per_tile)]


def kernel(x, c, ctx, c_ctx, norm_g, ada_w, ada_b, attn_w_in, attn_w_out, attn_sink, hy_w_in,
           hy_conv_w, hy_conv_b, hy_w1, hy_b1, hy_w2, hy_b2, hy_w3, hy_b3, hy_freq, hy_bias_d,
           hy_w_out, final_g):
    batch, seq, d = x.shape
    assert (seq, d) == (SEQ, D_MODEL) and ctx.shape[1] == CTX_LEN
    assert norm_g.shape[0] == 2 and attn_w_in.shape[0] == 1 and hy_w_in.shape[0] == 1
    w = HYENA_WIDTH

    cc = jnp.concatenate([c, c_ctx[None], jnp.zeros((8 - batch - 1, d), F32)], axis=0)
    mods, sd = _prep(cc, ada_w, ada_b, hy_w1[0], hy_b1[0], hy_w2[0], hy_b2[0], hy_w3[0], hy_b3[0],
                     hy_freq[0], hy_bias_d[0])
    part = lambda layer, r0, r1, k: mods[layer, r0:r1, None, k * d:(k + 1) * d]

    x2 = x.reshape(batch * seq, d)
    ctx2 = ctx.reshape(batch * CTX_LEN, d)

    tn = 1024
    kinds0 = (["q"] * N_HEADS + ["k"] * N_KV_HEADS + [None] * N_KV_HEADS
              + ["g"] * (ATTN_WIDTH // LANES))
    hx = _norm_mod(x2, norm_g[0], part(0, 0, batch, 0), part(0, 0, batch, 1), tm=1024,
                   rows_per_mod=seq)
    hc = _norm_mod(ctx2, norm_g[0], part(0, batch, batch + 1, 0), part(0, batch, batch + 1, 1),
                   tm=512, rows_per_mod=CTX_LEN)
    px = _proj(hx, attn_w_in[0], 0, ATTN_IN, tm=1024, tn=tn, tile_types=_tile_types(tn, kinds0),
               rope=_rope_tables(), q_scale=HEAD_DIM ** -0.5 * LOG2E)
    ckv = _proj(hc, attn_w_in[0], ATTN_WIDTH // tn, 2 * KV_WIDTH, tm=1024, tn=tn)
    og = _attention(px, ckv, attn_sink[0], batch)
    x2, hx = _out_proj(og, attn_w_out[0].astype(BF16), x2, part(0, 0, batch, 2),
                       next_norm=(norm_g[1], part(1, 0, batch, 0), part(1, 0, batch, 1)))

    kinds1 = ["c"] * (3 * w // LANES) + ["g"] * (w // LANES)
    tn1 = 512
    p = _proj(hx, hy_w_in[0], 0, 4 * w, tm=SEQ, tn=tn1, tile_types=_tile_types(tn1, kinds1),
              conv=(hy_conv_w[0], hy_conv_b[0].reshape(1, 3 * w)))
    fwd, inv, anti, phase = _fold_tables()
    kspec = _filter_spectra2(sd, fwd, anti, phase)
    y1 = _conv_fwd2(p, 2, kspec, 0, fwd, anti, batch)
    z = _conv_inv2(y1, inv, anti, p, 0)
    y2 = _conv_fwd2(z, 0, kspec, 1, fwd, anti, batch)
    yg = _conv_inv2(y2, inv, anti, p, 1, g_blk0=3)
    out, = _out_proj(yg, hy_w_out[0].astype(BF16), x2, part(1, 0, batch, 2), final_g=final_g)
    return out.reshape(batch, seq, d)
```

```python
import functools
import math

import jax
import jax.numpy as jnp
import numpy as np
from jax import lax
from jax.experimental import pallas as pl
from jax.experimental.pallas import tpu as pltpu

F32 = jnp.float32
BF16 = jnp.bfloat16

D_MODEL = 2048
SEQ = 2048
CTX_LEN = 256
GRID_W = 64
HEAD_DIM = 128
N_HEADS = 16
N_KV_HEADS = 4
GQA_GROUP = 4
ATTN_WIDTH = 2048
KV_WIDTH = 512
ATTN_IN = 2 * ATTN_WIDTH + 2 * KV_WIDTH
WINDOW = 128
BLOCK = 128
ROPE_BASE = 10000.0
HYENA_WIDTH = 2048
FILTER_EMB = 33
FILTER_HIDDEN = 64
DECAY_FAST = 0.3
DECAY_SLOW = 1.5
DECAY_TARGET = 1e-2
WINDOW_SHIFT = 0.05
NORM_EPS = 1e-6
NEG_INF = -1e30

LANES = 128
MXU_COLS = 256
ACC_ROWS = 1024
ONES_ROWS = 16
LOG2E = math.log2(math.e)
FFT_N = 2 * SEQ
VMEM_LIMIT = 56 * 1024 * 1024


def _cparams(sem):
    return pltpu.CompilerParams(dimension_semantics=sem, vmem_limit_bytes=VMEM_LIMIT)


def _mods_kernel(c_ref, w_ref, b_ref, o_ref):
    c = c_ref[...]
    s = c * jax.nn.sigmoid(c)
    s_hi = s.astype(BF16)
    s_lo = (s - s_hi.astype(F32)).astype(BF16)
    lhs = jnp.concatenate([s_hi, s_lo], axis=0)
    r = jnp.dot(lhs, w_ref[0].astype(BF16), preferred_element_type=F32)
    o_ref[0] = r[:8] + r[8:] + b_ref[0]


def _rope_slab(t, cos, sin):
    lane = lax.broadcasted_iota(jnp.int32, t.shape, 1)
    first = (lane % 64) < 32
    partner = jnp.where(first, pltpu.roll(t, 96, 1), pltpu.roll(t, 32, 1))
    return t * cos + partner * sin


def _norm_mod_rows(x, g, mul, add):
    ms = jnp.mean(x * x, axis=-1, keepdims=True)
    return x * lax.rsqrt(ms + NORM_EPS) * g * mul + add


def _norm_mod_kernel(x_ref, g_ref, sh_ref, sc_ref, o_ref, *, row_chunk):
    g = g_ref[...]
    mul = 1.0 + sc_ref[0]
    add = sh_ref[0]
    for r in range(0, x_ref.shape[0], row_chunk):
        o_ref[r:r + row_chunk, :] = _norm_mod_rows(x_ref[r:r + row_chunk, :], g, mul, add).astype(o_ref.dtype)


def _norm_mod(x, g, shift, scale, *, tm, rows_per_mod):
    m, d = x.shape
    if shift.shape[0] > 1:
        mod_map = lambda i: ((i * tm) // rows_per_mod, 0, 0)
    else:
        mod_map = lambda i: (0, 0, 0)
    return pl.pallas_call(
        functools.partial(_norm_mod_kernel, row_chunk=min(tm, 256)),
        grid=(m // tm,),
        in_specs=[pl.BlockSpec((tm, d), lambda i: (i, 0)), pl.BlockSpec((1, d), lambda i: (0, 0)),
                  pl.BlockSpec((1, 1, d), mod_map), pl.BlockSpec((1, 1, d), mod_map)],
        out_specs=pl.BlockSpec((tm, d), lambda i: (i, 0)),
        out_shape=jax.ShapeDtypeStruct((m, d), BF16),
        compiler_params=_cparams(("arbitrary",)),
        name="norm_mod",
    )(x, g.reshape(1, d), shift, scale)


def _proj_kernel(*refs, tile_types, has_rope, has_conv, q_scale):
    if has_rope:
        hx_ref, w_ref, cos_ref, sin_ref, o_ref, wb_ref = refs
    elif has_conv:
        hx_ref, w_ref, cw_ref, cb_ref, o_ref, wb_ref = refs
    else:
        hx_ref, w_ref, o_ref, wb_ref = refs
    j = pl.program_id(0)
    tm, tn = o_ref.shape

    @pl.when(pl.program_id(1) == 0)
    def _():
        wb_ref[...] = w_ref[...].astype(BF16)

    w_ref = wb_ref

    def plain():
        o_ref[...] = jnp.dot(hx_ref[...], w_ref[...], preferred_element_type=F32).astype(o_ref.dtype)

    def chunk_dot(c0):
        pieces = [jnp.dot(hx_ref[r:r + ACC_ROWS, :], w_ref[:, c0:c0 + MXU_COLS],
                          preferred_element_type=F32) for r in range(0, tm, ACC_ROWS)]
        return pieces[0] if len(pieces) == 1 else jnp.concatenate(pieces, axis=0)

    def gated():
        for c0 in range(0, tn, MXU_COLS):
            half = 0.5 * chunk_dot(c0)
            o_ref[:, c0:c0 + MXU_COLS] = (half + half * jnp.tanh(half)).astype(o_ref.dtype)

    def short_conv():
        for c0 in range(0, tn, MXU_COLS):
            acc = chunk_dot(c0)
            for h in range(MXU_COLS // LANES):
                lo = c0 + h * LANES
                u = acc[:, h * LANES:(h + 1) * LANES]
                w3 = cw_ref[:, lo:lo + LANES]
                b = cb_ref[:, lo:lo + LANES]
                y = pltpu.roll(u, 1, 0) * w3[0:1] + u * w3[1:2] + pltpu.roll(u, tm - 1, 0) * w3[2:3] + b
                o_ref[:, lo:lo + LANES] = y.astype(o_ref.dtype)
                first = u[0:1] * w3[1:2] + u[1:2] * w3[2:3] + b
                last = u[tm - 2:tm - 1] * w3[0:1] + u[tm - 1:tm] * w3[1:2] + b
                o_ref[0:1, lo:lo + LANES] = first.astype(o_ref.dtype)
                o_ref[tm - 1:tm, lo:lo + LANES] = last.astype(o_ref.dtype)

    def roped(types):
        cos = cos_ref[...]
        sin = sin_ref[...]
        for c0 in range(0, tn, MXU_COLS):
            acc = jnp.dot(hx_ref[...], w_ref[:, c0:c0 + MXU_COLS], preferred_element_type=F32)
            for h in range(MXU_COLS // LANES):
                ty = types[c0 // LANES + h]
                slab = acc[:, h * LANES:(h + 1) * LANES]
                if ty is not None:
                    slab = _rope_slab(slab, cos, sin)
                    if ty == "q":
                        slab = slab * q_scale
                lo = c0 + h * LANES
                o_ref[:, lo:lo + LANES] = slab.astype(o_ref.dtype)

    if tile_types is None:
        plain()
        return

    groups = {}
    for t, types in enumerate(tile_types):
        groups.setdefault(tuple(types), []).append(t)
    for types, tiles in groups.items():
        cond = functools.reduce(jnp.logical_or, [j == t for t in tiles])
        if all(ty is None for ty in types):
            pl.when(cond)(plain)
        elif all(ty == "g" for ty in types):
            pl.when(cond)(gated)
        elif all(ty == "c" for ty in types):
            pl.when(cond)(short_conv)
        else:
            assert "g" not in types and "c" not in types
            pl.when(cond)(functools.partial(roped, types))


def _proj(hx, w, col_blk0, n, *, tm, tn, tile_types=None, rope=None, conv=None, q_scale=1.0):
    m, d = hx.shape
    in_specs = [
        pl.BlockSpec((tm, d), lambda j, i: (i, 0)),
        pl.BlockSpec((d, tn), lambda j, i: (0, col_blk0 + j)),
    ]
    args = [hx, w]
    scratch = [pltpu.VMEM((d, tn), BF16)]
    if rope is not None:
        cos, sin = rope
        seq_tiles = cos.shape[0] // tm
        in_specs += [pl.BlockSpec((tm, LANES), lambda j, i: (i % seq_tiles, 0))] * 2
        args += [cos, sin]
    if conv is not None:
        assert rope is None and tm == SEQ
        cw, cb = conv
        last = cw.shape[1] // tn - 1
        in_specs += [pl.BlockSpec((cw.shape[0], tn), lambda j, i: (0, jnp.minimum(j, last))),
                     pl.BlockSpec((1, tn), lambda j, i: (0, jnp.minimum(j, last)))]
        args += [cw, cb]
    return pl.pallas_call(
        functools.partial(_proj_kernel, tile_types=tile_types, has_rope=rope is not None,
                          has_conv=conv is not None, q_scale=q_scale),
        grid=(n // tn, m // tm),
        in_specs=in_specs,
        out_specs=pl.BlockSpec((tm, tn), lambda j, i: (i, j)),
        out_shape=jax.ShapeDtypeStruct((m, n), BF16),
        scratch_shapes=scratch,
        compiler_params=_cparams(("arbitrary", "arbitrary")),
        name="proj",
    )(*args)


def _attn_kernel(sink_ref, q_ref, k_ref, v_ref, g_ref, kc_ref, vc_ref, o_ref,
                 vt_ref, vct_ref, bias_ref):
    kh = pl.program_id(1)
    band = 3 * BLOCK
    cols = GQA_GROUP * BLOCK
    n_blocks = SEQ // BLOCK

    vt_ref[:HEAD_DIM, :] = v_ref[...].T
    vct_ref[:HEAD_DIM, :] = vc_ref[...].T

    @pl.when((pl.program_id(0) == 0) & (kh == 0))
    def _():
        vt_ref[HEAD_DIM:, :] = jnp.ones((ONES_ROWS, SEQ), BF16)
        vct_ref[HEAD_DIM:, :] = jnp.ones((ONES_ROWS, CTX_LEN), BF16)
        krow = lax.broadcasted_iota(jnp.int32, (band, cols), 0)
        qcol = lax.broadcasted_iota(jnp.int32, (band, cols), 1) % BLOCK
        for idx, off in enumerate((0, -BLOCK, -2 * BLOCK)):
            bias_ref[idx] = jnp.where(jnp.abs(krow - qcol + off) <= WINDOW, 0.0, NEG_INF)

    kc = kc_ref[...]
    vct = vct_ref[...]
    sink_row = jnp.concatenate(
        [jnp.full((1, BLOCK), sink_ref[kh * GQA_GROUP + h] * LOG2E, F32) for h in range(GQA_GROUP)],
        axis=1)

    def body(n, carry):
        q0 = pl.multiple_of(n * BLOCK, BLOCK)
        ks = pl.multiple_of(jnp.clip((n - 1) * BLOCK, 0, SEQ - band), BLOCK)
        bidx = jnp.where(n == 0, 0, jnp.where(n == n_blocks - 1, 2, 1))
        qs = q_ref[pl.ds(q0, BLOCK), :]
        q4t = jnp.concatenate([qs[:, h * LANES:(h + 1) * LANES].T for h in range(GQA_GROUP)],
                              axis=1)
        kb = k_ref[pl.ds(ks, band), :]
        s_loc = jnp.dot(kb, q4t, preferred_element_type=F32) + bias_ref[bidx]
        s_ctx = jnp.dot(kc, q4t, preferred_element_type=F32)
        m = jnp.maximum(jnp.maximum(jnp.max(s_loc, axis=0, keepdims=True),
                                    jnp.max(s_ctx, axis=0, keepdims=True)), sink_row)
        p_loc = jnp.exp2(s_loc - m).astype(BF16)
        p_ctx = jnp.exp2(s_ctx - m).astype(BF16)
        ox = (jnp.dot(vt_ref[:, pl.ds(ks, band)], p_loc, preferred_element_type=F32)
              + jnp.dot(vct, p_ctx, preferred_element_type=F32))
        den = ox[HEAD_DIM:HEAD_DIM + 1, :] + jnp.exp2(sink_row - m)
        ot = ox[:HEAD_DIM, :] * (1.0 / den)
        gs = g_ref[pl.ds(q0, BLOCK), :].astype(F32)
        for h in range(GQA_GROUP):
            oh = ot[:, h * LANES:(h + 1) * LANES].T * gs[:, h * LANES:(h + 1) * LANES]
            o_ref[pl.ds(q0, BLOCK), h * LANES:(h + 1) * LANES] = oh.astype(o_ref.dtype)
        return carry

    lax.fori_loop(0, n_blocks, body, 0, unroll=8)


def _attention(px, ckv, sink, batch):
    gw = GQA_GROUP * HEAD_DIM
    k_blk0 = ATTN_WIDTH // HEAD_DIM
    v_blk0 = (ATTN_WIDTH + KV_WIDTH) // HEAD_DIM
    g_blk0 = (ATTN_WIDTH + 2 * KV_WIDTH) // gw
    return pl.pallas_call(
        _attn_kernel,
        grid=(batch, N_KV_HEADS),
        in_specs=[
            pl.BlockSpec(memory_space=pltpu.SMEM),
            pl.BlockSpec((SEQ, gw), lambda b, h: (b, h)),
            pl.BlockSpec((SEQ, HEAD_DIM), lambda b, h: (b, k_blk0 + h)),
            pl.BlockSpec((SEQ, HEAD_DIM), lambda b, h: (b, v_blk0 + h)),
            pl.BlockSpec((SEQ, gw), lambda b, h: (b, g_blk0 + h)),
            pl.BlockSpec((CTX_LEN, HEAD_DIM), lambda b, h: (b, h)),
            pl.BlockSpec((CTX_LEN, HEAD_DIM), lambda b, h: (b, N_KV_HEADS + h)),
        ],
        out_specs=pl.BlockSpec((SEQ, gw), lambda b, h: (b, h)),
        out_shape=jax.ShapeDtypeStruct((batch * SEQ, ATTN_WIDTH), BF16),
        scratch_shapes=[pltpu.VMEM((HEAD_DIM + ONES_ROWS, SEQ), BF16),
                        pltpu.VMEM((HEAD_DIM + ONES_ROWS, CTX_LEN), BF16),
                        pltpu.VMEM((3, 3 * BLOCK, GQA_GROUP * BLOCK), F32)],
        compiler_params=_cparams(("arbitrary", "arbitrary")),
        name="banded_attention",
    )(sink, px, px, px, px, ckv, ckv)


def _out_proj_kernel(*refs, final):
    if final:
        a_ref, w_ref, x_ref, gate_ref, fg_ref, o_ref = refs
    else:
        a_ref, w_ref, x_ref, gate_ref, ng_ref, sh_ref, sc_ref, o_ref, hx_ref = refs
    acc = jnp.dot(a_ref[...], w_ref[...], preferred_element_type=F32)
    y = x_ref[...] + gate_ref[0] * acc
    if final:
        ms = jnp.mean(y * y, axis=-1, keepdims=True)
        y = y * lax.rsqrt(ms + NORM_EPS) * fg_ref[...]
    else:
        hx_ref[...] = _norm_mod_rows(y, ng_ref[...], 1.0 + sc_ref[0], sh_ref[0]).astype(hx_ref.dtype)
    o_ref[...] = y


def _out_proj(a, w, x, gate, *, final_g=None, next_norm=None, tm=512):
    m, d = x.shape
    kdim = a.shape[1]
    final = final_g is not None
    per_batch = lambda i: ((i * tm) // SEQ, 0, 0)
    row_tile = pl.BlockSpec((tm, d), lambda i: (i, 0))
    vec = pl.BlockSpec((1, d), lambda i: (0, 0))
    in_specs = [
        pl.BlockSpec((tm, kdim), lambda i: (i, 0)),
        pl.BlockSpec((kdim, d), lambda i: (0, 0)),
        row_tile,
        pl.BlockSpec((1, 1, d), per_batch),
    ]
    args = [a, w, x, gate]
    out_specs = [row_tile]
    out_shape = [jax.ShapeDtypeStruct((m, d), F32)]
    if final:
        in_specs.append(vec)
        args.append(final_g.reshape(1, d))
    else:
        ng, shift, scale = next_norm
        in_specs += [vec, pl.BlockSpec((1, 1, d), per_batch), pl.BlockSpec((1, 1, d), per_batch)]
        args += [ng.reshape(1, d), shift, scale]
        out_specs.append(row_tile)
        out_shape.append(jax.ShapeDtypeStruct((m, d), BF16))
    return pl.pallas_call(
        functools.partial(_out_proj_kernel, final=final),
        grid=(m // tm,),
        in_specs=in_specs,
        out_specs=out_specs,
        out_shape=out_shape,
        compiler_params=_cparams(("arbitrary",)),
        name="out_proj",
    )(*args)


HALF = SEQ // 2
QUARTER = SEQ // 4
FLIP_BLOCK = 256


def _fold_tables():
    def tab(f, n):
        t2 = 2 * np.arange(n, dtype=np.int64) + 1
        ang = ((f[:, None] * t2[None, :]) % (2 * FFT_N)).astype(np.float64) * (math.pi / FFT_N)
        return np.cos(ang), np.sin(ang)

    r = np.arange(QUARTER, dtype=np.int64)
    g = np.arange(HALF, dtype=np.int64)
    f_ee, f_eo, f_o = 4 * r, 4 * r + 2, 2 * g + 1
    cee, see = tab(f_ee, QUARTER)
    ceo, seo = tab(f_eo, QUARTER)
    co, so = tab(f_o, HALF)
    see[0, :] = np.where(np.arange(QUARTER) % 2 == 0, 1.0, -1.0)
    const = lambda a, dt: jnp.asarray(np.ascontiguousarray(a).astype(np.float32)).astype(dt)
    phase = lambda f: np.stack([np.cos(f * (math.pi / FFT_N)), np.sin(f * (math.pi / FFT_N))])
    return dict(
        fq=const(np.stack([cee, see, ceo, seo]), BF16),
        fo=const(np.stack([co, so]), BF16),
        iq=const(np.stack([cee.T, ceo.T, see.T, seo.T]), BF16),
        io=const(np.stack([so.T, co.T]), BF16),
        anti=const(np.eye(FLIP_BLOCK)[::-1], BF16),
        pq=const(np.stack([phase(f_ee), phase(f_eo)], axis=1)[..., None], F32),
        po=const(phase(f_o)[..., None], F32),
    )


def _flip_rows(h, anti):
    nb = h.shape[0] // FLIP_BLOCK
    return jnp.concatenate(
        [jnp.dot(anti[...], h[(nb - 1 - a) * FLIP_BLOCK:(nb - a) * FLIP_BLOCK, :], preferred_element_type=F32)
         for a in range(nb)], axis=0)


def _fold(x, anti):
    half = x.shape[0] // 2
    x0 = x[:half].astype(F32)
    xr = _flip_rows(x[half:], anti)
    return (x0 + xr).astype(BF16), (x0 - xr).astype(BF16)


def _staggered(n, matmuls, finish):
    matmuls(0, 0)
    for t in range(1, n):
        matmuls(t, t % 2)
        finish(t - 1, (t - 1) % 2)
    finish(n - 1, (n - 1) % 2)


def _col_chunks(n):
    return [slice(c, c + MXU_COLS) for c in range(0, n, MXU_COLS)]


PLANES = {"a_ee": 0, "b_ee": QUARTER, "a_eo": 2 * QUARTER, "b_eo": 3 * QUARTER,
          "a_o": 2 * HALF, "b_o": 3 * HALF}
GROUPS = (("a_ee", "b_ee", QUARTER), ("a_eo", "b_eo", QUARTER), ("a_o", "b_o", HALF))


def _spectrum(fq_ref, fo_ref, x, anti):
    xs, xa = _fold(x, anti)
    xss, xsa = _fold(xs, anti)
    xas, xaa = _fold(xa, anti)
    dq = lambda k, v: jnp.dot(fq_ref[k], v, preferred_element_type=F32)
    do = lambda k, v: jnp.dot(fo_ref[k], v, preferred_element_type=F32)
    return {"a_ee": dq(0, xss), "b_ee": dq(1, xaa), "a_eo": dq(2, xsa), "b_eo": dq(3, xas),
            "a_o": do(0, xa), "b_o": do(1, xs)}


def _plane(ref, lead, name, rows, cols):
    r0 = PLANES[name]
    return ref[lead, r0:r0 + rows, cols]


def _resident(shape):
    zeros = (0,) * len(shape)
    return pl.BlockSpec(shape, lambda *_: zeros, pipeline_mode=pl.Buffered(1))


def _filter_sd_body(w1t_ref, w1c_ref, w1s_ref, b1_ref, w2_ref, b2_ref, fq_ref,
                    w3f_ref, b3f_ref, w3b_ref, b3b_ref, d_ref, o_ref, hid_ref, *, first, j):
    length = o_ref.shape[1]
    tn = o_ref.shape[2]
    hp = lax.Precision.HIGHEST

    @pl.when(first)
    def _():
        n = lax.broadcasted_iota(jnp.int32, (length, LANES), 0).astype(F32)
        lane = lax.broadcasted_iota(jnp.int32, (length, LANES), 1)
        bands = (FILTER_EMB - 1) // 2
        fr_step = (bands - 1 - 1e-4) / (bands - 1)
        fr = jnp.where(lane < bands, 1e-4 + lane.astype(F32) * fr_step, 0.0)
        ang = (2.0 * math.pi * n / length) * fr
        t = n / (length - 1)
        fq = fq_ref[...]
        pre = (t * w1t_ref[...]
               + jnp.dot(jnp.cos(ang), w1c_ref[...], precision=hp, preferred_element_type=F32)
               + jnp.dot(-jnp.sin(ang), w1s_ref[...], precision=hp, preferred_element_type=F32)
               + b1_ref[...])
        hid = jnp.sin(fq * pre)
        hid = jnp.sin(fq * (jnp.dot(hid, w2_ref[...], precision=hp, preferred_element_type=F32)
                            + b2_ref[...]))
        hid_hi = hid.astype(BF16)
        hid_ref[0] = hid_hi
        hid_ref[1] = (hid - hid_hi.astype(F32)).astype(BF16)

    def dot3(w_ref):
        wf = w_ref[...]
        w_hi = wf.astype(BF16)
        w_lo = (wf - w_hi.astype(F32)).astype(BF16)
        return (jnp.dot(hid_ref[0], w_hi, preferred_element_type=F32)
                + jnp.dot(hid_ref[1], w_hi, preferred_element_type=F32)
                + jnp.dot(hid_ref[0], w_lo, preferred_element_type=F32))

    row = lax.broadcasted_iota(jnp.int32, (length, tn), 0)
    chan = (lax.broadcasted_iota(jnp.int32, (length, tn), 1) + j * tn).astype(F32)
    min_decay = math.log(DECAY_TARGET) / DECAY_SLOW
    max_decay = math.log(DECAY_TARGET) / DECAY_FAST
    delta = min_decay + chan * ((max_decay - min_decay) / (HYENA_WIDTH - 1))
    t = row.astype(F32) / (length - 1)
    window = jnp.exp(-t * jnp.abs(delta)) + WINDOW_SHIFT
    hf = (dot3(w3f_ref) + b3f_ref[...]) * window
    hb = (dot3(w3b_ref) + b3b_ref[...]) * window
    hf = jnp.where(row == 0, hf + d_ref[0], hf)
    hb = jnp.where(row == 0, 0.0, hb)
    o_ref[0] = (hf + hb).astype(o_ref.dtype)
    o_ref[1] = (hf - hb).astype(o_ref.dtype)


def _prep_kernel(c_ref, aw_ref, ab_ref, *refs, filter_steps, ct):
    mods_ref, sd_ref, hid_ref = refs[-3:]
    s = pl.program_id(0)
    _mods_kernel(c_ref, aw_ref, ab_ref, mods_ref)

    @pl.when(s < filter_steps)
    def _():
        _filter_sd_body(*refs[:-3], sd_ref, hid_ref, first=s == 0, j=s % ct)


def _prep(cc, ada_w, ada_b, w1, b1, w2, b2, w3, b3, freq, bias_d, *, tn_mods=1024, tn=512):
    depth, d, n_mods = ada_w.shape
    mt = n_mods // tn_mods
    hpad = LANES - FILTER_HIDDEN
    bands = (FILTER_EMB - 1) // 2
    w = HYENA_WIDTH
    ct = w // tn
    filter_steps = 2 * ct
    assert filter_steps <= depth * mt
    n = w3.shape[1]
    w1t = jnp.pad(w1[0:1], ((0, 0), (0, hpad)))
    w1c = jnp.pad(w1[1:1 + bands], ((0, LANES - bands), (0, hpad)))
    w1s = jnp.pad(w1[1 + bands:], ((0, LANES - bands), (0, hpad)))
    b1p = jnp.pad(b1.reshape(1, -1), ((0, 0), (0, hpad)))
    w2p = jnp.pad(w2, ((0, hpad), (0, hpad)))
    b2p = jnp.pad(b2.reshape(1, -1), ((0, 0), (0, hpad)))
    fqp = jnp.pad(freq.reshape(1, -1), ((0, 0), (0, hpad)))
    w3p = jnp.pad(w3, ((0, hpad), (0, 0)))
    b3r = b3.reshape(1, n)
    small = lambda shape: pl.BlockSpec(shape, lambda s: (0, 0))
    order = lambda s: jnp.minimum(s, filter_steps - 1) // ct
    chan = lambda s: jnp.minimum(s, filter_steps - 1) % ct
    fwd_cols = lambda s: (0, 2 * order(s) * ct + chan(s))
    bwd_cols = lambda s: (0, (2 * order(s) + 1) * ct + chan(s))
    return pl.pallas_call(
        functools.partial(_prep_kernel, filter_steps=filter_steps, ct=ct),
        grid=(depth * mt,),
        in_specs=[pl.BlockSpec((8, d), lambda s: (0, 0)),
                  pl.BlockSpec((1, d, tn_mods), lambda s: (s // mt, 0, s % mt)),
                  pl.BlockSpec((1, 1, tn_mods), lambda s: (s // mt, 0, s % mt)),
                  small((1, LANES)), small((LANES, LANES)), small((LANES, LANES)), small((1, LANES)),
                  small((LANES, LANES)), small((1, LANES)), small((1, LANES)),
                  pl.BlockSpec((LANES, tn), fwd_cols), pl.BlockSpec((1, tn), fwd_cols),
                  pl.BlockSpec((LANES, tn), bwd_cols), pl.BlockSpec((1, tn), bwd_cols),
                  pl.BlockSpec((1, 1, tn), lambda s: (order(s), 0, chan(s)))],
        out_specs=[pl.BlockSpec((1, 8, tn_mods), lambda s: (s // mt, 0, s % mt)),
                   pl.BlockSpec((2, SEQ, tn), lambda s: (0, 0, order(s) * ct + chan(s)))],
        out_shape=[jax.ShapeDtypeStruct((depth, 8, n_mods), F32),
                   jax.ShapeDtypeStruct((2, SEQ, 2 * w), BF16)],
        scratch_shapes=[pltpu.VMEM((2, SEQ, LANES), BF16)],
        compiler_params=_cparams(("arbitrary",)),
        name="mods_and_filters",
    )(cc, ada_w, ada_b.reshape(depth, 1, n_mods),
      w1t, w1c, w1s, b1p, w2p, b2p, fqp, w3p, b3r, w3p, b3r, bias_d.reshape(2, 1, w))


def _filter_spec_kernel(s_ref, d_ref, fq_ref, fo_ref, anti_ref, pq_ref, po_ref, k_ref):
    phases = {"a_ee": (pq_ref[0, 0], pq_ref[1, 0]), "a_eo": (pq_ref[0, 1], pq_ref[1, 1]),
              "a_o": (po_ref[0], po_ref[1])}
    is0 = lax.broadcasted_iota(jnp.int32, (QUARTER, MXU_COLS), 0) == 0
    for cols in _col_chunks(k_ref.shape[2]):
        sp = _spectrum(fq_ref, fo_ref, s_ref[0, :, cols], anti_ref)
        dp = _spectrum(fq_ref, fo_ref, d_ref[0, :, cols], anti_ref)
        for a, b, rows in GROUPS:
            cos, sin = phases[a]
            ka = sp[a] * cos + sp[b] * sin
            kb = dp[b] * cos - dp[a] * sin
            if a == "a_ee":
                scale = jnp.where(is0, 1.0 / FFT_N, 2.0 / FFT_N)
                kb = jnp.where(is0, sp[b], kb)
            else:
                scale = 2.0 / FFT_N
            k_ref[0, PLANES[a]:PLANES[a] + rows, cols] = (ka * scale).astype(k_ref.dtype)
            k_ref[0, PLANES[b]:PLANES[b] + rows, cols] = (kb * scale).astype(k_ref.dtype)


def _filter_spectra(sd, tabs, *, tn=512):
    w = HYENA_WIDTH
    ct = w // tn
    consts = [tabs[k] for k in ("fq", "fo", "anti", "pq", "po")]
    return pl.pallas_call(
        _filter_spec_kernel,
        grid=(2, ct),
        in_specs=[
            pl.BlockSpec((1, SEQ, tn), lambda o, j: (0, 0, o * ct + j)),
            pl.BlockSpec((1, SEQ, tn), lambda o, j: (1, 0, o * ct + j)),
        ] + [_resident(c.shape) for c in consts],
        out_specs=pl.BlockSpec((1, 4 * HALF, tn), lambda o, j: (o, 0, j)),
        out_shape=jax.ShapeDtypeStruct((2, 4 * HALF, w), BF16),
        compiler_params=_cparams(("arbitrary", "arbitrary")),
        name="filter_spectra",
    )(sd, sd, *consts)


def _conv_fwd_kernel(x_ref, fq_ref, fo_ref, anti_ref, k_ref, y_ref):
    is0 = lax.broadcasted_iota(jnp.int32, (QUARTER, MXU_COLS), 0) == 0
    for cols in _col_chunks(y_ref.shape[2]):
        sp = _spectrum(fq_ref, fo_ref, x_ref[:, cols], anti_ref)
        for a, b, rows in GROUPS:
            ka = _plane(k_ref, 0, a, rows, cols).astype(F32)
            kb = _plane(k_ref, 0, b, rows, cols).astype(F32)
            bkb = sp[b] * kb
            if a == "a_ee":
                ya = sp[a] * ka - jnp.where(is0, 0.0, bkb)
                yb = jnp.where(is0, bkb, sp[a] * kb + sp[b] * ka)
            else:
                ya = sp[a] * ka - bkb
                yb = sp[a] * kb + sp[b] * ka
            y_ref[0, PLANES[a]:PLANES[a] + rows, cols] = ya.astype(y_ref.dtype)
            y_ref[0, PLANES[b]:PLANES[b] + rows, cols] = yb.astype(y_ref.dtype)


def _conv_fwd(x, x_blk0, kspec, order, tabs, batch, *, tn=512):
    w = HYENA_WIDTH
    ct = w // tn
    consts = [tabs[k] for k in ("fq", "fo", "anti")]
    return pl.pallas_call(
        _conv_fwd_kernel,
        grid=(batch, ct),
        in_specs=[pl.BlockSpec((SEQ, tn), lambda b, j: (b, x_blk0 * ct + j))]
        + [_resident(c.shape) for c in consts]
        + [pl.BlockSpec((1, 4 * HALF, tn), lambda b, j: (order, 0, j))],
        out_specs=pl.BlockSpec((1, 4 * HALF, tn), lambda b, j: (b, 0, j)),
        out_shape=jax.ShapeDtypeStruct((batch, 4 * HALF, w), BF16),
        compiler_params=_cparams(("arbitrary", "arbitrary")),
        name="long_conv_fwd",
    )(x, *consts, kspec)


def _conv_inv_kernel(*refs, gated):
    if gated:
        y_ref, iq_ref, io_ref, anti_ref, xm_ref, g_ref, o_ref, accq_ref, acco_ref = refs
    else:
        y_ref, iq_ref, io_ref, anti_ref, xm_ref, o_ref, accq_ref, acco_ref = refs
    chunks = _col_chunks(o_ref.shape[1])
    flipped = lambda v: _flip_rows(v.astype(BF16), anti_ref)

    def matmuls(t, slot):
        cols = chunks[t]
        for k, name in enumerate(("a_ee", "a_eo", "b_ee", "b_eo")):
            accq_ref[slot, k] = jnp.dot(iq_ref[k], _plane(y_ref, 0, name, QUARTER, cols),
                                        preferred_element_type=F32)
        for k, name in enumerate(("b_o", "a_o")):
            acco_ref[slot, k] = jnp.dot(io_ref[k], _plane(y_ref, 0, name, HALF, cols),
                                        preferred_element_type=F32)

    def finish(t, slot):
        cols = chunks[t]
        p, r, q, s = (accq_ref[slot, k] for k in range(4))
        ea = jnp.concatenate([p + r, flipped(p - r)], axis=0)
        eb = jnp.concatenate([q + s, flipped(s - q)], axis=0)
        p1 = ea + acco_ref[slot, 0]
        p2 = eb + acco_ref[slot, 1]
        for rows, y in ((slice(0, HALF), p1 + p2), (slice(HALF, SEQ), flipped(p1 - p2))):
            out = y * xm_ref[rows, cols].astype(F32)
            if gated:
                out = out * g_ref[rows, cols].astype(F32)
            o_ref[rows, cols] = out.astype(o_ref.dtype)

    _staggered(len(chunks), matmuls, finish)


def _conv_inv(y, tabs, p, m_blk0, g_blk0=None, *, tn=512):
    batch = y.shape[0]
    w = HYENA_WIDTH
    ct = w // tn
    gated = g_blk0 is not None
    col = lambda blk0: (lambda b, j: (b, blk0 * ct + j))
    consts = [tabs[k] for k in ("iq", "io", "anti")]
    in_specs = ([pl.BlockSpec((1, 4 * HALF, tn), lambda b, j: (b, 0, j))]
                + [_resident(c.shape) for c in consts]
                + [pl.BlockSpec((SEQ, tn), col(m_blk0))])
    args = [y, *consts, p]
    if gated:
        in_specs.append(pl.BlockSpec((SEQ, tn), col(g_blk0)))
        args.append(p)
    return pl.pallas_call(
        functools.partial(_conv_inv_kernel, gated=gated),
        grid=(batch, ct),
        in_specs=in_specs,
        out_specs=pl.BlockSpec((SEQ, tn), lambda b, j: (b, j)),
        out_shape=jax.ShapeDtypeStruct((batch * SEQ, w), BF16),
        scratch_shapes=[pltpu.VMEM((2, 4, QUARTER, MXU_COLS), F32),
                        pltpu.VMEM((2, 2, HALF, MXU_COLS), F32)],
        compiler_params=_cparams(("arbitrary", "arbitrary")),
        name="long_conv_inv",
    )(*args)


def _rope_tables():
    pos = np.arange(SEQ)
    row = (pos // GRID_W).astype(np.float32)
    col = (pos % GRID_W).astype(np.float32)
    half = HEAD_DIM // 2
    inv = (ROPE_BASE ** (-np.arange(0, half, 2, dtype=np.float32) / half)).astype(np.float32)
    ar = row[:, None] * inv[None]
    ac = col[:, None] * inv[None]
    cos = np.concatenate([np.cos(ar), np.cos(ar), np.cos(ac), np.cos(ac)], axis=1)
    sin = np.concatenate([-np.sin(ar), np.sin(ar), -np.sin(ac), np.sin(ac)], axis=1)
    return jnp.asarray(cos, F32), jnp.asarray(sin, F32)


def _tile_types(tn, kinds):
    per_tile = tn // LANES
    return [kinds[s0:s0 + per_tile] for s0 in range(0, len(kinds), per_tile)]


def kernel(x, c, ctx, c_ctx, norm_g, ada_w, ada_b, attn_w_in, attn_w_out, attn_sink, hy_w_in,
           hy_conv_w, hy_conv_b, hy_w1, hy_b1, hy_w2, hy_b2, hy_w3, hy_b3, hy_freq, hy_bias_d,
           hy_w_out, final_g):
    batch, seq, d = x.shape
    assert (seq, d) == (SEQ, D_MODEL) and ctx.shape[1] == CTX_LEN
    assert norm_g.shape[0] == 2 and attn_w_in.shape[0] == 1 and hy_w_in.shape[0] == 1
    w = HYENA_WIDTH

    cc = jnp.concatenate([c, c_ctx[None], jnp.zeros((8 - batch - 1, d), F32)], axis=0)
    mods, sd = _prep(cc, ada_w, ada_b, hy_w1[0], hy_b1[0], hy_w2[0], hy_b2[0], hy_w3[0], hy_b3[0],
                     hy_freq[0], hy_bias_d[0])
    part = lambda layer, r0, r1, k: mods[layer, r0:r1, None, k * d:(k + 1) * d]

    x2 = x.reshape(batch * seq, d)
    ctx2 = ctx.reshape(batch * CTX_LEN, d)

    tn = 1024
    kinds0 = (["q"] * N_HEADS + ["k"] * N_KV_HEADS + [None] * N_KV_HEADS
              + ["g"] * (ATTN_WIDTH // LANES))
    hx = _norm_mod(x2, norm_g[0], part(0, 0, batch, 0), part(0, 0, batch, 1), tm=1024,
                   rows_per_mod=seq)
    hc = _norm_mod(ctx2, norm_g[0], part(0, batch, batch + 1, 0), part(0, batch, batch + 1, 1),
                   tm=512, rows_per_mod=CTX_LEN)
    px = _proj(hx, attn_w_in[0], 0, ATTN_IN, tm=1024, tn=tn, tile_types=_tile_types(tn, kinds0),
               rope=_rope_tables(), q_scale=HEAD_DIM ** -0.5 * LOG2E)
    ckv = _proj(hc, attn_w_in[0], ATTN_WIDTH // tn, 2 * KV_WIDTH, tm=1024, tn=tn)
    og = _attention(px, ckv, attn_sink[0], batch)
    x2, hx = _out_proj(og, attn_w_out[0].astype(BF16), x2, part(0, 0, batch, 2),
                       next_norm=(norm_g[1], part(1, 0, batch, 0), part(1, 0, batch, 1)))

    kinds1 = ["c"] * (3 * w // LANES) + ["g"] * (w // LANES)
    tn1 = 512
    p = _proj(hx, hy_w_in[0], 0, 4 * w, tm=SEQ, tn=tn1, tile_types=_tile_types(tn1, kinds1),
              conv=(hy_conv_w[0], hy_conv_b[0].reshape(1, 3 * w)))
    tabs = _fold_tables()
    kspec = _filter_spectra(sd, tabs)
    y1 = _conv_fwd(p, 2, kspec, 0, tabs, batch)
    z = _conv_inv(y1, tabs, p, 0)
    y2 = _conv_fwd(z, 0, kspec, 1, tabs, batch)
    yg = _conv_inv(y2, tabs, p, 1, g_blk0=3)
    out, = _out_proj(yg, hy_w_out[0].astype(BF16), x2, part(1, 0, batch, 2), final_g=final_g)
    return out.reshape(batch, seq, d)
```

```python
import functools
import math

import jax
import jax.numpy as jnp
import numpy as np
from jax import lax
from jax.experimental import pallas as pl
from jax.experimental.pallas import tpu as pltpu

F32 = jnp.float32
BF16 = jnp.bfloat16

D_MODEL = 2048
SEQ = 2048
CTX_LEN = 256
GRID_W = 64
HEAD_DIM = 128
N_HEADS = 16
N_KV_HEADS = 4
GQA_GROUP = 4
ATTN_WIDTH = 2048
KV_WIDTH = 512
ATTN_IN = 2 * ATTN_WIDTH + 2 * KV_WIDTH
WINDOW = 128
BLOCK = 128
ROPE_BASE = 10000.0
HYENA_WIDTH = 2048
FILTER_EMB = 33
FILTER_HIDDEN = 64
DECAY_FAST = 0.3
DECAY_SLOW = 1.5
DECAY_TARGET = 1e-2
WINDOW_SHIFT = 0.05
NORM_EPS = 1e-6
NEG_INF = -1e30

LANES = 128
MXU_COLS = 256
ACC_ROWS = 1024
ONES_ROWS = 16
LOG2E = math.log2(math.e)
FFT_N = 2 * SEQ
VMEM_LIMIT = 56 * 1024 * 1024


def _cparams(sem):
    return pltpu.CompilerParams(dimension_semantics=sem, vmem_limit_bytes=VMEM_LIMIT)


def _mods_kernel(c_ref, w_ref, b_ref, o_ref):
    c = c_ref[...]
    s = c * jax.nn.sigmoid(c)
    s_hi = s.astype(BF16)
    s_lo = (s - s_hi.astype(F32)).astype(BF16)
    lhs = jnp.concatenate([s_hi, s_lo], axis=0)
    r = jnp.dot(lhs, w_ref[0].astype(BF16), preferred_element_type=F32)
    o_ref[0] = r[:8] + r[8:] + b_ref[0]


def _rope_slab(t, cos, sin):
    lane = lax.broadcasted_iota(jnp.int32, t.shape, 1)
    first = (lane % 64) < 32
    partner = jnp.where(first, pltpu.roll(t, 96, 1), pltpu.roll(t, 32, 1))
    return t * cos + partner * sin


def _norm_mod_rows(x, g, mul, add):
    ms = jnp.mean(x * x, axis=-1, keepdims=True)
    return x * lax.rsqrt(ms + NORM_EPS) * g * mul + add


def _norm_mod_kernel(x_ref, g_ref, sh_ref, sc_ref, o_ref, *, row_chunk):
    g = g_ref[...]
    mul = 1.0 + sc_ref[0]
    add = sh_ref[0]
    for r in range(0, x_ref.shape[0], row_chunk):
        o_ref[r:r + row_chunk, :] = _norm_mod_rows(x_ref[r:r + row_chunk, :], g, mul, add).astype(o_ref.dtype)


def _norm_mod(x, g, shift, scale, *, tm, rows_per_mod):
    m, d = x.shape
    if shift.shape[0] > 1:
        mod_map = lambda i: ((i * tm) // rows_per_mod, 0, 0)
    else:
        mod_map = lambda i: (0, 0, 0)
    return pl.pallas_call(
        functools.partial(_norm_mod_kernel, row_chunk=min(tm, 256)),
        grid=(m // tm,),
        in_specs=[pl.BlockSpec((tm, d), lambda i: (i, 0)), pl.BlockSpec((1, d), lambda i: (0, 0)),
                  pl.BlockSpec((1, 1, d), mod_map), pl.BlockSpec((1, 1, d), mod_map)],
        out_specs=pl.BlockSpec((tm, d), lambda i: (i, 0)),
        out_shape=jax.ShapeDtypeStruct((m, d), BF16),
        compiler_params=_cparams(("arbitrary",)),
        name="norm_mod",
    )(x, g.reshape(1, d), shift, scale)


def _proj_kernel(*refs, tile_types, has_rope, has_conv, q_scale):
    if has_rope:
        hx_ref, w_ref, cos_ref, sin_ref, o_ref, wb_ref = refs
    elif has_conv:
        hx_ref, w_ref, cw_ref, cb_ref, o_ref, wb_ref = refs
    else:
        hx_ref, w_ref, o_ref, wb_ref = refs
    j = pl.program_id(0)
    tm, tn = o_ref.shape

    @pl.when(pl.program_id(1) == 0)
    def _():
        wb_ref[...] = w_ref[...].astype(BF16)

    w_ref = wb_ref

    def plain():
        for c0 in range(0, tn, MXU_COLS):
            o_ref[:, c0:c0 + MXU_COLS] = chunk_dot(c0).astype(o_ref.dtype)

    def chunk_dot(c0):
        pieces = [jnp.dot(hx_ref[r:r + ACC_ROWS, :], w_ref[:, c0:c0 + MXU_COLS],
                          preferred_element_type=F32) for r in range(0, tm, ACC_ROWS)]
        return pieces[0] if len(pieces) == 1 else jnp.concatenate(pieces, axis=0)

    def gated():
        for c0 in range(0, tn, MXU_COLS):
            half = 0.5 * chunk_dot(c0)
            o_ref[:, c0:c0 + MXU_COLS] = (half + half * jnp.tanh(half)).astype(o_ref.dtype)

    def short_conv():
        for c0 in range(0, tn, MXU_COLS):
            acc = chunk_dot(c0)
            for h in range(MXU_COLS // LANES):
                lo = c0 + h * LANES
                u = acc[:, h * LANES:(h + 1) * LANES]
                w3 = cw_ref[:, lo:lo + LANES]
                b = cb_ref[:, lo:lo + LANES]
                y = pltpu.roll(u, 1, 0) * w3[0:1] + u * w3[1:2] + pltpu.roll(u, tm - 1, 0) * w3[2:3] + b
                o_ref[:, lo:lo + LANES] = y.astype(o_ref.dtype)
                first = u[0:1] * w3[1:2] + u[1:2] * w3[2:3] + b
                last = u[tm - 2:tm - 1] * w3[0:1] + u[tm - 1:tm] * w3[1:2] + b
                o_ref[0:1, lo:lo + LANES] = first.astype(o_ref.dtype)
                o_ref[tm - 1:tm, lo:lo + LANES] = last.astype(o_ref.dtype)

    def roped(types):
        cos = cos_ref[...]
        sin = sin_ref[...]
        for c0 in range(0, tn, MXU_COLS):
            acc = chunk_dot(c0)
            for h in range(MXU_COLS // LANES):
                ty = types[c0 // LANES + h]
                slab = acc[:, h * LANES:(h + 1) * LANES]
                if ty is not None:
                    slab = _rope_slab(slab, cos, sin)
                    if ty == "q":
                        slab = slab * q_scale
                lo = c0 + h * LANES
                o_ref[:, lo:lo + LANES] = slab.astype(o_ref.dtype)

    if tile_types is None:
        plain()
        return

    groups = {}
    for t, types in enumerate(tile_types):
        groups.setdefault(tuple(types), []).append(t)
    for types, tiles in groups.items():
        cond = functools.reduce(jnp.logical_or, [j == t for t in tiles])
        if all(ty is None for ty in types):
            pl.when(cond)(plain)
        elif all(ty == "g" for ty in types):
            pl.when(cond)(gated)
        elif all(ty == "c" for ty in types):
            pl.when(cond)(short_conv)
        else:
            assert "g" not in types and "c" not in types
            pl.when(cond)(functools.partial(roped, types))


def _proj(hx, w, col_blk0, n, *, tm, tn, tile_types=None, rope=None, conv=None, q_scale=1.0):
    m, d = hx.shape
    in_specs = [
        pl.BlockSpec((tm, d), lambda j, i: (i, 0)),
        pl.BlockSpec((d, tn), lambda j, i: (0, col_blk0 + j)),
    ]
    args = [hx, w]
    scratch = [pltpu.VMEM((d, tn), BF16)]
    if rope is not None:
        cos, sin = rope
        seq_tiles = cos.shape[0] // tm
        in_specs += [pl.BlockSpec((tm, LANES), lambda j, i: (i % seq_tiles, 0))] * 2
        args += [cos, sin]
    if conv is not None:
        assert rope is None and tm == SEQ
        cw, cb = conv
        last = cw.shape[1] // tn - 1
        in_specs += [pl.BlockSpec((cw.shape[0], tn), lambda j, i: (0, jnp.minimum(j, last))),
                     pl.BlockSpec((1, tn), lambda j, i: (0, jnp.minimum(j, last)))]
        args += [cw, cb]
    return pl.pallas_call(
        functools.partial(_proj_kernel, tile_types=tile_types, has_rope=rope is not None,
                          has_conv=conv is not None, q_scale=q_scale),
        grid=(n // tn, m // tm),
        in_specs=in_specs,
        out_specs=pl.BlockSpec((tm, tn), lambda j, i: (i, j)),
        out_shape=jax.ShapeDtypeStruct((m, n), BF16),
        scratch_shapes=scratch,
        compiler_params=_cparams(("arbitrary", "arbitrary")),
        name="proj",
    )(*args)


def _attn_kernel(sink_ref, q_ref, k_ref, v_ref, g_ref, kc_ref, vc_ref, o_ref,
                 vt_ref, vct_ref, bias_ref):
    kh = pl.program_id(1)
    band = 3 * BLOCK
    cols = GQA_GROUP * BLOCK
    n_blocks = SEQ // BLOCK

    vt_ref[:HEAD_DIM, :] = v_ref[...].T
    vct_ref[:HEAD_DIM, :] = vc_ref[...].T

    @pl.when((pl.program_id(0) == 0) & (kh == 0))
    def _():
        vt_ref[HEAD_DIM:, :] = jnp.ones((ONES_ROWS, SEQ), BF16)
        vct_ref[HEAD_DIM:, :] = jnp.ones((ONES_ROWS, CTX_LEN), BF16)
        krow = lax.broadcasted_iota(jnp.int32, (band, cols), 0)
        qcol = lax.broadcasted_iota(jnp.int32, (band, cols), 1) % BLOCK
        for idx, off in enumerate((0, -BLOCK, -2 * BLOCK)):
            bias_ref[idx] = jnp.where(jnp.abs(krow - qcol + off) <= WINDOW, 0.0, NEG_INF)

    kc = kc_ref[...]
    vct = vct_ref[...]
    sink_row = jnp.concatenate(
        [jnp.full((1, BLOCK), sink_ref[kh * GQA_GROUP + h] * LOG2E, F32) for h in range(GQA_GROUP)],
        axis=1)

    def body(n, carry):
        q0 = pl.multiple_of(n * BLOCK, BLOCK)
        ks = pl.multiple_of(jnp.clip((n - 1) * BLOCK, 0, SEQ - band), BLOCK)
        bidx = jnp.where(n == 0, 0, jnp.where(n == n_blocks - 1, 2, 1))
        qs = q_ref[pl.ds(q0, BLOCK), :]
        q4t = jnp.concatenate([qs[:, h * LANES:(h + 1) * LANES].T for h in range(GQA_GROUP)],
                              axis=1)
        kb = k_ref[pl.ds(ks, band), :]
        s_loc = jnp.dot(kb, q4t, preferred_element_type=F32) + bias_ref[bidx]
        s_ctx = jnp.dot(kc, q4t, preferred_element_type=F32)
        m = jnp.maximum(jnp.maximum(jnp.max(s_loc, axis=0, keepdims=True),
                                    jnp.max(s_ctx, axis=0, keepdims=True)), sink_row)
        p_loc = jnp.exp2(s_loc - m).astype(BF16)
        p_ctx = jnp.exp2(s_ctx - m).astype(BF16)
        ox = (jnp.dot(vt_ref[:, pl.ds(ks, band)], p_loc, preferred_element_type=F32)
              + jnp.dot(vct, p_ctx, preferred_element_type=F32))
        den = ox[HEAD_DIM:HEAD_DIM + 1, :] + jnp.exp2(sink_row - m)
        ot = ox[:HEAD_DIM, :] * (1.0 / den)
        gs = g_ref[pl.ds(q0, BLOCK), :].astype(F32)
        for h in range(GQA_GROUP):
            oh = ot[:, h * LANES:(h + 1) * LANES].T * gs[:, h * LANES:(h + 1) * LANES]
            o_ref[pl.ds(q0, BLOCK), h * LANES:(h + 1) * LANES] = oh.astype(o_ref.dtype)
        return carry

    lax.fori_loop(0, n_blocks, body, 0, unroll=8)


def _attention(px, ckv, sink, batch):
    gw = GQA_GROUP * HEAD_DIM
    k_blk0 = ATTN_WIDTH // HEAD_DIM
    v_blk0 = (ATTN_WIDTH + KV_WIDTH) // HEAD_DIM
    g_blk0 = (ATTN_WIDTH + 2 * KV_WIDTH) // gw
    return pl.pallas_call(
        _attn_kernel,
        grid=(batch, N_KV_HEADS),
        in_specs=[
            pl.BlockSpec(memory_space=pltpu.SMEM),
            pl.BlockSpec((SEQ, gw), lambda b, h: (b, h)),
            pl.BlockSpec((SEQ, HEAD_DIM), lambda b, h: (b, k_blk0 + h)),
            pl.BlockSpec((SEQ, HEAD_DIM), lambda b, h: (b, v_blk0 + h)),
            pl.BlockSpec((SEQ, gw), lambda b, h: (b, g_blk0 + h)),
            pl.BlockSpec((CTX_LEN, HEAD_DIM), lambda b, h: (b, h)),
            pl.BlockSpec((CTX_LEN, HEAD_DIM), lambda b, h: (b, N_KV_HEADS + h)),
        ],
        out_specs=pl.BlockSpec((SEQ, gw), lambda b, h: (b, h)),
        out_shape=jax.ShapeDtypeStruct((batch * SEQ, ATTN_WIDTH), BF16),
        scratch_shapes=[pltpu.VMEM((HEAD_DIM + ONES_ROWS, SEQ), BF16),
                        pltpu.VMEM((HEAD_DIM + ONES_ROWS, CTX_LEN), BF16),
                        pltpu.VMEM((3, 3 * BLOCK, GQA_GROUP * BLOCK), F32)],
        compiler_params=_cparams(("arbitrary", "arbitrary")),
        name="banded_attention",
    )(sink, px, px, px, px, ckv, ckv)


def _out_proj_kernel(*refs, final):
    if final:
        a_ref, w_ref, x_ref, gate_ref, fg_ref, o_ref = refs
    else:
        a_ref, w_ref, x_ref, gate_ref, ng_ref, sh_ref, sc_ref, o_ref, hx_ref = refs
    acc = jnp.dot(a_ref[...], w_ref[...], preferred_element_type=F32)
    y = x_ref[...] + gate_ref[0] * acc
    if final:
        ms = jnp.mean(y * y, axis=-1, keepdims=True)
        y = y * lax.rsqrt(ms + NORM_EPS) * fg_ref[...]
    else:
        hx_ref[...] = _norm_mod_rows(y, ng_ref[...], 1.0 + sc_ref[0], sh_ref[0]).astype(hx_ref.dtype)
    o_ref[...] = y


def _out_proj(a, w, x, gate, *, final_g=None, next_norm=None, tm=512):
    m, d = x.shape
    kdim = a.shape[1]
    final = final_g is not None
    per_batch = lambda i: ((i * tm) // SEQ, 0, 0)
    row_tile = pl.BlockSpec((tm, d), lambda i: (i, 0))
    vec = pl.BlockSpec((1, d), lambda i: (0, 0))
    in_specs = [
        pl.BlockSpec((tm, kdim), lambda i: (i, 0)),
        pl.BlockSpec((kdim, d), lambda i: (0, 0)),
        row_tile,
        pl.BlockSpec((1, 1, d), per_batch),
    ]
    args = [a, w, x, gate]
    out_specs = [row_tile]
    out_shape = [jax.ShapeDtypeStruct((m, d), F32)]
    if final:
        in_specs.append(vec)
        args.append(final_g.reshape(1, d))
    else:
        ng, shift, scale = next_norm
        in_specs += [vec, pl.BlockSpec((1, 1, d), per_batch), pl.BlockSpec((1, 1, d), per_batch)]
        args += [ng.reshape(1, d), shift, scale]
        out_specs.append(row_tile)
        out_shape.append(jax.ShapeDtypeStruct((m, d), BF16))
    return pl.pallas_call(
        functools.partial(_out_proj_kernel, final=final),
        grid=(m // tm,),
        in_specs=in_specs,
        out_specs=out_specs,
        out_shape=out_shape,
        compiler_params=_cparams(("arbitrary",)),
        name="out_proj",
    )(*args)


HALF = SEQ // 2
QUARTER = SEQ // 4
FLIP_BLOCK = 256


def _fold_tables():
    def tab(f, n):
        t2 = 2 * np.arange(n, dtype=np.int64) + 1
        ang = ((f[:, None] * t2[None, :]) % (2 * FFT_N)).astype(np.float64) * (math.pi / FFT_N)
        return np.cos(ang), np.sin(ang)

    r = np.arange(QUARTER, dtype=np.int64)
    g = np.arange(HALF, dtype=np.int64)
    f_ee, f_eo, f_o = 4 * r, 4 * r + 2, 2 * g + 1
    cee, see = tab(f_ee, QUARTER)
    ceo, seo = tab(f_eo, QUARTER)
    co, so = tab(f_o, HALF)
    see[0, :] = np.where(np.arange(QUARTER) % 2 == 0, 1.0, -1.0)
    const = lambda a, dt: jnp.asarray(np.ascontiguousarray(a).astype(np.float32)).astype(dt)
    phase = lambda f: np.stack([np.cos(f * (math.pi / FFT_N)), np.sin(f * (math.pi / FFT_N))])
    return dict(
        fq=const(np.stack([cee, see, ceo, seo]), BF16),
        fo=const(np.stack([co, so]), BF16),
        iq=const(np.stack([cee.T, ceo.T, see.T, seo.T]), BF16),
        io=const(np.stack([so.T, co.T]), BF16),
        anti=const(np.eye(FLIP_BLOCK)[::-1], BF16),
        pq=const(np.stack([phase(f_ee), phase(f_eo)], axis=1)[..., None], F32),
        po=const(phase(f_o)[..., None], F32),
    )


def _flip_rows(h, anti):
    nb = h.shape[0] // FLIP_BLOCK
    return jnp.concatenate(
        [jnp.dot(anti[...], h[(nb - 1 - a) * FLIP_BLOCK:(nb - a) * FLIP_BLOCK, :], preferred_element_type=F32)
         for a in range(nb)], axis=0)


def _fold(x, anti):
    half = x.shape[0] // 2
    x0 = x[:half].astype(F32)
    xr = _flip_rows(x[half:], anti)
    return (x0 + xr).astype(BF16), (x0 - xr).astype(BF16)


def _staggered(n, matmuls, finish):
    matmuls(0, 0)
    for t in range(1, n):
        matmuls(t, t % 2)
        finish(t - 1, (t - 1) % 2)
    finish(n - 1, (n - 1) % 2)


def _col_chunks(n):
    return [slice(c, c + MXU_COLS) for c in range(0, n, MXU_COLS)]


PLANES = {"a_ee": 0, "b_ee": QUARTER, "a_eo": 2 * QUARTER, "b_eo": 3 * QUARTER,
          "a_o": 2 * HALF, "b_o": 3 * HALF}
GROUPS = (("a_ee", "b_ee", QUARTER), ("a_eo", "b_eo", QUARTER), ("a_o", "b_o", HALF))


def _spectrum(fq_ref, fo_ref, x, anti):
    xs, xa = _fold(x, anti)
    xss, xsa = _fold(xs, anti)
    xas, xaa = _fold(xa, anti)
    dq = lambda k, v: jnp.dot(fq_ref[k], v, preferred_element_type=F32)
    do = lambda k, v: jnp.dot(fo_ref[k], v, preferred_element_type=F32)
    return {"a_ee": dq(0, xss), "b_ee": dq(1, xaa), "a_eo": dq(2, xsa), "b_eo": dq(3, xas),
            "a_o": do(0, xa), "b_o": do(1, xs)}


def _plane(ref, lead, name, rows, cols):
    r0 = PLANES[name]
    return ref[lead, r0:r0 + rows, cols]


def _resident(shape):
    zeros = (0,) * len(shape)
    return pl.BlockSpec(shape, lambda *_: zeros, pipeline_mode=pl.Buffered(1))


def _filter_sd_body(w1t_ref, w1c_ref, w1s_ref, b1_ref, w2_ref, b2_ref, fq_ref,
                    w3f_ref, b3f_ref, w3b_ref, b3b_ref, d_ref, o_ref, hid_ref, *, first, j):
    length = o_ref.shape[1]
    tn = o_ref.shape[2]
    hp = lax.Precision.HIGHEST

    @pl.when(first)
    def _():
        n = lax.broadcasted_iota(jnp.int32, (length, LANES), 0).astype(F32)
        lane = lax.broadcasted_iota(jnp.int32, (length, LANES), 1)
        bands = (FILTER_EMB - 1) // 2
        fr_step = (bands - 1 - 1e-4) / (bands - 1)
        fr = jnp.where(lane < bands, 1e-4 + lane.astype(F32) * fr_step, 0.0)
        ang = (2.0 * math.pi * n / length) * fr
        t = n / (length - 1)
        fq = fq_ref[...]
        pre = (t * w1t_ref[...]
               + jnp.dot(jnp.cos(ang), w1c_ref[...], precision=hp, preferred_element_type=F32)
               + jnp.dot(-jnp.sin(ang), w1s_ref[...], precision=hp, preferred_element_type=F32)
               + b1_ref[...])
        hid = jnp.sin(fq * pre)
        hid = jnp.sin(fq * (jnp.dot(hid, w2_ref[...], precision=hp, preferred_element_type=F32)
                            + b2_ref[...]))
        hid_hi = hid.astype(BF16)
        hid_ref[0] = hid_hi
        hid_ref[1] = (hid - hid_hi.astype(F32)).astype(BF16)

    def dot3(w_ref):
        wf = w_ref[...]
        w_hi = wf.astype(BF16)
        w_lo = (wf - w_hi.astype(F32)).astype(BF16)
        return (jnp.dot(hid_ref[0], w_hi, preferred_element_type=F32)
                + jnp.dot(hid_ref[1], w_hi, preferred_element_type=F32)
                + jnp.dot(hid_ref[0], w_lo, preferred_element_type=F32))

    row = lax.broadcasted_iota(jnp.int32, (length, tn), 0)
    chan = (lax.broadcasted_iota(jnp.int32, (length, tn), 1) + j * tn).astype(F32)
    min_decay = math.log(DECAY_TARGET) / DECAY_SLOW
    max_decay = math.log(DECAY_TARGET) / DECAY_FAST
    delta = min_decay + chan * ((max_decay - min_decay) / (HYENA_WIDTH - 1))
    t = row.astype(F32) / (length - 1)
    window = jnp.exp(-t * jnp.abs(delta)) + WINDOW_SHIFT
    hf = (dot3(w3f_ref) + b3f_ref[...]) * window
    hb = (dot3(w3b_ref) + b3b_ref[...]) * window
    hf = jnp.where(row == 0, hf + d_ref[0], hf)
    hb = jnp.where(row == 0, 0.0, hb)
    o_ref[0] = (hf + hb).astype(o_ref.dtype)
    o_ref[1] = (hf - hb).astype(o_ref.dtype)


def _prep_kernel(c_ref, aw_ref, ab_ref, *refs, filter_steps, ct):
    mods_ref, sd_ref, hid_ref = refs[-3:]
    s = pl.program_id(0)
    _mods_kernel(c_ref, aw_ref, ab_ref, mods_ref)

    @pl.when(s < filter_steps)
    def _():
        _filter_sd_body(*refs[:-3], sd_ref, hid_ref, first=s == 0, j=s % ct)


def _prep(cc, ada_w, ada_b, w1, b1, w2, b2, w3, b3, freq, bias_d, *, tn_mods=1024, tn=512):
    depth, d, n_mods = ada_w.shape
    mt = n_mods // tn_mods
    hpad = LANES - FILTER_HIDDEN
    bands = (FILTER_EMB - 1) // 2
    w = HYENA_WIDTH
    ct = w // tn
    filter_steps = 2 * ct
    assert filter_steps <= depth * mt
    n = w3.shape[1]
    w1t = jnp.pad(w1[0:1], ((0, 0), (0, hpad)))
    w1c = jnp.pad(w1[1:1 + bands], ((0, LANES - bands), (0, hpad)))
    w1s = jnp.pad(w1[1 + bands:], ((0, LANES - bands), (0, hpad)))
    b1p = jnp.pad(b1.reshape(1, -1), ((0, 0), (0, hpad)))
    w2p = jnp.pad(w2, ((0, hpad), (0, hpad)))
    b2p = jnp.pad(b2.reshape(1, -1), ((0, 0), (0, hpad)))
    fqp = jnp.pad(freq.reshape(1, -1), ((0, 0), (0, hpad)))
    w3p = jnp.pad(w3, ((0, hpad), (0, 0)))
    b3r = b3.reshape(1, n)
    small = lambda shape: pl.BlockSpec(shape, lambda s: (0, 0))
    order = lambda s: jnp.minimum(s, filter_steps - 1) // ct
    chan = lambda s: jnp.minimum(s, filter_steps - 1) % ct
    fwd_cols = lambda s: (0, 2 * order(s) * ct + chan(s))
    bwd_cols = lambda s: (0, (2 * order(s) + 1) * ct + chan(s))
    return pl.pallas_call(
        functools.partial(_prep_kernel, filter_steps=filter_steps, ct=ct),
        grid=(depth * mt,),
        in_specs=[pl.BlockSpec((8, d), lambda s: (0, 0)),
                  pl.BlockSpec((1, d, tn_mods), lambda s: (s // mt, 0, s % mt)),
                  pl.BlockSpec((1, 1, tn_mods), lambda s: (s // mt, 0, s % mt)),
                  small((1, LANES)), small((LANES, LANES)), small((LANES, LANES)), small((1, LANES)),
                  small((LANES, LANES)), small((1, LANES)), small((1, LANES)),
                  pl.BlockSpec((LANES, tn), fwd_cols), pl.BlockSpec((1, tn), fwd_cols),
                  pl.BlockSpec((LANES, tn), bwd_cols), pl.BlockSpec((1, tn), bwd_cols),
                  pl.BlockSpec((1, 1, tn), lambda s: (order(s), 0, chan(s)))],
        out_specs=[pl.BlockSpec((1, 8, tn_mods), lambda s: (s // mt, 0, s % mt)),
                   pl.BlockSpec((2, SEQ, tn), lambda s: (0, 0, order(s) * ct + chan(s)))],
        out_shape=[jax.ShapeDtypeStruct((depth, 8, n_mods), F32),
                   jax.ShapeDtypeStruct((2, SEQ, 2 * w), BF16)],
        scratch_shapes=[pltpu.VMEM((2, SEQ, LANES), BF16)],
        compiler_params=_cparams(("arbitrary",)),
        name="mods_and_filters",
    )(cc, ada_w, ada_b.reshape(depth, 1, n_mods),
      w1t, w1c, w1s, b1p, w2p, b2p, fqp, w3p, b3r, w3p, b3r, bias_d.reshape(2, 1, w))


def _filter_spec_kernel(s_ref, d_ref, fq_ref, fo_ref, anti_ref, pq_ref, po_ref, k_ref):
    phases = {"a_ee": (pq_ref[0, 0], pq_ref[1, 0]), "a_eo": (pq_ref[0, 1], pq_ref[1, 1]),
              "a_o": (po_ref[0], po_ref[1])}
    is0 = lax.broadcasted_iota(jnp.int32, (QUARTER, MXU_COLS), 0) == 0
    for cols in _col_chunks(k_ref.shape[2]):
        sp = _spectrum(fq_ref, fo_ref, s_ref[0, :, cols], anti_ref)
        dp = _spectrum(fq_ref, fo_ref, d_ref[0, :, cols], anti_ref)
        for a, b, rows in GROUPS:
            cos, sin = phases[a]
            ka = sp[a] * cos + sp[b] * sin
            kb = dp[b] * cos - dp[a] * sin
            if a == "a_ee":
                scale = jnp.where(is0, 1.0 / FFT_N, 2.0 / FFT_N)
                kb = jnp.where(is0, sp[b], kb)
            else:
                scale = 2.0 / FFT_N
            k_ref[0, PLANES[a]:PLANES[a] + rows, cols] = (ka * scale).astype(k_ref.dtype)
            k_ref[0, PLANES[b]:PLANES[b] + rows, cols] = (kb * scale).astype(k_ref.dtype)


def _filter_spectra(sd, tabs, *, tn=512):
    w = HYENA_WIDTH
    ct = w // tn
    consts = [tabs[k] for k in ("fq", "fo", "anti", "pq", "po")]
    return pl.pallas_call(
        _filter_spec_kernel,
        grid=(2, ct),
        in_specs=[
            pl.BlockSpec((1, SEQ, tn), lambda o, j: (0, 0, o * ct + j)),
            pl.BlockSpec((1, SEQ, tn), lambda o, j: (1, 0, o * ct + j)),
        ] + [_resident(c.shape) for c in consts],
        out_specs=pl.BlockSpec((1, 4 * HALF, tn), lambda o, j: (o, 0, j)),
        out_shape=jax.ShapeDtypeStruct((2, 4 * HALF, w), F32),
        compiler_params=_cparams(("arbitrary", "arbitrary")),
        name="filter_spectra",
    )(sd, sd, *consts)


def _conv_fwd_kernel(x_ref, fq_ref, fo_ref, anti_ref, k_ref, y_ref):
    is0 = lax.broadcasted_iota(jnp.int32, (QUARTER, MXU_COLS), 0) == 0
    for cols in _col_chunks(y_ref.shape[2]):
        sp = _spectrum(fq_ref, fo_ref, x_ref[:, cols], anti_ref)
        for a, b, rows in GROUPS:
            ka = _plane(k_ref, 0, a, rows, cols).astype(F32)
            kb = _plane(k_ref, 0, b, rows, cols).astype(F32)
            bkb = sp[b] * kb
            if a == "a_ee":
                ya = sp[a] * ka - jnp.where(is0, 0.0, bkb)
                yb = jnp.where(is0, bkb, sp[a] * kb + sp[b] * ka)
            else:
                ya = sp[a] * ka - bkb
                yb = sp[a] * kb + sp[b] * ka
            y_ref[0, PLANES[a]:PLANES[a] + rows, cols] = ya.astype(y_ref.dtype)
            y_ref[0, PLANES[b]:PLANES[b] + rows, cols] = yb.astype(y_ref.dtype)


def _conv_fwd(x, x_blk0, kspec, order, tabs, batch, *, tn=512):
    w = HYENA_WIDTH
    ct = w // tn
    consts = [tabs[k] for k in ("fq", "fo", "anti")]
    return pl.pallas_call(
        _conv_fwd_kernel,
        grid=(batch, ct),
        in_specs=[pl.BlockSpec((SEQ, tn), lambda b, j: (b, x_blk0 * ct + j))]
        + [_resident(c.shape) for c in consts]
        + [pl.BlockSpec((1, 4 * HALF, tn), lambda b, j: (order, 0, j))],
        out_specs=pl.BlockSpec((1, 4 * HALF, tn), lambda b, j: (b, 0, j)),
        out_shape=jax.ShapeDtypeStruct((batch, 4 * HALF, w), BF16),
        compiler_params=_cparams(("arbitrary", "arbitrary")),
        name="long_conv_fwd",
    )(x, *consts, kspec)


def _conv_inv_kernel(*refs, gated):
    if gated:
        y_ref, iq_ref, io_ref, anti_ref, xm_ref, g_ref, o_ref, accq_ref, acco_ref = refs
    else:
        y_ref, iq_ref, io_ref, anti_ref, xm_ref, o_ref, accq_ref, acco_ref = refs
    chunks = _col_chunks(o_ref.shape[1])
    flipped = lambda v: _flip_rows(v.astype(BF16), anti_ref)

    def matmuls(t, slot):
        cols = chunks[t]
        for k, name in enumerate(("a_ee", "a_eo", "b_ee", "b_eo")):
            accq_ref[slot, k] = jnp.dot(iq_ref[k], _plane(y_ref, 0, name, QUARTER, cols),
                                        preferred_element_type=F32)
        for k, name in enumerate(("b_o", "a_o")):
            acco_ref[slot, k] = jnp.dot(io_ref[k], _plane(y_ref, 0, name, HALF, cols),
                                        preferred_element_type=F32)

    def finish(t, slot):
        cols = chunks[t]
        p, r, q, s = (accq_ref[slot, k] for k in range(4))
        ea = jnp.concatenate([p + r, flipped(p - r)], axis=0)
        eb = jnp.concatenate([q + s, flipped(s - q)], axis=0)
        p1 = ea + acco_ref[slot, 0]
        p2 = eb + acco_ref[slot, 1]
        for rows, y in ((slice(0, HALF), p1 + p2), (slice(HALF, SEQ), flipped(p1 - p2))):
            out = y * xm_ref[rows, cols].astype(F32)
            if gated:
                out = out * g_ref[rows, cols].astype(F32)
            o_ref[rows, cols] = out.astype(o_ref.dtype)

    _staggered(len(chunks), matmuls, finish)


def _conv_inv(y, tabs, p, m_blk0, g_blk0=None, *, tn=512):
    batch = y.shape[0]
    w = HYENA_WIDTH
    ct = w // tn
    gated = g_blk0 is not None
    col = lambda blk0: (lambda b, j: (b, blk0 * ct + j))
    consts = [tabs[k] for k in ("iq", "io", "anti")]
    in_specs = ([pl.BlockSpec((1, 4 * HALF, tn), lambda b, j: (b, 0, j))]
                + [_resident(c.shape) for c in consts]
                + [pl.BlockSpec((SEQ, tn), col(m_blk0))])
    args = [y, *consts, p]
    if gated:
        in_specs.append(pl.BlockSpec((SEQ, tn), col(g_blk0)))
        args.append(p)
    return pl.pallas_call(
        functools.partial(_conv_inv_kernel, gated=gated),
        grid=(batch, ct),
        in_specs=in_specs,
        out_specs=pl.BlockSpec((SEQ, tn), lambda b, j: (b, j)),
        out_shape=jax.ShapeDtypeStruct((batch * SEQ, w), BF16),
        scratch_shapes=[pltpu.VMEM((2, 4, QUARTER, MXU_COLS), F32),
                        pltpu.VMEM((2, 2, HALF, MXU_COLS), F32)],
        compiler_params=_cparams(("arbitrary", "arbitrary")),
        name="long_conv_inv",
    )(*args)


def _rope_tables():
    pos = np.arange(SEQ)
    row = (pos // GRID_W).astype(np.float32)
    col = (pos % GRID_W).astype(np.float32)
    half = HEAD_DIM // 2
    inv = (ROPE_BASE ** (-np.arange(0, half, 2, dtype=np.float32) / half)).astype(np.float32)
    ar = row[:, None] * inv[None]
    ac = col[:, None] * inv[None]
    cos = np.concatenate([np.cos(ar), np.cos(ar), np.cos(ac), np.cos(ac)], axis=1)
    sin = np.concatenate([-np.sin(ar), np.sin(ar), -np.sin(ac), np.sin(ac)], axis=1)
    return jnp.asarray(cos, F32), jnp.asarray(sin, F32)


def _tile_types(tn, kinds):
    per_tile = tn // LANES
    return [kinds[s0:s0 + per_tile] for s0 in range(0, len(kinds), per_tile)]


def kernel(x, c, ctx, c_ctx, norm_g, ada_w, ada_b, attn_w_in, attn_w_out, attn_sink, hy_w_in,
           hy_conv_w, hy_conv_b, hy_w1, hy_b1, hy_w2, hy_b2, hy_w3, hy_b3, hy_freq, hy_bias_d,
           hy_w_out, final_g):
    batch, seq, d = x.shape
    assert (seq, d) == (SEQ, D_MODEL) and ctx.shape[1] == CTX_LEN
    assert norm_g.shape[0] == 2 and attn_w_in.shape[0] == 1 and hy_w_in.shape[0] == 1
    w = HYENA_WIDTH

    cc = jnp.concatenate([c, c_ctx[None], jnp.zeros((8 - batch - 1, d), F32)], axis=0)
    mods, sd = _prep(cc, ada_w, ada_b, hy_w1[0], hy_b1[0], hy_w2[0], hy_b2[0], hy_w3[0], hy_b3[0],
                     hy_freq[0], hy_bias_d[0])
    part = lambda layer, r0, r1, k: mods[layer, r0:r1, None, k * d:(k + 1) * d]

    x2 = x.reshape(batch * seq, d)
    ctx2 = ctx.reshape(batch * CTX_LEN, d)

    tn = 1024
    kinds0 = (["q"] * N_HEADS + ["k"] * N_KV_HEADS + [None] * N_KV_HEADS
              + ["g"] * (ATTN_WIDTH // LANES))
    hx = _norm_mod(x2, norm_g[0], part(0, 0, batch, 0), part(0, 0, batch, 1), tm=1024,
                   rows_per_mod=seq)
    hc = _norm_mod(ctx2, norm_g[0], part(0, batch, batch + 1, 0), part(0, batch, batch + 1, 1),
                   tm=512, rows_per_mod=CTX_LEN)
    px = _proj(hx, attn_w_in[0], 0, ATTN_IN, tm=SEQ, tn=tn, tile_types=_tile_types(tn, kinds0),
               rope=_rope_tables(), q_scale=HEAD_DIM ** -0.5 * LOG2E)
    ckv = _proj(hc, attn_w_in[0], ATTN_WIDTH // tn, 2 * KV_WIDTH, tm=1024, tn=tn)
    og = _attention(px, ckv, attn_sink[0], batch)
    x2, hx = _out_proj(og, attn_w_out[0].astype(BF16), x2, part(0, 0, batch, 2),
                       next_norm=(norm_g[1], part(1, 0, batch, 0), part(1, 0, batch, 1)))

    kinds1 = ["c"] * (3 * w // LANES) + ["g"] * (w // LANES)
    tn1 = 1024
    p = _proj(hx, hy_w_in[0], 0, 4 * w, tm=SEQ, tn=tn1, tile_types=_tile_types(tn1, kinds1),
              conv=(hy_conv_w[0], hy_conv_b[0].reshape(1, 3 * w)))
    tabs = _fold_tables()
    kspec = _filter_spectra(sd, tabs)
    y1 = _conv_fwd(p, 2, kspec, 0, tabs, batch)
    z = _conv_inv(y1, tabs, p, 0)
    y2 = _conv_fwd(z, 0, kspec, 1, tabs, batch)
    yg = _conv_inv(y2, tabs, p, 1, g_blk0=3)
    out, = _out_proj(yg, hy_w_out[0].astype(BF16), x2, part(1, 0, batch, 2), final_g=final_g)
    return out.reshape(batch, seq, d)
```

```python
import functools
import math

import jax
import jax.numpy as jnp
import numpy as np
from jax import lax
from jax.experimental import pallas as pl
from jax.experimental.pallas import tpu as pltpu

F32 = jnp.float32
BF16 = jnp.bfloat16

D_MODEL = 2048
SEQ = 2048
CTX_LEN = 256
GRID_W = 64
HEAD_DIM = 128
N_HEADS = 16
N_KV_HEADS = 4
GQA_GROUP = 4
ATTN_WIDTH = 2048
KV_WIDTH = 512
ATTN_IN = 2 * ATTN_WIDTH + 2 * KV_WIDTH
WINDOW = 128
BLOCK = 128
ROPE_BASE = 10000.0
HYENA_WIDTH = 2048
FILTER_EMB = 33
FILTER_HIDDEN = 64
DECAY_FAST = 0.3
DECAY_SLOW = 1.5
DECAY_TARGET = 1e-2
WINDOW_SHIFT = 0.05
NORM_EPS = 1e-6
NEG_INF = -1e30

LANES = 128
MXU_COLS = 256
ACC_ROWS = 1024
ONES_ROWS = 16
LOG2E = math.log2(math.e)
FFT_N = 2 * SEQ
VMEM_LIMIT = 56 * 1024 * 1024


def _cparams(sem):
    return pltpu.CompilerParams(dimension_semantics=sem, vmem_limit_bytes=VMEM_LIMIT)


def _mods_kernel(c_ref, w_ref, b_ref, o_ref):
    c = c_ref[...]
    s = c * jax.nn.sigmoid(c)
    s_hi = s.astype(BF16)
    s_lo = (s - s_hi.astype(F32)).astype(BF16)
    lhs = jnp.concatenate([s_hi, s_lo], axis=0)
    r = jnp.dot(lhs, w_ref[0].astype(BF16), preferred_element_type=F32)
    o_ref[0] = r[:8] + r[8:] + b_ref[0]


def _rope_slab(t, cos, sin):
    lane = lax.broadcasted_iota(jnp.int32, t.shape, 1)
    first = (lane % 64) < 32
    partner = jnp.where(first, pltpu.roll(t, 96, 1), pltpu.roll(t, 32, 1))
    return t * cos + partner * sin


def _norm_mod_rows(x, g, mul, add):
    ms = jnp.mean(x * x, axis=-1, keepdims=True)
    return x * lax.rsqrt(ms + NORM_EPS) * g * mul + add


def _norm_mod_kernel(x_ref, g_ref, sh_ref, sc_ref, o_ref, *, row_chunk):
    g = g_ref[...]
    mul = 1.0 + sc_ref[0]
    add = sh_ref[0]
    for r in range(0, x_ref.shape[0], row_chunk):
        o_ref[r:r + row_chunk, :] = _norm_mod_rows(x_ref[r:r + row_chunk, :], g, mul, add).astype(o_ref.dtype)


def _norm_mod(x, g, shift, scale, *, tm, rows_per_mod):
    m, d = x.shape
    if shift.shape[0] > 1:
        mod_map = lambda i: ((i * tm) // rows_per_mod, 0, 0)
    else:
        mod_map = lambda i: (0, 0, 0)
    return pl.pallas_call(
        functools.partial(_norm_mod_kernel, row_chunk=min(tm, 256)),
        grid=(m // tm,),
        in_specs=[pl.BlockSpec((tm, d), lambda i: (i, 0)), pl.BlockSpec((1, d), lambda i: (0, 0)),
                  pl.BlockSpec((1, 1, d), mod_map), pl.BlockSpec((1, 1, d), mod_map)],
        out_specs=pl.BlockSpec((tm, d), lambda i: (i, 0)),
        out_shape=jax.ShapeDtypeStruct((m, d), BF16),
        compiler_params=_cparams(("arbitrary",)),
        name="norm_mod",
    )(x, g.reshape(1, d), shift, scale)


def _proj_kernel(*refs, tile_types, has_rope, has_conv, q_scale):
    if has_rope:
        hx_ref, w_ref, cos_ref, sin_ref, o_ref, wb_ref = refs
    elif has_conv:
        hx_ref, w_ref, cw_ref, cb_ref, o_ref, wb_ref = refs
    else:
        hx_ref, w_ref, o_ref, wb_ref = refs
    j = pl.program_id(0)
    tm, tn = o_ref.shape

    @pl.when(pl.program_id(1) == 0)
    def _():
        wb_ref[...] = w_ref[...].astype(BF16)

    w_ref = wb_ref

    def plain():
        for c0 in range(0, tn, MXU_COLS):
            o_ref[:, c0:c0 + MXU_COLS] = chunk_dot(c0).astype(o_ref.dtype)

    def chunk_dot(c0):
        pieces = [jnp.dot(hx_ref[r:r + ACC_ROWS, :], w_ref[:, c0:c0 + MXU_COLS],
                          preferred_element_type=F32) for r in range(0, tm, ACC_ROWS)]
        return pieces[0] if len(pieces) == 1 else jnp.concatenate(pieces, axis=0)

    def gated():
        for c0 in range(0, tn, MXU_COLS):
            half = 0.5 * chunk_dot(c0)
            o_ref[:, c0:c0 + MXU_COLS] = (half + half * jnp.tanh(half)).astype(o_ref.dtype)

    def short_conv():
        for c0 in range(0, tn, MXU_COLS):
            acc = chunk_dot(c0)
            for h in range(MXU_COLS // LANES):
                lo = c0 + h * LANES
                u = acc[:, h * LANES:(h + 1) * LANES]
                w3 = cw_ref[:, lo:lo + LANES]
                b = cb_ref[:, lo:lo + LANES]
                y = pltpu.roll(u, 1, 0) * w3[0:1] + u * w3[1:2] + pltpu.roll(u, tm - 1, 0) * w3[2:3] + b
                o_ref[:, lo:lo + LANES] = y.astype(o_ref.dtype)
                first = u[0:1] * w3[1:2] + u[1:2] * w3[2:3] + b
                last = u[tm - 2:tm - 1] * w3[0:1] + u[tm - 1:tm] * w3[1:2] + b
                o_ref[0:1, lo:lo + LANES] = first.astype(o_ref.dtype)
                o_ref[tm - 1:tm, lo:lo + LANES] = last.astype(o_ref.dtype)

    def roped(types):
        cos = cos_ref[...]
        sin = sin_ref[...]
        for c0 in range(0, tn, MXU_COLS):
            acc = chunk_dot(c0)
            for h in range(MXU_COLS // LANES):
                ty = types[c0 // LANES + h]
                slab = acc[:, h * LANES:(h + 1) * LANES]
                if ty is not None:
                    slab = _rope_slab(slab, cos, sin)
                    if ty == "q":
                        slab = slab * q_scale
                lo = c0 + h * LANES
                o_ref[:, lo:lo + LANES] = slab.astype(o_ref.dtype)

    if tile_types is None:
        plain()
        return

    groups = {}
    for t, types in enumerate(tile_types):
        groups.setdefault(tuple(types), []).append(t)
    for types, tiles in groups.items():
        cond = functools.reduce(jnp.logical_or, [j == t for t in tiles])
        if all(ty is None for ty in types):
            pl.when(cond)(plain)
        elif all(ty == "g" for ty in types):
            pl.when(cond)(gated)
        elif all(ty == "c" for ty in types):
            pl.when(cond)(short_conv)
        else:
            assert "g" not in types and "c" not in types
            pl.when(cond)(functools.partial(roped, types))


def _proj(hx, w, col_blk0, n, *, tm, tn, tile_types=None, rope=None, conv=None, q_scale=1.0):
    m, d = hx.shape
    in_specs = [
        pl.BlockSpec((tm, d), lambda j, i: (i, 0)),
        pl.BlockSpec((d, tn), lambda j, i: (0, col_blk0 + j)),
    ]
    args = [hx, w]
    scratch = [pltpu.VMEM((d, tn), BF16)]
    if rope is not None:
        cos, sin = rope
        seq_tiles = cos.shape[0] // tm
        in_specs += [pl.BlockSpec((tm, LANES), lambda j, i: (i % seq_tiles, 0))] * 2
        args += [cos, sin]
    if conv is not None:
        assert rope is None and tm == SEQ
        cw, cb = conv
        last = cw.shape[1] // tn - 1
        in_specs += [pl.BlockSpec((cw.shape[0], tn), lambda j, i: (0, jnp.minimum(j, last))),
                     pl.BlockSpec((1, tn), lambda j, i: (0, jnp.minimum(j, last)))]
        args += [cw, cb]
    return pl.pallas_call(
        functools.partial(_proj_kernel, tile_types=tile_types, has_rope=rope is not None,
                          has_conv=conv is not None, q_scale=q_scale),
        grid=(n // tn, m // tm),
        in_specs=in_specs,
        out_specs=pl.BlockSpec((tm, tn), lambda j, i: (i, j)),
        out_shape=jax.ShapeDtypeStruct((m, n), BF16),
        scratch_shapes=scratch,
        compiler_params=_cparams(("arbitrary", "arbitrary")),
        name="proj",
    )(*args)


def _attn_kernel(sink_ref, q_ref, k_ref, v_ref, g_ref, kc_ref, vc_ref, o_ref,
                 vt_ref, vct_ref, bias_ref):
    kh = pl.program_id(1)
    band = 3 * BLOCK
    cols = GQA_GROUP * BLOCK
    n_blocks = SEQ // BLOCK

    vt_ref[:HEAD_DIM, :] = v_ref[...].T
    vct_ref[:HEAD_DIM, :] = vc_ref[...].T

    @pl.when((pl.program_id(0) == 0) & (kh == 0))
    def _():
        vt_ref[HEAD_DIM:, :] = jnp.ones((ONES_ROWS, SEQ), BF16)
        vct_ref[HEAD_DIM:, :] = jnp.ones((ONES_ROWS, CTX_LEN), BF16)
        krow = lax.broadcasted_iota(jnp.int32, (band, cols), 0)
        qcol = lax.broadcasted_iota(jnp.int32, (band, cols), 1) % BLOCK
        for idx, off in enumerate((0, -BLOCK, -2 * BLOCK)):
            bias_ref[idx] = jnp.where(jnp.abs(krow - qcol + off) <= WINDOW, 0.0, NEG_INF)

    kc = kc_ref[...]
    vct = vct_ref[...]
    sink_row = jnp.concatenate(
        [jnp.full((1, BLOCK), sink_ref[kh * GQA_GROUP + h] * LOG2E, F32) for h in range(GQA_GROUP)],
        axis=1)

    def body(n, carry):
        q0 = pl.multiple_of(n * BLOCK, BLOCK)
        ks = pl.multiple_of(jnp.clip((n - 1) * BLOCK, 0, SEQ - band), BLOCK)
        bidx = jnp.where(n == 0, 0, jnp.where(n == n_blocks - 1, 2, 1))
        qs = q_ref[pl.ds(q0, BLOCK), :]
        q4t = jnp.concatenate([qs[:, h * LANES:(h + 1) * LANES].T for h in range(GQA_GROUP)],
                              axis=1)
        kb = k_ref[pl.ds(ks, band), :]
        s_loc = jnp.dot(kb, q4t, preferred_element_type=F32) + bias_ref[bidx]
        s_ctx = jnp.dot(kc, q4t, preferred_element_type=F32)
        m = jnp.maximum(jnp.maximum(jnp.max(s_loc, axis=0, keepdims=True),
                                    jnp.max(s_ctx, axis=0, keepdims=True)), sink_row)
        p_loc = jnp.exp2(s_loc - m).astype(BF16)
        p_ctx = jnp.exp2(s_ctx - m).astype(BF16)
        ox = (jnp.dot(vt_ref[:, pl.ds(ks, band)], p_loc, preferred_element_type=F32)
              + jnp.dot(vct, p_ctx, preferred_element_type=F32))
        den = ox[HEAD_DIM:HEAD_DIM + 1, :] + jnp.exp2(sink_row - m)
        ot = ox[:HEAD_DIM, :] * (1.0 / den)
        gs = g_ref[pl.ds(q0, BLOCK), :].astype(F32)
        for h in range(GQA_GROUP):
            oh = ot[:, h * LANES:(h + 1) * LANES].T * gs[:, h * LANES:(h + 1) * LANES]
            o_ref[pl.ds(q0, BLOCK), h * LANES:(h + 1) * LANES] = oh.astype(o_ref.dtype)
        return carry

    lax.fori_loop(0, n_blocks, body, 0, unroll=8)


def _attention(px, ckv, sink, batch):
    gw = GQA_GROUP * HEAD_DIM
    k_blk0 = ATTN_WIDTH // HEAD_DIM
    v_blk0 = (ATTN_WIDTH + KV_WIDTH) // HEAD_DIM
    g_blk0 = (ATTN_WIDTH + 2 * KV_WIDTH) // gw
    return pl.pallas_call(
        _attn_kernel,
        grid=(batch, N_KV_HEADS),
        in_specs=[
            pl.BlockSpec(memory_space=pltpu.SMEM),
            pl.BlockSpec((SEQ, gw), lambda b, h: (b, h)),
            pl.BlockSpec((SEQ, HEAD_DIM), lambda b, h: (b, k_blk0 + h)),
            pl.BlockSpec((SEQ, HEAD_DIM), lambda b, h: (b, v_blk0 + h)),
            pl.BlockSpec((SEQ, gw), lambda b, h: (b, g_blk0 + h)),
            pl.BlockSpec((CTX_LEN, HEAD_DIM), lambda b, h: (b, h)),
            pl.BlockSpec((CTX_LEN, HEAD_DIM), lambda b, h: (b, N_KV_HEADS + h)),
        ],
        out_specs=pl.BlockSpec((SEQ, gw), lambda b, h: (b, h)),
        out_shape=jax.ShapeDtypeStruct((batch * SEQ, ATTN_WIDTH), BF16),
        scratch_shapes=[pltpu.VMEM((HEAD_DIM + ONES_ROWS, SEQ), BF16),
                        pltpu.VMEM((HEAD_DIM + ONES_ROWS, CTX_LEN), BF16),
                        pltpu.VMEM((3, 3 * BLOCK, GQA_GROUP * BLOCK), F32)],
        compiler_params=_cparams(("arbitrary", "arbitrary")),
        name="banded_attention",
    )(sink, px, px, px, px, ckv, ckv)


def _out_proj_kernel(*refs, final):
    if final:
        a_ref, w_ref, x_ref, gate_ref, fg_ref, o_ref = refs
    else:
        a_ref, w_ref, x_ref, gate_ref, ng_ref, sh_ref, sc_ref, o_ref, hx_ref = refs
    acc = jnp.dot(a_ref[...], w_ref[...], preferred_element_type=F32)
    y = x_ref[...] + gate_ref[0] * acc
    if final:
        ms = jnp.mean(y * y, axis=-1, keepdims=True)
        y = y * lax.rsqrt(ms + NORM_EPS) * fg_ref[...]
    else:
        hx_ref[...] = _norm_mod_rows(y, ng_ref[...], 1.0 + sc_ref[0], sh_ref[0]).astype(hx_ref.dtype)
    o_ref[...] = y


def _out_proj(a, w, x, gate, *, final_g=None, next_norm=None, tm=512):
    m, d = x.shape
    kdim = a.shape[1]
    final = final_g is not None
    per_batch = lambda i: ((i * tm) // SEQ, 0, 0)
    row_tile = pl.BlockSpec((tm, d), lambda i: (i, 0))
    vec = pl.BlockSpec((1, d), lambda i: (0, 0))
    in_specs = [
        pl.BlockSpec((tm, kdim), lambda i: (i, 0)),
        pl.BlockSpec((kdim, d), lambda i: (0, 0)),
        row_tile,
        pl.BlockSpec((1, 1, d), per_batch),
    ]
    args = [a, w, x, gate]
    out_specs = [row_tile]
    out_shape = [jax.ShapeDtypeStruct((m, d), F32)]
    if final:
        in_specs.append(vec)
        args.append(final_g.reshape(1, d))
    else:
        ng, shift, scale = next_norm
        in_specs += [vec, pl.BlockSpec((1, 1, d), per_batch), pl.BlockSpec((1, 1, d), per_batch)]
        args += [ng.reshape(1, d), shift, scale]
        out_specs.append(row_tile)
        out_shape.append(jax.ShapeDtypeStruct((m, d), BF16))
    return pl.pallas_call(
        functools.partial(_out_proj_kernel, final=final),
        grid=(m // tm,),
        in_specs=in_specs,
        out_specs=out_specs,
        out_shape=out_shape,
        compiler_params=_cparams(("arbitrary",)),
        name="out_proj",
    )(*args)


HALF = SEQ // 2
QUARTER = SEQ // 4
FLIP_BLOCK = 256


def _fold_tables():
    def tab(f, n):
        t2 = 2 * np.arange(n, dtype=np.int64) + 1
        ang = ((f[:, None] * t2[None, :]) % (2 * FFT_N)).astype(np.float64) * (math.pi / FFT_N)
        return np.cos(ang), np.sin(ang)

    r = np.arange(QUARTER, dtype=np.int64)
    g = np.arange(HALF, dtype=np.int64)
    f_ee, f_eo, f_o = 4 * r, 4 * r + 2, 2 * g + 1
    cee, see = tab(f_ee, QUARTER)
    ceo, seo = tab(f_eo, QUARTER)
    co, so = tab(f_o, HALF)
    see[0, :] = np.where(np.arange(QUARTER) % 2 == 0, 1.0, -1.0)
    const = lambda a, dt: jnp.asarray(np.ascontiguousarray(a).astype(np.float32)).astype(dt)
    phase = lambda f: np.stack([np.cos(f * (math.pi / FFT_N)), np.sin(f * (math.pi / FFT_N))])
    return dict(
        fq=const(np.stack([cee, see, ceo, seo]), BF16),
        fo=const(np.stack([co, so]), BF16),
        iq=const(np.stack([cee.T, ceo.T, see.T, seo.T]), BF16),
        io=const(np.stack([so.T, co.T]), BF16),
        anti=const(np.eye(FLIP_BLOCK)[::-1], BF16),
        pq=const(np.stack([phase(f_ee), phase(f_eo)], axis=1)[..., None], F32),
        po=const(phase(f_o)[..., None], F32),
    )


def _flip_rows(h, anti):
    nb = h.shape[0] // FLIP_BLOCK
    return jnp.concatenate(
        [jnp.dot(anti[...], h[(nb - 1 - a) * FLIP_BLOCK:(nb - a) * FLIP_BLOCK, :], preferred_element_type=F32)
         for a in range(nb)], axis=0)


def _fold(x, anti):
    half = x.shape[0] // 2
    x0 = x[:half].astype(F32)
    xr = _flip_rows(x[half:], anti)
    return (x0 + xr).astype(BF16), (x0 - xr).astype(BF16)


def _staggered(n, matmuls, finish):
    matmuls(0, 0)
    for t in range(1, n):
        matmuls(t, t % 2)
        finish(t - 1, (t - 1) % 2)
    finish(n - 1, (n - 1) % 2)


def _col_chunks(n):
    return [slice(c, c + MXU_COLS) for c in range(0, n, MXU_COLS)]


PLANES = {"a_ee": 0, "b_ee": QUARTER, "a_eo": 2 * QUARTER, "b_eo": 3 * QUARTER,
          "a_o": 2 * HALF, "b_o": 3 * HALF}
GROUPS = (("a_ee", "b_ee", QUARTER), ("a_eo", "b_eo", QUARTER), ("a_o", "b_o", HALF))


def _spectrum(fq_ref, fo_ref, x, anti):
    xs, xa = _fold(x, anti)
    xss, xsa = _fold(xs, anti)
    xas, xaa = _fold(xa, anti)
    dq = lambda k, v: jnp.dot(fq_ref[k], v, preferred_element_type=F32)
    do = lambda k, v: jnp.dot(fo_ref[k], v, preferred_element_type=F32)
    return {"a_ee": dq(0, xss), "b_ee": dq(1, xaa), "a_eo": dq(2, xsa), "b_eo": dq(3, xas),
            "a_o": do(0, xa), "b_o": do(1, xs)}


def _plane(ref, lead, name, rows, cols):
    r0 = PLANES[name]
    return ref[lead, r0:r0 + rows, cols]


def _resident(shape):
    zeros = (0,) * len(shape)
    return pl.BlockSpec(shape, lambda *_: zeros, pipeline_mode=pl.Buffered(1))


def _filter_sd_body(w1t_ref, w1c_ref, w1s_ref, b1_ref, w2_ref, b2_ref, fq_ref,
                    w3f_ref, b3f_ref, w3b_ref, b3b_ref, d_ref, o_ref, hid_ref, *, first, j):
    length = o_ref.shape[1]
    tn = o_ref.shape[2]
    hp = lax.Precision.HIGHEST

    @pl.when(first)
    def _():
        n = lax.broadcasted_iota(jnp.int32, (length, LANES), 0).astype(F32)
        lane = lax.broadcasted_iota(jnp.int32, (length, LANES), 1)
        bands = (FILTER_EMB - 1) // 2
        fr_step = (bands - 1 - 1e-4) / (bands - 1)
        fr = jnp.where(lane < bands, 1e-4 + lane.astype(F32) * fr_step, 0.0)
        ang = (2.0 * math.pi * n / length) * fr
        t = n / (length - 1)
        fq = fq_ref[...]
        pre = (t * w1t_ref[...]
               + jnp.dot(jnp.cos(ang), w1c_ref[...], precision=hp, preferred_element_type=F32)
               + jnp.dot(-jnp.sin(ang), w1s_ref[...], precision=hp, preferred_element_type=F32)
               + b1_ref[...])
        hid = jnp.sin(fq * pre)
        hid = jnp.sin(fq * (jnp.dot(hid, w2_ref[...], precision=hp, preferred_element_type=F32)
                            + b2_ref[...]))
        hid_hi = hid.astype(BF16)
        hid_ref[0] = hid_hi
        hid_ref[1] = (hid - hid_hi.astype(F32)).astype(BF16)

    def dot3(w_ref):
        wf = w_ref[...]
        w_hi = wf.astype(BF16)
        w_lo = (wf - w_hi.astype(F32)).astype(BF16)
        return (jnp.dot(hid_ref[0], w_hi, preferred_element_type=F32)
                + jnp.dot(hid_ref[1], w_hi, preferred_element_type=F32)
                + jnp.dot(hid_ref[0], w_lo, preferred_element_type=F32))

    row = lax.broadcasted_iota(jnp.int32, (length, tn), 0)
    chan = (lax.broadcasted_iota(jnp.int32, (length, tn), 1) + j * tn).astype(F32)
    min_decay = math.log(DECAY_TARGET) / DECAY_SLOW
    max_decay = math.log(DECAY_TARGET) / DECAY_FAST
    delta = min_decay + chan * ((max_decay - min_decay) / (HYENA_WIDTH - 1))
    t = row.astype(F32) / (length - 1)
    window = jnp.exp(-t * jnp.abs(delta)) + WINDOW_SHIFT
    hf = (dot3(w3f_ref) + b3f_ref[...]) * window
    hb = (dot3(w3b_ref) + b3b_ref[...]) * window
    hf = jnp.where(row == 0, hf + d_ref[0], hf)
    hb = jnp.where(row == 0, 0.0, hb)
    o_ref[0] = (hf + hb).astype(o_ref.dtype)
    o_ref[1] = (hf - hb).astype(o_ref.dtype)


def _prep_kernel(c_ref, aw_ref, ab_ref, *refs, filter_steps, ct):
    mods_ref, sd_ref, hid_ref = refs[-3:]
    s = pl.program_id(0)
    _mods_kernel(c_ref, aw_ref, ab_ref, mods_ref)

    @pl.when(s < filter_steps)
    def _():
        _filter_sd_body(*refs[:-3], sd_ref, hid_ref, first=s == 0, j=s % ct)


def _prep(cc, ada_w, ada_b, w1, b1, w2, b2, w3, b3, freq, bias_d, *, tn_mods=1024, tn=512):
    depth, d, n_mods = ada_w.shape
    mt = n_mods // tn_mods
    hpad = LANES - FILTER_HIDDEN
    bands = (FILTER_EMB - 1) // 2
    w = HYENA_WIDTH
    ct = w // tn
    filter_steps = 2 * ct
    assert filter_steps <= depth * mt
    n = w3.shape[1]
    w1t = jnp.pad(w1[0:1], ((0, 0), (0, hpad)))
    w1c = jnp.pad(w1[1:1 + bands], ((0, LANES - bands), (0, hpad)))
    w1s = jnp.pad(w1[1 + bands:], ((0, LANES - bands), (0, hpad)))
    b1p = jnp.pad(b1.reshape(1, -1), ((0, 0), (0, hpad)))
    w2p = jnp.pad(w2, ((0, hpad), (0, hpad)))
    b2p = jnp.pad(b2.reshape(1, -1), ((0, 0), (0, hpad)))
    fqp = jnp.pad(freq.reshape(1, -1), ((0, 0), (0, hpad)))
    w3p = jnp.pad(w3, ((0, hpad), (0, 0)))
    b3r = b3.reshape(1, n)
    small = lambda shape: pl.BlockSpec(shape, lambda s: (0, 0))
    order = lambda s: jnp.minimum(s, filter_steps - 1) // ct
    chan = lambda s: jnp.minimum(s, filter_steps - 1) % ct
    fwd_cols = lambda s: (0, 2 * order(s) * ct + chan(s))
    bwd_cols = lambda s: (0, (2 * order(s) + 1) * ct + chan(s))
    return pl.pallas_call(
        functools.partial(_prep_kernel, filter_steps=filter_steps, ct=ct),
        grid=(depth * mt,),
        in_specs=[pl.BlockSpec((8, d), lambda s: (0, 0)),
                  pl.BlockSpec((1, d, tn_mods), lambda s: (s // mt, 0, s % mt)),
                  pl.BlockSpec((1, 1, tn_mods), lambda s: (s // mt, 0, s % mt)),
                  small((1, LANES)), small((LANES, LANES)), small((LANES, LANES)), small((1, LANES)),
                  small((LANES, LANES)), small((1, LANES)), small((1, LANES)),
                  pl.BlockSpec((LANES, tn), fwd_cols), pl.BlockSpec((1, tn), fwd_cols),
                  pl.BlockSpec((LANES, tn), bwd_cols), pl.BlockSpec((1, tn), bwd_cols),
                  pl.BlockSpec((1, 1, tn), lambda s: (order(s), 0, chan(s)))],
        out_specs=[pl.BlockSpec((1, 8, tn_mods), lambda s: (s // mt, 0, s % mt)),
                   pl.BlockSpec((2, SEQ, tn), lambda s: (0, 0, order(s) * ct + chan(s)))],
        out_shape=[jax.ShapeDtypeStruct((depth, 8, n_mods), F32),
                   jax.ShapeDtypeStruct((2, SEQ, 2 * w), BF16)],
        scratch_shapes=[pltpu.VMEM((2, SEQ, LANES), BF16)],
        compiler_params=_cparams(("arbitrary",)),
        name="mods_and_filters",
    )(cc, ada_w, ada_b.reshape(depth, 1, n_mods),
      w1t, w1c, w1s, b1p, w2p, b2p, fqp, w3p, b3r, w3p, b3r, bias_d.reshape(2, 1, w))


def _filter_spec_kernel(s_ref, d_ref, fq_ref, fo_ref, anti_ref, pq_ref, po_ref, k_ref):
    phases = {"a_ee": (pq_ref[0, 0], pq_ref[1, 0]), "a_eo": (pq_ref[0, 1], pq_ref[1, 1]),
              "a_o": (po_ref[0], po_ref[1])}
    is0 = lax.broadcasted_iota(jnp.int32, (QUARTER, MXU_COLS), 0) == 0
    for cols in _col_chunks(k_ref.shape[2]):
        sp = _spectrum(fq_ref, fo_ref, s_ref[0, :, cols], anti_ref)
        dp = _spectrum(fq_ref, fo_ref, d_ref[0, :, cols], anti_ref)
        for a, b, rows in GROUPS:
            cos, sin = phases[a]
            ka = sp[a] * cos + sp[b] * sin
            kb = dp[b] * cos - dp[a] * sin
            if a == "a_ee":
                scale = jnp.where(is0, 1.0 / FFT_N, 2.0 / FFT_N)
                kb = jnp.where(is0, sp[b], kb)
            else:
                scale = 2.0 / FFT_N
            k_ref[0, PLANES[a]:PLANES[a] + rows, cols] = (ka * scale).astype(k_ref.dtype)
            k_ref[0, PLANES[b]:PLANES[b] + rows, cols] = (kb * scale).astype(k_ref.dtype)


def _filter_spectra(sd, tabs, *, tn=512):
    w = HYENA_WIDTH
    ct = w // tn
    consts = [tabs[k] for k in ("fq", "fo", "anti", "pq", "po")]
    return pl.pallas_call(
        _filter_spec_kernel,
        grid=(2, ct),
        in_specs=[
            pl.BlockSpec((1, SEQ, tn), lambda o, j: (0, 0, o * ct + j)),
            pl.BlockSpec((1, SEQ, tn), lambda o, j: (1, 0, o * ct + j)),
        ] + [_resident(c.shape) for c in consts],
        out_specs=pl.BlockSpec((1, 4 * HALF, tn), lambda o, j: (o, 0, j)),
        out_shape=jax.ShapeDtypeStruct((2, 4 * HALF, w), F32),
        compiler_params=_cparams(("arbitrary", "arbitrary")),
        name="filter_spectra",
    )(sd, sd, *consts)


def _conv_fwd_kernel(x_ref, fq_ref, fo_ref, anti_ref, k_ref, y_ref):
    is0 = lax.broadcasted_iota(jnp.int32, (QUARTER, MXU_COLS), 0) == 0
    for cols in _col_chunks(y_ref.shape[2]):
        sp = _spectrum(fq_ref, fo_ref, x_ref[:, cols], anti_ref)
        for a, b, rows in GROUPS:
            ka = _plane(k_ref, 0, a, rows, cols).astype(F32)
            kb = _plane(k_ref, 0, b, rows, cols).astype(F32)
            bkb = sp[b] * kb
            if a == "a_ee":
                ya = sp[a] * ka - jnp.where(is0, 0.0, bkb)
                yb = jnp.where(is0, bkb, sp[a] * kb + sp[b] * ka)
            else:
                ya = sp[a] * ka - bkb
                yb = sp[a] * kb + sp[b] * ka
            y_ref[0, PLANES[a]:PLANES[a] + rows, cols] = ya.astype(y_ref.dtype)
            y_ref[0, PLANES[b]:PLANES[b] + rows, cols] = yb.astype(y_ref.dtype)


def _conv_fwd(x, x_blk0, kspec, order, tabs, batch, *, tn=512):
    w = HYENA_WIDTH
    ct = w // tn
    consts = [tabs[k] for k in ("fq", "fo", "anti")]
    return pl.pallas_call(
        _conv_fwd_kernel,
        grid=(ct, batch),
        in_specs=[pl.BlockSpec((SEQ, tn), lambda j, b: (b, x_blk0 * ct + j))]
        + [_resident(c.shape) for c in consts]
        + [pl.BlockSpec((1, 4 * HALF, tn), lambda j, b: (order, 0, j))],
        out_specs=pl.BlockSpec((1, 4 * HALF, tn), lambda j, b: (b, 0, j)),
        out_shape=jax.ShapeDtypeStruct((batch, 4 * HALF, w), BF16),
        compiler_params=_cparams(("arbitrary", "arbitrary")),
        name="long_conv_fwd",
    )(x, *consts, kspec)


def _conv_inv_kernel(*refs, gated):
    if gated:
        y_ref, iq_ref, io_ref, anti_ref, xm_ref, g_ref, o_ref, accq_ref, acco_ref = refs
    else:
        y_ref, iq_ref, io_ref, anti_ref, xm_ref, o_ref, accq_ref, acco_ref = refs
    chunks = _col_chunks(o_ref.shape[1])
    flipped = lambda v: _flip_rows(v.astype(BF16), anti_ref)

    def matmuls(t, slot):
        cols = chunks[t]
        for k, name in enumerate(("a_ee", "a_eo", "b_ee", "b_eo")):
            accq_ref[slot, k] = jnp.dot(iq_ref[k], _plane(y_ref, 0, name, QUARTER, cols),
                                        preferred_element_type=F32)
        for k, name in enumerate(("b_o", "a_o")):
            acco_ref[slot, k] = jnp.dot(io_ref[k], _plane(y_ref, 0, name, HALF, cols),
                                        preferred_element_type=F32)

    def finish(t, slot):
        cols = chunks[t]
        p, r, q, s = (accq_ref[slot, k] for k in range(4))
        ea = jnp.concatenate([p + r, flipped(p - r)], axis=0)
        eb = jnp.concatenate([q + s, flipped(s - q)], axis=0)
        p1 = ea + acco_ref[slot, 0]
        p2 = eb + acco_ref[slot, 1]
        for rows, y in ((slice(0, HALF), p1 + p2), (slice(HALF, SEQ), flipped(p1 - p2))):
            out = y * xm_ref[rows, cols].astype(F32)
            if gated:
                out = out * g_ref[rows, cols].astype(F32)
            o_ref[rows, cols] = out.astype(o_ref.dtype)

    _staggered(len(chunks), matmuls, finish)


def _conv_inv(y, tabs, p, m_blk0, g_blk0=None, *, tn=512):
    batch = y.shape[0]
    w = HYENA_WIDTH
    ct = w // tn
    gated = g_blk0 is not None
    col = lambda blk0: (lambda b, j: (b, blk0 * ct + j))
    consts = [tabs[k] for k in ("iq", "io", "anti")]
    in_specs = ([pl.BlockSpec((1, 4 * HALF, tn), lambda b, j: (b, 0, j))]
                + [_resident(c.shape) for c in consts]
                + [pl.BlockSpec((SEQ, tn), col(m_blk0))])
    args = [y, *consts, p]
    if gated:
        in_specs.append(pl.BlockSpec((SEQ, tn), col(g_blk0)))
        args.append(p)
    return pl.pallas_call(
        functools.partial(_conv_inv_kernel, gated=gated),
        grid=(batch, ct),
        in_specs=in_specs,
        out_specs=pl.BlockSpec((SEQ, tn), lambda b, j: (b, j)),
        out_shape=jax.ShapeDtypeStruct((batch * SEQ, w), BF16),
        scratch_shapes=[pltpu.VMEM((2, 4, QUARTER, MXU_COLS), F32),
                        pltpu.VMEM((2, 2, HALF, MXU_COLS), F32)],
        compiler_params=_cparams(("arbitrary", "arbitrary")),
        name="long_conv_inv",
    )(*args)


def _rope_tables():
    pos = np.arange(SEQ)
    row = (pos // GRID_W).astype(np.float32)
    col = (pos % GRID_W).astype(np.float32)
    half = HEAD_DIM // 2
    inv = (ROPE_BASE ** (-np.arange(0, half, 2, dtype=np.float32) / half)).astype(np.float32)
    ar = row[:, None] * inv[None]
    ac = col[:, None] * inv[None]
    cos = np.concatenate([np.cos(ar), np.cos(ar), np.cos(ac), np.cos(ac)], axis=1)
    sin = np.concatenate([-np.sin(ar), np.sin(ar), -np.sin(ac), np.sin(ac)], axis=1)
    return jnp.asarray(cos, F32), jnp.asarray(sin, F32)


def _tile_types(tn, kinds):
    per_tile = tn // LANES
    return [kinds[s0:s0 + per_tile] for s0 in range(0, len(kinds), per_tile)]


def kernel(x, c, ctx, c_ctx, norm_g, ada_w, ada_b, attn_w_in, attn_w_out, attn_sink, hy_w_in,
           hy_conv_w, hy_conv_b, hy_w1, hy_b1, hy_w2, hy_b2, hy_w3, hy_b3, hy_freq, hy_bias_d,
           hy_w_out, final_g):
    batch, seq, d = x.shape
    assert (seq, d) == (SEQ, D_MODEL) and ctx.shape[1] == CTX_LEN
    assert norm_g.shape[0] == 2 and attn_w_in.shape[0] == 1 and hy_w_in.shape[0] == 1
    w = HYENA_WIDTH

    cc = jnp.concatenate([c, c_ctx[None], jnp.zeros((8 - batch - 1, d), F32)], axis=0)
    mods, sd = _prep(cc, ada_w, ada_b, hy_w1[0], hy_b1[0], hy_w2[0], hy_b2[0], hy_w3[0], hy_b3[0],
                     hy_freq[0], hy_bias_d[0])
    part = lambda layer, r0, r1, k: mods[layer, r0:r1, None, k * d:(k + 1) * d]

    x2 = x.reshape(batch * seq, d)
    ctx2 = ctx.reshape(batch * CTX_LEN, d)

    tn = 1024
    kinds0 = (["q"] * N_HEADS + ["k"] * N_KV_HEADS + [None] * N_KV_HEADS
              + ["g"] * (ATTN_WIDTH // LANES))
    hx = _norm_mod(x2, norm_g[0], part(0, 0, batch, 0), part(0, 0, batch, 1), tm=1024,
                   rows_per_mod=seq)
    hc = _norm_mod(ctx2, norm_g[0], part(0, batch, batch + 1, 0), part(0, batch, batch + 1, 1),
                   tm=512, rows_per_mod=CTX_LEN)
    px = _proj(hx, attn_w_in[0], 0, ATTN_IN, tm=SEQ, tn=tn, tile_types=_tile_types(tn, kinds0),
               rope=_rope_tables(), q_scale=HEAD_DIM ** -0.5 * LOG2E)
    ckv = _proj(hc, attn_w_in[0], ATTN_WIDTH // tn, 2 * KV_WIDTH, tm=1024, tn=tn)
    og = _attention(px, ckv, attn_sink[0], batch)
    x2, hx = _out_proj(og, attn_w_out[0].astype(BF16), x2, part(0, 0, batch, 2),
                       next_norm=(norm_g[1], part(1, 0, batch, 0), part(1, 0, batch, 1)))

    kinds1 = ["c"] * (3 * w // LANES) + ["g"] * (w // LANES)
    tn1 = 1024
    p = _proj(hx, hy_w_in[0], 0, 4 * w, tm=SEQ, tn=tn1, tile_types=_tile_types(tn1, kinds1),
              conv=(hy_conv_w[0], hy_conv_b[0].reshape(1, 3 * w)))
    tabs = _fold_tables()
    kspec = _filter_spectra(sd, tabs)
    y1 = _conv_fwd(p, 2, kspec, 0, tabs, batch)
    z = _conv_inv(y1, tabs, p, 0)
    y2 = _conv_fwd(z, 0, kspec, 1, tabs, batch)
    yg = _conv_inv(y2, tabs, p, 1, g_blk0=3)
    out, = _out_proj(yg, hy_w_out[0].astype(BF16), x2, part(1, 0, batch, 2), final_g=final_g)
    return out.reshape(batch, seq, d)
```

```python
import functools
import math

import jax
import jax.numpy as jnp
import numpy as np
from jax import lax
from jax.experimental import pallas as pl
from jax.experimental.pallas import tpu as pltpu

F32 = jnp.float32
BF16 = jnp.bfloat16

D_MODEL = 2048
SEQ = 2048
CTX_LEN = 256
GRID_W = 64
HEAD_DIM = 128
N_HEADS = 16
N_KV_HEADS = 4
GQA_GROUP = 4
ATTN_WIDTH = 2048
KV_WIDTH = 512
ATTN_IN = 2 * ATTN_WIDTH + 2 * KV_WIDTH
WINDOW = 128
BLOCK = 128
ROPE_BASE = 10000.0
HYENA_WIDTH = 2048
FILTER_EMB = 33
FILTER_HIDDEN = 64
DECAY_FAST = 0.3
DECAY_SLOW = 1.5
DECAY_TARGET = 1e-2
WINDOW_SHIFT = 0.05
NORM_EPS = 1e-6
NEG_INF = -1e30

LANES = 128
MXU_COLS = 256
ACC_ROWS = 1024
ONES_ROWS = 16
LOG2E = math.log2(math.e)
FFT_N = 2 * SEQ
VMEM_LIMIT = 56 * 1024 * 1024


def _cparams(sem):
    return pltpu.CompilerParams(dimension_semantics=sem, vmem_limit_bytes=VMEM_LIMIT)


def _mods_kernel(c_ref, w_ref, b_ref, o_ref):
    c = c_ref[...]
    s = c * jax.nn.sigmoid(c)
    s_hi = s.astype(BF16)
    s_lo = (s - s_hi.astype(F32)).astype(BF16)
    lhs = jnp.concatenate([s_hi, s_lo], axis=0)
    r = jnp.dot(lhs, w_ref[0].astype(BF16), preferred_element_type=F32)
    o_ref[0] = r[:8] + r[8:] + b_ref[0]


def _rope_slab(t, cos, sin):
    lane = lax.broadcasted_iota(jnp.int32, t.shape, 1)
    first = (lane % 64) < 32
    partner = jnp.where(first, pltpu.roll(t, 96, 1), pltpu.roll(t, 32, 1))
    return t * cos + partner * sin


def _norm_mod_rows(x, g, mul, add):
    ms = jnp.mean(x * x, axis=-1, keepdims=True)
    return x * lax.rsqrt(ms + NORM_EPS) * g * mul + add


def _norm_mod_kernel(x_ref, g_ref, sh_ref, sc_ref, o_ref, *, row_chunk):
    g = g_ref[...]
    mul = 1.0 + sc_ref[0]
    add = sh_ref[0]
    for r in range(0, x_ref.shape[0], row_chunk):
        o_ref[r:r + row_chunk, :] = _norm_mod_rows(x_ref[r:r + row_chunk, :], g, mul, add).astype(o_ref.dtype)


def _norm_mod(x, g, shift, scale, *, tm, rows_per_mod):
    m, d = x.shape
    if shift.shape[0] > 1:
        mod_map = lambda i: ((i * tm) // rows_per_mod, 0, 0)
    else:
        mod_map = lambda i: (0, 0, 0)
    return pl.pallas_call(
        functools.partial(_norm_mod_kernel, row_chunk=min(tm, 256)),
        grid=(m // tm,),
        in_specs=[pl.BlockSpec((tm, d), lambda i: (i, 0)), pl.BlockSpec((1, d), lambda i: (0, 0)),
                  pl.BlockSpec((1, 1, d), mod_map), pl.BlockSpec((1, 1, d), mod_map)],
        out_specs=pl.BlockSpec((tm, d), lambda i: (i, 0)),
        out_shape=jax.ShapeDtypeStruct((m, d), BF16),
        compiler_params=_cparams(("arbitrary",)),
        name="norm_mod",
    )(x, g.reshape(1, d), shift, scale)


def _ctx_kv_kernel(x_ref, g_ref, sh_ref, sc_ref, w_ref, o_ref, hx_ref, *, row_chunk):
    g = g_ref[...]
    mul = 1.0 + sc_ref[0]
    add = sh_ref[0]
    for r in range(0, x_ref.shape[0], row_chunk):
        hx_ref[r:r + row_chunk, :] = _norm_mod_rows(x_ref[r:r + row_chunk, :], g, mul, add).astype(BF16)
    wb = w_ref[...].astype(BF16)
    for c0 in range(0, o_ref.shape[1], MXU_COLS):
        o_ref[:, c0:c0 + MXU_COLS] = jnp.dot(hx_ref[...], wb[:, c0:c0 + MXU_COLS],
                                             preferred_element_type=F32).astype(o_ref.dtype)


def _ctx_kv(ctx, g, shift, scale, w, col0, n):
    m, d = ctx.shape
    assert col0 % n == 0 and m <= ACC_ROWS
    return pl.pallas_call(
        functools.partial(_ctx_kv_kernel, row_chunk=256),
        grid=(1,),
        in_specs=[pl.BlockSpec((m, d), lambda i: (0, 0)), pl.BlockSpec((1, d), lambda i: (0, 0)),
                  pl.BlockSpec((1, 1, d), lambda i: (0, 0, 0)), pl.BlockSpec((1, 1, d), lambda i: (0, 0, 0)),
                  pl.BlockSpec((d, n), lambda i: (0, col0 // n))],
        out_specs=pl.BlockSpec((m, n), lambda i: (0, 0)),
        out_shape=jax.ShapeDtypeStruct((m, n), BF16),
        scratch_shapes=[pltpu.VMEM((m, d), BF16)],
        compiler_params=_cparams(("arbitrary",)),
        name="ctx_kv",
    )(ctx, g.reshape(1, d), shift, scale, w)


def _proj_kernel(*refs, tile_types, has_rope, has_conv, q_scale):
    if has_rope:
        hx_ref, w_ref, cos_ref, sin_ref, o_ref, wb_ref = refs
    elif has_conv:
        hx_ref, w_ref, cw_ref, cb_ref, o_ref, wb_ref = refs
    else:
        hx_ref, w_ref, o_ref, wb_ref = refs
    j = pl.program_id(0)
    tm, tn = o_ref.shape

    @pl.when(pl.program_id(1) == 0)
    def _():
        wb_ref[...] = w_ref[...].astype(BF16)

    w_ref = wb_ref

    def plain():
        for c0 in range(0, tn, MXU_COLS):
            o_ref[:, c0:c0 + MXU_COLS] = chunk_dot(c0).astype(o_ref.dtype)

    def chunk_dot(c0):
        pieces = [jnp.dot(hx_ref[r:r + ACC_ROWS, :], w_ref[:, c0:c0 + MXU_COLS],
                          preferred_element_type=F32) for r in range(0, tm, ACC_ROWS)]
        return pieces[0] if len(pieces) == 1 else jnp.concatenate(pieces, axis=0)

    def gated():
        for c0 in range(0, tn, MXU_COLS):
            half = 0.5 * chunk_dot(c0)
            o_ref[:, c0:c0 + MXU_COLS] = (half + half * jnp.tanh(half)).astype(o_ref.dtype)

    def short_conv():
        for c0 in range(0, tn, MXU_COLS):
            acc = chunk_dot(c0)
            for h in range(MXU_COLS // LANES):
                lo = c0 + h * LANES
                u = acc[:, h * LANES:(h + 1) * LANES]
                w3 = cw_ref[:, lo:lo + LANES]
                b = cb_ref[:, lo:lo + LANES]
                y = pltpu.roll(u, 1, 0) * w3[0:1] + u * w3[1:2] + pltpu.roll(u, tm - 1, 0) * w3[2:3] + b
                o_ref[:, lo:lo + LANES] = y.astype(o_ref.dtype)
                first = u[0:1] * w3[1:2] + u[1:2] * w3[2:3] + b
                last = u[tm - 2:tm - 1] * w3[0:1] + u[tm - 1:tm] * w3[1:2] + b
                o_ref[0:1, lo:lo + LANES] = first.astype(o_ref.dtype)
                o_ref[tm - 1:tm, lo:lo + LANES] = last.astype(o_ref.dtype)

    def roped(types):
        cos = cos_ref[...]
        sin = sin_ref[...]
        for c0 in range(0, tn, MXU_COLS):
            acc = chunk_dot(c0)
            for h in range(MXU_COLS // LANES):
                ty = types[c0 // LANES + h]
                slab = acc[:, h * LANES:(h + 1) * LANES]
                if ty is not None:
                    slab = _rope_slab(slab, cos, sin)
                    if ty == "q":
                        slab = slab * q_scale
                lo = c0 + h * LANES
                o_ref[:, lo:lo + LANES] = slab.astype(o_ref.dtype)

    if tile_types is None:
        plain()
        return

    groups = {}
    for t, types in enumerate(tile_types):
        groups.setdefault(tuple(types), []).append(t)
    for types, tiles in groups.items():
        cond = functools.reduce(jnp.logical_or, [j == t for t in tiles])
        if all(ty is None for ty in types):
            pl.when(cond)(plain)
        elif all(ty == "g" for ty in types):
            pl.when(cond)(gated)
        elif all(ty == "c" for ty in types):
            pl.when(cond)(short_conv)
        else:
            assert "g" not in types and "c" not in types
            pl.when(cond)(functools.partial(roped, types))


def _proj(hx, w, col_blk0, n, *, tm, tn, tile_types=None, rope=None, conv=None, q_scale=1.0):
    m, d = hx.shape
    in_specs = [
        pl.BlockSpec((tm, d), lambda j, i: (i, 0)),
        pl.BlockSpec((d, tn), lambda j, i: (0, col_blk0 + j)),
    ]
    args = [hx, w]
    scratch = [pltpu.VMEM((d, tn), BF16)]
    if rope is not None:
        cos, sin = rope
        seq_tiles = cos.shape[0] // tm
        in_specs += [pl.BlockSpec((tm, LANES), lambda j, i: (i % seq_tiles, 0))] * 2
        args += [cos, sin]
    if conv is not None:
        assert rope is None and tm == SEQ
        cw, cb = conv
        last = cw.shape[1] // tn - 1
        in_specs += [pl.BlockSpec((cw.shape[0], tn), lambda j, i: (0, jnp.minimum(j, last))),
                     pl.BlockSpec((1, tn), lambda j, i: (0, jnp.minimum(j, last)))]
        args += [cw, cb]
    return pl.pallas_call(
        functools.partial(_proj_kernel, tile_types=tile_types, has_rope=rope is not None,
                          has_conv=conv is not None, q_scale=q_scale),
        grid=(n // tn, m // tm),
        in_specs=in_specs,
        out_specs=pl.BlockSpec((tm, tn), lambda j, i: (i, j)),
        out_shape=jax.ShapeDtypeStruct((m, n), BF16),
        scratch_shapes=scratch,
        compiler_params=_cparams(("arbitrary", "arbitrary")),
        name="proj",
    )(*args)


def _attn_kernel(sink_ref, q_ref, k_ref, v_ref, g_ref, kc_ref, vc_ref, o_ref,
                 vt_ref, vct_ref, bias_ref):
    kh = pl.program_id(1)
    band = 3 * BLOCK
    cols = GQA_GROUP * BLOCK
    n_blocks = SEQ // BLOCK

    vt_ref[:HEAD_DIM, :] = v_ref[...].T
    vct_ref[:HEAD_DIM, :] = vc_ref[...].T

    @pl.when((pl.program_id(0) == 0) & (kh == 0))
    def _():
        vt_ref[HEAD_DIM:, :] = jnp.ones((ONES_ROWS, SEQ), BF16)
        vct_ref[HEAD_DIM:, :] = jnp.ones((ONES_ROWS, CTX_LEN), BF16)
        krow = lax.broadcasted_iota(jnp.int32, (band, cols), 0)
        qcol = lax.broadcasted_iota(jnp.int32, (band, cols), 1) % BLOCK
        for idx, off in enumerate((0, -BLOCK, -2 * BLOCK)):
            bias_ref[idx] = jnp.where(jnp.abs(krow - qcol + off) <= WINDOW, 0.0, NEG_INF)

    kc = kc_ref[...]
    vct = vct_ref[...]
    sink_row = jnp.concatenate(
        [jnp.full((1, BLOCK), sink_ref[kh * GQA_GROUP + h] * LOG2E, F32) for h in range(GQA_GROUP)],
        axis=1)

    def body(n, carry):
        q0 = pl.multiple_of(n * BLOCK, BLOCK)
        ks = pl.multiple_of(jnp.clip((n - 1) * BLOCK, 0, SEQ - band), BLOCK)
        bidx = jnp.where(n == 0, 0, jnp.where(n == n_blocks - 1, 2, 1))
        qs = q_ref[pl.ds(q0, BLOCK), :]
        q4t = jnp.concatenate([qs[:, h * LANES:(h + 1) * LANES].T for h in range(GQA_GROUP)],
                              axis=1)
        kb = k_ref[pl.ds(ks, band), :]
        s_loc = jnp.dot(kb, q4t, preferred_element_type=F32) + bias_ref[bidx]
        s_ctx = jnp.dot(kc, q4t, preferred_element_type=F32)
        m = jnp.maximum(jnp.maximum(jnp.max(s_loc, axis=0, keepdims=True),
                                    jnp.max(s_ctx, axis=0, keepdims=True)), sink_row)
        p_loc = jnp.exp2(s_loc - m).astype(BF16)
        p_ctx = jnp.exp2(s_ctx - m).astype(BF16)
        ox = (jnp.dot(vt_ref[:, pl.ds(ks, band)], p_loc, preferred_element_type=F32)
              + jnp.dot(vct, p_ctx, preferred_element_type=F32))
        den = ox[HEAD_DIM:HEAD_DIM + 1, :] + jnp.exp2(sink_row - m)
        ot = ox[:HEAD_DIM, :] * (1.0 / den)
        gs = g_ref[pl.ds(q0, BLOCK), :].astype(F32)
        for h in range(GQA_GROUP):
            oh = ot[:, h * LANES:(h + 1) * LANES].T * gs[:, h * LANES:(h + 1) * LANES]
            o_ref[pl.ds(q0, BLOCK), h * LANES:(h + 1) * LANES] = oh.astype(o_ref.dtype)
        return carry

    lax.fori_loop(0, n_blocks, body, 0, unroll=8)


def _attention(px, ckv, sink, batch):
    gw = GQA_GROUP * HEAD_DIM
    k_blk0 = ATTN_WIDTH // HEAD_DIM
    v_blk0 = (ATTN_WIDTH + KV_WIDTH) // HEAD_DIM
    g_blk0 = (ATTN_WIDTH + 2 * KV_WIDTH) // gw
    return pl.pallas_call(
        _attn_kernel,
        grid=(batch, N_KV_HEADS),
        in_specs=[
            pl.BlockSpec(memory_space=pltpu.SMEM),
            pl.BlockSpec((SEQ, gw), lambda b, h: (b, h)),
            pl.BlockSpec((SEQ, HEAD_DIM), lambda b, h: (b, k_blk0 + h)),
            pl.BlockSpec((SEQ, HEAD_DIM), lambda b, h: (b, v_blk0 + h)),
            pl.BlockSpec((SEQ, gw), lambda b, h: (b, g_blk0 + h)),
            pl.BlockSpec((CTX_LEN, HEAD_DIM), lambda b, h: (b, h)),
            pl.BlockSpec((CTX_LEN, HEAD_DIM), lambda b, h: (b, N_KV_HEADS + h)),
        ],
        out_specs=pl.BlockSpec((SEQ, gw), lambda b, h: (b, h)),
        out_shape=jax.ShapeDtypeStruct((batch * SEQ, ATTN_WIDTH), BF16),
        scratch_shapes=[pltpu.VMEM((HEAD_DIM + ONES_ROWS, SEQ), BF16),
                        pltpu.VMEM((HEAD_DIM + ONES_ROWS, CTX_LEN), BF16),
                        pltpu.VMEM((3, 3 * BLOCK, GQA_GROUP * BLOCK), F32)],
        compiler_params=_cparams(("arbitrary", "arbitrary")),
        name="banded_attention",
    )(sink, px, px, px, px, ckv, ckv)


def _out_proj_kernel(*refs, final):
    if final:
        a_ref, w_ref, x_ref, gate_ref, fg_ref, o_ref = refs
    else:
        a_ref, w_ref, x_ref, gate_ref, ng_ref, sh_ref, sc_ref, o_ref, hx_ref = refs
    acc = jnp.dot(a_ref[...], w_ref[...], preferred_element_type=F32)
    y = x_ref[...] + gate_ref[0] * acc
    if final:
        ms = jnp.mean(y * y, axis=-1, keepdims=True)
        y = y * lax.rsqrt(ms + NORM_EPS) * fg_ref[...]
    else:
        hx_ref[...] = _norm_mod_rows(y, ng_ref[...], 1.0 + sc_ref[0], sh_ref[0]).astype(hx_ref.dtype)
    o_ref[...] = y


def _out_proj(a, w, x, gate, *, final_g=None, next_norm=None, tm=512):
    m, d = x.shape
    kdim = a.shape[1]
    final = final_g is not None
    per_batch = lambda i: ((i * tm) // SEQ, 0, 0)
    row_tile = pl.BlockSpec((tm, d), lambda i: (i, 0))
    vec = pl.BlockSpec((1, d), lambda i: (0, 0))
    in_specs = [
        pl.BlockSpec((tm, kdim), lambda i: (i, 0)),
        pl.BlockSpec((kdim, d), lambda i: (0, 0)),
        row_tile,
        pl.BlockSpec((1, 1, d), per_batch),
    ]
    args = [a, w, x, gate]
    out_specs = [row_tile]
    out_shape = [jax.ShapeDtypeStruct((m, d), F32)]
    if final:
        in_specs.append(vec)
        args.append(final_g.reshape(1, d))
    else:
        ng, shift, scale = next_norm
        in_specs += [vec, pl.BlockSpec((1, 1, d), per_batch), pl.BlockSpec((1, 1, d), per_batch)]
        args += [ng.reshape(1, d), shift, scale]
        out_specs.append(row_tile)
        out_shape.append(jax.ShapeDtypeStruct((m, d), BF16))
    return pl.pallas_call(
        functools.partial(_out_proj_kernel, final=final),
        grid=(m // tm,),
        in_specs=in_specs,
        out_specs=out_specs,
        out_shape=out_shape,
        compiler_params=_cparams(("arbitrary",)),
        name="out_proj",
    )(*args)


HALF = SEQ // 2
QUARTER = SEQ // 4
FLIP_BLOCK = 256


def _fold_tables():
    def tab(f, n):
        t2 = 2 * np.arange(n, dtype=np.int64) + 1
        ang = ((f[:, None] * t2[None, :]) % (2 * FFT_N)).astype(np.float64) * (math.pi / FFT_N)
        return np.cos(ang), np.sin(ang)

    r = np.arange(QUARTER, dtype=np.int64)
    g = np.arange(HALF, dtype=np.int64)
    f_ee, f_eo, f_o = 4 * r, 4 * r + 2, 2 * g + 1
    cee, see = tab(f_ee, QUARTER)
    ceo, seo = tab(f_eo, QUARTER)
    co, so = tab(f_o, HALF)
    see[0, :] = np.where(np.arange(QUARTER) % 2 == 0, 1.0, -1.0)
    const = lambda a, dt: jnp.asarray(np.ascontiguousarray(a).astype(np.float32)).astype(dt)
    phase = lambda f: np.stack([np.cos(f * (math.pi / FFT_N)), np.sin(f * (math.pi / FFT_N))])
    return dict(
        fq=const(np.stack([cee, see, ceo, seo]), BF16),
        fo=const(np.stack([co, so]), BF16),
        iq=const(np.stack([cee.T, ceo.T, see.T, seo.T]), BF16),
        io=const(np.stack([so.T, co.T]), BF16),
        anti=const(np.eye(FLIP_BLOCK)[::-1], BF16),
        pq=const(np.stack([phase(f_ee), phase(f_eo)], axis=1)[..., None], F32),
        po=const(phase(f_o)[..., None], F32),
    )


def _flip_rows(h, anti):
    nb = h.shape[0] // FLIP_BLOCK
    return jnp.concatenate(
        [jnp.dot(anti[...], h[(nb - 1 - a) * FLIP_BLOCK:(nb - a) * FLIP_BLOCK, :], preferred_element_type=F32)
         for a in range(nb)], axis=0)


def _fold(x, anti):
    half = x.shape[0] // 2
    x0 = x[:half].astype(F32)
    xr = _flip_rows(x[half:], anti)
    return (x0 + xr).astype(BF16), (x0 - xr).astype(BF16)


def _staggered(n, matmuls, finish):
    matmuls(0, 0)
    for t in range(1, n):
        matmuls(t, t % 2)
        finish(t - 1, (t - 1) % 2)
    finish(n - 1, (n - 1) % 2)


def _col_chunks(n):
    return [slice(c, c + MXU_COLS) for c in range(0, n, MXU_COLS)]


PLANES = {"a_ee": 0, "b_ee": QUARTER, "a_eo": 2 * QUARTER, "b_eo": 3 * QUARTER,
          "a_o": 2 * HALF, "b_o": 3 * HALF}
GROUPS = (("a_ee", "b_ee", QUARTER), ("a_eo", "b_eo", QUARTER), ("a_o", "b_o", HALF))


def _spectrum(fq_ref, fo_ref, x, anti):
    xs, xa = _fold(x, anti)
    xss, xsa = _fold(xs, anti)
    xas, xaa = _fold(xa, anti)
    dq = lambda k, v: jnp.dot(fq_ref[k], v, preferred_element_type=F32)
    do = lambda k, v: jnp.dot(fo_ref[k], v, preferred_element_type=F32)
    return {"a_ee": dq(0, xss), "b_ee": dq(1, xaa), "a_eo": dq(2, xsa), "b_eo": dq(3, xas),
            "a_o": do(0, xa), "b_o": do(1, xs)}


def _plane(ref, lead, name, rows, cols):
    r0 = PLANES[name]
    return ref[lead, r0:r0 + rows, cols]


def _resident(shape):
    zeros = (0,) * len(shape)
    return pl.BlockSpec(shape, lambda *_: zeros, pipeline_mode=pl.Buffered(1))


def _filter_sd_body(w1t_ref, w1c_ref, w1s_ref, b1_ref, w2_ref, b2_ref, fq_ref,
                    w3f_ref, b3f_ref, w3b_ref, b3b_ref, d_ref, o_ref, hid_ref, *, first, j):
    length = o_ref.shape[1]
    tn = o_ref.shape[2]
    hp = lax.Precision.HIGHEST

    @pl.when(first)
    def _():
        n = lax.broadcasted_iota(jnp.int32, (length, LANES), 0).astype(F32)
        lane = lax.broadcasted_iota(jnp.int32, (length, LANES), 1)
        bands = (FILTER_EMB - 1) // 2
        fr_step = (bands - 1 - 1e-4) / (bands - 1)
        fr = jnp.where(lane < bands, 1e-4 + lane.astype(F32) * fr_step, 0.0)
        ang = (2.0 * math.pi * n / length) * fr
        t = n / (length - 1)
        fq = fq_ref[...]
        pre = (t * w1t_ref[...]
               + jnp.dot(jnp.cos(ang), w1c_ref[...], precision=hp, preferred_element_type=F32)
               + jnp.dot(-jnp.sin(ang), w1s_ref[...], precision=hp, preferred_element_type=F32)
               + b1_ref[...])
        hid = jnp.sin(fq * pre)
        hid = jnp.sin(fq * (jnp.dot(hid, w2_ref[...], precision=hp, preferred_element_type=F32)
                            + b2_ref[...]))
        hid_hi = hid.astype(BF16)
        hid_ref[0] = hid_hi
        hid_ref[1] = (hid - hid_hi.astype(F32)).astype(BF16)

    def dot3(w_ref):
        wf = w_ref[...]
        wf = jnp.concatenate([wf, jnp.zeros((LANES - wf.shape[0], wf.shape[1]), F32)], axis=0)
        w_hi = wf.astype(BF16)
        w_lo = (wf - w_hi.astype(F32)).astype(BF16)
        return (jnp.dot(hid_ref[0], w_hi, preferred_element_type=F32)
                + jnp.dot(hid_ref[1], w_hi, preferred_element_type=F32)
                + jnp.dot(hid_ref[0], w_lo, preferred_element_type=F32))

    row = lax.broadcasted_iota(jnp.int32, (length, tn), 0)
    chan = (lax.broadcasted_iota(jnp.int32, (length, tn), 1) + j * tn).astype(F32)
    min_decay = math.log(DECAY_TARGET) / DECAY_SLOW
    max_decay = math.log(DECAY_TARGET) / DECAY_FAST
    delta = min_decay + chan * ((max_decay - min_decay) / (HYENA_WIDTH - 1))
    t = row.astype(F32) / (length - 1)
    window = jnp.exp(-t * jnp.abs(delta)) + WINDOW_SHIFT
    hf = (dot3(w3f_ref) + b3f_ref[...]) * window
    hb = (dot3(w3b_ref) + b3b_ref[...]) * window
    hf = jnp.where(row == 0, hf + d_ref[0], hf)
    hb = jnp.where(row == 0, 0.0, hb)
    o_ref[0] = (hf + hb).astype(o_ref.dtype)
    o_ref[1] = (hf - hb).astype(o_ref.dtype)


def _prep_kernel(c_ref, aw_ref, ab_ref, *refs, filter_steps, ct):
    wa_ref, wh_ref = refs[-7:-5]
    mods_ref, sd_ref, wa_bf_ref, wh_bf_ref, hid_ref = refs[-5:]
    s = pl.program_id(0)
    _mods_kernel(c_ref, aw_ref, ab_ref, mods_ref)

    @pl.when(s < filter_steps)
    def _():
        _filter_sd_body(*refs[:-7], sd_ref, hid_ref, first=s == 0, j=s % ct)

    @pl.when(s < CAST_STEPS)
    def _():
        wa_bf_ref[...] = wa_ref[...].astype(BF16)

    @pl.when((s >= CAST_STEPS) & (s < 2 * CAST_STEPS))
    def _():
        wh_bf_ref[...] = wh_ref[...].astype(BF16)


CAST_STEPS = 8


def _prep(cc, ada_w, ada_b, w1, b1, w2, b2, w3, b3, freq, bias_d, w_out_a, w_out_h, *,
          tn_mods=512, tn=512):
    depth, d, n_mods = ada_w.shape
    mt = n_mods // tn_mods
    hpad = LANES - FILTER_HIDDEN
    bands = (FILTER_EMB - 1) // 2
    w = HYENA_WIDTH
    ct = w // tn
    filter_steps = 2 * ct
    assert filter_steps <= depth * mt
    n = w3.shape[1]
    w1t = jnp.pad(w1[0:1], ((0, 0), (0, hpad)))
    w1c = jnp.pad(w1[1:1 + bands], ((0, LANES - bands), (0, hpad)))
    w1s = jnp.pad(w1[1 + bands:], ((0, LANES - bands), (0, hpad)))
    b1p = jnp.pad(b1.reshape(1, -1), ((0, 0), (0, hpad)))
    w2p = jnp.pad(w2, ((0, hpad), (0, hpad)))
    b2p = jnp.pad(b2.reshape(1, -1), ((0, 0), (0, hpad)))
    fqp = jnp.pad(freq.reshape(1, -1), ((0, 0), (0, hpad)))
    b3r = b3.reshape(1, n)
    small = lambda shape: pl.BlockSpec(shape, lambda s: (0, 0))
    order = lambda s: jnp.minimum(s, filter_steps - 1) // ct
    chan = lambda s: jnp.minimum(s, filter_steps - 1) % ct
    fwd_cols = lambda s: (0, 2 * order(s) * ct + chan(s))
    bwd_cols = lambda s: (0, (2 * order(s) + 1) * ct + chan(s))
    assert w_out_a.shape == w_out_h.shape and 2 * CAST_STEPS <= depth * mt
    cast_blk = (w_out_a.shape[0] // CAST_STEPS, w_out_a.shape[1])
    first_rows = lambda s: (jnp.minimum(s, CAST_STEPS - 1), 0)
    second_rows = lambda s: (jnp.clip(s - CAST_STEPS, 0, CAST_STEPS - 1), 0)
    return pl.pallas_call(
        functools.partial(_prep_kernel, filter_steps=filter_steps, ct=ct),
        grid=(depth * mt,),
        in_specs=[pl.BlockSpec((8, d), lambda s: (0, 0)),
                  pl.BlockSpec((1, d, tn_mods), lambda s: (s // mt, 0, s % mt)),
                  pl.BlockSpec((1, 1, tn_mods), lambda s: (s // mt, 0, s % mt)),
                  small((1, LANES)), small((LANES, LANES)), small((LANES, LANES)), small((1, LANES)),
                  small((LANES, LANES)), small((1, LANES)), small((1, LANES)),
                  pl.BlockSpec((FILTER_HIDDEN, tn), fwd_cols), pl.BlockSpec((1, tn), fwd_cols),
                  pl.BlockSpec((FILTER_HIDDEN, tn), bwd_cols), pl.BlockSpec((1, tn), bwd_cols),
                  pl.BlockSpec((1, 1, tn), lambda s: (order(s), 0, chan(s))),
                  pl.BlockSpec(cast_blk, first_rows), pl.BlockSpec(cast_blk, second_rows)],
        out_specs=[pl.BlockSpec((1, 8, tn_mods), lambda s: (s // mt, 0, s % mt)),
                   pl.BlockSpec((2, SEQ, tn), lambda s: (0, 0, order(s) * ct + chan(s))),
                   pl.BlockSpec(cast_blk, first_rows), pl.BlockSpec(cast_blk, second_rows)],
        out_shape=[jax.ShapeDtypeStruct((depth, 8, n_mods), F32),
                   jax.ShapeDtypeStruct((2, SEQ, 2 * w), BF16),
                   jax.ShapeDtypeStruct(w_out_a.shape, BF16),
                   jax.ShapeDtypeStruct(w_out_h.shape, BF16)],
        scratch_shapes=[pltpu.VMEM((2, SEQ, LANES), BF16)],
        compiler_params=_cparams(("arbitrary",)),
        name="mods_and_filters",
    )(cc, ada_w, ada_b.reshape(depth, 1, n_mods),
      w1t, w1c, w1s, b1p, w2p, b2p, fqp, w3, b3r, w3, b3r, bias_d.reshape(2, 1, w),
      w_out_a, w_out_h)


def _filter_spec_kernel(s_ref, d_ref, fq_ref, fo_ref, anti_ref, pq_ref, po_ref, k_ref):
    phases = {"a_ee": (pq_ref[0, 0], pq_ref[1, 0]), "a_eo": (pq_ref[0, 1], pq_ref[1, 1]),
              "a_o": (po_ref[0], po_ref[1])}
    is0 = lax.broadcasted_iota(jnp.int32, (QUARTER, MXU_COLS), 0) == 0
    for cols in _col_chunks(k_ref.shape[2]):
        sp = _spectrum(fq_ref, fo_ref, s_ref[0, :, cols], anti_ref)
        dp = _spectrum(fq_ref, fo_ref, d_ref[0, :, cols], anti_ref)
        for a, b, rows in GROUPS:
            cos, sin = phases[a]
            ka = sp[a] * cos + sp[b] * sin
            kb = dp[b] * cos - dp[a] * sin
            if a == "a_ee":
                scale = jnp.where(is0, 1.0 / FFT_N, 2.0 / FFT_N)
                kb = jnp.where(is0, sp[b], kb)
            else:
                scale = 2.0 / FFT_N
            k_ref[0, PLANES[a]:PLANES[a] + rows, cols] = (ka * scale).astype(k_ref.dtype)
            k_ref[0, PLANES[b]:PLANES[b] + rows, cols] = (kb * scale).astype(k_ref.dtype)


def _filter_spectra(sd, tabs, *, tn=512):
    w = HYENA_WIDTH
    ct = w // tn
    consts = [tabs[k] for k in ("fq", "fo", "anti", "pq", "po")]
    return pl.pallas_call(
        _filter_spec_kernel,
        grid=(2, ct),
        in_specs=[
            pl.BlockSpec((1, SEQ, tn), lambda o, j: (0, 0, o * ct + j)),
            pl.BlockSpec((1, SEQ, tn), lambda o, j: (1, 0, o * ct + j)),
        ] + [_resident(c.shape) for c in consts],
        out_specs=pl.BlockSpec((1, 4 * HALF, tn), lambda o, j: (o, 0, j)),
        out_shape=jax.ShapeDtypeStruct((2, 4 * HALF, w), F32),
        compiler_params=_cparams(("arbitrary", "arbitrary")),
        name="filter_spectra",
    )(sd, sd, *consts)


def _conv_fwd_kernel(x_ref, fq_ref, fo_ref, anti_ref, k_ref, y_ref):
    is0 = lax.broadcasted_iota(jnp.int32, (QUARTER, MXU_COLS), 0) == 0
    for cols in _col_chunks(y_ref.shape[2]):
        sp = _spectrum(fq_ref, fo_ref, x_ref[:, cols], anti_ref)
        for a, b, rows in GROUPS:
            ka = _plane(k_ref, 0, a, rows, cols).astype(F32)
            kb = _plane(k_ref, 0, b, rows, cols).astype(F32)
            bkb = sp[b] * kb
            if a == "a_ee":
                ya = sp[a] * ka - jnp.where(is0, 0.0, bkb)
                yb = jnp.where(is0, bkb, sp[a] * kb + sp[b] * ka)
            else:
                ya = sp[a] * ka - bkb
                yb = sp[a] * kb + sp[b] * ka
            y_ref[0, PLANES[a]:PLANES[a] + rows, cols] = ya.astype(y_ref.dtype)
            y_ref[0, PLANES[b]:PLANES[b] + rows, cols] = yb.astype(y_ref.dtype)


def _conv_fwd(x, x_blk0, kspec, order, tabs, batch, *, tn=512):
    w = HYENA_WIDTH
    ct = w // tn
    consts = [tabs[k] for k in ("fq", "fo", "anti")]
    return pl.pallas_call(
        _conv_fwd_kernel,
        grid=(ct, batch),
        in_specs=[pl.BlockSpec((SEQ, tn), lambda j, b: (b, x_blk0 * ct + j))]
        + [_resident(c.shape) for c in consts]
        + [pl.BlockSpec((1, 4 * HALF, tn), lambda j, b: (order, 0, j))],
        out_specs=pl.BlockSpec((1, 4 * HALF, tn), lambda j, b: (b, 0, j)),
        out_shape=jax.ShapeDtypeStruct((batch, 4 * HALF, w), BF16),
        compiler_params=_cparams(("arbitrary", "arbitrary")),
        name="long_conv_fwd",
    )(x, *consts, kspec)


def _conv_inv_kernel(*refs, gated):
    if gated:
        y_ref, iq_ref, io_ref, anti_ref, xm_ref, g_ref, o_ref, accq_ref, acco_ref = refs
    else:
        y_ref, iq_ref, io_ref, anti_ref, xm_ref, o_ref, accq_ref, acco_ref = refs
    chunks = _col_chunks(o_ref.shape[1])
    flipped = lambda v: _flip_rows(v.astype(BF16), anti_ref)

    def matmuls(t, slot):
        cols = chunks[t]
        for k, name in enumerate(("a_ee", "a_eo", "b_ee", "b_eo")):
            accq_ref[slot, k] = jnp.dot(iq_ref[k], _plane(y_ref, 0, name, QUARTER, cols),
                                        preferred_element_type=F32)
        for k, name in enumerate(("b_o", "a_o")):
            acco_ref[slot, k] = jnp.dot(io_ref[k], _plane(y_ref, 0, name, HALF, cols),
                                        preferred_element_type=F32)

    def finish(t, slot):
        cols = chunks[t]
        p, r, q, s = (accq_ref[slot, k] for k in range(4))
        ea = jnp.concatenate([p + r, flipped(p - r)], axis=0)
        eb = jnp.concatenate([q + s, flipped(s - q)], axis=0)
        p1 = ea + acco_ref[slot, 0]
        p2 = eb + acco_ref[slot, 1]
        for rows, y in ((slice(0, HALF), p1 + p2), (slice(HALF, SEQ), flipped(p1 - p2))):
            out = y * xm_ref[rows, cols].astype(F32)
            if gated:
                out = out * g_ref[rows, cols].astype(F32)
            o_ref[rows, cols] = out.astype(o_ref.dtype)

    _staggered(len(chunks), matmuls, finish)


def _conv_inv(y, tabs, p, m_blk0, g_blk0=None, *, tn=512):
    batch = y.shape[0]
    w = HYENA_WIDTH
    ct = w // tn
    gated = g_blk0 is not None
    col = lambda blk0: (lambda b, j: (b, blk0 * ct + j))
    consts = [tabs[k] for k in ("iq", "io", "anti")]
    in_specs = ([pl.BlockSpec((1, 4 * HALF, tn), lambda b, j: (b, 0, j))]
                + [_resident(c.shape) for c in consts]
                + [pl.BlockSpec((SEQ, tn), col(m_blk0))])
    args = [y, *consts, p]
    if gated:
        in_specs.append(pl.BlockSpec((SEQ, tn), col(g_blk0)))
        args.append(p)
    return pl.pallas_call(
        functools.partial(_conv_inv_kernel, gated=gated),
        grid=(batch, ct),
        in_specs=in_specs,
        out_specs=pl.BlockSpec((SEQ, tn), lambda b, j: (b, j)),
        out_shape=jax.ShapeDtypeStruct((batch * SEQ, w), BF16),
        scratch_shapes=[pltpu.VMEM((2, 4, QUARTER, MXU_COLS), F32),
                        pltpu.VMEM((2, 2, HALF, MXU_COLS), F32)],
        compiler_params=_cparams(("arbitrary", "arbitrary")),
        name="long_conv_inv",
    )(*args)


def _rope_tables():
    pos = np.arange(SEQ)
    row = (pos // GRID_W).astype(np.float32)
    col = (pos % GRID_W).astype(np.float32)
    half = HEAD_DIM // 2
    inv = (ROPE_BASE ** (-np.arange(0, half, 2, dtype=np.float32) / half)).astype(np.float32)
    ar = row[:, None] * inv[None]
    ac = col[:, None] * inv[None]
    cos = np.concatenate([np.cos(ar), np.cos(ar), np.cos(ac), np.cos(ac)], axis=1)
    sin = np.concatenate([-np.sin(ar), np.sin(ar), -np.sin(ac), np.sin(ac)], axis=1)
    return jnp.asarray(cos, F32), jnp.asarray(sin, F32)


def _tile_types(tn, kinds):
    per_tile = tn // LANES
    return [kinds[s0:s0 + per_tile] for s0 in range(0, len(kinds), per_tile)]


def kernel(x, c, ctx, c_ctx, norm_g, ada_w, ada_b, attn_w_in, attn_w_out, attn_sink, hy_w_in,
           hy_conv_w, hy_conv_b, hy_w1, hy_b1, hy_w2, hy_b2, hy_w3, hy_b3, hy_freq, hy_bias_d,
           hy_w_out, final_g):
    batch, seq, d = x.shape
    assert (seq, d) == (SEQ, D_MODEL) and ctx.shape[1] == CTX_LEN
    assert norm_g.shape[0] == 2 and attn_w_in.shape[0] == 1 and hy_w_in.shape[0] == 1
    w = HYENA_WIDTH

    cc = jnp.concatenate([c, c_ctx[None], jnp.zeros((8 - batch - 1, d), F32)], axis=0)
    mods, sd, w_out_attn, w_out_hyena = _prep(
        cc, ada_w, ada_b, hy_w1[0], hy_b1[0], hy_w2[0], hy_b2[0], hy_w3[0], hy_b3[0], hy_freq[0],
        hy_bias_d[0], attn_w_out[0], hy_w_out[0])
    part = lambda layer, r0, r1, k: mods[layer, r0:r1, None, k * d:(k + 1) * d]

    x2 = x.reshape(batch * seq, d)
    ctx2 = ctx.reshape(batch * CTX_LEN, d)

    tn = 1024
    kinds0 = (["q"] * N_HEADS + ["k"] * N_KV_HEADS + [None] * N_KV_HEADS
              + ["g"] * (ATTN_WIDTH // LANES))
    hx = _norm_mod(x2, norm_g[0], part(0, 0, batch, 0), part(0, 0, batch, 1), tm=1024,
                   rows_per_mod=seq)
    px = _proj(hx, attn_w_in[0], 0, ATTN_IN, tm=SEQ, tn=tn, tile_types=_tile_types(tn, kinds0),
               rope=_rope_tables(), q_scale=HEAD_DIM ** -0.5 * LOG2E)
    ckv = _ctx_kv(ctx2, norm_g[0], part(0, batch, batch + 1, 0), part(0, batch, batch + 1, 1),
                  attn_w_in[0], ATTN_WIDTH, 2 * KV_WIDTH)
    og = _attention(px, ckv, attn_sink[0], batch)
    x2, hx = _out_proj(og, w_out_attn, x2, part(0, 0, batch, 2),
                       next_norm=(norm_g[1], part(1, 0, batch, 0), part(1, 0, batch, 1)))

    kinds1 = ["c"] * (3 * w // LANES) + ["g"] * (w // LANES)
    tn1 = 1024
    p = _proj(hx, hy_w_in[0], 0, 4 * w, tm=SEQ, tn=tn1, tile_types=_tile_types(tn1, kinds1),
              conv=(hy_conv_w[0], hy_conv_b[0].reshape(1, 3 * w)))
    tabs = _fold_tables()
    kspec = _filter_spectra(sd, tabs)
    y1 = _conv_fwd(p, 2, kspec, 0, tabs, batch)
    z = _conv_inv(y1, tabs, p, 0)
    y2 = _conv_fwd(z, 0, kspec, 1, tabs, batch)
    yg = _conv_inv(y2, tabs, p, 1, g_blk0=3)
    out, = _out_proj(yg, w_out_hyena, x2, part(1, 0, batch, 2), final_g=final_g)
    return out.reshape(batch, seq, d)
```

```python
import functools
import math

import jax
import jax.numpy as jnp
import numpy as np
from jax import lax
from jax.experimental import pallas as pl
from jax.experimental.pallas import tpu as pltpu

F32 = jnp.float32
BF16 = jnp.bfloat16

D_MODEL = 2048
SEQ = 2048
CTX_LEN = 256
GRID_W = 64
HEAD_DIM = 128
N_HEADS = 16
N_KV_HEADS = 4
GQA_GROUP = 4
ATTN_WIDTH = 2048
KV_WIDTH = 512
ATTN_IN = 2 * ATTN_WIDTH + 2 * KV_WIDTH
WINDOW = 128
BLOCK = 128
ROPE_BASE = 10000.0
HYENA_WIDTH = 2048
FILTER_EMB = 33
FILTER_HIDDEN = 64
DECAY_FAST = 0.3
DECAY_SLOW = 1.5
DECAY_TARGET = 1e-2
WINDOW_SHIFT = 0.05
NORM_EPS = 1e-6
NEG_INF = -1e30

LANES = 128
MOD_ROWS = 8
MXU_COLS = 256
ACC_ROWS = 1024
ONES_ROWS = 16
LOG2E = math.log2(math.e)
FFT_N = 2 * SEQ
VMEM_LIMIT = 56 * 1024 * 1024


def _cparams(sem):
    return pltpu.CompilerParams(dimension_semantics=sem, vmem_limit_bytes=VMEM_LIMIT)


def _mods_kernel(c_ref, w_ref, b_ref, o_ref):
    c = c_ref[...]
    s = c * jax.nn.sigmoid(c)
    s_hi = s.astype(BF16)
    s_lo = (s - s_hi.astype(F32)).astype(BF16)
    lhs = jnp.concatenate([s_hi, s_lo], axis=0)
    r = jnp.dot(lhs, w_ref[0].astype(BF16), preferred_element_type=F32)
    o_ref[0] = r[:MOD_ROWS] + r[MOD_ROWS:] + b_ref[0]


def _rope_slab(t, cos, sin):
    lane = lax.broadcasted_iota(jnp.int32, t.shape, 1)
    first = (lane % 64) < 32
    partner = jnp.where(first, pltpu.roll(t, 96, 1), pltpu.roll(t, 32, 1))
    return t * cos + partner * sin


def _norm_mod_rows(x, g, mul, add):
    ms = jnp.mean(x * x, axis=-1, keepdims=True)
    return x * lax.rsqrt(ms + NORM_EPS) * g * mul + add


def _norm_mod_kernel(x_ref, g_ref, sh_ref, sc_ref, o_ref, *, row_chunk):
    g = g_ref[...]
    mul = 1.0 + sc_ref[0]
    add = sh_ref[0]
    for r in range(0, x_ref.shape[0], row_chunk):
        o_ref[r:r + row_chunk, :] = _norm_mod_rows(x_ref[r:r + row_chunk, :], g, mul, add).astype(o_ref.dtype)


def _mod_row(layer, row, kind):
    return (layer * MOD_ROWS + row) * 3 + kind


def _mod_spec(base, tm, rows_per_mod, d):
    return pl.BlockSpec((1, 1, d), lambda i: (base + 3 * ((i * tm) // rows_per_mod), 0, 0))


def _norm_mod(x, g, mods, layer, *, tm):
    m, d = x.shape
    return pl.pallas_call(
        functools.partial(_norm_mod_kernel, row_chunk=min(tm, 256)),
        grid=(m // tm,),
        in_specs=[pl.BlockSpec((tm, d), lambda i: (i, 0)), pl.BlockSpec((1, d), lambda i: (0, 0)),
                  _mod_spec(_mod_row(layer, 0, 0), tm, SEQ, d),
                  _mod_spec(_mod_row(layer, 0, 1), tm, SEQ, d)],
        out_specs=pl.BlockSpec((tm, d), lambda i: (i, 0)),
        out_shape=jax.ShapeDtypeStruct((m, d), BF16),
        compiler_params=_cparams(("arbitrary",)),
        name="norm_mod",
    )(x, g.reshape(1, d), mods, mods)


def _ctx_kv_kernel(x_ref, g_ref, sh_ref, sc_ref, w_ref, o_ref, hx_ref, *, row_chunk):
    g = g_ref[...]
    mul = 1.0 + sc_ref[0]
    add = sh_ref[0]
    for r in range(0, x_ref.shape[0], row_chunk):
        hx_ref[r:r + row_chunk, :] = _norm_mod_rows(x_ref[r:r + row_chunk, :], g, mul, add).astype(BF16)
    wb = w_ref[...].astype(BF16)
    for c0 in range(0, o_ref.shape[1], MXU_COLS):
        o_ref[:, c0:c0 + MXU_COLS] = jnp.dot(hx_ref[...], wb[:, c0:c0 + MXU_COLS],
                                             preferred_element_type=F32).astype(o_ref.dtype)


def _ctx_kv(ctx, g, mods, layer, mod_row, w, col0, n):
    m, d = ctx.shape
    assert col0 % n == 0 and m <= ACC_ROWS
    shift = _mod_row(layer, mod_row, 0)
    scale = _mod_row(layer, mod_row, 1)
    return pl.pallas_call(
        functools.partial(_ctx_kv_kernel, row_chunk=256),
        grid=(1,),
        in_specs=[pl.BlockSpec((m, d), lambda i: (0, 0)), pl.BlockSpec((1, d), lambda i: (0, 0)),
                  pl.BlockSpec((1, 1, d), lambda i: (shift, 0, 0)),
                  pl.BlockSpec((1, 1, d), lambda i: (scale, 0, 0)),
                  pl.BlockSpec((d, n), lambda i: (0, col0 // n))],
        out_specs=pl.BlockSpec((m, n), lambda i: (0, 0)),
        out_shape=jax.ShapeDtypeStruct((m, n), BF16),
        scratch_shapes=[pltpu.VMEM((m, d), BF16)],
        compiler_params=_cparams(("arbitrary",)),
        name="ctx_kv",
    )(ctx, g.reshape(1, d), mods, mods, w)


def _proj_kernel(*refs, tile_types, has_rope, has_conv, q_scale):
    if has_rope:
        hx_ref, w_ref, cos_ref, sin_ref, o_ref, wb_ref = refs
    elif has_conv:
        hx_ref, w_ref, cw_ref, cb_ref, o_ref, wb_ref = refs
    else:
        hx_ref, w_ref, o_ref, wb_ref = refs
    j = pl.program_id(0)
    tm, tn = o_ref.shape

    @pl.when(pl.program_id(1) == 0)
    def _():
        wb_ref[...] = w_ref[...].astype(BF16)

    w_ref = wb_ref

    def plain():
        for c0 in range(0, tn, MXU_COLS):
            o_ref[:, c0:c0 + MXU_COLS] = chunk_dot(c0).astype(o_ref.dtype)

    def chunk_dot(c0):
        pieces = [jnp.dot(hx_ref[r:r + ACC_ROWS, :], w_ref[:, c0:c0 + MXU_COLS],
                          preferred_element_type=F32) for r in range(0, tm, ACC_ROWS)]
        return pieces[0] if len(pieces) == 1 else jnp.concatenate(pieces, axis=0)

    def gated():
        for c0 in range(0, tn, MXU_COLS):
            half = 0.5 * chunk_dot(c0)
            o_ref[:, c0:c0 + MXU_COLS] = (half + half * jnp.tanh(half)).astype(o_ref.dtype)

    def short_conv():
        for c0 in range(0, tn, MXU_COLS):
            acc = chunk_dot(c0)
            for h in range(MXU_COLS // LANES):
                lo = c0 + h * LANES
                u = acc[:, h * LANES:(h + 1) * LANES]
                w3 = cw_ref[:, lo:lo + LANES]
                b = cb_ref[:, lo:lo + LANES]
                y = pltpu.roll(u, 1, 0) * w3[0:1] + u * w3[1:2] + pltpu.roll(u, tm - 1, 0) * w3[2:3] + b
                o_ref[:, lo:lo + LANES] = y.astype(o_ref.dtype)
                first = u[0:1] * w3[1:2] + u[1:2] * w3[2:3] + b
                last = u[tm - 2:tm - 1] * w3[0:1] + u[tm - 1:tm] * w3[1:2] + b
                o_ref[0:1, lo:lo + LANES] = first.astype(o_ref.dtype)
                o_ref[tm - 1:tm, lo:lo + LANES] = last.astype(o_ref.dtype)

    def roped(types):
        cos = cos_ref[...]
        sin = sin_ref[...]
        for c0 in range(0, tn, MXU_COLS):
            acc = chunk_dot(c0)
            for h in range(MXU_COLS // LANES):
                ty = types[c0 // LANES + h]
                slab = acc[:, h * LANES:(h + 1) * LANES]
                if ty is not None:
                    slab = _rope_slab(slab, cos, sin)
                    if ty == "q":
                        slab = slab * q_scale
                lo = c0 + h * LANES
                o_ref[:, lo:lo + LANES] = slab.astype(o_ref.dtype)

    if tile_types is None:
        plain()
        return

    groups = {}
    for t, types in enumerate(tile_types):
        groups.setdefault(tuple(types), []).append(t)
    for types, tiles in groups.items():
        cond = functools.reduce(jnp.logical_or, [j == t for t in tiles])
        if all(ty is None for ty in types):
            pl.when(cond)(plain)
        elif all(ty == "g" for ty in types):
            pl.when(cond)(gated)
        elif all(ty == "c" for ty in types):
            pl.when(cond)(short_conv)
        else:
            assert "g" not in types and "c" not in types
            pl.when(cond)(functools.partial(roped, types))


def _proj(hx, w, col_blk0, n, *, tm, tn, tile_types=None, rope=None, conv=None, q_scale=1.0):
    m, d = hx.shape
    in_specs = [
        pl.BlockSpec((tm, d), lambda j, i: (i, 0)),
        pl.BlockSpec((d, tn), lambda j, i: (0, col_blk0 + j)),
    ]
    args = [hx, w]
    scratch = [pltpu.VMEM((d, tn), BF16)]
    if rope is not None:
        cos, sin = rope
        seq_tiles = cos.shape[0] // tm
        in_specs += [pl.BlockSpec((tm, LANES), lambda j, i: (i % seq_tiles, 0))] * 2
        args += [cos, sin]
    if conv is not None:
        assert rope is None and tm == SEQ
        cw, cb = conv
        last = cw.shape[1] // tn - 1
        in_specs += [pl.BlockSpec((cw.shape[0], tn), lambda j, i: (0, jnp.minimum(j, last))),
                     pl.BlockSpec((1, tn), lambda j, i: (0, jnp.minimum(j, last)))]
        args += [cw, cb]
    return pl.pallas_call(
        functools.partial(_proj_kernel, tile_types=tile_types, has_rope=rope is not None,
                          has_conv=conv is not None, q_scale=q_scale),
        grid=(n // tn, m // tm),
        in_specs=in_specs,
        out_specs=pl.BlockSpec((tm, tn), lambda j, i: (i, j)),
        out_shape=jax.ShapeDtypeStruct((m, n), BF16),
        scratch_shapes=scratch,
        compiler_params=_cparams(("arbitrary", "arbitrary")),
        name="proj",
    )(*args)


def _attn_kernel(sink_ref, q_ref, k_ref, v_ref, g_ref, kc_ref, vc_ref, o_ref,
                 vt_ref, vct_ref, bias_ref):
    kh = pl.program_id(1)
    band = 3 * BLOCK
    cols = GQA_GROUP * BLOCK
    n_blocks = SEQ // BLOCK

    vt_ref[:HEAD_DIM, :] = v_ref[...].T
    vct_ref[:HEAD_DIM, :] = vc_ref[...].T

    @pl.when((pl.program_id(0) == 0) & (kh == 0))
    def _():
        vt_ref[HEAD_DIM:, :] = jnp.ones((ONES_ROWS, SEQ), BF16)
        vct_ref[HEAD_DIM:, :] = jnp.ones((ONES_ROWS, CTX_LEN), BF16)
        krow = lax.broadcasted_iota(jnp.int32, (band, cols), 0)
        qcol = lax.broadcasted_iota(jnp.int32, (band, cols), 1) % BLOCK
        for idx, off in enumerate((0, -BLOCK, -2 * BLOCK)):
            bias_ref[idx] = jnp.where(jnp.abs(krow - qcol + off) <= WINDOW, 0.0, NEG_INF)

    kc = kc_ref[...]
    vct = vct_ref[...]
    sink_row = jnp.concatenate(
        [jnp.full((1, BLOCK), sink_ref[kh * GQA_GROUP + h] * LOG2E, F32) for h in range(GQA_GROUP)],
        axis=1)

    def body(n, carry):
        q0 = pl.multiple_of(n * BLOCK, BLOCK)
        ks = pl.multiple_of(jnp.clip((n - 1) * BLOCK, 0, SEQ - band), BLOCK)
        bidx = jnp.where(n == 0, 0, jnp.where(n == n_blocks - 1, 2, 1))
        qs = q_ref[pl.ds(q0, BLOCK), :]
        q4t = jnp.concatenate([qs[:, h * LANES:(h + 1) * LANES].T for h in range(GQA_GROUP)],
                              axis=1)
        kb = k_ref[pl.ds(ks, band), :]
        s_loc = jnp.dot(kb, q4t, preferred_element_type=F32) + bias_ref[bidx]
        s_ctx = jnp.dot(kc, q4t, preferred_element_type=F32)
        m = jnp.maximum(jnp.maximum(jnp.max(s_loc, axis=0, keepdims=True),
                                    jnp.max(s_ctx, axis=0, keepdims=True)), sink_row)
        p_loc = jnp.exp2(s_loc - m).astype(BF16)
        p_ctx = jnp.exp2(s_ctx - m).astype(BF16)
        ox = (jnp.dot(vt_ref[:, pl.ds(ks, band)], p_loc, preferred_element_type=F32)
              + jnp.dot(vct, p_ctx, preferred_element_type=F32))
        den = ox[HEAD_DIM:HEAD_DIM + 1, :] + jnp.exp2(sink_row - m)
        ot = ox[:HEAD_DIM, :] * (1.0 / den)
        gs = g_ref[pl.ds(q0, BLOCK), :].astype(F32)
        for h in range(GQA_GROUP):
            oh = ot[:, h * LANES:(h + 1) * LANES].T * gs[:, h * LANES:(h + 1) * LANES]
            o_ref[pl.ds(q0, BLOCK), h * LANES:(h + 1) * LANES] = oh.astype(o_ref.dtype)
        return carry

    lax.fori_loop(0, n_blocks, body, 0, unroll=8)


def _attention(px, ckv, sink, batch):
    gw = GQA_GROUP * HEAD_DIM
    k_blk0 = ATTN_WIDTH // HEAD_DIM
    v_blk0 = (ATTN_WIDTH + KV_WIDTH) // HEAD_DIM
    g_blk0 = (ATTN_WIDTH + 2 * KV_WIDTH) // gw
    return pl.pallas_call(
        _attn_kernel,
        grid=(batch, N_KV_HEADS),
        in_specs=[
            pl.BlockSpec(memory_space=pltpu.SMEM),
            pl.BlockSpec((SEQ, gw), lambda b, h: (b, h)),
            pl.BlockSpec((SEQ, HEAD_DIM), lambda b, h: (b, k_blk0 + h)),
            pl.BlockSpec((SEQ, HEAD_DIM), lambda b, h: (b, v_blk0 + h)),
            pl.BlockSpec((SEQ, gw), lambda b, h: (b, g_blk0 + h)),
            pl.BlockSpec((CTX_LEN, HEAD_DIM), lambda b, h: (b, h)),
            pl.BlockSpec((CTX_LEN, HEAD_DIM), lambda b, h: (b, N_KV_HEADS + h)),
        ],
        out_specs=pl.BlockSpec((SEQ, gw), lambda b, h: (b, h)),
        out_shape=jax.ShapeDtypeStruct((batch * SEQ, ATTN_WIDTH), BF16),
        scratch_shapes=[pltpu.VMEM((HEAD_DIM + ONES_ROWS, SEQ), BF16),
                        pltpu.VMEM((HEAD_DIM + ONES_ROWS, CTX_LEN), BF16),
                        pltpu.VMEM((3, 3 * BLOCK, GQA_GROUP * BLOCK), F32)],
        compiler_params=_cparams(("arbitrary", "arbitrary")),
        name="banded_attention",
    )(sink, px, px, px, px, ckv, ckv)


def _out_proj_kernel(*refs, final):
    if final:
        a_ref, w_ref, x_ref, gate_ref, fg_ref, o_ref = refs
    else:
        a_ref, w_ref, x_ref, gate_ref, ng_ref, sh_ref, sc_ref, o_ref, hx_ref = refs
    acc = jnp.dot(a_ref[...], w_ref[...], preferred_element_type=F32)
    y = x_ref[...] + gate_ref[0] * acc
    if final:
        ms = jnp.mean(y * y, axis=-1, keepdims=True)
        y = y * lax.rsqrt(ms + NORM_EPS) * fg_ref[...]
    else:
        hx_ref[...] = _norm_mod_rows(y, ng_ref[...], 1.0 + sc_ref[0], sh_ref[0]).astype(hx_ref.dtype)
    o_ref[...] = y


def _out_proj(a, w, x, mods, layer, *, final_g=None, next_norm_g=None, tm=512):
    m, d = x.shape
    kdim = a.shape[1]
    final = final_g is not None
    row_tile = pl.BlockSpec((tm, d), lambda i: (i, 0))
    vec = pl.BlockSpec((1, d), lambda i: (0, 0))
    in_specs = [
        pl.BlockSpec((tm, kdim), lambda i: (i, 0)),
        pl.BlockSpec((kdim, d), lambda i: (0, 0)),
        row_tile,
        _mod_spec(_mod_row(layer, 0, 2), tm, SEQ, d),
    ]
    args = [a, w, x, mods]
    out_specs = [row_tile]
    out_shape = [jax.ShapeDtypeStruct((m, d), F32)]
    if final:
        in_specs.append(vec)
        args.append(final_g.reshape(1, d))
    else:
        in_specs += [vec, _mod_spec(_mod_row(layer + 1, 0, 0), tm, SEQ, d),
                     _mod_spec(_mod_row(layer + 1, 0, 1), tm, SEQ, d)]
        args += [next_norm_g.reshape(1, d), mods, mods]
        out_specs.append(row_tile)
        out_shape.append(jax.ShapeDtypeStruct((m, d), BF16))
    return pl.pallas_call(
        functools.partial(_out_proj_kernel, final=final),
        grid=(m // tm,),
        in_specs=in_specs,
        out_specs=out_specs,
        out_shape=out_shape,
        compiler_params=_cparams(("arbitrary",)),
        name="out_proj",
    )(*args)


HALF = SEQ // 2
QUARTER = SEQ // 4
FLIP_BLOCK = 256


def _fold_tables():
    def tab(f, n):
        t2 = 2 * np.arange(n, dtype=np.int64) + 1
        ang = ((f[:, None] * t2[None, :]) % (2 * FFT_N)).astype(np.float64) * (math.pi / FFT_N)
        return np.cos(ang), np.sin(ang)

    r = np.arange(QUARTER, dtype=np.int64)
    g = np.arange(HALF, dtype=np.int64)
    f_ee, f_eo, f_o = 4 * r, 4 * r + 2, 2 * g + 1
    cee, see = tab(f_ee, QUARTER)
    ceo, seo = tab(f_eo, QUARTER)
    co, so = tab(f_o, HALF)
    see[0, :] = np.where(np.arange(QUARTER) % 2 == 0, 1.0, -1.0)
    const = lambda a, dt: jnp.asarray(np.ascontiguousarray(a).astype(np.float32)).astype(dt)
    phase = lambda f: np.stack([np.cos(f * (math.pi / FFT_N)), np.sin(f * (math.pi / FFT_N))])
    return dict(
        fq=const(np.stack([cee, see, ceo, seo]), BF16),
        fo=const(np.stack([co, so]), BF16),
        iq=const(np.stack([cee.T, ceo.T, see.T, seo.T]), BF16),
        io=const(np.stack([so.T, co.T]), BF16),
        anti=const(np.eye(FLIP_BLOCK)[::-1], BF16),
        pq=const(np.stack([phase(f_ee), phase(f_eo)], axis=1)[..., None], F32),
        po=const(phase(f_o)[..., None], F32),
    )


def _flip_rows(h, anti):
    nb = h.shape[0] // FLIP_BLOCK
    return jnp.concatenate(
        [jnp.dot(anti[...], h[(nb - 1 - a) * FLIP_BLOCK:(nb - a) * FLIP_BLOCK, :], preferred_element_type=F32)
         for a in range(nb)], axis=0)


def _fold(x, anti):
    half = x.shape[0] // 2
    x0 = x[:half].astype(F32)
    xr = _flip_rows(x[half:], anti)
    return (x0 + xr).astype(BF16), (x0 - xr).astype(BF16)


def _staggered(n, matmuls, finish):
    matmuls(0, 0)
    for t in range(1, n):
        matmuls(t, t % 2)
        finish(t - 1, (t - 1) % 2)
    finish(n - 1, (n - 1) % 2)


def _col_chunks(n):
    return [slice(c, c + MXU_COLS) for c in range(0, n, MXU_COLS)]


PLANES = {"a_ee": 0, "b_ee": QUARTER, "a_eo": 2 * QUARTER, "b_eo": 3 * QUARTER,
          "a_o": 2 * HALF, "b_o": 3 * HALF}
GROUPS = (("a_ee", "b_ee", QUARTER), ("a_eo", "b_eo", QUARTER), ("a_o", "b_o", HALF))


def _spectrum(fq_ref, fo_ref, x, anti):
    xs, xa = _fold(x, anti)
    xss, xsa = _fold(xs, anti)
    xas, xaa = _fold(xa, anti)
    dq = lambda k, v: jnp.dot(fq_ref[k], v, preferred_element_type=F32)
    do = lambda k, v: jnp.dot(fo_ref[k], v, preferred_element_type=F32)
    return {"a_ee": dq(0, xss), "b_ee": dq(1, xaa), "a_eo": dq(2, xsa), "b_eo": dq(3, xas),
            "a_o": do(0, xa), "b_o": do(1, xs)}


def _plane(ref, lead, name, rows, cols):
    r0 = PLANES[name]
    return ref[lead, r0:r0 + rows, cols]


def _resident(shape):
    zeros = (0,) * len(shape)
    return pl.BlockSpec(shape, lambda *_: zeros, pipeline_mode=pl.Buffered(1))


def _pack_filter_mlp(w1, b1, w2, b2, freq):
    hpad = LANES - FILTER_HIDDEN
    bands = (FILTER_EMB - 1) // 2
    row = lambda v: jnp.pad(v.reshape(1, -1), ((0, 7), (0, hpad)))
    mat = lambda m: jnp.pad(m, ((0, LANES - m.shape[0]), (0, hpad)))
    return jnp.concatenate([row(w1[0]), row(b1), row(b2), row(freq), jnp.zeros((LANES - 32, LANES), F32),
                            mat(w1[1:1 + bands]), mat(w1[1 + bands:]), mat(w2)], axis=0)


def _filter_sd_body(fp_ref, w3f_ref, b3f_ref, w3b_ref, b3b_ref, d_ref, o_ref, hid_ref, *, first, j):
    length = o_ref.shape[1]
    tn = o_ref.shape[2]
    hp = lax.Precision.HIGHEST
    w1t_ref, b1_ref, b2_ref, fq_ref = (fp_ref.at[r:r + 1] for r in (0, 8, 16, 24))
    w1c_ref, w1s_ref, w2_ref = (fp_ref.at[r:r + LANES] for r in (LANES, 2 * LANES, 3 * LANES))

    @pl.when(first)
    def _():
        n = lax.broadcasted_iota(jnp.int32, (length, LANES), 0).astype(F32)
        lane = lax.broadcasted_iota(jnp.int32, (length, LANES), 1)
        bands = (FILTER_EMB - 1) // 2
        fr_step = (bands - 1 - 1e-4) / (bands - 1)
        fr = jnp.where(lane < bands, 1e-4 + lane.astype(F32) * fr_step, 0.0)
        ang = (2.0 * math.pi * n / length) * fr
        t = n / (length - 1)
        fq = fq_ref[...]
        pre = (t * w1t_ref[...]
               + jnp.dot(jnp.cos(ang), w1c_ref[...], precision=hp, preferred_element_type=F32)
               + jnp.dot(-jnp.sin(ang), w1s_ref[...], precision=hp, preferred_element_type=F32)
               + b1_ref[...])
        hid = jnp.sin(fq * pre)
        hid = jnp.sin(fq * (jnp.dot(hid, w2_ref[...], precision=hp, preferred_element_type=F32)
                            + b2_ref[...]))
        hid_hi = hid.astype(BF16)
        hid_ref[0] = hid_hi
        hid_ref[1] = (hid - hid_hi.astype(F32)).astype(BF16)

    def dot3(w_ref):
        wf = w_ref[...]
        wf = jnp.concatenate([wf, jnp.zeros((LANES - wf.shape[0], wf.shape[1]), F32)], axis=0)
        w_hi = wf.astype(BF16)
        w_lo = (wf - w_hi.astype(F32)).astype(BF16)
        return (jnp.dot(hid_ref[0], w_hi, preferred_element_type=F32)
                + jnp.dot(hid_ref[1], w_hi, preferred_element_type=F32)
                + jnp.dot(hid_ref[0], w_lo, preferred_element_type=F32))

    row = lax.broadcasted_iota(jnp.int32, (length, tn), 0)
    chan = (lax.broadcasted_iota(jnp.int32, (length, tn), 1) + j * tn).astype(F32)
    min_decay = math.log(DECAY_TARGET) / DECAY_SLOW
    max_decay = math.log(DECAY_TARGET) / DECAY_FAST
    delta = min_decay + chan * ((max_decay - min_decay) / (HYENA_WIDTH - 1))
    t = row.astype(F32) / (length - 1)
    window = jnp.exp(-t * jnp.abs(delta)) + WINDOW_SHIFT
    hf = (dot3(w3f_ref) + b3f_ref[...]) * window
    hb = (dot3(w3b_ref) + b3b_ref[...]) * window
    hf = jnp.where(row == 0, hf + d_ref[0], hf)
    hb = jnp.where(row == 0, 0.0, hb)
    o_ref[0] = (hf + hb).astype(o_ref.dtype)
    o_ref[1] = (hf - hb).astype(o_ref.dtype)


def _prep_kernel(c_ref, aw_ref, ab_ref, *refs, filter_steps, ct):
    wa_ref, wh_ref = refs[-7:-5]
    mods_ref, sd_ref, wa_bf_ref, wh_bf_ref, hid_ref = refs[-5:]
    s = pl.program_id(0)
    _mods_kernel(c_ref, aw_ref, ab_ref, mods_ref)

    @pl.when(s < filter_steps)
    def _():
        _filter_sd_body(*refs[:-7], sd_ref, hid_ref, first=s == 0, j=s % ct)

    @pl.when(s < CAST_STEPS)
    def _():
        wa_bf_ref[...] = wa_ref[...].astype(BF16)

    @pl.when((s >= CAST_STEPS) & (s < 2 * CAST_STEPS))
    def _():
        wh_bf_ref[...] = wh_ref[...].astype(BF16)


CAST_STEPS = 8


def _prep(cc, ada_w, ada_b, w1, b1, w2, b2, w3, b3, freq, bias_d, w_out_a, w_out_h, *,
          tn_mods=512, tn=512):
    depth, d, n_mods = ada_w.shape
    mt = n_mods // tn_mods
    w = HYENA_WIDTH
    ct = w // tn
    filter_steps = 2 * ct
    assert filter_steps <= depth * mt
    n = w3.shape[1]
    fp = _pack_filter_mlp(w1, b1, w2, b2, freq)
    b3r = b3.reshape(1, n)
    order = lambda s: jnp.minimum(s, filter_steps - 1) // ct
    chan = lambda s: jnp.minimum(s, filter_steps - 1) % ct
    fwd_cols = lambda s: (0, 2 * order(s) * ct + chan(s))
    bwd_cols = lambda s: (0, (2 * order(s) + 1) * ct + chan(s))
    assert w_out_a.shape == w_out_h.shape and 2 * CAST_STEPS <= depth * mt
    cast_blk = (w_out_a.shape[0] // CAST_STEPS, w_out_a.shape[1])
    first_rows = lambda s: (jnp.minimum(s, CAST_STEPS - 1), 0)
    second_rows = lambda s: (jnp.clip(s - CAST_STEPS, 0, CAST_STEPS - 1), 0)
    return pl.pallas_call(
        functools.partial(_prep_kernel, filter_steps=filter_steps, ct=ct),
        grid=(depth * mt,),
        in_specs=[pl.BlockSpec((MOD_ROWS, d), lambda s: (0, 0)),
                  pl.BlockSpec((1, d, tn_mods), lambda s: (s // mt, 0, s % mt)),
                  pl.BlockSpec((1, 1, tn_mods), lambda s: (s // mt, 0, s % mt)),
                  pl.BlockSpec(fp.shape, lambda s: (0, 0)),
                  pl.BlockSpec((FILTER_HIDDEN, tn), fwd_cols), pl.BlockSpec((1, tn), fwd_cols),
                  pl.BlockSpec((FILTER_HIDDEN, tn), bwd_cols), pl.BlockSpec((1, tn), bwd_cols),
                  pl.BlockSpec((1, 1, tn), lambda s: (order(s), 0, chan(s))),
                  pl.BlockSpec(cast_blk, first_rows), pl.BlockSpec(cast_blk, second_rows)],
        out_specs=[pl.BlockSpec((1, MOD_ROWS, tn_mods), lambda s: (s // mt, 0, s % mt)),
                   pl.BlockSpec((2, SEQ, tn), lambda s: (0, 0, order(s) * ct + chan(s))),
                   pl.BlockSpec(cast_blk, first_rows), pl.BlockSpec(cast_blk, second_rows)],
        out_shape=[jax.ShapeDtypeStruct((depth, MOD_ROWS, n_mods), F32),
                   jax.ShapeDtypeStruct((2, SEQ, 2 * w), BF16),
                   jax.ShapeDtypeStruct(w_out_a.shape, BF16),
                   jax.ShapeDtypeStruct(w_out_h.shape, BF16)],
        scratch_shapes=[pltpu.VMEM((2, SEQ, LANES), BF16)],
        compiler_params=_cparams(("arbitrary",)),
        name="mods_and_filters",
    )(cc, ada_w, ada_b.reshape(depth, 1, n_mods),
      fp, w3, b3r, w3, b3r, bias_d.reshape(2, 1, w),
      w_out_a, w_out_h)


def _filter_spec_kernel(s_ref, d_ref, fq_ref, fo_ref, anti_ref, pq_ref, po_ref, k_ref):
    phases = {"a_ee": (pq_ref[0, 0], pq_ref[1, 0]), "a_eo": (pq_ref[0, 1], pq_ref[1, 1]),
              "a_o": (po_ref[0], po_ref[1])}
    is0 = lax.broadcasted_iota(jnp.int32, (QUARTER, MXU_COLS), 0) == 0
    for cols in _col_chunks(k_ref.shape[2]):
        sp = _spectrum(fq_ref, fo_ref, s_ref[0, :, cols], anti_ref)
        dp = _spectrum(fq_ref, fo_ref, d_ref[0, :, cols], anti_ref)
        for a, b, rows in GROUPS:
            cos, sin = phases[a]
            ka = sp[a] * cos + sp[b] * sin
            kb = dp[b] * cos - dp[a] * sin
            if a == "a_ee":
                scale = jnp.where(is0, 1.0 / FFT_N, 2.0 / FFT_N)
                kb = jnp.where(is0, sp[b], kb)
            else:
                scale = 2.0 / FFT_N
            k_ref[0, PLANES[a]:PLANES[a] + rows, cols] = (ka * scale).astype(k_ref.dtype)
            k_ref[0, PLANES[b]:PLANES[b] + rows, cols] = (kb * scale).astype(k_ref.dtype)


def _filter_spectra(sd, tabs, *, tn=512):
    w = HYENA_WIDTH
    ct = w // tn
    consts = [tabs[k] for k in ("fq", "fo", "anti", "pq", "po")]
    return pl.pallas_call(
        _filter_spec_kernel,
        grid=(2, ct),
        in_specs=[
            pl.BlockSpec((1, SEQ, tn), lambda o, j: (0, 0, o * ct + j)),
            pl.BlockSpec((1, SEQ, tn), lambda o, j: (1, 0, o * ct + j)),
        ] + [_resident(c.shape) for c in consts],
        out_specs=pl.BlockSpec((1, 4 * HALF, tn), lambda o, j: (o, 0, j)),
        out_shape=jax.ShapeDtypeStruct((2, 4 * HALF, w), F32),
        compiler_params=_cparams(("arbitrary", "arbitrary")),
        name="filter_spectra",
    )(sd, sd, *consts)


def _conv_fwd_kernel(x_ref, fq_ref, fo_ref, anti_ref, k_ref, y_ref):
    is0 = lax.broadcasted_iota(jnp.int32, (QUARTER, MXU_COLS), 0) == 0
    for cols in _col_chunks(y_ref.shape[2]):
        sp = _spectrum(fq_ref, fo_ref, x_ref[:, cols], anti_ref)
        for a, b, rows in GROUPS:
            ka = _plane(k_ref, 0, a, rows, cols).astype(F32)
            kb = _plane(k_ref, 0, b, rows, cols).astype(F32)
            bkb = sp[b] * kb
            if a == "a_ee":
                ya = sp[a] * ka - jnp.where(is0, 0.0, bkb)
                yb = jnp.where(is0, bkb, sp[a] * kb + sp[b] * ka)
            else:
                ya = sp[a] * ka - bkb
                yb = sp[a] * kb + sp[b] * ka
            y_ref[0, PLANES[a]:PLANES[a] + rows, cols] = ya.astype(y_ref.dtype)
            y_ref[0, PLANES[b]:PLANES[b] + rows, cols] = yb.astype(y_ref.dtype)


def _conv_fwd(x, x_blk0, kspec, order, tabs, batch, *, tn=512):
    w = HYENA_WIDTH
    ct = w // tn
    consts = [tabs[k] for k in ("fq", "fo", "anti")]
    return pl.pallas_call(
        _conv_fwd_kernel,
        grid=(ct, batch),
        in_specs=[pl.BlockSpec((SEQ, tn), lambda j, b: (b, x_blk0 * ct + j))]
        + [_resident(c.shape) for c in consts]
        + [pl.BlockSpec((1, 4 * HALF, tn), lambda j, b: (order, 0, j))],
        out_specs=pl.BlockSpec((1, 4 * HALF, tn), lambda j, b: (b, 0, j)),
        out_shape=jax.ShapeDtypeStruct((batch, 4 * HALF, w), BF16),
        compiler_params=_cparams(("arbitrary", "arbitrary")),
        name="long_conv_fwd",
    )(x, *consts, kspec)


def _conv_inv_kernel(*refs, gated):
    if gated:
        y_ref, iq_ref, io_ref, anti_ref, xm_ref, g_ref, o_ref, accq_ref, acco_ref = refs
    else:
        y_ref, iq_ref, io_ref, anti_ref, xm_ref, o_ref, accq_ref, acco_ref = refs
    chunks = _col_chunks(o_ref.shape[1])
    flipped = lambda v: _flip_rows(v.astype(BF16), anti_ref)

    def matmuls(t, slot):
        cols = chunks[t]
        for k, name in enumerate(("a_ee", "a_eo", "b_ee", "b_eo")):
            accq_ref[slot, k] = jnp.dot(iq_ref[k], _plane(y_ref, 0, name, QUARTER, cols),
                                        preferred_element_type=F32)
        for k, name in enumerate(("b_o", "a_o")):
            acco_ref[slot, k] = jnp.dot(io_ref[k], _plane(y_ref, 0, name, HALF, cols),
                                        preferred_element_type=F32)

    def finish(t, slot):
        cols = chunks[t]
        p, r, q, s = (accq_ref[slot, k] for k in range(4))
        ea = jnp.concatenate([p + r, flipped(p - r)], axis=0)
        eb = jnp.concatenate([q + s, flipped(s - q)], axis=0)
        p1 = ea + acco_ref[slot, 0]
        p2 = eb + acco_ref[slot, 1]
        for rows, y in ((slice(0, HALF), p1 + p2), (slice(HALF, SEQ), flipped(p1 - p2))):
            out = y * xm_ref[rows, cols].astype(F32)
            if gated:
                out = out * g_ref[rows, cols].astype(F32)
            o_ref[rows, cols] = out.astype(o_ref.dtype)

    _staggered(len(chunks), matmuls, finish)


def _conv_inv(y, tabs, p, m_blk0, g_blk0=None, *, tn=512):
    batch = y.shape[0]
    w = HYENA_WIDTH
    ct = w // tn
    gated = g_blk0 is not None
    col = lambda blk0: (lambda b, j: (b, blk0 * ct + j))
    consts = [tabs[k] for k in ("iq", "io", "anti")]
    in_specs = ([pl.BlockSpec((1, 4 * HALF, tn), lambda b, j: (b, 0, j))]
                + [_resident(c.shape) for c in consts]
                + [pl.BlockSpec((SEQ, tn), col(m_blk0))])
    args = [y, *consts, p]
    if gated:
        in_specs.append(pl.BlockSpec((SEQ, tn), col(g_blk0)))
        args.append(p)
    return pl.pallas_call(
        functools.partial(_conv_inv_kernel, gated=gated),
        grid=(batch, ct),
        in_specs=in_specs,
        out_specs=pl.BlockSpec((SEQ, tn), lambda b, j: (b, j)),
        out_shape=jax.ShapeDtypeStruct((batch * SEQ, w), BF16),
        scratch_shapes=[pltpu.VMEM((2, 4, QUARTER, MXU_COLS), F32),
                        pltpu.VMEM((2, 2, HALF, MXU_COLS), F32)],
        compiler_params=_cparams(("arbitrary", "arbitrary")),
        name="long_conv_inv",
    )(*args)


def _rope_tables():
    pos = np.arange(SEQ)
    row = (pos // GRID_W).astype(np.float32)
    col = (pos % GRID_W).astype(np.float32)
    half = HEAD_DIM // 2
    inv = (ROPE_BASE ** (-np.arange(0, half, 2, dtype=np.float32) / half)).astype(np.float32)
    ar = row[:, None] * inv[None]
    ac = col[:, None] * inv[None]
    cos = np.concatenate([np.cos(ar), np.cos(ar), np.cos(ac), np.cos(ac)], axis=1)
    sin = np.concatenate([-np.sin(ar), np.sin(ar), -np.sin(ac), np.sin(ac)], axis=1)
    return jnp.asarray(cos, F32), jnp.asarray(sin, F32)


def _tile_types(tn, kinds):
    per_tile = tn // LANES
    return [kinds[s0:s0 + per_tile] for s0 in range(0, len(kinds), per_tile)]


def kernel(x, c, ctx, c_ctx, norm_g, ada_w, ada_b, attn_w_in, attn_w_out, attn_sink, hy_w_in,
           hy_conv_w, hy_conv_b, hy_w1, hy_b1, hy_w2, hy_b2, hy_w3, hy_b3, hy_freq, hy_bias_d,
           hy_w_out, final_g):
    batch, seq, d = x.shape
    assert (seq, d) == (SEQ, D_MODEL) and ctx.shape[1] == CTX_LEN
    assert norm_g.shape[0] == 2 and attn_w_in.shape[0] == 1 and hy_w_in.shape[0] == 1
    w = HYENA_WIDTH

    ctx_row = batch
    cc = jnp.concatenate([c, c_ctx[None], jnp.zeros((MOD_ROWS - batch - 1, d), F32)], axis=0)
    mods, sd, w_out_attn, w_out_hyena = _prep(
        cc, ada_w, ada_b, hy_w1[0], hy_b1[0], hy_w2[0], hy_b2[0], hy_w3[0], hy_b3[0], hy_freq[0],
        hy_bias_d[0], attn_w_out[0], hy_w_out[0])
    mods = mods.reshape(-1, 1, d)

    x2 = x.reshape(batch * seq, d)
    ctx2 = ctx.reshape(batch * CTX_LEN, d)

    tn = 1024
    kinds0 = (["q"] * N_HEADS + ["k"] * N_KV_HEADS + [None] * N_KV_HEADS
              + ["g"] * (ATTN_WIDTH // LANES))
    hx = _norm_mod(x2, norm_g[0], mods, 0, tm=1024)
    px = _proj(hx, attn_w_in[0], 0, ATTN_IN, tm=SEQ, tn=tn, tile_types=_tile_types(tn, kinds0),
               rope=_rope_tables(), q_scale=HEAD_DIM ** -0.5 * LOG2E)
    ckv = _ctx_kv(ctx2, norm_g[0], mods, 0, ctx_row, attn_w_in[0], ATTN_WIDTH, 2 * KV_WIDTH)
    og = _attention(px, ckv, attn_sink[0], batch)
    x2, hx = _out_proj(og, w_out_attn, x2, mods, 0, next_norm_g=norm_g[1])

    kinds1 = ["c"] * (3 * w // LANES) + ["g"] * (w // LANES)
    tn1 = 1024
    p = _proj(hx, hy_w_in[0], 0, 4 * w, tm=SEQ, tn=tn1, tile_types=_tile_types(tn1, kinds1),
              conv=(hy_conv_w[0], hy_conv_b[0].reshape(1, 3 * w)))
    tabs = _fold_tables()
    kspec = _filter_spectra(sd, tabs)
    y1 = _conv_fwd(p, 2, kspec, 0, tabs, batch)
    z = _conv_inv(y1, tabs, p, 0)
    y2 = _conv_fwd(z, 0, kspec, 1, tabs, batch)
    yg = _conv_inv(y2, tabs, p, 1, g_blk0=3)
    out, = _out_proj(yg, w_out_hyena, x2, mods, 1, final_g=final_g)
    return out.reshape(batch, seq, d)
```

```python
import functools
import math

import jax
import jax.numpy as jnp
import numpy as np
from jax import lax
from jax.experimental import pallas as pl
from jax.experimental.pallas import tpu as pltpu

F32 = jnp.float32
BF16 = jnp.bfloat16

D_MODEL = 2048
SEQ = 2048
CTX_LEN = 256
GRID_W = 64
HEAD_DIM = 128
N_HEADS = 16
N_KV_HEADS = 4
GQA_GROUP = 4
ATTN_WIDTH = 2048
KV_WIDTH = 512
ATTN_IN = 2 * ATTN_WIDTH + 2 * KV_WIDTH
WINDOW = 128
BLOCK = 128
ROPE_BASE = 10000.0
HYENA_WIDTH = 2048
FILTER_EMB = 33
FILTER_HIDDEN = 64
DECAY_FAST = 0.3
DECAY_SLOW = 1.5
DECAY_TARGET = 1e-2
WINDOW_SHIFT = 0.05
NORM_EPS = 1e-6
NEG_INF = -1e30

LANES = 128
MOD_ROWS = 8
MXU_COLS = 256
ACC_ROWS = 1024
ONES_ROWS = 16
LOG2E = math.log2(math.e)
FFT_N = 2 * SEQ
VMEM_LIMIT = 56 * 1024 * 1024


def _cparams(sem):
    return pltpu.CompilerParams(dimension_semantics=sem, vmem_limit_bytes=VMEM_LIMIT)


def _mods_kernel(c_ref, w_ref, b_ref, o_ref):
    c = c_ref[...]
    s = c * jax.nn.sigmoid(c)
    s_hi = s.astype(BF16)
    s_lo = (s - s_hi.astype(F32)).astype(BF16)
    lhs = jnp.concatenate([s_hi, s_lo], axis=0)
    r = jnp.dot(lhs, w_ref[0].astype(BF16), preferred_element_type=F32)
    o_ref[0] = r[:MOD_ROWS] + r[MOD_ROWS:] + b_ref[0]


def _rope_slab(t, cos, sin):
    lane = lax.broadcasted_iota(jnp.int32, t.shape, 1)
    first = (lane % 64) < 32
    partner = jnp.where(first, pltpu.roll(t, 96, 1), pltpu.roll(t, 32, 1))
    return t * cos + partner * sin


def _norm_mod_rows(x, g, mul, add):
    ms = jnp.mean(x * x, axis=-1, keepdims=True)
    return x * lax.rsqrt(ms + NORM_EPS) * g * mul + add


def _norm_mod_kernel(x_ref, g_ref, sh_ref, sc_ref, o_ref, *, row_chunk):
    g = g_ref[...]
    mul = 1.0 + sc_ref[0]
    add = sh_ref[0]
    for r in range(0, x_ref.shape[0], row_chunk):
        o_ref[r:r + row_chunk, :] = _norm_mod_rows(x_ref[r:r + row_chunk, :], g, mul, add).astype(o_ref.dtype)


def _mod_row(layer, row, kind):
    return (layer * MOD_ROWS + row) * 3 + kind


def _mod_spec(base, tm, rows_per_mod, d):
    return pl.BlockSpec((1, 1, d), lambda i: (base + 3 * ((i * tm) // rows_per_mod), 0, 0))


def _norm_mod(x, g, mods, layer, *, tm):
    m, d = x.shape
    return pl.pallas_call(
        functools.partial(_norm_mod_kernel, row_chunk=min(tm, 256)),
        grid=(m // tm,),
        in_specs=[pl.BlockSpec((tm, d), lambda i: (i, 0)), pl.BlockSpec((1, d), lambda i: (0, 0)),
                  _mod_spec(_mod_row(layer, 0, 0), tm, SEQ, d),
                  _mod_spec(_mod_row(layer, 0, 1), tm, SEQ, d)],
        out_specs=pl.BlockSpec((tm, d), lambda i: (i, 0)),
        out_shape=jax.ShapeDtypeStruct((m, d), BF16),
        compiler_params=_cparams(("arbitrary",)),
        name="norm_mod",
    )(x, g.reshape(1, d), mods, mods)


def _ctx_kv_kernel(x_ref, g_ref, sh_ref, sc_ref, w_ref, o_ref, hx_ref, *, row_chunk):
    g = g_ref[...]
    mul = 1.0 + sc_ref[0]
    add = sh_ref[0]
    for r in range(0, x_ref.shape[0], row_chunk):
        hx_ref[r:r + row_chunk, :] = _norm_mod_rows(x_ref[r:r + row_chunk, :], g, mul, add).astype(BF16)
    wb = w_ref[...].astype(BF16)
    for c0 in range(0, o_ref.shape[1], MXU_COLS):
        o_ref[:, c0:c0 + MXU_COLS] = jnp.dot(hx_ref[...], wb[:, c0:c0 + MXU_COLS],
                                             preferred_element_type=F32).astype(o_ref.dtype)


def _ctx_kv(ctx, g, mods, layer, mod_row, w, col0, n):
    m, d = ctx.shape
    assert col0 % n == 0 and m <= ACC_ROWS
    shift = _mod_row(layer, mod_row, 0)
    scale = _mod_row(layer, mod_row, 1)
    return pl.pallas_call(
        functools.partial(_ctx_kv_kernel, row_chunk=256),
        grid=(1,),
        in_specs=[pl.BlockSpec((m, d), lambda i: (0, 0)), pl.BlockSpec((1, d), lambda i: (0, 0)),
                  pl.BlockSpec((1, 1, d), lambda i: (shift, 0, 0)),
                  pl.BlockSpec((1, 1, d), lambda i: (scale, 0, 0)),
                  pl.BlockSpec((d, n), lambda i: (0, col0 // n))],
        out_specs=pl.BlockSpec((m, n), lambda i: (0, 0)),
        out_shape=jax.ShapeDtypeStruct((m, n), BF16),
        scratch_shapes=[pltpu.VMEM((m, d), BF16)],
        compiler_params=_cparams(("arbitrary",)),
        name="ctx_kv",
    )(ctx, g.reshape(1, d), mods, mods, w)


def _proj_kernel(*refs, tile_types, has_rope, has_conv, q_scale):
    if has_rope:
        hx_ref, w_ref, cos_ref, sin_ref, o_ref, wb_ref = refs
    elif has_conv:
        hx_ref, w_ref, cw_ref, cb_ref, o_ref, wb_ref = refs
    else:
        hx_ref, w_ref, o_ref, wb_ref = refs
    j = pl.program_id(0)
    tm, tn = o_ref.shape

    @pl.when(pl.program_id(1) == 0)
    def _():
        wb_ref[...] = w_ref[...].astype(BF16)

    w_ref = wb_ref

    def plain():
        for c0 in range(0, tn, MXU_COLS):
            o_ref[:, c0:c0 + MXU_COLS] = chunk_dot(c0).astype(o_ref.dtype)

    def chunk_dot(c0):
        pieces = [jnp.dot(hx_ref[r:r + ACC_ROWS, :], w_ref[:, c0:c0 + MXU_COLS],
                          preferred_element_type=F32) for r in range(0, tm, ACC_ROWS)]
        return pieces[0] if len(pieces) == 1 else jnp.concatenate(pieces, axis=0)

    def gated():
        for c0 in range(0, tn, MXU_COLS):
            half = 0.5 * chunk_dot(c0)
            o_ref[:, c0:c0 + MXU_COLS] = (half + half * jnp.tanh(half)).astype(o_ref.dtype)

    def short_conv():
        for c0 in range(0, tn, MXU_COLS):
            acc = chunk_dot(c0)
            for h in range(MXU_COLS // LANES):
                lo = c0 + h * LANES
                u = acc[:, h * LANES:(h + 1) * LANES]
                w3 = cw_ref[:, lo:lo + LANES]
                b = cb_ref[:, lo:lo + LANES]
                y = pltpu.roll(u, 1, 0) * w3[0:1] + u * w3[1:2] + pltpu.roll(u, tm - 1, 0) * w3[2:3] + b
                o_ref[:, lo:lo + LANES] = y.astype(o_ref.dtype)
                first = u[0:1] * w3[1:2] + u[1:2] * w3[2:3] + b
                last = u[tm - 2:tm - 1] * w3[0:1] + u[tm - 1:tm] * w3[1:2] + b
                o_ref[0:1, lo:lo + LANES] = first.astype(o_ref.dtype)
                o_ref[tm - 1:tm, lo:lo + LANES] = last.astype(o_ref.dtype)

    def roped(types):
        cos = cos_ref[...]
        sin = sin_ref[...]
        for c0 in range(0, tn, MXU_COLS):
            acc = chunk_dot(c0)
            for h in range(MXU_COLS // LANES):
                ty = types[c0 // LANES + h]
                slab = acc[:, h * LANES:(h + 1) * LANES]
                if ty is not None:
                    slab = _rope_slab(slab, cos, sin)
                    if ty == "q":
                        slab = slab * q_scale
                lo = c0 + h * LANES
                o_ref[:, lo:lo + LANES] = slab.astype(o_ref.dtype)

    if tile_types is None:
        plain()
        return

    groups = {}
    for t, types in enumerate(tile_types):
        groups.setdefault(tuple(types), []).append(t)
    for types, tiles in groups.items():
        cond = functools.reduce(jnp.logical_or, [j == t for t in tiles])
        if all(ty is None for ty in types):
            pl.when(cond)(plain)
        elif all(ty == "g" for ty in types):
            pl.when(cond)(gated)
        elif all(ty == "c" for ty in types):
            pl.when(cond)(short_conv)
        else:
            assert "g" not in types and "c" not in types
            pl.when(cond)(functools.partial(roped, types))


def _proj(hx, w, col_blk0, n, *, tm, tn, tile_types=None, rope=None, conv=None, q_scale=1.0):
    m, d = hx.shape
    in_specs = [
        pl.BlockSpec((tm, d), lambda j, i: (i, 0)),
        pl.BlockSpec((d, tn), lambda j, i: (0, col_blk0 + j)),
    ]
    args = [hx, w]
    scratch = [pltpu.VMEM((d, tn), BF16)]
    if rope is not None:
        cos, sin = rope
        seq_tiles = cos.shape[0] // tm
        in_specs += [pl.BlockSpec((tm, LANES), lambda j, i: (i % seq_tiles, 0))] * 2
        args += [cos, sin]
    if conv is not None:
        assert rope is None and tm == SEQ
        cw, cb = conv
        last = cw.shape[1] // tn - 1
        in_specs += [pl.BlockSpec((cw.shape[0], tn), lambda j, i: (0, jnp.minimum(j, last))),
                     pl.BlockSpec((1, tn), lambda j, i: (0, jnp.minimum(j, last)))]
        args += [cw, cb]
    return pl.pallas_call(
        functools.partial(_proj_kernel, tile_types=tile_types, has_rope=rope is not None,
                          has_conv=conv is not None, q_scale=q_scale),
        grid=(n // tn, m // tm),
        in_specs=in_specs,
        out_specs=pl.BlockSpec((tm, tn), lambda j, i: (i, j)),
        out_shape=jax.ShapeDtypeStruct((m, n), BF16),
        scratch_shapes=scratch,
        compiler_params=_cparams(("arbitrary", "arbitrary")),
        name="proj",
    )(*args)


def _attn_kernel(sink_ref, q_ref, k_ref, v_ref, g_ref, kc_ref, vc_ref, o_ref,
                 vt_ref, vct_ref, bias_ref):
    kh = pl.program_id(1)
    band = 3 * BLOCK
    cols = GQA_GROUP * BLOCK
    n_blocks = SEQ // BLOCK

    vt_ref[:HEAD_DIM, :] = v_ref[...].T
    vct_ref[:HEAD_DIM, :] = vc_ref[...].T

    @pl.when((pl.program_id(0) == 0) & (kh == 0))
    def _():
        vt_ref[HEAD_DIM:, :] = jnp.ones((ONES_ROWS, SEQ), BF16)
        vct_ref[HEAD_DIM:, :] = jnp.ones((ONES_ROWS, CTX_LEN), BF16)
        krow = lax.broadcasted_iota(jnp.int32, (band, cols), 0)
        qcol = lax.broadcasted_iota(jnp.int32, (band, cols), 1) % BLOCK
        for idx, off in enumerate((0, -BLOCK, -2 * BLOCK)):
            bias_ref[idx] = jnp.where(jnp.abs(krow - qcol + off) <= WINDOW, 0.0, NEG_INF)

    kc = kc_ref[...]
    vct = vct_ref[...]
    sink_row = jnp.concatenate(
        [jnp.full((1, BLOCK), sink_ref[kh * GQA_GROUP + h] * LOG2E, F32) for h in range(GQA_GROUP)],
        axis=1)

    def body(n, carry):
        q0 = pl.multiple_of(n * BLOCK, BLOCK)
        ks = pl.multiple_of(jnp.clip((n - 1) * BLOCK, 0, SEQ - band), BLOCK)
        bidx = jnp.where(n == 0, 0, jnp.where(n == n_blocks - 1, 2, 1))
        qs = q_ref[pl.ds(q0, BLOCK), :]
        q4t = jnp.concatenate([qs[:, h * LANES:(h + 1) * LANES].T for h in range(GQA_GROUP)],
                              axis=1)
        kb = k_ref[pl.ds(ks, band), :]
        s_loc = jnp.dot(kb, q4t, preferred_element_type=F32) + bias_ref[bidx]
        s_ctx = jnp.dot(kc, q4t, preferred_element_type=F32)
        m = jnp.maximum(jnp.maximum(jnp.max(s_loc, axis=0, keepdims=True),
                                    jnp.max(s_ctx, axis=0, keepdims=True)), sink_row)
        p_loc = jnp.exp2(s_loc - m).astype(BF16)
        p_ctx = jnp.exp2(s_ctx - m).astype(BF16)
        ox = (jnp.dot(vt_ref[:, pl.ds(ks, band)], p_loc, preferred_element_type=F32)
              + jnp.dot(vct, p_ctx, preferred_element_type=F32))
        den = ox[HEAD_DIM:HEAD_DIM + 1, :] + jnp.exp2(sink_row - m)
        ot = ox[:HEAD_DIM, :] * (1.0 / den)
        gs = g_ref[pl.ds(q0, BLOCK), :].astype(F32)
        for h in range(GQA_GROUP):
            oh = ot[:, h * LANES:(h + 1) * LANES].T * gs[:, h * LANES:(h + 1) * LANES]
            o_ref[pl.ds(q0, BLOCK), h * LANES:(h + 1) * LANES] = oh.astype(o_ref.dtype)
        return carry

    lax.fori_loop(0, n_blocks, body, 0, unroll=8)


def _attention(px, ckv, sink, batch):
    gw = GQA_GROUP * HEAD_DIM
    k_blk0 = ATTN_WIDTH // HEAD_DIM
    v_blk0 = (ATTN_WIDTH + KV_WIDTH) // HEAD_DIM
    g_blk0 = (ATTN_WIDTH + 2 * KV_WIDTH) // gw
    return pl.pallas_call(
        _attn_kernel,
        grid=(batch, N_KV_HEADS),
        in_specs=[
            pl.BlockSpec(memory_space=pltpu.SMEM),
            pl.BlockSpec((SEQ, gw), lambda b, h: (b, h)),
            pl.BlockSpec((SEQ, HEAD_DIM), lambda b, h: (b, k_blk0 + h)),
            pl.BlockSpec((SEQ, HEAD_DIM), lambda b, h: (b, v_blk0 + h)),
            pl.BlockSpec((SEQ, gw), lambda b, h: (b, g_blk0 + h)),
            pl.BlockSpec((CTX_LEN, HEAD_DIM), lambda b, h: (b, h)),
            pl.BlockSpec((CTX_LEN, HEAD_DIM), lambda b, h: (b, N_KV_HEADS + h)),
        ],
        out_specs=pl.BlockSpec((SEQ, gw), lambda b, h: (b, h)),
        out_shape=jax.ShapeDtypeStruct((batch * SEQ, ATTN_WIDTH), BF16),
        scratch_shapes=[pltpu.VMEM((HEAD_DIM + ONES_ROWS, SEQ), BF16),
                        pltpu.VMEM((HEAD_DIM + ONES_ROWS, CTX_LEN), BF16),
                        pltpu.VMEM((3, 3 * BLOCK, GQA_GROUP * BLOCK), F32)],
        compiler_params=_cparams(("arbitrary", "arbitrary")),
        name="banded_attention",
    )(sink, px, px, px, px, ckv, ckv)


def _out_proj_kernel(*refs, final):
    if final:
        a_ref, w_ref, x_ref, gate_ref, fg_ref, o_ref = refs
    else:
        a_ref, w_ref, x_ref, gate_ref, ng_ref, sh_ref, sc_ref, o_ref, hx_ref = refs
    acc = jnp.dot(a_ref[...], w_ref[...], preferred_element_type=F32)
    y = x_ref[...] + gate_ref[0] * acc
    if final:
        ms = jnp.mean(y * y, axis=-1, keepdims=True)
        y = y * lax.rsqrt(ms + NORM_EPS) * fg_ref[...]
    else:
        hx_ref[...] = _norm_mod_rows(y, ng_ref[...], 1.0 + sc_ref[0], sh_ref[0]).astype(hx_ref.dtype)
    o_ref[...] = y


def _out_proj(a, w, x, mods, layer, *, final_g=None, next_norm_g=None, tm=512):
    m, d = x.shape
    kdim = a.shape[1]
    final = final_g is not None
    row_tile = pl.BlockSpec((tm, d), lambda i: (i, 0))
    vec = pl.BlockSpec((1, d), lambda i: (0, 0))
    in_specs = [
        pl.BlockSpec((tm, kdim), lambda i: (i, 0)),
        pl.BlockSpec((kdim, d), lambda i: (0, 0)),
        row_tile,
        _mod_spec(_mod_row(layer, 0, 2), tm, SEQ, d),
    ]
    args = [a, w, x, mods]
    out_specs = [row_tile]
    out_shape = [jax.ShapeDtypeStruct((m, d), F32)]
    if final:
        in_specs.append(vec)
        args.append(final_g.reshape(1, d))
    else:
        in_specs += [vec, _mod_spec(_mod_row(layer + 1, 0, 0), tm, SEQ, d),
                     _mod_spec(_mod_row(layer + 1, 0, 1), tm, SEQ, d)]
        args += [next_norm_g.reshape(1, d), mods, mods]
        out_specs.append(row_tile)
        out_shape.append(jax.ShapeDtypeStruct((m, d), BF16))
    return pl.pallas_call(
        functools.partial(_out_proj_kernel, final=final),
        grid=(m // tm,),
        in_specs=in_specs,
        out_specs=out_specs,
        out_shape=out_shape,
        compiler_params=_cparams(("arbitrary",)),
        name="out_proj",
    )(*args)


HALF = SEQ // 2
QUARTER = SEQ // 4
FLIP_BLOCK = 256


def _fold_tables():
    def tab(f, n):
        t2 = 2 * np.arange(n, dtype=np.int64) + 1
        ang = ((f[:, None] * t2[None, :]) % (2 * FFT_N)).astype(np.float64) * (math.pi / FFT_N)
        return np.cos(ang), np.sin(ang)

    r = np.arange(QUARTER, dtype=np.int64)
    g = np.arange(HALF, dtype=np.int64)
    f_ee, f_eo, f_o = 4 * r, 4 * r + 2, 2 * g + 1
    cee, see = tab(f_ee, QUARTER)
    ceo, seo = tab(f_eo, QUARTER)
    co, so = tab(f_o, HALF)
    see[0, :] = np.where(np.arange(QUARTER) % 2 == 0, 1.0, -1.0)
    const = lambda a, dt: jnp.asarray(np.ascontiguousarray(a).astype(np.float32)).astype(dt)
    phase = lambda f: np.stack([np.cos(f * (math.pi / FFT_N)), np.sin(f * (math.pi / FFT_N))])
    return dict(
        fq=const(np.stack([cee, see, ceo, seo]), BF16),
        fo=const(np.stack([co, so]), BF16),
        iq=const(np.stack([cee.T, ceo.T, see.T, seo.T]), BF16),
        io=const(np.stack([so.T, co.T]), BF16),
        anti=const(np.eye(FLIP_BLOCK)[::-1], BF16),
        pq=const(np.stack([phase(f_ee), phase(f_eo)], axis=1)[..., None], F32),
        po=const(phase(f_o)[..., None], F32),
    )


def _flip_rows(h, anti):
    nb = h.shape[0] // FLIP_BLOCK
    return jnp.concatenate(
        [jnp.dot(anti[...], h[(nb - 1 - a) * FLIP_BLOCK:(nb - a) * FLIP_BLOCK, :], preferred_element_type=F32)
         for a in range(nb)], axis=0)


def _fold_two_levels(x, anti):
    q0, q1, q2, q3 = (x[k * QUARTER:(k + 1) * QUARTER] for k in range(4))
    f1, f2, f3 = (_flip_rows(q, anti) for q in (q1, q2, q3))
    q0, q1, q2 = (q.astype(F32) for q in (q0, q1, q2))
    xs0, xa0 = q0 + f3, q0 - f3
    fxs1, fxa1 = f1 + q2, f1 - q2
    bf = lambda v: v.astype(BF16)
    xs = jnp.concatenate([bf(xs0), bf(q1 + f2)], axis=0)
    xa = jnp.concatenate([bf(xa0), bf(q1 - f2)], axis=0)
    return xs, xa, bf(xs0 + fxs1), bf(xs0 - fxs1), bf(xa0 + fxa1), bf(xa0 - fxa1)


def _staggered(n, matmuls, finish):
    matmuls(0, 0)
    for t in range(1, n):
        matmuls(t, t % 2)
        finish(t - 1, (t - 1) % 2)
    finish(n - 1, (n - 1) % 2)


def _col_chunks(n):
    return [slice(c, c + MXU_COLS) for c in range(0, n, MXU_COLS)]


PLANES = {"a_ee": 0, "b_ee": QUARTER, "a_eo": 2 * QUARTER, "b_eo": 3 * QUARTER,
          "a_o": 2 * HALF, "b_o": 3 * HALF}
GROUPS = (("a_ee", "b_ee", QUARTER), ("a_eo", "b_eo", QUARTER), ("a_o", "b_o", HALF))


def _spectrum(fq_ref, fo_ref, x, anti):
    xs, xa, xss, xsa, xas, xaa = _fold_two_levels(x, anti)
    dq = lambda k, v: jnp.dot(fq_ref[k], v, preferred_element_type=F32)
    do = lambda k, v: jnp.dot(fo_ref[k], v, preferred_element_type=F32)
    return {"a_ee": dq(0, xss), "b_ee": dq(1, xaa), "a_eo": dq(2, xsa), "b_eo": dq(3, xas),
            "a_o": do(0, xa), "b_o": do(1, xs)}


def _plane(ref, lead, name, rows, cols):
    r0 = PLANES[name]
    return ref[lead, r0:r0 + rows, cols]


def _resident(shape):
    zeros = (0,) * len(shape)
    return pl.BlockSpec(shape, lambda *_: zeros, pipeline_mode=pl.Buffered(1))


def _pack_filter_mlp(w1, b1, w2, b2, freq):
    hpad = LANES - FILTER_HIDDEN
    bands = (FILTER_EMB - 1) // 2
    row = lambda v: jnp.pad(v.reshape(1, -1), ((0, 7), (0, hpad)))
    mat = lambda m: jnp.pad(m, ((0, LANES - m.shape[0]), (0, hpad)))
    return jnp.concatenate([row(w1[0]), row(b1), row(b2), row(freq), jnp.zeros((LANES - 32, LANES), F32),
                            mat(w1[1:1 + bands]), mat(w1[1 + bands:]), mat(w2)], axis=0)


def _filter_sd_body(fp_ref, w3f_ref, b3f_ref, w3b_ref, b3b_ref, d_ref, o_ref, hid_ref, *, first, j):
    length = o_ref.shape[1]
    tn = o_ref.shape[2]
    hp = lax.Precision.HIGHEST
    w1t_ref, b1_ref, b2_ref, fq_ref = (fp_ref.at[r:r + 1] for r in (0, 8, 16, 24))
    w1c_ref, w1s_ref, w2_ref = (fp_ref.at[r:r + LANES] for r in (LANES, 2 * LANES, 3 * LANES))

    @pl.when(first)
    def _():
        n = lax.broadcasted_iota(jnp.int32, (length, LANES), 0).astype(F32)
        lane = lax.broadcasted_iota(jnp.int32, (length, LANES), 1)
        bands = (FILTER_EMB - 1) // 2
        fr_step = (bands - 1 - 1e-4) / (bands - 1)
        fr = jnp.where(lane < bands, 1e-4 + lane.astype(F32) * fr_step, 0.0)
        ang = (2.0 * math.pi * n / length) * fr
        t = n / (length - 1)
        fq = fq_ref[...]
        pre = (t * w1t_ref[...]
               + jnp.dot(jnp.cos(ang), w1c_ref[...], precision=hp, preferred_element_type=F32)
               + jnp.dot(-jnp.sin(ang), w1s_ref[...], precision=hp, preferred_element_type=F32)
               + b1_ref[...])
        hid = jnp.sin(fq * pre)
        hid = jnp.sin(fq * (jnp.dot(hid, w2_ref[...], precision=hp, preferred_element_type=F32)
                            + b2_ref[...]))
        hid_hi = hid.astype(BF16)
        hid_ref[0] = hid_hi
        hid_ref[1] = (hid - hid_hi.astype(F32)).astype(BF16)

    def dot3(w_ref):
        wf = w_ref[...]
        wf = jnp.concatenate([wf, jnp.zeros((LANES - wf.shape[0], wf.shape[1]), F32)], axis=0)
        w_hi = wf.astype(BF16)
        w_lo = (wf - w_hi.astype(F32)).astype(BF16)
        return (jnp.dot(hid_ref[0], w_hi, preferred_element_type=F32)
                + jnp.dot(hid_ref[1], w_hi, preferred_element_type=F32)
                + jnp.dot(hid_ref[0], w_lo, preferred_element_type=F32))

    row = lax.broadcasted_iota(jnp.int32, (length, tn), 0)
    chan = (lax.broadcasted_iota(jnp.int32, (length, tn), 1) + j * tn).astype(F32)
    min_decay = math.log(DECAY_TARGET) / DECAY_SLOW
    max_decay = math.log(DECAY_TARGET) / DECAY_FAST
    delta = min_decay + chan * ((max_decay - min_decay) / (HYENA_WIDTH - 1))
    t = row.astype(F32) / (length - 1)
    window = jnp.exp(-t * jnp.abs(delta)) + WINDOW_SHIFT
    hf = (dot3(w3f_ref) + b3f_ref[...]) * window
    hb = (dot3(w3b_ref) + b3b_ref[...]) * window
    hf = jnp.where(row == 0, hf + d_ref[0], hf)
    hb = jnp.where(row == 0, 0.0, hb)
    o_ref[0] = (hf + hb).astype(o_ref.dtype)
    o_ref[1] = (hf - hb).astype(o_ref.dtype)


def _prep_kernel(c_ref, aw_ref, ab_ref, *refs, filter_steps, ct):
    wa_ref, wh_ref = refs[-7:-5]
    mods_ref, sd_ref, wa_bf_ref, wh_bf_ref, hid_ref = refs[-5:]
    s = pl.program_id(0)
    _mods_kernel(c_ref, aw_ref, ab_ref, mods_ref)

    @pl.when(s < filter_steps)
    def _():
        _filter_sd_body(*refs[:-7], sd_ref, hid_ref, first=s == 0, j=s % ct)

    @pl.when(s < CAST_STEPS)
    def _():
        wa_bf_ref[...] = wa_ref[...].astype(BF16)

    @pl.when((s >= CAST_STEPS) & (s < 2 * CAST_STEPS))
    def _():
        wh_bf_ref[...] = wh_ref[...].astype(BF16)


CAST_STEPS = 8


def _prep(cc, ada_w, ada_b, w1, b1, w2, b2, w3, b3, freq, bias_d, w_out_a, w_out_h, *,
          tn_mods=512, tn=256):
    depth, d, n_mods = ada_w.shape
    mt = n_mods // tn_mods
    w = HYENA_WIDTH
    ct = w // tn
    filter_steps = 2 * ct
    assert filter_steps <= depth * mt
    n = w3.shape[1]
    fp = _pack_filter_mlp(w1, b1, w2, b2, freq)
    b3r = b3.reshape(1, n)
    order = lambda s: jnp.minimum(s, filter_steps - 1) // ct
    chan = lambda s: jnp.minimum(s, filter_steps - 1) % ct
    fwd_cols = lambda s: (0, 2 * order(s) * ct + chan(s))
    bwd_cols = lambda s: (0, (2 * order(s) + 1) * ct + chan(s))
    assert w_out_a.shape == w_out_h.shape and 2 * CAST_STEPS <= depth * mt
    cast_blk = (w_out_a.shape[0] // CAST_STEPS, w_out_a.shape[1])
    first_rows = lambda s: (jnp.minimum(s, CAST_STEPS - 1), 0)
    second_rows = lambda s: (jnp.clip(s - CAST_STEPS, 0, CAST_STEPS - 1), 0)
    return pl.pallas_call(
        functools.partial(_prep_kernel, filter_steps=filter_steps, ct=ct),
        grid=(depth * mt,),
        in_specs=[pl.BlockSpec((MOD_ROWS, d), lambda s: (0, 0)),
                  pl.BlockSpec((1, d, tn_mods), lambda s: (s // mt, 0, s % mt)),
                  pl.BlockSpec((1, 1, tn_mods), lambda s: (s // mt, 0, s % mt)),
                  pl.BlockSpec(fp.shape, lambda s: (0, 0)),
                  pl.BlockSpec((FILTER_HIDDEN, tn), fwd_cols), pl.BlockSpec((1, tn), fwd_cols),
                  pl.BlockSpec((FILTER_HIDDEN, tn), bwd_cols), pl.BlockSpec((1, tn), bwd_cols),
                  pl.BlockSpec((1, 1, tn), lambda s: (order(s), 0, chan(s))),
                  pl.BlockSpec(cast_blk, first_rows), pl.BlockSpec(cast_blk, second_rows)],
        out_specs=[pl.BlockSpec((1, MOD_ROWS, tn_mods), lambda s: (s // mt, 0, s % mt)),
                   pl.BlockSpec((2, SEQ, tn), lambda s: (0, 0, order(s) * ct + chan(s))),
                   pl.BlockSpec(cast_blk, first_rows), pl.BlockSpec(cast_blk, second_rows)],
        out_shape=[jax.ShapeDtypeStruct((depth, MOD_ROWS, n_mods), F32),
                   jax.ShapeDtypeStruct((2, SEQ, 2 * w), BF16),
                   jax.ShapeDtypeStruct(w_out_a.shape, BF16),
                   jax.ShapeDtypeStruct(w_out_h.shape, BF16)],
        scratch_shapes=[pltpu.VMEM((2, SEQ, LANES), BF16)],
        compiler_params=_cparams(("arbitrary",)),
        name="mods_and_filters",
    )(cc, ada_w, ada_b.reshape(depth, 1, n_mods),
      fp, w3, b3r, w3, b3r, bias_d.reshape(2, 1, w),
      w_out_a, w_out_h)


def _filter_spec_kernel(s_ref, d_ref, fq_ref, fo_ref, anti_ref, pq_ref, po_ref, k_ref):
    phases = {"a_ee": (pq_ref[0, 0], pq_ref[1, 0]), "a_eo": (pq_ref[0, 1], pq_ref[1, 1]),
              "a_o": (po_ref[0], po_ref[1])}
    is0 = lax.broadcasted_iota(jnp.int32, (QUARTER, MXU_COLS), 0) == 0
    for cols in _col_chunks(k_ref.shape[2]):
        sp = _spectrum(fq_ref, fo_ref, s_ref[0, :, cols], anti_ref)
        dp = _spectrum(fq_ref, fo_ref, d_ref[0, :, cols], anti_ref)
        for a, b, rows in GROUPS:
            cos, sin = phases[a]
            ka = sp[a] * cos + sp[b] * sin
            kb = dp[b] * cos - dp[a] * sin
            if a == "a_ee":
                scale = jnp.where(is0, 1.0 / FFT_N, 2.0 / FFT_N)
                kb = jnp.where(is0, sp[b], kb)
            else:
                scale = 2.0 / FFT_N
            k_ref[0, PLANES[a]:PLANES[a] + rows, cols] = (ka * scale).astype(k_ref.dtype)
            k_ref[0, PLANES[b]:PLANES[b] + rows, cols] = (kb * scale).astype(k_ref.dtype)


def _filter_spectra(sd, tabs, *, tn=512):
    w = HYENA_WIDTH
    ct = w // tn
    consts = [tabs[k] for k in ("fq", "fo", "anti", "pq", "po")]
    return pl.pallas_call(
        _filter_spec_kernel,
        grid=(2, ct),
        in_specs=[
            pl.BlockSpec((1, SEQ, tn), lambda o, j: (0, 0, o * ct + j)),
            pl.BlockSpec((1, SEQ, tn), lambda o, j: (1, 0, o * ct + j)),
        ] + [_resident(c.shape) for c in consts],
        out_specs=pl.BlockSpec((1, 4 * HALF, tn), lambda o, j: (o, 0, j)),
        out_shape=jax.ShapeDtypeStruct((2, 4 * HALF, w), F32),
        compiler_params=_cparams(("arbitrary", "arbitrary")),
        name="filter_spectra",
    )(sd, sd, *consts)


def _conv_fwd_kernel(x_ref, fq_ref, fo_ref, anti_ref, k_ref, y_ref):
    is0 = lax.broadcasted_iota(jnp.int32, (QUARTER, MXU_COLS), 0) == 0
    for cols in _col_chunks(y_ref.shape[2]):
        sp = _spectrum(fq_ref, fo_ref, x_ref[:, cols], anti_ref)
        for a, b, rows in GROUPS:
            ka = _plane(k_ref, 0, a, rows, cols).astype(F32)
            kb = _plane(k_ref, 0, b, rows, cols).astype(F32)
            bkb = sp[b] * kb
            if a == "a_ee":
                ya = sp[a] * ka - jnp.where(is0, 0.0, bkb)
                yb = jnp.where(is0, bkb, sp[a] * kb + sp[b] * ka)
            else:
                ya = sp[a] * ka - bkb
                yb = sp[a] * kb + sp[b] * ka
            y_ref[0, PLANES[a]:PLANES[a] + rows, cols] = ya.astype(y_ref.dtype)
            y_ref[0, PLANES[b]:PLANES[b] + rows, cols] = yb.astype(y_ref.dtype)


def _conv_fwd(x, x_blk0, kspec, order, tabs, batch, *, tn=512):
    w = HYENA_WIDTH
    ct = w // tn
    consts = [tabs[k] for k in ("fq", "fo", "anti")]
    return pl.pallas_call(
        _conv_fwd_kernel,
        grid=(ct, batch),
        in_specs=[pl.BlockSpec((SEQ, tn), lambda j, b: (b, x_blk0 * ct + j))]
        + [_resident(c.shape) for c in consts]
        + [pl.BlockSpec((1, 4 * HALF, tn), lambda j, b: (order, 0, j))],
        out_specs=pl.BlockSpec((1, 4 * HALF, tn), lambda j, b: (b, 0, j)),
        out_shape=jax.ShapeDtypeStruct((batch, 4 * HALF, w), BF16),
        compiler_params=_cparams(("arbitrary", "arbitrary")),
        name="long_conv_fwd",
    )(x, *consts, kspec)


def _conv_inv_kernel(*refs, gated):
    if gated:
        y_ref, iq_ref, io_ref, anti_ref, xm_ref, g_ref, o_ref, accq_ref, acco_ref = refs
    else:
        y_ref, iq_ref, io_ref, anti_ref, xm_ref, o_ref, accq_ref, acco_ref = refs
    chunks = _col_chunks(o_ref.shape[1])
    flipped = lambda v: _flip_rows(v.astype(BF16), anti_ref)

    def matmuls(t, slot):
        cols = chunks[t]
        for k, name in enumerate(("a_ee", "a_eo", "b_ee", "b_eo")):
            accq_ref[slot, k] = jnp.dot(iq_ref[k], _plane(y_ref, 0, name, QUARTER, cols),
                                        preferred_element_type=F32)
        for k, name in enumerate(("b_o", "a_o")):
            acco_ref[slot, k] = jnp.dot(io_ref[k], _plane(y_ref, 0, name, HALF, cols),
                                        preferred_element_type=F32)

    def finish(t, slot):
        cols = chunks[t]
        p, r, q, s = (accq_ref[slot, k] for k in range(4))
        osum = acco_ref[slot, 0] + acco_ref[slot, 1]
        odif = acco_ref[slot, 0] - acco_ref[slot, 1]
        e1, e2, d1, d2 = p + r, q + s, p - r, s - q
        quarters = (e1 + e2 + osum[:QUARTER],
                    flipped(d1 + d2) + osum[QUARTER:],
                    d1 - d2 + flipped(odif[QUARTER:]),
                    flipped(e1 - e2 + odif[:QUARTER]))
        for k, y in enumerate(quarters):
            rows = slice(k * QUARTER, (k + 1) * QUARTER)
            out = y * xm_ref[rows, cols].astype(F32)
            if gated:
                out = out * g_ref[rows, cols].astype(F32)
            o_ref[rows, cols] = out.astype(o_ref.dtype)

    _staggered(len(chunks), matmuls, finish)


def _conv_inv(y, tabs, p, m_blk0, g_blk0=None, *, tn=512):
    batch = y.shape[0]
    w = HYENA_WIDTH
    ct = w // tn
    gated = g_blk0 is not None
    col = lambda blk0: (lambda b, j: (b, blk0 * ct + j))
    consts = [tabs[k] for k in ("iq", "io", "anti")]
    in_specs = ([pl.BlockSpec((1, 4 * HALF, tn), lambda b, j: (b, 0, j))]
                + [_resident(c.shape) for c in consts]
                + [pl.BlockSpec((SEQ, tn), col(m_blk0))])
    args = [y, *consts, p]
    if gated:
        in_specs.append(pl.BlockSpec((SEQ, tn), col(g_blk0)))
        args.append(p)
    return pl.pallas_call(
        functools.partial(_conv_inv_kernel, gated=gated),
        grid=(batch, ct),
        in_specs=in_specs,
        out_specs=pl.BlockSpec((SEQ, tn), lambda b, j: (b, j)),
        out_shape=jax.ShapeDtypeStruct((batch * SEQ, w), BF16),
        scratch_shapes=[pltpu.VMEM((2, 4, QUARTER, MXU_COLS), F32),
                        pltpu.VMEM((2, 2, HALF, MXU_COLS), F32)],
        compiler_params=_cparams(("arbitrary", "arbitrary")),
        name="long_conv_inv",
    )(*args)


def _rope_tables():
    pos = np.arange(SEQ)
    row = (pos // GRID_W).astype(np.float32)
    col = (pos % GRID_W).astype(np.float32)
    half = HEAD_DIM // 2
    inv = (ROPE_BASE ** (-np.arange(0, half, 2, dtype=np.float32) / half)).astype(np.float32)
    ar = row[:, None] * inv[None]
    ac = col[:, None] * inv[None]
    cos = np.concatenate([np.cos(ar), np.cos(ar), np.cos(ac), np.cos(ac)], axis=1)
    sin = np.concatenate([-np.sin(ar), np.sin(ar), -np.sin(ac), np.sin(ac)], axis=1)
    return jnp.asarray(cos, F32), jnp.asarray(sin, F32)


def _tile_types(tn, kinds):
    per_tile = tn // LANES
    return [kinds[s0:s0 + per_tile] for s0 in range(0, len(kinds), per_tile)]


def kernel(x, c, ctx, c_ctx, norm_g, ada_w, ada_b, attn_w_in, attn_w_out, attn_sink, hy_w_in,
           hy_conv_w, hy_conv_b, hy_w1, hy_b1, hy_w2, hy_b2, hy_w3, hy_b3, hy_freq, hy_bias_d,
           hy_w_out, final_g):
    batch, seq, d = x.shape
    assert (seq, d) == (SEQ, D_MODEL) and ctx.shape[1] == CTX_LEN
    assert norm_g.shape[0] == 2 and attn_w_in.shape[0] == 1 and hy_w_in.shape[0] == 1
    w = HYENA_WIDTH

    ctx_row = batch
    cc = jnp.concatenate([c, c_ctx[None], jnp.zeros((MOD_ROWS - batch - 1, d), F32)], axis=0)
    mods, sd, w_out_attn, w_out_hyena = _prep(
        cc, ada_w, ada_b, hy_w1[0], hy_b1[0], hy_w2[0], hy_b2[0], hy_w3[0], hy_b3[0], hy_freq[0],
        hy_bias_d[0], attn_w_out[0], hy_w_out[0])
    mods = mods.reshape(-1, 1, d)

    x2 = x.reshape(batch * seq, d)
    ctx2 = ctx.reshape(batch * CTX_LEN, d)

    tn = 1024
    kinds0 = (["q"] * N_HEADS + ["k"] * N_KV_HEADS + [None] * N_KV_HEADS
              + ["g"] * (ATTN_WIDTH // LANES))
    hx = _norm_mod(x2, norm_g[0], mods, 0, tm=1024)
    px = _proj(hx, attn_w_in[0], 0, ATTN_IN, tm=SEQ, tn=tn, tile_types=_tile_types(tn, kinds0),
               rope=_rope_tables(), q_scale=HEAD_DIM ** -0.5 * LOG2E)
    ckv = _ctx_kv(ctx2, norm_g[0], mods, 0, ctx_row, attn_w_in[0], ATTN_WIDTH, 2 * KV_WIDTH)
    og = _attention(px, ckv, attn_sink[0], batch)
    x2, hx = _out_proj(og, w_out_attn, x2, mods, 0, next_norm_g=norm_g[1])

    kinds1 = ["c"] * (3 * w // LANES) + ["g"] * (w // LANES)
    tn1 = 1024
    p = _proj(hx, hy_w_in[0], 0, 4 * w, tm=SEQ, tn=tn1, tile_types=_tile_types(tn1, kinds1),
              conv=(hy_conv_w[0], hy_conv_b[0].reshape(1, 3 * w)))
    tabs = _fold_tables()
    kspec = _filter_spectra(sd, tabs)
    y1 = _conv_fwd(p, 2, kspec, 0, tabs, batch)
    z = _conv_inv(y1, tabs, p, 0)
    y2 = _conv_fwd(z, 0, kspec, 1, tabs, batch)
    yg = _conv_inv(y2, tabs, p, 1, g_blk0=3)
    out, = _out_proj(yg, w_out_hyena, x2, mods, 1, final_g=final_g)
    return out.reshape(batch, seq, d)
```

```python
import functools
import math

import jax
import jax.numpy as jnp
import numpy as np
from jax import lax
from jax.experimental import pallas as pl
from jax.experimental.pallas import tpu as pltpu

F32 = jnp.float32
BF16 = jnp.bfloat16

D_MODEL = 2048
SEQ = 2048
CTX_LEN = 256
GRID_W = 64
HEAD_DIM = 128
N_HEADS = 16
N_KV_HEADS = 4
GQA_GROUP = 4
ATTN_WIDTH = 2048
KV_WIDTH = 512
ATTN_IN = 2 * ATTN_WIDTH + 2 * KV_WIDTH
WINDOW = 128
BLOCK = 128
ROPE_BASE = 10000.0
HYENA_WIDTH = 2048
FILTER_EMB = 33
FILTER_HIDDEN = 64
DECAY_FAST = 0.3
DECAY_SLOW = 1.5
DECAY_TARGET = 1e-2
WINDOW_SHIFT = 0.05
NORM_EPS = 1e-6
NEG_INF = -1e30

LANES = 128
MOD_ROWS = 8
MXU_COLS = 256
ACC_ROWS = 1024
ONES_ROWS = 16
LOG2E = math.log2(math.e)
FFT_N = 2 * SEQ
VMEM_LIMIT = 56 * 1024 * 1024


def _cparams(sem):
    return pltpu.CompilerParams(dimension_semantics=sem, vmem_limit_bytes=VMEM_LIMIT)


def _mods_kernel(c_ref, w_ref, b_ref, o_ref):
    c = c_ref[...]
    s = c * jax.nn.sigmoid(c)
    s_hi = s.astype(BF16)
    s_lo = (s - s_hi.astype(F32)).astype(BF16)
    lhs = jnp.concatenate([s_hi, s_lo], axis=0)
    r = jnp.dot(lhs, w_ref[0].astype(BF16), preferred_element_type=F32)
    o_ref[0] = r[:MOD_ROWS] + r[MOD_ROWS:] + b_ref[0]


def _rope_slab(t, cos, sin):
    lane = lax.broadcasted_iota(jnp.int32, t.shape, 1)
    first = (lane % 64) < 32
    partner = jnp.where(first, pltpu.roll(t, 96, 1), pltpu.roll(t, 32, 1))
    return t * cos + partner * sin


def _norm_mod_rows(x, gain, add):
    ms = jnp.mean(x * x, axis=-1, keepdims=True)
    return x * lax.rsqrt(ms + NORM_EPS) * gain + add


def _norm_mod_kernel(x_ref, g_ref, sh_ref, sc_ref, o_ref, *, row_chunk):
    gain = g_ref[...] * (1.0 + sc_ref[0])
    add = sh_ref[0]
    for r in range(0, x_ref.shape[0], row_chunk):
        o_ref[r:r + row_chunk, :] = _norm_mod_rows(x_ref[r:r + row_chunk, :], gain, add).astype(o_ref.dtype)


def _mod_row(layer, row, kind):
    return (layer * MOD_ROWS + row) * 3 + kind


def _mod_spec(base, tm, rows_per_mod, d):
    return pl.BlockSpec((1, 1, d), lambda i: (base + 3 * ((i * tm) // rows_per_mod), 0, 0))


def _norm_mod(x, g, mods, layer, *, tm):
    m, d = x.shape
    return pl.pallas_call(
        functools.partial(_norm_mod_kernel, row_chunk=min(tm, 256)),
        grid=(m // tm,),
        in_specs=[pl.BlockSpec((tm, d), lambda i: (i, 0)), pl.BlockSpec((1, d), lambda i: (0, 0)),
                  _mod_spec(_mod_row(layer, 0, 0), tm, SEQ, d),
                  _mod_spec(_mod_row(layer, 0, 1), tm, SEQ, d)],
        out_specs=pl.BlockSpec((tm, d), lambda i: (i, 0)),
        out_shape=jax.ShapeDtypeStruct((m, d), BF16),
        compiler_params=_cparams(("arbitrary",)),
        name="norm_mod",
    )(x, g.reshape(1, d), mods, mods)


def _ctx_kv_kernel(x_ref, g_ref, sh_ref, sc_ref, w_ref, o_ref, hx_ref, *, row_chunk):
    gain = g_ref[...] * (1.0 + sc_ref[0])
    add = sh_ref[0]
    for r in range(0, x_ref.shape[0], row_chunk):
        hx_ref[r:r + row_chunk, :] = _norm_mod_rows(x_ref[r:r + row_chunk, :], gain, add).astype(BF16)
    wb = w_ref[...].astype(BF16)
    for c0 in range(0, o_ref.shape[1], MXU_COLS):
        o_ref[:, c0:c0 + MXU_COLS] = jnp.dot(hx_ref[...], wb[:, c0:c0 + MXU_COLS],
                                             preferred_element_type=F32).astype(o_ref.dtype)


def _ctx_kv(ctx, g, mods, layer, mod_row, w, col0, n):
    m, d = ctx.shape
    assert col0 % n == 0 and m <= ACC_ROWS
    shift = _mod_row(layer, mod_row, 0)
    scale = _mod_row(layer, mod_row, 1)
    return pl.pallas_call(
        functools.partial(_ctx_kv_kernel, row_chunk=256),
        grid=(1,),
        in_specs=[pl.BlockSpec((m, d), lambda i: (0, 0)), pl.BlockSpec((1, d), lambda i: (0, 0)),
                  pl.BlockSpec((1, 1, d), lambda i: (shift, 0, 0)),
                  pl.BlockSpec((1, 1, d), lambda i: (scale, 0, 0)),
                  pl.BlockSpec((d, n), lambda i: (0, col0 // n))],
        out_specs=pl.BlockSpec((m, n), lambda i: (0, 0)),
        out_shape=jax.ShapeDtypeStruct((m, n), BF16),
        scratch_shapes=[pltpu.VMEM((m, d), BF16)],
        compiler_params=_cparams(("arbitrary",)),
        name="ctx_kv",
    )(ctx, g.reshape(1, d), mods, mods, w)


def _proj_kernel(*refs, tile_types, has_rope, has_conv, q_scale):
    if has_rope:
        hx_ref, w_ref, cos_ref, sin_ref, o_ref, wb_ref = refs
    elif has_conv:
        hx_ref, w_ref, cw_ref, cb_ref, o_ref, wb_ref = refs
    else:
        hx_ref, w_ref, o_ref, wb_ref = refs
    j = pl.program_id(0)
    tm, tn = o_ref.shape

    @pl.when(pl.program_id(1) == 0)
    def _():
        wb_ref[...] = w_ref[...].astype(BF16)

    w_ref = wb_ref

    def plain():
        for c0 in range(0, tn, MXU_COLS):
            o_ref[:, c0:c0 + MXU_COLS] = chunk_dot(c0).astype(o_ref.dtype)

    def chunk_dot(c0):
        pieces = [jnp.dot(hx_ref[r:r + ACC_ROWS, :], w_ref[:, c0:c0 + MXU_COLS],
                          preferred_element_type=F32) for r in range(0, tm, ACC_ROWS)]
        return pieces[0] if len(pieces) == 1 else jnp.concatenate(pieces, axis=0)

    def gated():
        for c0 in range(0, tn, MXU_COLS):
            half = 0.5 * chunk_dot(c0)
            o_ref[:, c0:c0 + MXU_COLS] = (half + half * jnp.tanh(half)).astype(o_ref.dtype)

    def short_conv():
        for c0 in range(0, tn, MXU_COLS):
            acc = chunk_dot(c0)
            for h in range(MXU_COLS // LANES):
                lo = c0 + h * LANES
                u = acc[:, h * LANES:(h + 1) * LANES]
                w3 = cw_ref[:, lo:lo + LANES]
                b = cb_ref[:, lo:lo + LANES]
                y = pltpu.roll(u, 1, 0) * w3[0:1] + u * w3[1:2] + pltpu.roll(u, tm - 1, 0) * w3[2:3] + b
                o_ref[:, lo:lo + LANES] = y.astype(o_ref.dtype)
                first = u[0:1] * w3[1:2] + u[1:2] * w3[2:3] + b
                last = u[tm - 2:tm - 1] * w3[0:1] + u[tm - 1:tm] * w3[1:2] + b
                o_ref[0:1, lo:lo + LANES] = first.astype(o_ref.dtype)
                o_ref[tm - 1:tm, lo:lo + LANES] = last.astype(o_ref.dtype)

    def roped(types):
        cos = cos_ref[...]
        sin = sin_ref[...]
        for c0 in range(0, tn, MXU_COLS):
            acc = chunk_dot(c0)
            for h in range(MXU_COLS // LANES):
                ty = types[c0 // LANES + h]
                slab = acc[:, h * LANES:(h + 1) * LANES]
                if ty is not None:
                    slab = _rope_slab(slab, cos, sin)
                    if ty == "q":
                        slab = slab * q_scale
                lo = c0 + h * LANES
                o_ref[:, lo:lo + LANES] = slab.astype(o_ref.dtype)

    if tile_types is None:
        plain()
        return

    groups = {}
    for t, types in enumerate(tile_types):
        groups.setdefault(tuple(types), []).append(t)
    for types, tiles in groups.items():
        cond = functools.reduce(jnp.logical_or, [j == t for t in tiles])
        if all(ty is None for ty in types):
            pl.when(cond)(plain)
        elif all(ty == "g" for ty in types):
            pl.when(cond)(gated)
        elif all(ty == "c" for ty in types):
            pl.when(cond)(short_conv)
        else:
            assert "g" not in types and "c" not in types
            pl.when(cond)(functools.partial(roped, types))


def _proj(hx, w, col_blk0, n, *, tm, tn, tile_types=None, rope=None, conv=None, q_scale=1.0):
    m, d = hx.shape
    in_specs = [
        pl.BlockSpec((tm, d), lambda j, i: (i, 0)),
        pl.BlockSpec((d, tn), lambda j, i: (0, col_blk0 + j)),
    ]
    args = [hx, w]
    scratch = [pltpu.VMEM((d, tn), BF16)]
    if rope is not None:
        cos, sin = rope
        seq_tiles = cos.shape[0] // tm
        in_specs += [pl.BlockSpec((tm, LANES), lambda j, i: (i % seq_tiles, 0))] * 2
        args += [cos, sin]
    if conv is not None:
        assert rope is None and tm == SEQ
        cw, cb = conv
        last = cw.shape[1] // tn - 1
        in_specs += [pl.BlockSpec((cw.shape[0], tn), lambda j, i: (0, jnp.minimum(j, last))),
                     pl.BlockSpec((1, tn), lambda j, i: (0, jnp.minimum(j, last)))]
        args += [cw, cb]
    return pl.pallas_call(
        functools.partial(_proj_kernel, tile_types=tile_types, has_rope=rope is not None,
                          has_conv=conv is not None, q_scale=q_scale),
        grid=(n // tn, m // tm),
        in_specs=in_specs,
        out_specs=pl.BlockSpec((tm, tn), lambda j, i: (i, j)),
        out_shape=jax.ShapeDtypeStruct((m, n), BF16),
        scratch_shapes=scratch,
        compiler_params=_cparams(("arbitrary", "arbitrary")),
        name="proj",
    )(*args)


def _attn_kernel(sink_ref, q_ref, k_ref, v_ref, g_ref, kc_ref, vc_ref, o_ref,
                 vt_ref, vct_ref, bias_ref):
    kh = pl.program_id(1)
    band = 3 * BLOCK
    cols = GQA_GROUP * BLOCK
    n_blocks = SEQ // BLOCK

    vt_ref[:HEAD_DIM, :] = v_ref[...].T
    vct_ref[:HEAD_DIM, :] = vc_ref[...].T

    @pl.when((pl.program_id(0) == 0) & (kh == 0))
    def _():
        vt_ref[HEAD_DIM:, :] = jnp.ones((ONES_ROWS, SEQ), BF16)
        vct_ref[HEAD_DIM:, :] = jnp.ones((ONES_ROWS, CTX_LEN), BF16)
        krow = lax.broadcasted_iota(jnp.int32, (band, cols), 0)
        qcol = lax.broadcasted_iota(jnp.int32, (band, cols), 1) % BLOCK
        for idx, off in enumerate((0, -BLOCK, -2 * BLOCK)):
            bias_ref[idx] = jnp.where(jnp.abs(krow - qcol + off) <= WINDOW, 0.0, NEG_INF)

    kc = kc_ref[...]
    vct = vct_ref[...]
    sink_row = jnp.concatenate(
        [jnp.full((1, BLOCK), sink_ref[kh * GQA_GROUP + h] * LOG2E, F32) for h in range(GQA_GROUP)],
        axis=1)

    def body(n, carry):
        q0 = pl.multiple_of(n * BLOCK, BLOCK)
        ks = pl.multiple_of(jnp.clip((n - 1) * BLOCK, 0, SEQ - band), BLOCK)
        bidx = jnp.where(n == 0, 0, jnp.where(n == n_blocks - 1, 2, 1))
        qs = q_ref[pl.ds(q0, BLOCK), :]
        q4t = jnp.concatenate([qs[:, h * LANES:(h + 1) * LANES].T for h in range(GQA_GROUP)],
                              axis=1)
        kb = k_ref[pl.ds(ks, band), :]
        s_loc = jnp.dot(kb, q4t, preferred_element_type=F32) + bias_ref[bidx]
        s_ctx = jnp.dot(kc, q4t, preferred_element_type=F32)
        m = jnp.maximum(jnp.maximum(jnp.max(s_loc, axis=0, keepdims=True),
                                    jnp.max(s_ctx, axis=0, keepdims=True)), sink_row)
        p_loc = jnp.exp2(s_loc - m).astype(BF16)
        p_ctx = jnp.exp2(s_ctx - m).astype(BF16)
        ox = (jnp.dot(vt_ref[:, pl.ds(ks, band)], p_loc, preferred_element_type=F32)
              + jnp.dot(vct, p_ctx, preferred_element_type=F32))
        den = ox[HEAD_DIM:HEAD_DIM + 1, :] + jnp.exp2(sink_row - m)
        ot = ox[:HEAD_DIM, :] * (1.0 / den)
        gs = g_ref[pl.ds(q0, BLOCK), :].astype(F32)
        for h in range(GQA_GROUP):
            oh = ot[:, h * LANES:(h + 1) * LANES].T * gs[:, h * LANES:(h + 1) * LANES]
            o_ref[pl.ds(q0, BLOCK), h * LANES:(h + 1) * LANES] = oh.astype(o_ref.dtype)
        return carry

    lax.fori_loop(0, n_blocks, body, 0, unroll=8)


def _attention(px, ckv, sink, batch):
    gw = GQA_GROUP * HEAD_DIM
    k_blk0 = ATTN_WIDTH // HEAD_DIM
    v_blk0 = (ATTN_WIDTH + KV_WIDTH) // HEAD_DIM
    g_blk0 = (ATTN_WIDTH + 2 * KV_WIDTH) // gw
    return pl.pallas_call(
        _attn_kernel,
        grid=(batch, N_KV_HEADS),
        in_specs=[
            pl.BlockSpec(memory_space=pltpu.SMEM),
            pl.BlockSpec((SEQ, gw), lambda b, h: (b, h)),
            pl.BlockSpec((SEQ, HEAD_DIM), lambda b, h: (b, k_blk0 + h)),
            pl.BlockSpec((SEQ, HEAD_DIM), lambda b, h: (b, v_blk0 + h)),
            pl.BlockSpec((SEQ, gw), lambda b, h: (b, g_blk0 + h)),
            pl.BlockSpec((CTX_LEN, HEAD_DIM), lambda b, h: (b, h)),
            pl.BlockSpec((CTX_LEN, HEAD_DIM), lambda b, h: (b, N_KV_HEADS + h)),
        ],
        out_specs=pl.BlockSpec((SEQ, gw), lambda b, h: (b, h)),
        out_shape=jax.ShapeDtypeStruct((batch * SEQ, ATTN_WIDTH), BF16),
        scratch_shapes=[pltpu.VMEM((HEAD_DIM + ONES_ROWS, SEQ), BF16),
                        pltpu.VMEM((HEAD_DIM + ONES_ROWS, CTX_LEN), BF16),
                        pltpu.VMEM((3, 3 * BLOCK, GQA_GROUP * BLOCK), F32)],
        compiler_params=_cparams(("arbitrary", "arbitrary")),
        name="banded_attention",
    )(sink, px, px, px, px, ckv, ckv)


def _out_proj_kernel(*refs, final):
    if final:
        a_ref, w_ref, x_ref, gate_ref, fg_ref, o_ref = refs
    else:
        a_ref, w_ref, x_ref, gate_ref, ng_ref, sh_ref, sc_ref, o_ref, hx_ref = refs
    acc = jnp.dot(a_ref[...], w_ref[...], preferred_element_type=F32)
    y = x_ref[...] + gate_ref[0] * acc
    if final:
        ms = jnp.mean(y * y, axis=-1, keepdims=True)
        y = y * lax.rsqrt(ms + NORM_EPS) * fg_ref[...]
    else:
        gain = ng_ref[...] * (1.0 + sc_ref[0])
        hx_ref[...] = _norm_mod_rows(y, gain, sh_ref[0]).astype(hx_ref.dtype)
    o_ref[...] = y


def _out_proj(a, w, x, mods, layer, *, final_g=None, next_norm_g=None, tm=512):
    m, d = x.shape
    kdim = a.shape[1]
    final = final_g is not None
    row_tile = pl.BlockSpec((tm, d), lambda i: (i, 0))
    vec = pl.BlockSpec((1, d), lambda i: (0, 0))
    in_specs = [
        pl.BlockSpec((tm, kdim), lambda i: (i, 0)),
        pl.BlockSpec((kdim, d), lambda i: (0, 0)),
        row_tile,
        _mod_spec(_mod_row(layer, 0, 2), tm, SEQ, d),
    ]
    args = [a, w, x, mods]
    out_specs = [row_tile]
    out_shape = [jax.ShapeDtypeStruct((m, d), F32)]
    if final:
        in_specs.append(vec)
        args.append(final_g.reshape(1, d))
    else:
        in_specs += [vec, _mod_spec(_mod_row(layer + 1, 0, 0), tm, SEQ, d),
                     _mod_spec(_mod_row(layer + 1, 0, 1), tm, SEQ, d)]
        args += [next_norm_g.reshape(1, d), mods, mods]
        out_specs.append(row_tile)
        out_shape.append(jax.ShapeDtypeStruct((m, d), BF16))
    return pl.pallas_call(
        functools.partial(_out_proj_kernel, final=final),
        grid=(m // tm,),
        in_specs=in_specs,
        out_specs=out_specs,
        out_shape=out_shape,
        compiler_params=_cparams(("arbitrary",)),
        name="out_proj",
    )(*args)


HALF = SEQ // 2
QUARTER = SEQ // 4
FLIP_BLOCK = 256


def _fold_tables():
    def tab(f, n):
        t2 = 2 * np.arange(n, dtype=np.int64) + 1
        ang = ((f[:, None] * t2[None, :]) % (2 * FFT_N)).astype(np.float64) * (math.pi / FFT_N)
        return np.cos(ang), np.sin(ang)

    r = np.arange(QUARTER, dtype=np.int64)
    g = np.arange(HALF, dtype=np.int64)
    f_ee, f_eo, f_o = 4 * r, 4 * r + 2, 2 * g + 1
    cee, see = tab(f_ee, QUARTER)
    ceo, seo = tab(f_eo, QUARTER)
    co, so = tab(f_o, HALF)
    see[0, :] = np.where(np.arange(QUARTER) % 2 == 0, 1.0, -1.0)
    const = lambda a, dt: jnp.asarray(np.ascontiguousarray(a).astype(np.float32)).astype(dt)
    phase = lambda f: np.stack([np.cos(f * (math.pi / FFT_N)), np.sin(f * (math.pi / FFT_N))])
    return dict(
        fq=const(np.stack([cee, see, ceo, seo]), BF16),
        fo=const(np.stack([co, so]), BF16),
        iq=const(np.stack([cee.T, ceo.T, see.T, seo.T]), BF16),
        io=const(np.stack([so.T, co.T]), BF16),
        anti=const(np.eye(FLIP_BLOCK)[::-1], BF16),
        pq=const(np.stack([phase(f_ee), phase(f_eo)], axis=1)[..., None], F32),
        po=const(phase(f_o)[..., None], F32),
    )


def _flip_rows(h, anti):
    nb = h.shape[0] // FLIP_BLOCK
    return jnp.concatenate(
        [jnp.dot(anti[...], h[(nb - 1 - a) * FLIP_BLOCK:(nb - a) * FLIP_BLOCK, :], preferred_element_type=F32)
         for a in range(nb)], axis=0)


def _fold_two_levels(x, anti):
    q0, q1, q2, q3 = (x[k * QUARTER:(k + 1) * QUARTER] for k in range(4))
    f1, f2, f3 = (_flip_rows(q, anti) for q in (q1, q2, q3))
    q0, q1, q2 = (q.astype(F32) for q in (q0, q1, q2))
    xs0, xa0 = q0 + f3, q0 - f3
    fxs1, fxa1 = f1 + q2, f1 - q2
    bf = lambda v: v.astype(BF16)
    xs = jnp.concatenate([bf(xs0), bf(q1 + f2)], axis=0)
    xa = jnp.concatenate([bf(xa0), bf(q1 - f2)], axis=0)
    return xs, xa, bf(xs0 + fxs1), bf(xs0 - fxs1), bf(xa0 + fxa1), bf(xa0 - fxa1)


def _staggered(n, matmuls, finish):
    matmuls(0, 0)
    for t in range(1, n):
        matmuls(t, t % 2)
        finish(t - 1, (t - 1) % 2)
    finish(n - 1, (n - 1) % 2)


def _col_chunks(n):
    return [slice(c, c + MXU_COLS) for c in range(0, n, MXU_COLS)]


PLANES = {"a_ee": 0, "b_ee": QUARTER, "a_eo": 2 * QUARTER, "b_eo": 3 * QUARTER,
          "a_o": 2 * HALF, "b_o": 3 * HALF}
GROUPS = (("a_ee", "b_ee", QUARTER), ("a_eo", "b_eo", QUARTER), ("a_o", "b_o", HALF))


def _spectrum(fq_ref, fo_ref, x, anti):
    xs, xa, xss, xsa, xas, xaa = _fold_two_levels(x, anti)
    dq = lambda k, v: jnp.dot(fq_ref[k], v, preferred_element_type=F32)
    do = lambda k, v: jnp.dot(fo_ref[k], v, preferred_element_type=F32)
    return {"a_ee": dq(0, xss), "b_ee": dq(1, xaa), "a_eo": dq(2, xsa), "b_eo": dq(3, xas),
            "a_o": do(0, xa), "b_o": do(1, xs)}


def _plane(ref, lead, name, rows, cols):
    r0 = PLANES[name]
    return ref[lead, r0:r0 + rows, cols]


def _resident(shape):
    zeros = (0,) * len(shape)
    return pl.BlockSpec(shape, lambda *_: zeros, pipeline_mode=pl.Buffered(1))


def _pack_filter_mlp(w1, b1, w2, b2, freq):
    hpad = LANES - FILTER_HIDDEN
    bands = (FILTER_EMB - 1) // 2
    row = lambda v: jnp.pad(v.reshape(1, -1), ((0, 7), (0, hpad)))
    mat = lambda m: jnp.pad(m, ((0, LANES - m.shape[0]), (0, hpad)))
    return jnp.concatenate([row(w1[0]), row(b1), row(b2), row(freq), jnp.zeros((LANES - 32, LANES), F32),
                            mat(w1[1:1 + bands]), mat(w1[1 + bands:]), mat(w2)], axis=0)


def _filter_sd_body(fp_ref, w3f_ref, b3f_ref, w3b_ref, b3b_ref, d_ref, o_ref, hid_ref, *, first, j):
    length = o_ref.shape[1]
    tn = o_ref.shape[2]
    hp = lax.Precision.HIGHEST
    w1t_ref, b1_ref, b2_ref, fq_ref = (fp_ref.at[r:r + 1] for r in (0, 8, 16, 24))
    w1c_ref, w1s_ref, w2_ref = (fp_ref.at[r:r + LANES] for r in (LANES, 2 * LANES, 3 * LANES))

    @pl.when(first)
    def _():
        n = lax.broadcasted_iota(jnp.int32, (length, LANES), 0).astype(F32)
        lane = lax.broadcasted_iota(jnp.int32, (length, LANES), 1)
        bands = (FILTER_EMB - 1) // 2
        fr_step = (bands - 1 - 1e-4) / (bands - 1)
        fr = jnp.where(lane < bands, 1e-4 + lane.astype(F32) * fr_step, 0.0)
        ang = (2.0 * math.pi * n / length) * fr
        t = n / (length - 1)
        fq = fq_ref[...]
        pre = (t * w1t_ref[...]
               + jnp.dot(jnp.cos(ang), w1c_ref[...], precision=hp, preferred_element_type=F32)
               + jnp.dot(-jnp.sin(ang), w1s_ref[...], precision=hp, preferred_element_type=F32)
               + b1_ref[...])
        hid = jnp.sin(fq * pre)
        hid = jnp.sin(fq * (jnp.dot(hid, w2_ref[...], precision=hp, preferred_element_type=F32)
                            + b2_ref[...]))
        hid_hi = hid.astype(BF16)
        hid_ref[0] = hid_hi
        hid_ref[1] = (hid - hid_hi.astype(F32)).astype(BF16)

    def dot3(w_ref):
        wf = w_ref[...]
        wf = jnp.concatenate([wf, jnp.zeros((LANES - wf.shape[0], wf.shape[1]), F32)], axis=0)
        w_hi = wf.astype(BF16)
        w_lo = (wf - w_hi.astype(F32)).astype(BF16)
        return (jnp.dot(hid_ref[0], w_hi, preferred_element_type=F32)
                + jnp.dot(hid_ref[1], w_hi, preferred_element_type=F32)
                + jnp.dot(hid_ref[0], w_lo, preferred_element_type=F32))

    row = lax.broadcasted_iota(jnp.int32, (length, tn), 0)
    chan = (lax.broadcasted_iota(jnp.int32, (length, tn), 1) + j * tn).astype(F32)
    min_decay = math.log(DECAY_TARGET) / DECAY_SLOW
    max_decay = math.log(DECAY_TARGET) / DECAY_FAST
    delta = min_decay + chan * ((max_decay - min_decay) / (HYENA_WIDTH - 1))
    t = row.astype(F32) / (length - 1)
    window = jnp.exp(-t * jnp.abs(delta)) + WINDOW_SHIFT
    hf = (dot3(w3f_ref) + b3f_ref[...]) * window
    hb = (dot3(w3b_ref) + b3b_ref[...]) * window
    hf = jnp.where(row == 0, hf + d_ref[0], hf)
    hb = jnp.where(row == 0, 0.0, hb)
    o_ref[0] = (hf + hb).astype(o_ref.dtype)
    o_ref[1] = (hf - hb).astype(o_ref.dtype)


def _prep_kernel(c_ref, aw_ref, ab_ref, *refs, filter_steps, ct):
    wa_ref, wh_ref = refs[-7:-5]
    mods_ref, sd_ref, wa_bf_ref, wh_bf_ref, hid_ref = refs[-5:]
    s = pl.program_id(0)
    _mods_kernel(c_ref, aw_ref, ab_ref, mods_ref)

    @pl.when(s < filter_steps)
    def _():
        _filter_sd_body(*refs[:-7], sd_ref, hid_ref, first=s == 0, j=s % ct)

    @pl.when(s < CAST_STEPS)
    def _():
        wa_bf_ref[...] = wa_ref[...].astype(BF16)

    @pl.when((s >= CAST_STEPS) & (s < 2 * CAST_STEPS))
    def _():
        wh_bf_ref[...] = wh_ref[...].astype(BF16)


CAST_STEPS = 8


def _prep(cc, ada_w, ada_b, w1, b1, w2, b2, w3, b3, freq, bias_d, w_out_a, w_out_h, *,
          tn_mods=512, tn=256):
    depth, d, n_mods = ada_w.shape
    mt = n_mods // tn_mods
    w = HYENA_WIDTH
    ct = w // tn
    filter_steps = 2 * ct
    assert filter_steps <= depth * mt
    n = w3.shape[1]
    fp = _pack_filter_mlp(w1, b1, w2, b2, freq)
    b3r = b3.reshape(1, n)
    order = lambda s: jnp.minimum(s, filter_steps - 1) // ct
    chan = lambda s: jnp.minimum(s, filter_steps - 1) % ct
    fwd_cols = lambda s: (0, 2 * order(s) * ct + chan(s))
    bwd_cols = lambda s: (0, (2 * order(s) + 1) * ct + chan(s))
    assert w_out_a.shape == w_out_h.shape and 2 * CAST_STEPS <= depth * mt
    cast_blk = (w_out_a.shape[0] // CAST_STEPS, w_out_a.shape[1])
    first_rows = lambda s: (jnp.minimum(s, CAST_STEPS - 1), 0)
    second_rows = lambda s: (jnp.clip(s - CAST_STEPS, 0, CAST_STEPS - 1), 0)
    return pl.pallas_call(
        functools.partial(_prep_kernel, filter_steps=filter_steps, ct=ct),
        grid=(depth * mt,),
        in_specs=[pl.BlockSpec((MOD_ROWS, d), lambda s: (0, 0)),
                  pl.BlockSpec((1, d, tn_mods), lambda s: (s // mt, 0, s % mt)),
                  pl.BlockSpec((1, 1, tn_mods), lambda s: (s // mt, 0, s % mt)),
                  pl.BlockSpec(fp.shape, lambda s: (0, 0)),
                  pl.BlockSpec((FILTER_HIDDEN, tn), fwd_cols), pl.BlockSpec((1, tn), fwd_cols),
                  pl.BlockSpec((FILTER_HIDDEN, tn), bwd_cols), pl.BlockSpec((1, tn), bwd_cols),
                  pl.BlockSpec((1, 1, tn), lambda s: (order(s), 0, chan(s))),
                  pl.BlockSpec(cast_blk, first_rows), pl.BlockSpec(cast_blk, second_rows)],
        out_specs=[pl.BlockSpec((1, MOD_ROWS, tn_mods), lambda s: (s // mt, 0, s % mt)),
                   pl.BlockSpec((2, SEQ, tn), lambda s: (0, 0, order(s) * ct + chan(s))),
                   pl.BlockSpec(cast_blk, first_rows), pl.BlockSpec(cast_blk, second_rows)],
        out_shape=[jax.ShapeDtypeStruct((depth, MOD_ROWS, n_mods), F32),
                   jax.ShapeDtypeStruct((2, SEQ, 2 * w), BF16),
                   jax.ShapeDtypeStruct(w_out_a.shape, BF16),
                   jax.ShapeDtypeStruct(w_out_h.shape, BF16)],
        scratch_shapes=[pltpu.VMEM((2, SEQ, LANES), BF16)],
        compiler_params=_cparams(("arbitrary",)),
        name="mods_and_filters",
    )(cc, ada_w, ada_b.reshape(depth, 1, n_mods),
      fp, w3, b3r, w3, b3r, bias_d.reshape(2, 1, w),
      w_out_a, w_out_h)


def _filter_spec_kernel(s_ref, d_ref, fq_ref, fo_ref, anti_ref, pq_ref, po_ref, k_ref):
    phases = {"a_ee": (pq_ref[0, 0], pq_ref[1, 0]), "a_eo": (pq_ref[0, 1], pq_ref[1, 1]),
              "a_o": (po_ref[0], po_ref[1])}
    is0 = lax.broadcasted_iota(jnp.int32, (QUARTER, MXU_COLS), 0) == 0
    for cols in _col_chunks(k_ref.shape[2]):
        sp = _spectrum(fq_ref, fo_ref, s_ref[0, :, cols], anti_ref)
        dp = _spectrum(fq_ref, fo_ref, d_ref[0, :, cols], anti_ref)
        for a, b, rows in GROUPS:
            cos, sin = phases[a]
            ka = sp[a] * cos + sp[b] * sin
            kb = dp[b] * cos - dp[a] * sin
            if a == "a_ee":
                scale = jnp.where(is0, 1.0 / FFT_N, 2.0 / FFT_N)
                kb = jnp.where(is0, sp[b], kb)
            else:
                scale = 2.0 / FFT_N
            k_ref[0, PLANES[a]:PLANES[a] + rows, cols] = (ka * scale).astype(k_ref.dtype)
            k_ref[0, PLANES[b]:PLANES[b] + rows, cols] = (kb * scale).astype(k_ref.dtype)


def _filter_spectra(sd, tabs, *, tn=512):
    w = HYENA_WIDTH
    ct = w // tn
    consts = [tabs[k] for k in ("fq", "fo", "anti", "pq", "po")]
    return pl.pallas_call(
        _filter_spec_kernel,
        grid=(2, ct),
        in_specs=[
            pl.BlockSpec((1, SEQ, tn), lambda o, j: (0, 0, o * ct + j)),
            pl.BlockSpec((1, SEQ, tn), lambda o, j: (1, 0, o * ct + j)),
        ] + [_resident(c.shape) for c in consts],
        out_specs=pl.BlockSpec((1, 4 * HALF, tn), lambda o, j: (o, 0, j)),
        out_shape=jax.ShapeDtypeStruct((2, 4 * HALF, w), F32),
        compiler_params=_cparams(("arbitrary", "arbitrary")),
        name="filter_spectra",
    )(sd, sd, *consts)


def _conv_fwd_kernel(x_ref, fq_ref, fo_ref, anti_ref, k_ref, y_ref):
    is0 = lax.broadcasted_iota(jnp.int32, (QUARTER, MXU_COLS), 0) == 0
    for cols in _col_chunks(y_ref.shape[2]):
        sp = _spectrum(fq_ref, fo_ref, x_ref[:, cols], anti_ref)
        for a, b, rows in GROUPS:
            ka = _plane(k_ref, 0, a, rows, cols).astype(F32)
            kb = _plane(k_ref, 0, b, rows, cols).astype(F32)
            bkb = sp[b] * kb
            if a == "a_ee":
                ya = sp[a] * ka - jnp.where(is0, 0.0, bkb)
                yb = jnp.where(is0, bkb, sp[a] * kb + sp[b] * ka)
            else:
                ya = sp[a] * ka - bkb
                yb = sp[a] * kb + sp[b] * ka
            y_ref[0, PLANES[a]:PLANES[a] + rows, cols] = ya.astype(y_ref.dtype)
            y_ref[0, PLANES[b]:PLANES[b] + rows, cols] = yb.astype(y_ref.dtype)


def _conv_fwd(x, x_blk0, kspec, order, tabs, batch, *, tn=512):
    w = HYENA_WIDTH
    ct = w // tn
    consts = [tabs[k] for k in ("fq", "fo", "anti")]
    return pl.pallas_call(
        _conv_fwd_kernel,
        grid=(ct, batch),
        in_specs=[pl.BlockSpec((SEQ, tn), lambda j, b: (b, x_blk0 * ct + j))]
        + [_resident(c.shape) for c in consts]
        + [pl.BlockSpec((1, 4 * HALF, tn), lambda j, b: (order, 0, j))],
        out_specs=pl.BlockSpec((1, 4 * HALF, tn), lambda j, b: (b, 0, j)),
        out_shape=jax.ShapeDtypeStruct((batch, 4 * HALF, w), BF16),
        compiler_params=_cparams(("arbitrary", "arbitrary")),
        name="long_conv_fwd",
    )(x, *consts, kspec)


def _conv_inv_kernel(*refs, gated):
    if gated:
        y_ref, iq_ref, io_ref, anti_ref, xm_ref, g_ref, o_ref, accq_ref, acco_ref = refs
    else:
        y_ref, iq_ref, io_ref, anti_ref, xm_ref, o_ref, accq_ref, acco_ref = refs
    chunks = _col_chunks(o_ref.shape[1])
    flipped = lambda v: _flip_rows(v.astype(BF16), anti_ref)

    def matmuls(t, slot):
        cols = chunks[t]
        for k, name in enumerate(("a_ee", "a_eo", "b_ee", "b_eo")):
            accq_ref[slot, k] = jnp.dot(iq_ref[k], _plane(y_ref, 0, name, QUARTER, cols),
                                        preferred_element_type=F32)
        for k, name in enumerate(("b_o", "a_o")):
            acco_ref[slot, k] = jnp.dot(io_ref[k], _plane(y_ref, 0, name, HALF, cols),
                                        preferred_element_type=F32)

    def finish(t, slot):
        cols = chunks[t]
        p, r, q, s = (accq_ref[slot, k] for k in range(4))
        osum = acco_ref[slot, 0] + acco_ref[slot, 1]
        odif = acco_ref[slot, 0] - acco_ref[slot, 1]
        e1, e2, d1, d2 = p + r, q + s, p - r, s - q
        quarters = (e1 + e2 + osum[:QUARTER],
                    flipped(d1 + d2) + osum[QUARTER:],
                    d1 - d2 + flipped(odif[QUARTER:]),
                    flipped(e1 - e2 + odif[:QUARTER]))
        for k, y in enumerate(quarters):
            rows = slice(k * QUARTER, (k + 1) * QUARTER)
            out = y * xm_ref[rows, cols].astype(F32)
            if gated:
                out = out * g_ref[rows, cols].astype(F32)
            o_ref[rows, cols] = out.astype(o_ref.dtype)

    _staggered(len(chunks), matmuls, finish)


def _conv_inv(y, tabs, p, m_blk0, g_blk0=None, *, tn=512):
    batch = y.shape[0]
    w = HYENA_WIDTH
    ct = w // tn
    gated = g_blk0 is not None
    col = lambda blk0: (lambda b, j: (b, blk0 * ct + j))
    consts = [tabs[k] for k in ("iq", "io", "anti")]
    in_specs = ([pl.BlockSpec((1, 4 * HALF, tn), lambda b, j: (b, 0, j))]
                + [_resident(c.shape) for c in consts]
                + [pl.BlockSpec((SEQ, tn), col(m_blk0))])
    args = [y, *consts, p]
    if gated:
        in_specs.append(pl.BlockSpec((SEQ, tn), col(g_blk0)))
        args.append(p)
    return pl.pallas_call(
        functools.partial(_conv_inv_kernel, gated=gated),
        grid=(batch, ct),
        in_specs=in_specs,
        out_specs=pl.BlockSpec((SEQ, tn), lambda b, j: (b, j)),
        out_shape=jax.ShapeDtypeStruct((batch * SEQ, w), BF16),
        scratch_shapes=[pltpu.VMEM((2, 4, QUARTER, MXU_COLS), F32),
                        pltpu.VMEM((2, 2, HALF, MXU_COLS), F32)],
        compiler_params=_cparams(("arbitrary", "arbitrary")),
        name="long_conv_inv",
    )(*args)


def _rope_tables():
    pos = np.arange(SEQ)
    row = (pos // GRID_W).astype(np.float32)
    col = (pos % GRID_W).astype(np.float32)
    half = HEAD_DIM // 2
    inv = (ROPE_BASE ** (-np.arange(0, half, 2, dtype=np.float32) / half)).astype(np.float32)
    ar = row[:, None] * inv[None]
    ac = col[:, None] * inv[None]
    cos = np.concatenate([np.cos(ar), np.cos(ar), np.cos(ac), np.cos(ac)], axis=1)
    sin = np.concatenate([-np.sin(ar), np.sin(ar), -np.sin(ac), np.sin(ac)], axis=1)
    return jnp.asarray(cos, F32), jnp.asarray(sin, F32)


def _tile_types(tn, kinds):
    per_tile = tn // LANES
    return [kinds[s0:s0 + per_tile] for s0 in range(0, len(kinds), per_tile)]


def kernel(x, c, ctx, c_ctx, norm_g, ada_w, ada_b, attn_w_in, attn_w_out, attn_sink, hy_w_in,
           hy_conv_w, hy_conv_b, hy_w1, hy_b1, hy_w2, hy_b2, hy_w3, hy_b3, hy_freq, hy_bias_d,
           hy_w_out, final_g):
    batch, seq, d = x.shape
    assert (seq, d) == (SEQ, D_MODEL) and ctx.shape[1] == CTX_LEN
    assert norm_g.shape[0] == 2 and attn_w_in.shape[0] == 1 and hy_w_in.shape[0] == 1
    w = HYENA_WIDTH

    ctx_row = batch
    cc = jnp.concatenate([c, c_ctx[None], jnp.zeros((MOD_ROWS - batch - 1, d), F32)], axis=0)
    mods, sd, w_out_attn, w_out_hyena = _prep(
        cc, ada_w, ada_b, hy_w1[0], hy_b1[0], hy_w2[0], hy_b2[0], hy_w3[0], hy_b3[0], hy_freq[0],
        hy_bias_d[0], attn_w_out[0], hy_w_out[0])
    mods = mods.reshape(-1, 1, d)

    x2 = x.reshape(batch * seq, d)
    ctx2 = ctx.reshape(batch * CTX_LEN, d)

    tn = 1024
    kinds0 = (["q"] * N_HEADS + ["k"] * N_KV_HEADS + [None] * N_KV_HEADS
              + ["g"] * (ATTN_WIDTH // LANES))
    hx = _norm_mod(x2, norm_g[0], mods, 0, tm=1024)
    px = _proj(hx, attn_w_in[0], 0, ATTN_IN, tm=SEQ, tn=tn, tile_types=_tile_types(tn, kinds0),
               rope=_rope_tables(), q_scale=HEAD_DIM ** -0.5 * LOG2E)
    ckv = _ctx_kv(ctx2, norm_g[0], mods, 0, ctx_row, attn_w_in[0], ATTN_WIDTH, 2 * KV_WIDTH)
    og = _attention(px, ckv, attn_sink[0], batch)
    x2, hx = _out_proj(og, w_out_attn, x2, mods, 0, next_norm_g=norm_g[1])

    kinds1 = ["c"] * (3 * w // LANES) + ["g"] * (w // LANES)
    tn1 = 1024
    p = _proj(hx, hy_w_in[0], 0, 4 * w, tm=SEQ, tn=tn1, tile_types=_tile_types(tn1, kinds1),
              conv=(hy_conv_w[0], hy_conv_b[0].reshape(1, 3 * w)))
    tabs = _fold_tables()
    kspec = _filter_spectra(sd, tabs)
    y1 = _conv_fwd(p, 2, kspec, 0, tabs, batch)
    z = _conv_inv(y1, tabs, p, 0)
    y2 = _conv_fwd(z, 0, kspec, 1, tabs, batch)
    yg = _conv_inv(y2, tabs, p, 1, g_blk0=3)
    out, = _out_proj(yg, w_out_hyena, x2, mods, 1, final_g=final_g)
    return out.reshape(batch, seq, d)
```

```python
import functools
import math

import jax
import jax.numpy as jnp
import numpy as np
from jax import lax
from jax.experimental import pallas as pl
from jax.experimental.pallas import tpu as pltpu

F32 = jnp.float32
BF16 = jnp.bfloat16

D_MODEL = 2048
SEQ = 2048
CTX_LEN = 256
GRID_W = 64
HEAD_DIM = 128
N_HEADS = 16
N_KV_HEADS = 4
GQA_GROUP = 4
ATTN_WIDTH = 2048
KV_WIDTH = 512
ATTN_IN = 2 * ATTN_WIDTH + 2 * KV_WIDTH
WINDOW = 128
BLOCK = 128
ROPE_BASE = 10000.0
HYENA_WIDTH = 2048
FILTER_EMB = 33
FILTER_HIDDEN = 64
DECAY_FAST = 0.3
DECAY_SLOW = 1.5
DECAY_TARGET = 1e-2
WINDOW_SHIFT = 0.05
NORM_EPS = 1e-6
NEG_INF = -1e30

LANES = 128
MOD_ROWS = 8
MXU_COLS = 256
ACC_ROWS = 1024
ONES_ROWS = 16
LOG2E = math.log2(math.e)
FFT_N = 2 * SEQ
VMEM_LIMIT = 56 * 1024 * 1024


def _cparams(sem):
    return pltpu.CompilerParams(dimension_semantics=sem, vmem_limit_bytes=VMEM_LIMIT)


def _mods_kernel(c_ref, w_ref, b_ref, o_ref):
    c = c_ref[...]
    s = c * jax.nn.sigmoid(c)
    s_hi = s.astype(BF16)
    s_lo = (s - s_hi.astype(F32)).astype(BF16)
    lhs = jnp.concatenate([s_hi, s_lo], axis=0)
    r = jnp.dot(lhs, w_ref[0].astype(BF16), preferred_element_type=F32)
    o_ref[0] = r[:MOD_ROWS] + r[MOD_ROWS:] + b_ref[0]


def _rope_slab(t, cos, sin):
    lane = lax.broadcasted_iota(jnp.int32, t.shape, 1)
    first = (lane % 64) < 32
    partner = jnp.where(first, pltpu.roll(t, 96, 1), pltpu.roll(t, 32, 1))
    return t * cos + partner * sin


def _norm_mod_rows(x, gain, add):
    ms = jnp.mean(x * x, axis=-1, keepdims=True)
    return x * lax.rsqrt(ms + NORM_EPS) * gain + add


def _norm_mod_kernel(x_ref, g_ref, sh_ref, sc_ref, o_ref, *, row_chunk):
    gain = g_ref[...] * (1.0 + sc_ref[0])
    add = sh_ref[0]
    for r in range(0, x_ref.shape[0], row_chunk):
        o_ref[r:r + row_chunk, :] = _norm_mod_rows(x_ref[r:r + row_chunk, :], gain, add).astype(o_ref.dtype)


def _mod_row(layer, row, kind):
    return (layer * MOD_ROWS + row) * 3 + kind


def _mod_spec(base, tm, rows_per_mod, d):
    return pl.BlockSpec((1, 1, d), lambda i: (base + 3 * ((i * tm) // rows_per_mod), 0, 0))


def _norm_mod(x, g, mods, layer, *, tm):
    m, d = x.shape
    return pl.pallas_call(
        functools.partial(_norm_mod_kernel, row_chunk=min(tm, 256)),
        grid=(m // tm,),
        in_specs=[pl.BlockSpec((tm, d), lambda i: (i, 0)), pl.BlockSpec((1, d), lambda i: (0, 0)),
                  _mod_spec(_mod_row(layer, 0, 0), tm, SEQ, d),
                  _mod_spec(_mod_row(layer, 0, 1), tm, SEQ, d)],
        out_specs=pl.BlockSpec((tm, d), lambda i: (i, 0)),
        out_shape=jax.ShapeDtypeStruct((m, d), BF16),
        compiler_params=_cparams(("arbitrary",)),
        name="norm_mod",
    )(x, g.reshape(1, d), mods, mods)


def _ctx_kv_kernel(x_ref, g_ref, sh_ref, sc_ref, w_ref, o_ref, hx_ref, *, row_chunk):
    gain = g_ref[...] * (1.0 + sc_ref[0])
    add = sh_ref[0]
    for r in range(0, x_ref.shape[0], row_chunk):
        hx_ref[r:r + row_chunk, :] = _norm_mod_rows(x_ref[r:r + row_chunk, :], gain, add).astype(BF16)
    wb = w_ref[...].astype(BF16)
    for c0 in range(0, o_ref.shape[1], MXU_COLS):
        o_ref[:, c0:c0 + MXU_COLS] = jnp.dot(hx_ref[...], wb[:, c0:c0 + MXU_COLS],
                                             preferred_element_type=F32).astype(o_ref.dtype)


def _ctx_kv(ctx, g, mods, layer, mod_row, w, col0, n):
    m, d = ctx.shape
    assert col0 % n == 0 and m <= ACC_ROWS
    shift = _mod_row(layer, mod_row, 0)
    scale = _mod_row(layer, mod_row, 1)
    return pl.pallas_call(
        functools.partial(_ctx_kv_kernel, row_chunk=256),
        grid=(1,),
        in_specs=[pl.BlockSpec((m, d), lambda i: (0, 0)), pl.BlockSpec((1, d), lambda i: (0, 0)),
                  pl.BlockSpec((1, 1, d), lambda i: (shift, 0, 0)),
                  pl.BlockSpec((1, 1, d), lambda i: (scale, 0, 0)),
                  pl.BlockSpec((d, n), lambda i: (0, col0 // n))],
        out_specs=pl.BlockSpec((m, n), lambda i: (0, 0)),
        out_shape=jax.ShapeDtypeStruct((m, n), BF16),
        scratch_shapes=[pltpu.VMEM((m, d), BF16)],
        compiler_params=_cparams(("arbitrary",)),
        name="ctx_kv",
    )(ctx, g.reshape(1, d), mods, mods, w)


def _proj_kernel(*refs, tile_types, has_rope, has_conv, q_scale):
    if has_rope:
        hx_ref, w_ref, cos_ref, sin_ref, o_ref, wb_ref = refs
    elif has_conv:
        hx_ref, w_ref, cw_ref, cb_ref, o_ref, wb_ref = refs
    else:
        hx_ref, w_ref, o_ref, wb_ref = refs
    j = pl.program_id(0)
    tm, tn = o_ref.shape

    @pl.when(pl.program_id(1) == 0)
    def _():
        wb_ref[...] = w_ref[...].astype(BF16)

    w_ref = wb_ref

    def plain():
        for c0 in range(0, tn, MXU_COLS):
            o_ref[:, c0:c0 + MXU_COLS] = chunk_dot(c0).astype(o_ref.dtype)

    def chunk_dot(c0):
        pieces = [jnp.dot(hx_ref[r:r + ACC_ROWS, :], w_ref[:, c0:c0 + MXU_COLS],
                          preferred_element_type=F32) for r in range(0, tm, ACC_ROWS)]
        return pieces[0] if len(pieces) == 1 else jnp.concatenate(pieces, axis=0)

    def gated():
        for c0 in range(0, tn, MXU_COLS):
            half = 0.5 * chunk_dot(c0)
            o_ref[:, c0:c0 + MXU_COLS] = (half + half * jnp.tanh(half)).astype(o_ref.dtype)

    def short_conv():
        for c0 in range(0, tn, MXU_COLS):
            acc = chunk_dot(c0)
            for h in range(MXU_COLS // LANES):
                lo = c0 + h * LANES
                u = acc[:, h * LANES:(h + 1) * LANES]
                w3 = cw_ref[:, lo:lo + LANES]
                b = cb_ref[:, lo:lo + LANES]
                y = pltpu.roll(u, 1, 0) * w3[0:1] + u * w3[1:2] + pltpu.roll(u, tm - 1, 0) * w3[2:3] + b
                o_ref[:, lo:lo + LANES] = y.astype(o_ref.dtype)
                first = u[0:1] * w3[1:2] + u[1:2] * w3[2:3] + b
                last = u[tm - 2:tm - 1] * w3[0:1] + u[tm - 1:tm] * w3[1:2] + b
                o_ref[0:1, lo:lo + LANES] = first.astype(o_ref.dtype)
                o_ref[tm - 1:tm, lo:lo + LANES] = last.astype(o_ref.dtype)

    def roped(types):
        cos = cos_ref[...]
        sin = sin_ref[...]
        for c0 in range(0, tn, MXU_COLS):
            acc = chunk_dot(c0)
            for h in range(MXU_COLS // LANES):
                ty = types[c0 // LANES + h]
                slab = acc[:, h * LANES:(h + 1) * LANES]
                if ty is not None:
                    slab = _rope_slab(slab, cos, sin)
                    if ty == "q":
                        slab = slab * q_scale
                lo = c0 + h * LANES
                o_ref[:, lo:lo + LANES] = slab.astype(o_ref.dtype)

    if tile_types is None:
        plain()
        return

    groups = {}
    for t, types in enumerate(tile_types):
        groups.setdefault(tuple(types), []).append(t)
    for types, tiles in groups.items():
        cond = functools.reduce(jnp.logical_or, [j == t for t in tiles])
        if all(ty is None for ty in types):
            pl.when(cond)(plain)
        elif all(ty == "g" for ty in types):
            pl.when(cond)(gated)
        elif all(ty == "c" for ty in types):
            pl.when(cond)(short_conv)
        else:
            assert "g" not in types and "c" not in types
            pl.when(cond)(functools.partial(roped, types))


def _proj(hx, w, col_blk0, n, *, tm, tn, tile_types=None, rope=None, conv=None, q_scale=1.0):
    m, d = hx.shape
    in_specs = [
        pl.BlockSpec((tm, d), lambda j, i: (i, 0)),
        pl.BlockSpec((d, tn), lambda j, i: (0, col_blk0 + j)),
    ]
    args = [hx, w]
    scratch = [pltpu.VMEM((d, tn), BF16)]
    if rope is not None:
        cos, sin = rope
        seq_tiles = cos.shape[0] // tm
        in_specs += [pl.BlockSpec((tm, LANES), lambda j, i: (i % seq_tiles, 0))] * 2
        args += [cos, sin]
    if conv is not None:
        assert rope is None and tm == SEQ
        cw, cb = conv
        last = cw.shape[1] // tn - 1
        in_specs += [pl.BlockSpec((cw.shape[0], tn), lambda j, i: (0, jnp.minimum(j, last))),
                     pl.BlockSpec((1, tn), lambda j, i: (0, jnp.minimum(j, last)))]
        args += [cw, cb]
    return pl.pallas_call(
        functools.partial(_proj_kernel, tile_types=tile_types, has_rope=rope is not None,
                          has_conv=conv is not None, q_scale=q_scale),
        grid=(n // tn, m // tm),
        in_specs=in_specs,
        out_specs=pl.BlockSpec((tm, tn), lambda j, i: (i, j)),
        out_shape=jax.ShapeDtypeStruct((m, n), BF16),
        scratch_shapes=scratch,
        compiler_params=_cparams(("arbitrary", "arbitrary")),
        name="proj",
    )(*args)


def _attn_kernel(sink_ref, q_ref, k_ref, v_ref, g_ref, kc_ref, vc_ref, o_ref,
                 vt_ref, vct_ref, bias_ref):
    kh = pl.program_id(1)
    band = 3 * BLOCK
    cols = GQA_GROUP * BLOCK
    n_blocks = SEQ // BLOCK

    vt_ref[:HEAD_DIM, :] = v_ref[...].T
    vct_ref[:HEAD_DIM, :] = vc_ref[...].T

    @pl.when((pl.program_id(0) == 0) & (kh == 0))
    def _():
        vt_ref[HEAD_DIM:, :] = jnp.ones((ONES_ROWS, SEQ), BF16)
        vct_ref[HEAD_DIM:, :] = jnp.ones((ONES_ROWS, CTX_LEN), BF16)
        krow = lax.broadcasted_iota(jnp.int32, (band, cols), 0)
        qcol = lax.broadcasted_iota(jnp.int32, (band, cols), 1) % BLOCK
        for idx, off in enumerate((0, -BLOCK, -2 * BLOCK)):
            bias_ref[idx] = jnp.where(jnp.abs(krow - qcol + off) <= WINDOW, 0.0, NEG_INF)

    kc = kc_ref[...]
    vct = vct_ref[...]
    sink_row = jnp.concatenate(
        [jnp.full((1, BLOCK), sink_ref[kh * GQA_GROUP + h] * LOG2E, F32) for h in range(GQA_GROUP)],
        axis=1)

    def body(n, carry):
        q0 = pl.multiple_of(n * BLOCK, BLOCK)
        ks = pl.multiple_of(jnp.clip((n - 1) * BLOCK, 0, SEQ - band), BLOCK)
        bidx = jnp.where(n == 0, 0, jnp.where(n == n_blocks - 1, 2, 1))
        qs = q_ref[pl.ds(q0, BLOCK), :]
        q4t = jnp.concatenate([qs[:, h * LANES:(h + 1) * LANES].T for h in range(GQA_GROUP)],
                              axis=1)
        kb = k_ref[pl.ds(ks, band), :]
        s_loc = jnp.dot(kb, q4t, preferred_element_type=F32) + bias_ref[bidx]
        s_ctx = jnp.dot(kc, q4t, preferred_element_type=F32)
        m = jnp.maximum(jnp.maximum(jnp.max(s_loc, axis=0, keepdims=True),
                                    jnp.max(s_ctx, axis=0, keepdims=True)), sink_row)
        p_loc = jnp.exp2(s_loc - m).astype(BF16)
        p_ctx = jnp.exp2(s_ctx - m).astype(BF16)
        ox = (jnp.dot(vt_ref[:, pl.ds(ks, band)], p_loc, preferred_element_type=F32)
              + jnp.dot(vct, p_ctx, preferred_element_type=F32))
        den = ox[HEAD_DIM:HEAD_DIM + 1, :] + jnp.exp2(sink_row - m)
        ot = ox[:HEAD_DIM, :] * (1.0 / den)
        gs = g_ref[pl.ds(q0, BLOCK), :].astype(F32)
        for h in range(GQA_GROUP):
            oh = ot[:, h * LANES:(h + 1) * LANES].T * gs[:, h * LANES:(h + 1) * LANES]
            o_ref[pl.ds(q0, BLOCK), h * LANES:(h + 1) * LANES] = oh.astype(o_ref.dtype)
        return carry

    lax.fori_loop(0, n_blocks, body, 0, unroll=8)


def _attention(px, ckv, sink, batch):
    gw = GQA_GROUP * HEAD_DIM
    k_blk0 = ATTN_WIDTH // HEAD_DIM
    v_blk0 = (ATTN_WIDTH + KV_WIDTH) // HEAD_DIM
    g_blk0 = (ATTN_WIDTH + 2 * KV_WIDTH) // gw
    return pl.pallas_call(
        _attn_kernel,
        grid=(batch, N_KV_HEADS),
        in_specs=[
            pl.BlockSpec(memory_space=pltpu.SMEM),
            pl.BlockSpec((SEQ, gw), lambda b, h: (b, h)),
            pl.BlockSpec((SEQ, HEAD_DIM), lambda b, h: (b, k_blk0 + h)),
            pl.BlockSpec((SEQ, HEAD_DIM), lambda b, h: (b, v_blk0 + h)),
            pl.BlockSpec((SEQ, gw), lambda b, h: (b, g_blk0 + h)),
            pl.BlockSpec((CTX_LEN, HEAD_DIM), lambda b, h: (b, h)),
            pl.BlockSpec((CTX_LEN, HEAD_DIM), lambda b, h: (b, N_KV_HEADS + h)),
        ],
        out_specs=pl.BlockSpec((SEQ, gw), lambda b, h: (b, h)),
        out_shape=jax.ShapeDtypeStruct((batch * SEQ, ATTN_WIDTH), BF16),
        scratch_shapes=[pltpu.VMEM((HEAD_DIM + ONES_ROWS, SEQ), BF16),
                        pltpu.VMEM((HEAD_DIM + ONES_ROWS, CTX_LEN), BF16),
                        pltpu.VMEM((3, 3 * BLOCK, GQA_GROUP * BLOCK), F32)],
        compiler_params=_cparams(("arbitrary", "arbitrary")),
        name="banded_attention",
    )(sink, px, px, px, px, ckv, ckv)


def _out_proj_kernel(*refs, final):
    if final:
        a_ref, w_ref, x_ref, gate_ref, fg_ref, o_ref = refs
    else:
        a_ref, w_ref, x_ref, gate_ref, ng_ref, sh_ref, sc_ref, o_ref, hx_ref = refs
    acc = jnp.dot(a_ref[...], w_ref[...], preferred_element_type=F32)
    y = x_ref[...] + gate_ref[0] * acc
    if final:
        ms = jnp.mean(y * y, axis=-1, keepdims=True)
        y = y * lax.rsqrt(ms + NORM_EPS) * fg_ref[...]
    else:
        gain = ng_ref[...] * (1.0 + sc_ref[0])
        hx_ref[...] = _norm_mod_rows(y, gain, sh_ref[0]).astype(hx_ref.dtype)
    o_ref[...] = y


def _out_proj(a, w, x, mods, layer, *, final_g=None, next_norm_g=None, tm=512):
    m, d = x.shape
    kdim = a.shape[1]
    final = final_g is not None
    row_tile = pl.BlockSpec((tm, d), lambda i: (i, 0))
    vec = pl.BlockSpec((1, d), lambda i: (0, 0))
    in_specs = [
        pl.BlockSpec((tm, kdim), lambda i: (i, 0)),
        pl.BlockSpec((kdim, d), lambda i: (0, 0)),
        row_tile,
        _mod_spec(_mod_row(layer, 0, 2), tm, SEQ, d),
    ]
    args = [a, w, x, mods]
    out_specs = [row_tile]
    out_shape = [jax.ShapeDtypeStruct((m, d), F32)]
    if final:
        in_specs.append(vec)
        args.append(final_g.reshape(1, d))
    else:
        in_specs += [vec, _mod_spec(_mod_row(layer + 1, 0, 0), tm, SEQ, d),
                     _mod_spec(_mod_row(layer + 1, 0, 1), tm, SEQ, d)]
        args += [next_norm_g.reshape(1, d), mods, mods]
        out_specs.append(row_tile)
        out_shape.append(jax.ShapeDtypeStruct((m, d), BF16))
    return pl.pallas_call(
        functools.partial(_out_proj_kernel, final=final),
        grid=(m // tm,),
        in_specs=in_specs,
        out_specs=out_specs,
        out_shape=out_shape,
        compiler_params=_cparams(("arbitrary",)),
        name="out_proj",
    )(*args)


HALF = SEQ // 2
QUARTER = SEQ // 4
FLIP_BLOCK = 256


def _fold_tables():
    def tab(f, n):
        t2 = 2 * np.arange(n, dtype=np.int64) + 1
        ang = ((f[:, None] * t2[None, :]) % (2 * FFT_N)).astype(np.float64) * (math.pi / FFT_N)
        return np.cos(ang), np.sin(ang)

    r = np.arange(QUARTER, dtype=np.int64)
    g = np.arange(HALF, dtype=np.int64)
    f_ee, f_eo, f_o = 4 * r, 4 * r + 2, 2 * g + 1
    cee, see = tab(f_ee, QUARTER)
    ceo, seo = tab(f_eo, QUARTER)
    co, so = tab(f_o, HALF)
    see[0, :] = np.where(np.arange(QUARTER) % 2 == 0, 1.0, -1.0)
    const = lambda a, dt: jnp.asarray(np.ascontiguousarray(a).astype(np.float32)).astype(dt)
    phase = lambda f: np.stack([np.cos(f * (math.pi / FFT_N)), np.sin(f * (math.pi / FFT_N))])
    return dict(
        fq=const(np.stack([cee, see, ceo, seo]), BF16),
        fo=const(np.stack([co, so]), BF16),
        iq=const(np.stack([cee.T, ceo.T, see.T, seo.T]), BF16),
        io=const(np.stack([so.T, co.T]), BF16),
        anti=const(np.eye(FLIP_BLOCK)[::-1], BF16),
        pq=const(np.stack([phase(f_ee), phase(f_eo)], axis=1)[..., None], F32),
        po=const(phase(f_o)[..., None], F32),
    )


def _flip_rows(h, anti):
    nb = h.shape[0] // FLIP_BLOCK
    return jnp.concatenate(
        [jnp.dot(anti[...], h[(nb - 1 - a) * FLIP_BLOCK:(nb - a) * FLIP_BLOCK, :], preferred_element_type=F32)
         for a in range(nb)], axis=0)


def _fold_two_levels(x, anti):
    q0, q1, q2, q3 = (x[k * QUARTER:(k + 1) * QUARTER] for k in range(4))
    f1, f2, f3 = (_flip_rows(q, anti) for q in (q1, q2, q3))
    q0, q1, q2 = (q.astype(F32) for q in (q0, q1, q2))
    xs0, xa0 = q0 + f3, q0 - f3
    fxs1, fxa1 = f1 + q2, f1 - q2
    bf = lambda v: v.astype(BF16)
    xs = jnp.concatenate([bf(xs0), bf(q1 + f2)], axis=0)
    xa = jnp.concatenate([bf(xa0), bf(q1 - f2)], axis=0)
    return xs, xa, bf(xs0 + fxs1), bf(xs0 - fxs1), bf(xa0 + fxa1), bf(xa0 - fxa1)


def _staggered(n, matmuls, finish):
    matmuls(0, 0)
    for t in range(1, n):
        matmuls(t, t % 2)
        finish(t - 1, (t - 1) % 2)
    finish(n - 1, (n - 1) % 2)


def _col_chunks(n):
    return [slice(c, c + MXU_COLS) for c in range(0, n, MXU_COLS)]


PLANES = {"a_ee": 0, "b_ee": QUARTER, "a_eo": 2 * QUARTER, "b_eo": 3 * QUARTER,
          "a_o": 2 * HALF, "b_o": 3 * HALF}
GROUPS = (("a_ee", "b_ee", QUARTER), ("a_eo", "b_eo", QUARTER), ("a_o", "b_o", HALF))


def _spectrum(fq_ref, fo_ref, x, anti):
    xs, xa, xss, xsa, xas, xaa = _fold_two_levels(x, anti)
    dq = lambda k, v: jnp.dot(fq_ref[k], v, preferred_element_type=F32)
    do = lambda k, v: jnp.dot(fo_ref[k], v, preferred_element_type=F32)
    return {"a_ee": dq(0, xss), "b_ee": dq(1, xaa), "a_eo": dq(2, xsa), "b_eo": dq(3, xas),
            "a_o": do(0, xa), "b_o": do(1, xs)}


def _plane(ref, lead, name, rows, cols):
    r0 = PLANES[name]
    return ref[lead, r0:r0 + rows, cols]


def _resident(shape):
    zeros = (0,) * len(shape)
    return pl.BlockSpec(shape, lambda *_: zeros, pipeline_mode=pl.Buffered(1))


def _pack_filter_mlp(w1, b1, w2, b2, freq):
    hpad = LANES - FILTER_HIDDEN
    bands = (FILTER_EMB - 1) // 2
    row = lambda v: jnp.pad(v.reshape(1, -1), ((0, 7), (0, hpad)))
    mat = lambda m: jnp.pad(m, ((0, LANES - m.shape[0]), (0, hpad)))
    return jnp.concatenate([row(w1[0]), row(b1), row(b2), row(freq), jnp.zeros((LANES - 32, LANES), F32),
                            mat(w1[1:1 + bands]), mat(w1[1 + bands:]), mat(w2)], axis=0)


def _filter_sd_body(fp_ref, w3f_ref, b3f_ref, w3b_ref, b3b_ref, d_ref, o_ref, hid_ref, *, first, j):
    length = o_ref.shape[1]
    tn = o_ref.shape[2]
    hp = lax.Precision.HIGHEST
    w1t_ref, b1_ref, b2_ref, fq_ref = (fp_ref.at[r:r + 1] for r in (0, 8, 16, 24))
    w1c_ref, w1s_ref, w2_ref = (fp_ref.at[r:r + LANES] for r in (LANES, 2 * LANES, 3 * LANES))

    @pl.when(first)
    def _():
        n = lax.broadcasted_iota(jnp.int32, (length, LANES), 0).astype(F32)
        lane = lax.broadcasted_iota(jnp.int32, (length, LANES), 1)
        bands = (FILTER_EMB - 1) // 2
        fr_step = (bands - 1 - 1e-4) / (bands - 1)
        fr = jnp.where(lane < bands, 1e-4 + lane.astype(F32) * fr_step, 0.0)
        ang = (2.0 * math.pi * n / length) * fr
        t = n / (length - 1)
        fq = fq_ref[...]
        pre = (t * w1t_ref[...]
               + jnp.dot(jnp.cos(ang), w1c_ref[...], precision=hp, preferred_element_type=F32)
               + jnp.dot(-jnp.sin(ang), w1s_ref[...], precision=hp, preferred_element_type=F32)
               + b1_ref[...])
        hid = jnp.sin(fq * pre)
        hid = jnp.sin(fq * (jnp.dot(hid, w2_ref[...], precision=hp, preferred_element_type=F32)
                            + b2_ref[...]))
        hid_hi = hid.astype(BF16)
        hid_ref[0] = hid_hi
        hid_ref[1] = (hid - hid_hi.astype(F32)).astype(BF16)

    def dot3(w_ref):
        wf = w_ref[...]
        wf = jnp.concatenate([wf, jnp.zeros((LANES - wf.shape[0], wf.shape[1]), F32)], axis=0)
        w_hi = wf.astype(BF16)
        w_lo = (wf - w_hi.astype(F32)).astype(BF16)
        return (jnp.dot(hid_ref[0], w_hi, preferred_element_type=F32)
                + jnp.dot(hid_ref[1], w_hi, preferred_element_type=F32)
                + jnp.dot(hid_ref[0], w_lo, preferred_element_type=F32))

    row = lax.broadcasted_iota(jnp.int32, (length, tn), 0)
    chan = (lax.broadcasted_iota(jnp.int32, (length, tn), 1) + j * tn).astype(F32)
    min_decay = math.log(DECAY_TARGET) / DECAY_SLOW
    max_decay = math.log(DECAY_TARGET) / DECAY_FAST
    delta = min_decay + chan * ((max_decay - min_decay) / (HYENA_WIDTH - 1))
    t = row.astype(F32) / (length - 1)
    window = jnp.exp(-t * jnp.abs(delta)) + WINDOW_SHIFT
    hf = (dot3(w3f_ref) + b3f_ref[...]) * window
    hb = (dot3(w3b_ref) + b3b_ref[...]) * window
    hf = jnp.where(row == 0, hf + d_ref[0], hf)
    hb = jnp.where(row == 0, 0.0, hb)
    o_ref[0] = (hf + hb).astype(o_ref.dtype)
    o_ref[1] = (hf - hb).astype(o_ref.dtype)


def _prep_kernel(c_ref, aw_ref, ab_ref, *refs, filter_steps, ct):
    wa_ref, wh_ref = refs[-7:-5]
    mods_ref, sd_ref, wa_bf_ref, wh_bf_ref, hid_ref = refs[-5:]
    s = pl.program_id(0)
    _mods_kernel(c_ref, aw_ref, ab_ref, mods_ref)

    @pl.when(s < filter_steps)
    def _():
        _filter_sd_body(*refs[:-7], sd_ref, hid_ref, first=s == 0, j=s % ct)

    @pl.when(s < CAST_STEPS)
    def _():
        wa_bf_ref[...] = wa_ref[...].astype(BF16)

    @pl.when((s >= CAST_STEPS) & (s < 2 * CAST_STEPS))
    def _():
        wh_bf_ref[...] = wh_ref[...].astype(BF16)


CAST_STEPS = 8


def _prep(cc, ada_w, ada_b, w1, b1, w2, b2, w3, b3, freq, bias_d, w_out_a, w_out_h, *,
          tn_mods=512, tn=256):
    depth, d, n_mods = ada_w.shape
    mt = n_mods // tn_mods
    w = HYENA_WIDTH
    ct = w // tn
    filter_steps = 2 * ct
    assert filter_steps <= depth * mt
    n = w3.shape[1]
    fp = _pack_filter_mlp(w1, b1, w2, b2, freq)
    b3r = b3.reshape(1, n)
    order = lambda s: jnp.minimum(s, filter_steps - 1) // ct
    chan = lambda s: jnp.minimum(s, filter_steps - 1) % ct
    fwd_cols = lambda s: (0, 2 * order(s) * ct + chan(s))
    bwd_cols = lambda s: (0, (2 * order(s) + 1) * ct + chan(s))
    assert w_out_a.shape == w_out_h.shape and 2 * CAST_STEPS <= depth * mt
    cast_blk = (w_out_a.shape[0] // CAST_STEPS, w_out_a.shape[1])
    first_rows = lambda s: (jnp.minimum(s, CAST_STEPS - 1), 0)
    second_rows = lambda s: (jnp.clip(s - CAST_STEPS, 0, CAST_STEPS - 1), 0)
    return pl.pallas_call(
        functools.partial(_prep_kernel, filter_steps=filter_steps, ct=ct),
        grid=(depth * mt,),
        in_specs=[pl.BlockSpec((MOD_ROWS, d), lambda s: (0, 0)),
                  pl.BlockSpec((1, d, tn_mods), lambda s: (s // mt, 0, s % mt)),
                  pl.BlockSpec((1, 1, tn_mods), lambda s: (s // mt, 0, s % mt)),
                  pl.BlockSpec(fp.shape, lambda s: (0, 0)),
                  pl.BlockSpec((FILTER_HIDDEN, tn), fwd_cols), pl.BlockSpec((1, tn), fwd_cols),
                  pl.BlockSpec((FILTER_HIDDEN, tn), bwd_cols), pl.BlockSpec((1, tn), bwd_cols),
                  pl.BlockSpec((1, 1, tn), lambda s: (order(s), 0, chan(s))),
                  pl.BlockSpec(cast_blk, first_rows), pl.BlockSpec(cast_blk, second_rows)],
        out_specs=[pl.BlockSpec((1, MOD_ROWS, tn_mods), lambda s: (s // mt, 0, s % mt)),
                   pl.BlockSpec((2, SEQ, tn), lambda s: (0, 0, order(s) * ct + chan(s))),
                   pl.BlockSpec(cast_blk, first_rows), pl.BlockSpec(cast_blk, second_rows)],
        out_shape=[jax.ShapeDtypeStruct((depth, MOD_ROWS, n_mods), F32),
                   jax.ShapeDtypeStruct((2, SEQ, 2 * w), BF16),
                   jax.ShapeDtypeStruct(w_out_a.shape, BF16),
                   jax.ShapeDtypeStruct(w_out_h.shape, BF16)],
        scratch_shapes=[pltpu.VMEM((2, SEQ, LANES), BF16)],
        compiler_params=_cparams(("arbitrary",)),
        name="mods_and_filters",
    )(cc, ada_w, ada_b.reshape(depth, 1, n_mods),
      fp, w3, b3r, w3, b3r, bias_d.reshape(2, 1, w),
      w_out_a, w_out_h)


def _filter_spec_kernel(s_ref, d_ref, fq_ref, fo_ref, anti_ref, pq_ref, po_ref, k_ref):
    phases = {"a_ee": (pq_ref[0, 0], pq_ref[1, 0]), "a_eo": (pq_ref[0, 1], pq_ref[1, 1]),
              "a_o": (po_ref[0], po_ref[1])}
    is0 = lax.broadcasted_iota(jnp.int32, (QUARTER, MXU_COLS), 0) == 0
    for cols in _col_chunks(k_ref.shape[2]):
        sp = _spectrum(fq_ref, fo_ref, s_ref[0, :, cols], anti_ref)
        dp = _spectrum(fq_ref, fo_ref, d_ref[0, :, cols], anti_ref)
        for a, b, rows in GROUPS:
            cos, sin = phases[a]
            ka = sp[a] * cos + sp[b] * sin
            kb = dp[b] * cos - dp[a] * sin
            if a == "a_ee":
                scale = jnp.where(is0, 1.0 / FFT_N, 2.0 / FFT_N)
                kb = jnp.where(is0, sp[b], kb)
            else:
                scale = 2.0 / FFT_N
            k_ref[0, PLANES[a]:PLANES[a] + rows, cols] = (ka * scale).astype(k_ref.dtype)
            k_ref[0, PLANES[b]:PLANES[b] + rows, cols] = (kb * scale).astype(k_ref.dtype)


def _filter_spectra(sd, tabs, *, tn=512):
    w = HYENA_WIDTH
    ct = w // tn
    consts = [tabs[k] for k in ("fq", "fo", "anti", "pq", "po")]
    return pl.pallas_call(
        _filter_spec_kernel,
        grid=(2, ct),
        in_specs=[
            pl.BlockSpec((1, SEQ, tn), lambda o, j: (0, 0, o * ct + j)),
            pl.BlockSpec((1, SEQ, tn), lambda o, j: (1, 0, o * ct + j)),
        ] + [_resident(c.shape) for c in consts],
        out_specs=pl.BlockSpec((1, 4 * HALF, tn), lambda o, j: (o, 0, j)),
        out_shape=jax.ShapeDtypeStruct((2, 4 * HALF, w), F32),
        compiler_params=_cparams(("arbitrary", "arbitrary")),
        name="filter_spectra",
    )(sd, sd, *consts)


def _conv_fwd_kernel(x_ref, fq_ref, fo_ref, anti_ref, k_ref, y_ref):
    is0 = lax.broadcasted_iota(jnp.int32, (QUARTER, MXU_COLS), 0) == 0
    for cols in _col_chunks(y_ref.shape[2]):
        sp = _spectrum(fq_ref, fo_ref, x_ref[:, cols], anti_ref)
        for a, b, rows in GROUPS:
            ka = _plane(k_ref, 0, a, rows, cols).astype(F32)
            kb = _plane(k_ref, 0, b, rows, cols).astype(F32)
            bkb = sp[b] * kb
            if a == "a_ee":
                ya = sp[a] * ka - jnp.where(is0, 0.0, bkb)
                yb = jnp.where(is0, bkb, sp[a] * kb + sp[b] * ka)
            else:
                ya = sp[a] * ka - bkb
                yb = sp[a] * kb + sp[b] * ka
            y_ref[0, PLANES[a]:PLANES[a] + rows, cols] = ya.astype(y_ref.dtype)
            y_ref[0, PLANES[b]:PLANES[b] + rows, cols] = yb.astype(y_ref.dtype)


def _conv_fwd(x, x_blk0, kspec, order, tabs, batch, *, tn=512):
    w = HYENA_WIDTH
    ct = w // tn
    consts = [tabs[k] for k in ("fq", "fo", "anti")]
    return pl.pallas_call(
        _conv_fwd_kernel,
        grid=(ct, batch),
        in_specs=[pl.BlockSpec((SEQ, tn), lambda j, b: (b, x_blk0 * ct + j))]
        + [_resident(c.shape) for c in consts]
        + [pl.BlockSpec((1, 4 * HALF, tn), lambda j, b: (order, 0, j))],
        out_specs=pl.BlockSpec((1, 4 * HALF, tn), lambda j, b: (b, 0, j)),
        out_shape=jax.ShapeDtypeStruct((batch, 4 * HALF, w), BF16),
        compiler_params=_cparams(("arbitrary", "arbitrary")),
        name="long_conv_fwd",
    )(x, *consts, kspec)


def _conv_inv_kernel(*refs, gated):
    if gated:
        y_ref, iq_ref, io_ref, anti_ref, xm_ref, g_ref, o_ref, accq_ref, acco_ref = refs
    else:
        y_ref, iq_ref, io_ref, anti_ref, xm_ref, o_ref, accq_ref, acco_ref = refs
    chunks = _col_chunks(o_ref.shape[1])
    flipped = lambda v: _flip_rows(v.astype(BF16), anti_ref)

    def matmuls(t, slot):
        cols = chunks[t]
        for k, name in enumerate(("a_ee", "a_eo", "b_ee", "b_eo")):
            accq_ref[slot, k] = jnp.dot(iq_ref[k], _plane(y_ref, 0, name, QUARTER, cols),
                                        preferred_element_type=F32)
        for k, name in enumerate(("b_o", "a_o")):
            acco_ref[slot, k] = jnp.dot(io_ref[k], _plane(y_ref, 0, name, HALF, cols),
                                        preferred_element_type=F32)

    def finish(t, slot):
        cols = chunks[t]
        p, r, q, s = (accq_ref[slot, k] for k in range(4))
        osum = acco_ref[slot, 0] + acco_ref[slot, 1]
        odif = acco_ref[slot, 0] - acco_ref[slot, 1]
        e1, e2, d1, d2 = p + r, q + s, p - r, s - q
        quarters = (e1 + e2 + osum[:QUARTER],
                    flipped(d1 + d2) + osum[QUARTER:],
                    d1 - d2 + flipped(odif[QUARTER:]),
                    flipped(e1 - e2 + odif[:QUARTER]))
        for k, y in enumerate(quarters):
            rows = slice(k * QUARTER, (k + 1) * QUARTER)
            mult = xm_ref[rows, cols]
            if gated:
                mult = mult * g_ref[rows, cols]
            o_ref[rows, cols] = (y * mult.astype(F32)).astype(o_ref.dtype)

    _staggered(len(chunks), matmuls, finish)


def _conv_inv(y, tabs, p, m_blk0, g_blk0=None, *, tn=512):
    batch = y.shape[0]
    w = HYENA_WIDTH
    ct = w // tn
    gated = g_blk0 is not None
    col = lambda blk0: (lambda b, j: (b, blk0 * ct + j))
    consts = [tabs[k] for k in ("iq", "io", "anti")]
    in_specs = ([pl.BlockSpec((1, 4 * HALF, tn), lambda b, j: (b, 0, j))]
                + [_resident(c.shape) for c in consts]
                + [pl.BlockSpec((SEQ, tn), col(m_blk0))])
    args = [y, *consts, p]
    if gated:
        in_specs.append(pl.BlockSpec((SEQ, tn), col(g_blk0)))
        args.append(p)
    return pl.pallas_call(
        functools.partial(_conv_inv_kernel, gated=gated),
        grid=(batch, ct),
        in_specs=in_specs,
        out_specs=pl.BlockSpec((SEQ, tn), lambda b, j: (b, j)),
        out_shape=jax.ShapeDtypeStruct((batch * SEQ, w), BF16),
        scratch_shapes=[pltpu.VMEM((2, 4, QUARTER, MXU_COLS), F32),
                        pltpu.VMEM((2, 2, HALF, MXU_COLS), F32)],
        compiler_params=_cparams(("arbitrary", "arbitrary")),
        name="long_conv_inv",
    )(*args)


def _rope_tables():
    pos = np.arange(SEQ)
    row = (pos // GRID_W).astype(np.float32)
    col = (pos % GRID_W).astype(np.float32)
    half = HEAD_DIM // 2
    inv = (ROPE_BASE ** (-np.arange(0, half, 2, dtype=np.float32) / half)).astype(np.float32)
    ar = row[:, None] * inv[None]
    ac = col[:, None] * inv[None]
    cos = np.concatenate([np.cos(ar), np.cos(ar), np.cos(ac), np.cos(ac)], axis=1)
    sin = np.concatenate([-np.sin(ar), np.sin(ar), -np.sin(ac), np.sin(ac)], axis=1)
    return jnp.asarray(cos, F32), jnp.asarray(sin, F32)


def _tile_types(tn, kinds):
    per_tile = tn // LANES
    return [kinds[s0:s0 + per_tile] for s0 in range(0, len(kinds), per_tile)]


def kernel(x, c, ctx, c_ctx, norm_g, ada_w, ada_b, attn_w_in, attn_w_out, attn_sink, hy_w_in,
           hy_conv_w, hy_conv_b, hy_w1, hy_b1, hy_w2, hy_b2, hy_w3, hy_b3, hy_freq, hy_bias_d,
           hy_w_out, final_g):
    batch, seq, d = x.shape
    assert (seq, d) == (SEQ, D_MODEL) and ctx.shape[1] == CTX_LEN
    assert norm_g.shape[0] == 2 and attn_w_in.shape[0] == 1 and hy_w_in.shape[0] == 1
    w = HYENA_WIDTH

    ctx_row = batch
    cc = jnp.concatenate([c, c_ctx[None], jnp.zeros((MOD_ROWS - batch - 1, d), F32)], axis=0)
    mods, sd, w_out_attn, w_out_hyena = _prep(
        cc, ada_w, ada_b, hy_w1[0], hy_b1[0], hy_w2[0], hy_b2[0], hy_w3[0], hy_b3[0], hy_freq[0],
        hy_bias_d[0], attn_w_out[0], hy_w_out[0])
    mods = mods.reshape(-1, 1, d)

    x2 = x.reshape(batch * seq, d)
    ctx2 = ctx.reshape(batch * CTX_LEN, d)

    tn = 1024
    kinds0 = (["q"] * N_HEADS + ["k"] * N_KV_HEADS + [None] * N_KV_HEADS
              + ["g"] * (ATTN_WIDTH // LANES))
    hx = _norm_mod(x2, norm_g[0], mods, 0, tm=1024)
    px = _proj(hx, attn_w_in[0], 0, ATTN_IN, tm=SEQ, tn=tn, tile_types=_tile_types(tn, kinds0),
               rope=_rope_tables(), q_scale=HEAD_DIM ** -0.5 * LOG2E)
    ckv = _ctx_kv(ctx2, norm_g[0], mods, 0, ctx_row, attn_w_in[0], ATTN_WIDTH, 2 * KV_WIDTH)
    og = _attention(px, ckv, attn_sink[0], batch)
    x2, hx = _out_proj(og, w_out_attn, x2, mods, 0, next_norm_g=norm_g[1])

    kinds1 = ["c"] * (3 * w // LANES) + ["g"] * (w // LANES)
    tn1 = 1024
    p = _proj(hx, hy_w_in[0], 0, 4 * w, tm=SEQ, tn=tn1, tile_types=_tile_types(tn1, kinds1),
              conv=(hy_conv_w[0], hy_conv_b[0].reshape(1, 3 * w)))
    tabs = _fold_tables()
    kspec = _filter_spectra(sd, tabs)
    y1 = _conv_fwd(p, 2, kspec, 0, tabs, batch)
    z = _conv_inv(y1, tabs, p, 0)
    y2 = _conv_fwd(z, 0, kspec, 1, tabs, batch)
    yg = _conv_inv(y2, tabs, p, 1, g_blk0=3)
    out, = _out_proj(yg, w_out_hyena, x2, mods, 1, final_g=final_g)
    return out.reshape(batch, seq, d)
```

```python
import functools
import math

import jax
import jax.numpy as jnp
import numpy as np
from jax import lax
from jax.experimental import pallas as pl
from jax.experimental.pallas import tpu as pltpu

F32 = jnp.float32
BF16 = jnp.bfloat16

D_MODEL = 2048
SEQ = 2048
CTX_LEN = 256
GRID_W = 64
HEAD_DIM = 128
N_HEADS = 16
N_KV_HEADS = 4
GQA_GROUP = 4
ATTN_WIDTH = 2048
KV_WIDTH = 512
ATTN_IN = 2 * ATTN_WIDTH + 2 * KV_WIDTH
WINDOW = 128
BLOCK = 128
ROPE_BASE = 10000.0
HYENA_WIDTH = 2048
FILTER_EMB = 33
FILTER_HIDDEN = 64
DECAY_FAST = 0.3
DECAY_SLOW = 1.5
DECAY_TARGET = 1e-2
WINDOW_SHIFT = 0.05
NORM_EPS = 1e-6
NEG_INF = -1e30

LANES = 128
MOD_ROWS = 8
MXU_COLS = 256
ACC_ROWS = 1024
ONES_ROWS = 16
LOG2E = math.log2(math.e)
FFT_N = 2 * SEQ
VMEM_LIMIT = 56 * 1024 * 1024


def _cparams(sem):
    return pltpu.CompilerParams(dimension_semantics=sem, vmem_limit_bytes=VMEM_LIMIT)


def _mods_kernel(c_ref, w_ref, b_ref, o_ref):
    c = c_ref[...]
    s = c * jax.nn.sigmoid(c)
    s_hi = s.astype(BF16)
    s_lo = (s - s_hi.astype(F32)).astype(BF16)
    lhs = jnp.concatenate([s_hi, s_lo], axis=0)
    r = jnp.dot(lhs, w_ref[0].astype(BF16), preferred_element_type=F32)
    o_ref[0] = r[:MOD_ROWS] + r[MOD_ROWS:] + b_ref[0]


def _rope_slab(t, cos, sin):
    lane = lax.broadcasted_iota(jnp.int32, t.shape, 1)
    first = (lane % 64) < 32
    partner = jnp.where(first, pltpu.roll(t, 96, 1), pltpu.roll(t, 32, 1))
    return t * cos + partner * sin


def _norm_mod_rows(x, gain, add):
    ms = jnp.mean(x * x, axis=-1, keepdims=True)
    return x * lax.rsqrt(ms + NORM_EPS) * gain + add


def _norm_mod_kernel(x_ref, g_ref, sh_ref, sc_ref, o_ref, *, row_chunk):
    gain = g_ref[...] * (1.0 + sc_ref[0])
    add = sh_ref[0]
    for r in range(0, x_ref.shape[0], row_chunk):
        o_ref[r:r + row_chunk, :] = _norm_mod_rows(x_ref[r:r + row_chunk, :], gain, add).astype(o_ref.dtype)


def _mod_row(layer, row, kind):
    return (layer * MOD_ROWS + row) * 3 + kind


def _mod_spec(base, tm, rows_per_mod, d):
    return pl.BlockSpec((1, 1, d), lambda i: (base + 3 * ((i * tm) // rows_per_mod), 0, 0))


def _norm_mod(x, g, mods, layer, *, tm):
    m, d = x.shape
    return pl.pallas_call(
        functools.partial(_norm_mod_kernel, row_chunk=min(tm, 256)),
        grid=(m // tm,),
        in_specs=[pl.BlockSpec((tm, d), lambda i: (i, 0)), pl.BlockSpec((1, d), lambda i: (0, 0)),
                  _mod_spec(_mod_row(layer, 0, 0), tm, SEQ, d),
                  _mod_spec(_mod_row(layer, 0, 1), tm, SEQ, d)],
        out_specs=pl.BlockSpec((tm, d), lambda i: (i, 0)),
        out_shape=jax.ShapeDtypeStruct((m, d), BF16),
        compiler_params=_cparams(("arbitrary",)),
        name="norm_mod",
    )(x, g.reshape(1, d), mods, mods)


def _ctx_kv_kernel(x_ref, g_ref, sh_ref, sc_ref, w_ref, o_ref, hx_ref, *, row_chunk):
    gain = g_ref[...] * (1.0 + sc_ref[0])
    add = sh_ref[0]
    for r in range(0, x_ref.shape[0], row_chunk):
        hx_ref[r:r + row_chunk, :] = _norm_mod_rows(x_ref[r:r + row_chunk, :], gain, add).astype(BF16)
    wb = w_ref[...].astype(BF16)
    for c0 in range(0, o_ref.shape[1], MXU_COLS):
        o_ref[:, c0:c0 + MXU_COLS] = jnp.dot(hx_ref[...], wb[:, c0:c0 + MXU_COLS],
                                             preferred_element_type=F32).astype(o_ref.dtype)


def _ctx_kv(ctx, g, mods, layer, mod_row, w, col0, n):
    m, d = ctx.shape
    assert col0 % n == 0 and m <= ACC_ROWS
    shift = _mod_row(layer, mod_row, 0)
    scale = _mod_row(layer, mod_row, 1)
    return pl.pallas_call(
        functools.partial(_ctx_kv_kernel, row_chunk=256),
        grid=(1,),
        in_specs=[pl.BlockSpec((m, d), lambda i: (0, 0)), pl.BlockSpec((1, d), lambda i: (0, 0)),
                  pl.BlockSpec((1, 1, d), lambda i: (shift, 0, 0)),
                  pl.BlockSpec((1, 1, d), lambda i: (scale, 0, 0)),
                  pl.BlockSpec((d, n), lambda i: (0, col0 // n))],
        out_specs=pl.BlockSpec((m, n), lambda i: (0, 0)),
        out_shape=jax.ShapeDtypeStruct((m, n), BF16),
        scratch_shapes=[pltpu.VMEM((m, d), BF16)],
        compiler_params=_cparams(("arbitrary",)),
        name="ctx_kv",
    )(ctx, g.reshape(1, d), mods, mods, w)


def _proj_kernel(*refs, tile_types, has_rope, has_conv, q_scale):
    if has_rope:
        hx_ref, w_ref, cos_ref, sin_ref, o_ref, wb_ref = refs
    elif has_conv:
        hx_ref, w_ref, cw_ref, cb_ref, o_ref, wb_ref = refs
    else:
        hx_ref, w_ref, o_ref, wb_ref = refs
    j = pl.program_id(0)
    tm, tn = o_ref.shape

    @pl.when(pl.program_id(1) == 0)
    def _():
        wb_ref[...] = w_ref[...].astype(BF16)

    w_ref = wb_ref

    def plain():
        for c0 in range(0, tn, MXU_COLS):
            o_ref[:, c0:c0 + MXU_COLS] = chunk_dot(c0).astype(o_ref.dtype)

    def chunk_dot(c0):
        pieces = [jnp.dot(hx_ref[r:r + ACC_ROWS, :], w_ref[:, c0:c0 + MXU_COLS],
                          preferred_element_type=F32) for r in range(0, tm, ACC_ROWS)]
        return pieces[0] if len(pieces) == 1 else jnp.concatenate(pieces, axis=0)

    def gated():
        for c0 in range(0, tn, MXU_COLS):
            half = 0.5 * chunk_dot(c0)
            o_ref[:, c0:c0 + MXU_COLS] = (half + half * jnp.tanh(half)).astype(o_ref.dtype)

    def short_conv():
        for c0 in range(0, tn, MXU_COLS):
            acc = chunk_dot(c0)
            for h in range(MXU_COLS // LANES):
                lo = c0 + h * LANES
                u = acc[:, h * LANES:(h + 1) * LANES]
                w3 = cw_ref[:, lo:lo + LANES]
                b = cb_ref[:, lo:lo + LANES]
                y = pltpu.roll(u, 1, 0) * w3[0:1] + u * w3[1:2] + pltpu.roll(u, tm - 1, 0) * w3[2:3] + b
                o_ref[:, lo:lo + LANES] = y.astype(o_ref.dtype)
                first = u[0:1] * w3[1:2] + u[1:2] * w3[2:3] + b
                last = u[tm - 2:tm - 1] * w3[0:1] + u[tm - 1:tm] * w3[1:2] + b
                o_ref[0:1, lo:lo + LANES] = first.astype(o_ref.dtype)
                o_ref[tm - 1:tm, lo:lo + LANES] = last.astype(o_ref.dtype)

    def roped(types):
        cos = cos_ref[...]
        sin = sin_ref[...]
        for c0 in range(0, tn, MXU_COLS):
            acc = chunk_dot(c0)
            for h in range(MXU_COLS // LANES):
                ty = types[c0 // LANES + h]
                slab = acc[:, h * LANES:(h + 1) * LANES]
                if ty is not None:
                    slab = _rope_slab(slab, cos, sin)
                    if ty == "q":
                        slab = slab * q_scale
                lo = c0 + h * LANES
                o_ref[:, lo:lo + LANES] = slab.astype(o_ref.dtype)

    if tile_types is None:
        plain()
        return

    groups = {}
    for t, types in enumerate(tile_types):
        groups.setdefault(tuple(types), []).append(t)
    for types, tiles in groups.items():
        cond = functools.reduce(jnp.logical_or, [j == t for t in tiles])
        if all(ty is None for ty in types):
            pl.when(cond)(plain)
        elif all(ty == "g" for ty in types):
            pl.when(cond)(gated)
        elif all(ty == "c" for ty in types):
            pl.when(cond)(short_conv)
        else:
            assert "g" not in types and "c" not in types
            pl.when(cond)(functools.partial(roped, types))


def _proj(hx, w, col_blk0, n, *, tm, tn, tile_types=None, rope=None, conv=None, q_scale=1.0):
    m, d = hx.shape
    in_specs = [
        pl.BlockSpec((tm, d), lambda j, i: (i, 0)),
        pl.BlockSpec((d, tn), lambda j, i: (0, col_blk0 + j)),
    ]
    args = [hx, w]
    scratch = [pltpu.VMEM((d, tn), BF16)]
    if rope is not None:
        cos, sin = rope
        seq_tiles = cos.shape[0] // tm
        in_specs += [pl.BlockSpec((tm, LANES), lambda j, i: (i % seq_tiles, 0))] * 2
        args += [cos, sin]
    if conv is not None:
        assert rope is None and tm == SEQ
        cw, cb = conv
        last = cw.shape[1] // tn - 1
        in_specs += [pl.BlockSpec((cw.shape[0], tn), lambda j, i: (0, jnp.minimum(j, last))),
                     pl.BlockSpec((1, tn), lambda j, i: (0, jnp.minimum(j, last)))]
        args += [cw, cb]
    return pl.pallas_call(
        functools.partial(_proj_kernel, tile_types=tile_types, has_rope=rope is not None,
                          has_conv=conv is not None, q_scale=q_scale),
        grid=(n // tn, m // tm),
        in_specs=in_specs,
        out_specs=pl.BlockSpec((tm, tn), lambda j, i: (i, j)),
        out_shape=jax.ShapeDtypeStruct((m, n), BF16),
        scratch_shapes=scratch,
        compiler_params=_cparams(("arbitrary", "arbitrary")),
        name="proj",
    )(*args)


def _attn_kernel(sink_ref, q_ref, k_ref, v_ref, g_ref, kc_ref, vc_ref, o_ref,
                 vt_ref, vct_ref, bias_ref):
    kh = pl.program_id(1)
    band = 3 * BLOCK
    cols = GQA_GROUP * BLOCK
    n_blocks = SEQ // BLOCK

    vt_ref[:HEAD_DIM, :] = v_ref[...].T
    vct_ref[:HEAD_DIM, :] = vc_ref[...].T

    @pl.when((pl.program_id(0) == 0) & (kh == 0))
    def _():
        vt_ref[HEAD_DIM:, :] = jnp.ones((ONES_ROWS, SEQ), BF16)
        vct_ref[HEAD_DIM:, :] = jnp.ones((ONES_ROWS, CTX_LEN), BF16)
        krow = lax.broadcasted_iota(jnp.int32, (band, cols), 0)
        qcol = lax.broadcasted_iota(jnp.int32, (band, cols), 1) % BLOCK
        for idx, off in enumerate((0, -BLOCK, -2 * BLOCK)):
            bias_ref[idx] = jnp.where(jnp.abs(krow - qcol + off) <= WINDOW, 0.0, NEG_INF)

    kc = kc_ref[...]
    vct = vct_ref[...]
    sink_row = jnp.concatenate(
        [jnp.full((1, BLOCK), sink_ref[kh * GQA_GROUP + h] * LOG2E, F32) for h in range(GQA_GROUP)],
        axis=1)

    def body(n, carry):
        q0 = pl.multiple_of(n * BLOCK, BLOCK)
        ks = pl.multiple_of(jnp.clip((n - 1) * BLOCK, 0, SEQ - band), BLOCK)
        bidx = jnp.where(n == 0, 0, jnp.where(n == n_blocks - 1, 2, 1))
        qs = q_ref[pl.ds(q0, BLOCK), :]
        q4t = jnp.concatenate([qs[:, h * LANES:(h + 1) * LANES].T for h in range(GQA_GROUP)],
                              axis=1)
        kb = k_ref[pl.ds(ks, band), :]
        s_loc = jnp.dot(kb, q4t, preferred_element_type=F32) + bias_ref[bidx]
        s_ctx = jnp.dot(kc, q4t, preferred_element_type=F32)
        m = jnp.maximum(jnp.maximum(jnp.max(s_loc, axis=0, keepdims=True),
                                    jnp.max(s_ctx, axis=0, keepdims=True)), sink_row)
        p_loc = jnp.exp2(s_loc - m).astype(BF16)
        p_ctx = jnp.exp2(s_ctx - m).astype(BF16)
        ox = (jnp.dot(vt_ref[:, pl.ds(ks, band)], p_loc, preferred_element_type=F32)
              + jnp.dot(vct, p_ctx, preferred_element_type=F32))
        den = ox[HEAD_DIM:HEAD_DIM + 1, :] + jnp.exp2(sink_row - m)
        ot = ox[:HEAD_DIM, :] * (1.0 / den)
        gs = g_ref[pl.ds(q0, BLOCK), :].astype(F32)
        for h in range(GQA_GROUP):
            oh = ot[:, h * LANES:(h + 1) * LANES].T * gs[:, h * LANES:(h + 1) * LANES]
            o_ref[pl.ds(q0, BLOCK), h * LANES:(h + 1) * LANES] = oh.astype(o_ref.dtype)
        return carry

    lax.fori_loop(0, n_blocks, body, 0, unroll=8)


def _attention(px, ckv, sink, batch):
    gw = GQA_GROUP * HEAD_DIM
    k_blk0 = ATTN_WIDTH // HEAD_DIM
    v_blk0 = (ATTN_WIDTH + KV_WIDTH) // HEAD_DIM
    g_blk0 = (ATTN_WIDTH + 2 * KV_WIDTH) // gw
    return pl.pallas_call(
        _attn_kernel,
        grid=(batch, N_KV_HEADS),
        in_specs=[
            pl.BlockSpec(memory_space=pltpu.SMEM),
            pl.BlockSpec((SEQ, gw), lambda b, h: (b, h)),
            pl.BlockSpec((SEQ, HEAD_DIM), lambda b, h: (b, k_blk0 + h)),
            pl.BlockSpec((SEQ, HEAD_DIM), lambda b, h: (b, v_blk0 + h)),
            pl.BlockSpec((SEQ, gw), lambda b, h: (b, g_blk0 + h)),
            pl.BlockSpec((CTX_LEN, HEAD_DIM), lambda b, h: (b, h)),
            pl.BlockSpec((CTX_LEN, HEAD_DIM), lambda b, h: (b, N_KV_HEADS + h)),
        ],
        out_specs=pl.BlockSpec((SEQ, gw), lambda b, h: (b, h)),
        out_shape=jax.ShapeDtypeStruct((batch * SEQ, ATTN_WIDTH), BF16),
        scratch_shapes=[pltpu.VMEM((HEAD_DIM + ONES_ROWS, SEQ), BF16),
                        pltpu.VMEM((HEAD_DIM + ONES_ROWS, CTX_LEN), BF16),
                        pltpu.VMEM((3, 3 * BLOCK, GQA_GROUP * BLOCK), F32)],
        compiler_params=_cparams(("arbitrary", "arbitrary")),
        name="banded_attention",
    )(sink, px, px, px, px, ckv, ckv)


def _out_proj_kernel(*refs, final):
    if final:
        a_ref, w_ref, x_ref, gate_ref, fg_ref, o_ref = refs
    else:
        a_ref, w_ref, x_ref, gate_ref, ng_ref, sh_ref, sc_ref, o_ref, hx_ref = refs
    acc = jnp.dot(a_ref[...], w_ref[...], preferred_element_type=F32)
    y = x_ref[...] + gate_ref[0] * acc
    if final:
        ms = jnp.mean(y * y, axis=-1, keepdims=True)
        y = y * lax.rsqrt(ms + NORM_EPS) * fg_ref[...]
    else:
        gain = ng_ref[...] * (1.0 + sc_ref[0])
        hx_ref[...] = _norm_mod_rows(y, gain, sh_ref[0]).astype(hx_ref.dtype)
    o_ref[...] = y


def _out_proj(a, w, x, mods, layer, *, final_g=None, next_norm_g=None, tm=512):
    m, d = x.shape
    kdim = a.shape[1]
    final = final_g is not None
    row_tile = pl.BlockSpec((tm, d), lambda i: (i, 0))
    vec = pl.BlockSpec((1, d), lambda i: (0, 0))
    in_specs = [
        pl.BlockSpec((tm, kdim), lambda i: (i, 0)),
        pl.BlockSpec((kdim, d), lambda i: (0, 0)),
        row_tile,
        _mod_spec(_mod_row(layer, 0, 2), tm, SEQ, d),
    ]
    args = [a, w, x, mods]
    out_specs = [row_tile]
    out_shape = [jax.ShapeDtypeStruct((m, d), F32)]
    if final:
        in_specs.append(vec)
        args.append(final_g.reshape(1, d))
    else:
        in_specs += [vec, _mod_spec(_mod_row(layer + 1, 0, 0), tm, SEQ, d),
                     _mod_spec(_mod_row(layer + 1, 0, 1), tm, SEQ, d)]
        args += [next_norm_g.reshape(1, d), mods, mods]
        out_specs.append(row_tile)
        out_shape.append(jax.ShapeDtypeStruct((m, d), BF16))
    return pl.pallas_call(
        functools.partial(_out_proj_kernel, final=final),
        grid=(m // tm,),
        in_specs=in_specs,
        out_specs=out_specs,
        out_shape=out_shape,
        compiler_params=_cparams(("arbitrary",)),
        name="out_proj",
    )(*args)


HALF = SEQ // 2
QUARTER = SEQ // 4
FLIP_BLOCK = 256


def _fold_tables():
    def tab(f, n):
        t2 = 2 * np.arange(n, dtype=np.int64) + 1
        ang = ((f[:, None] * t2[None, :]) % (2 * FFT_N)).astype(np.float64) * (math.pi / FFT_N)
        return np.cos(ang), np.sin(ang)

    r = np.arange(QUARTER, dtype=np.int64)
    g = np.arange(HALF, dtype=np.int64)
    f_ee, f_eo, f_o = 4 * r, 4 * r + 2, 2 * g + 1
    cee, see = tab(f_ee, QUARTER)
    ceo, seo = tab(f_eo, QUARTER)
    co, so = tab(f_o, HALF)
    see[0, :] = np.where(np.arange(QUARTER) % 2 == 0, 1.0, -1.0)
    const = lambda a, dt: jnp.asarray(np.ascontiguousarray(a).astype(np.float32)).astype(dt)
    phase = lambda f: np.stack([np.cos(f * (math.pi / FFT_N)), np.sin(f * (math.pi / FFT_N))])
    return dict(
        fq=const(np.stack([cee, see, ceo, seo]), BF16),
        fo=const(np.stack([co, so]), BF16),
        iq=const(np.stack([cee.T, ceo.T, see.T, seo.T]), BF16),
        io=const(np.stack([so.T, co.T]), BF16),
        anti=const(np.eye(FLIP_BLOCK)[::-1], BF16),
        pq=const(np.stack([phase(f_ee), phase(f_eo)], axis=1)[..., None], F32),
        po=const(phase(f_o)[..., None], F32),
    )


def _flip_rows(h, anti):
    nb = h.shape[0] // FLIP_BLOCK
    return jnp.concatenate(
        [jnp.dot(anti[...], h[(nb - 1 - a) * FLIP_BLOCK:(nb - a) * FLIP_BLOCK, :], preferred_element_type=F32)
         for a in range(nb)], axis=0)


def _fold_two_levels(x, anti):
    q0, q1, q2, q3 = (x[k * QUARTER:(k + 1) * QUARTER] for k in range(4))
    f1, f2, f3 = (_flip_rows(q, anti) for q in (q1, q2, q3))
    q0, q1, q2 = (q.astype(F32) for q in (q0, q1, q2))
    xs0, xa0 = q0 + f3, q0 - f3
    fxs1, fxa1 = f1 + q2, f1 - q2
    bf = lambda v: v.astype(BF16)
    xs = jnp.concatenate([bf(xs0), bf(q1 + f2)], axis=0)
    xa = jnp.concatenate([bf(xa0), bf(q1 - f2)], axis=0)
    return xs, xa, bf(xs0 + fxs1), bf(xs0 - fxs1), bf(xa0 + fxa1), bf(xa0 - fxa1)


def _staggered(n, matmuls, finish):
    matmuls(0, 0)
    for t in range(1, n):
        matmuls(t, t % 2)
        finish(t - 1, (t - 1) % 2)
    finish(n - 1, (n - 1) % 2)


def _col_chunks(n):
    return [slice(c, c + MXU_COLS) for c in range(0, n, MXU_COLS)]


PLANES = {"a_ee": 0, "b_ee": QUARTER, "a_eo": 2 * QUARTER, "b_eo": 3 * QUARTER,
          "a_o": 2 * HALF, "b_o": 3 * HALF}
GROUPS = (("a_ee", "b_ee", QUARTER), ("a_eo", "b_eo", QUARTER), ("a_o", "b_o", HALF))


def _spectrum(fq_ref, fo_ref, x, anti):
    xs, xa, xss, xsa, xas, xaa = _fold_two_levels(x, anti)
    dq = lambda k, v: jnp.dot(fq_ref[k], v, preferred_element_type=F32)
    do = lambda k, v: jnp.dot(fo_ref[k], v, preferred_element_type=F32)
    return {"a_ee": dq(0, xss), "b_ee": dq(1, xaa), "a_eo": dq(2, xsa), "b_eo": dq(3, xas),
            "a_o": do(0, xa), "b_o": do(1, xs)}


def _plane(ref, lead, name, rows, cols):
    r0 = PLANES[name]
    return ref[lead, r0:r0 + rows, cols]


def _resident(shape):
    zeros = (0,) * len(shape)
    return pl.BlockSpec(shape, lambda *_: zeros, pipeline_mode=pl.Buffered(1))


def _pack_filter_mlp(w1, b1, w2, b2, freq):
    hpad = LANES - FILTER_HIDDEN
    bands = (FILTER_EMB - 1) // 2
    row = lambda v: jnp.pad(v.reshape(1, -1), ((0, 7), (0, hpad)))
    mat = lambda m: jnp.pad(m, ((0, LANES - m.shape[0]), (0, hpad)))
    return jnp.concatenate([row(w1[0]), row(b1), row(b2), row(freq), jnp.zeros((LANES - 32, LANES), F32),
                            mat(w1[1:1 + bands]), mat(w1[1 + bands:]), mat(w2)], axis=0)


def _filter_sd_body(fp_ref, w3f_ref, b3f_ref, w3b_ref, b3b_ref, d_ref, o_ref, hid_ref, *, first, j):
    length = o_ref.shape[1]
    tn = o_ref.shape[2]
    hp = lax.Precision.HIGHEST
    w1t_ref, b1_ref, b2_ref, fq_ref = (fp_ref.at[r:r + 1] for r in (0, 8, 16, 24))
    w1c_ref, w1s_ref, w2_ref = (fp_ref.at[r:r + LANES] for r in (LANES, 2 * LANES, 3 * LANES))

    @pl.when(first)
    def _():
        n = lax.broadcasted_iota(jnp.int32, (length, LANES), 0).astype(F32)
        bands = (FILTER_EMB - 1) // 2
        fr_step = (bands - 1 - 1e-4) / (bands - 1)

        def angles(rows, stride):
            r = lax.broadcasted_iota(jnp.int32, (rows, LANES), 0).astype(F32)
            lane = lax.broadcasted_iota(jnp.int32, (rows, LANES), 1)
            fr = jnp.where(lane < bands, 1e-4 + lane.astype(F32) * fr_step, 0.0)
            return (2.0 * math.pi * stride * r / length) * fr

        outer, inner = angles(length // TRIG_INNER, TRIG_INNER), angles(TRIG_INNER, 1)
        c1, s1 = jnp.cos(outer)[:, None, :], jnp.sin(outer)[:, None, :]
        c0, s0 = jnp.cos(inner)[None, :, :], jnp.sin(inner)[None, :, :]
        cos_ang = (c1 * c0 - s1 * s0).reshape(length, LANES)
        sin_ang = (s1 * c0 + c1 * s0).reshape(length, LANES)
        t = n / (length - 1)
        fq = fq_ref[...]
        pre = (t * w1t_ref[...]
               + jnp.dot(cos_ang, w1c_ref[...], precision=hp, preferred_element_type=F32)
               + jnp.dot(-sin_ang, w1s_ref[...], precision=hp, preferred_element_type=F32)
               + b1_ref[...])
        hid = jnp.sin(fq * pre)
        hid = jnp.sin(fq * (jnp.dot(hid, w2_ref[...], precision=hp, preferred_element_type=F32)
                            + b2_ref[...]))
        hid_hi = hid.astype(BF16)
        hid_ref[0] = hid_hi
        hid_ref[1] = (hid - hid_hi.astype(F32)).astype(BF16)

    def dot3(w_ref):
        wf = w_ref[...]
        wf = jnp.concatenate([wf, jnp.zeros((LANES - wf.shape[0], wf.shape[1]), F32)], axis=0)
        w_hi = wf.astype(BF16)
        w_lo = (wf - w_hi.astype(F32)).astype(BF16)
        return (jnp.dot(hid_ref[0], w_hi, preferred_element_type=F32)
                + jnp.dot(hid_ref[1], w_hi, preferred_element_type=F32)
                + jnp.dot(hid_ref[0], w_lo, preferred_element_type=F32))

    row = lax.broadcasted_iota(jnp.int32, (length, tn), 0)
    chan = (lax.broadcasted_iota(jnp.int32, (length, tn), 1) + j * tn).astype(F32)
    min_decay = math.log(DECAY_TARGET) / DECAY_SLOW
    max_decay = math.log(DECAY_TARGET) / DECAY_FAST
    delta = min_decay + chan * ((max_decay - min_decay) / (HYENA_WIDTH - 1))
    t = row.astype(F32) / (length - 1)
    window = jnp.exp(-t * jnp.abs(delta)) + WINDOW_SHIFT
    hf = (dot3(w3f_ref) + b3f_ref[...]) * window
    hb = (dot3(w3b_ref) + b3b_ref[...]) * window
    hf = jnp.where(row == 0, hf + d_ref[0], hf)
    hb = jnp.where(row == 0, 0.0, hb)
    o_ref[0] = (hf + hb).astype(o_ref.dtype)
    o_ref[1] = (hf - hb).astype(o_ref.dtype)


def _prep_kernel(c_ref, aw_ref, ab_ref, *refs, filter_steps, ct):
    wa_ref, wh_ref = refs[-7:-5]
    mods_ref, sd_ref, wa_bf_ref, wh_bf_ref, hid_ref = refs[-5:]
    s = pl.program_id(0)
    _mods_kernel(c_ref, aw_ref, ab_ref, mods_ref)

    @pl.when(s < filter_steps)
    def _():
        _filter_sd_body(*refs[:-7], sd_ref, hid_ref, first=s == 0, j=s % ct)

    @pl.when(s < CAST_STEPS)
    def _():
        wa_bf_ref[...] = wa_ref[...].astype(BF16)

    @pl.when((s >= CAST_STEPS) & (s < 2 * CAST_STEPS))
    def _():
        wh_bf_ref[...] = wh_ref[...].astype(BF16)


CAST_STEPS = 8
TRIG_INNER = 64


def _prep(cc, ada_w, ada_b, w1, b1, w2, b2, w3, b3, freq, bias_d, w_out_a, w_out_h, *,
          tn_mods=512, tn=256):
    depth, d, n_mods = ada_w.shape
    mt = n_mods // tn_mods
    w = HYENA_WIDTH
    ct = w // tn
    filter_steps = 2 * ct
    assert filter_steps <= depth * mt
    n = w3.shape[1]
    fp = _pack_filter_mlp(w1, b1, w2, b2, freq)
    b3r = b3.reshape(1, n)
    order = lambda s: jnp.minimum(s, filter_steps - 1) // ct
    chan = lambda s: jnp.minimum(s, filter_steps - 1) % ct
    fwd_cols = lambda s: (0, 2 * order(s) * ct + chan(s))
    bwd_cols = lambda s: (0, (2 * order(s) + 1) * ct + chan(s))
    assert w_out_a.shape == w_out_h.shape and 2 * CAST_STEPS <= depth * mt
    cast_blk = (w_out_a.shape[0] // CAST_STEPS, w_out_a.shape[1])
    first_rows = lambda s: (jnp.minimum(s, CAST_STEPS - 1), 0)
    second_rows = lambda s: (jnp.clip(s - CAST_STEPS, 0, CAST_STEPS - 1), 0)
    return pl.pallas_call(
        functools.partial(_prep_kernel, filter_steps=filter_steps, ct=ct),
        grid=(depth * mt,),
        in_specs=[pl.BlockSpec((MOD_ROWS, d), lambda s: (0, 0)),
                  pl.BlockSpec((1, d, tn_mods), lambda s: (s // mt, 0, s % mt)),
                  pl.BlockSpec((1, 1, tn_mods), lambda s: (s // mt, 0, s % mt)),
                  pl.BlockSpec(fp.shape, lambda s: (0, 0)),
                  pl.BlockSpec((FILTER_HIDDEN, tn), fwd_cols), pl.BlockSpec((1, tn), fwd_cols),
                  pl.BlockSpec((FILTER_HIDDEN, tn), bwd_cols), pl.BlockSpec((1, tn), bwd_cols),
                  pl.BlockSpec((1, 1, tn), lambda s: (order(s), 0, chan(s))),
                  pl.BlockSpec(cast_blk, first_rows), pl.BlockSpec(cast_blk, second_rows)],
        out_specs=[pl.BlockSpec((1, MOD_ROWS, tn_mods), lambda s: (s // mt, 0, s % mt)),
                   pl.BlockSpec((2, SEQ, tn), lambda s: (0, 0, order(s) * ct + chan(s))),
                   pl.BlockSpec(cast_blk, first_rows), pl.BlockSpec(cast_blk, second_rows)],
        out_shape=[jax.ShapeDtypeStruct((depth, MOD_ROWS, n_mods), F32),
                   jax.ShapeDtypeStruct((2, SEQ, 2 * w), BF16),
                   jax.ShapeDtypeStruct(w_out_a.shape, BF16),
                   jax.ShapeDtypeStruct(w_out_h.shape, BF16)],
        scratch_shapes=[pltpu.VMEM((2, SEQ, LANES), BF16)],
        compiler_params=_cparams(("arbitrary",)),
        name="mods_and_filters",
    )(cc, ada_w, ada_b.reshape(depth, 1, n_mods),
      fp, w3, b3r, w3, b3r, bias_d.reshape(2, 1, w),
      w_out_a, w_out_h)


def _filter_spec_kernel(s_ref, d_ref, fq_ref, fo_ref, anti_ref, pq_ref, po_ref, k_ref):
    phases = {"a_ee": (pq_ref[0, 0], pq_ref[1, 0]), "a_eo": (pq_ref[0, 1], pq_ref[1, 1]),
              "a_o": (po_ref[0], po_ref[1])}
    is0 = lax.broadcasted_iota(jnp.int32, (QUARTER, MXU_COLS), 0) == 0
    for cols in _col_chunks(k_ref.shape[2]):
        sp = _spectrum(fq_ref, fo_ref, s_ref[0, :, cols], anti_ref)
        dp = _spectrum(fq_ref, fo_ref, d_ref[0, :, cols], anti_ref)
        for a, b, rows in GROUPS:
            cos, sin = phases[a]
            ka = sp[a] * cos + sp[b] * sin
            kb = dp[b] * cos - dp[a] * sin
            if a == "a_ee":
                scale = jnp.where(is0, 1.0 / FFT_N, 2.0 / FFT_N)
                kb = jnp.where(is0, sp[b], kb)
            else:
                scale = 2.0 / FFT_N
            k_ref[0, PLANES[a]:PLANES[a] + rows, cols] = (ka * scale).astype(k_ref.dtype)
            k_ref[0, PLANES[b]:PLANES[b] + rows, cols] = (kb * scale).astype(k_ref.dtype)


def _filter_spectra(sd, tabs, *, tn=512):
    w = HYENA_WIDTH
    ct = w // tn
    consts = [tabs[k] for k in ("fq", "fo", "anti", "pq", "po")]
    return pl.pallas_call(
        _filter_spec_kernel,
        grid=(2, ct),
        in_specs=[
            pl.BlockSpec((1, SEQ, tn), lambda o, j: (0, 0, o * ct + j)),
            pl.BlockSpec((1, SEQ, tn), lambda o, j: (1, 0, o * ct + j)),
        ] + [_resident(c.shape) for c in consts],
        out_specs=pl.BlockSpec((1, 4 * HALF, tn), lambda o, j: (o, 0, j)),
        out_shape=jax.ShapeDtypeStruct((2, 4 * HALF, w), F32),
        compiler_params=_cparams(("arbitrary", "arbitrary")),
        name="filter_spectra",
    )(sd, sd, *consts)


def _conv_fwd_kernel(x_ref, fq_ref, fo_ref, anti_ref, k_ref, y_ref):
    is0 = lax.broadcasted_iota(jnp.int32, (QUARTER, MXU_COLS), 0) == 0
    for cols in _col_chunks(y_ref.shape[2]):
        sp = _spectrum(fq_ref, fo_ref, x_ref[:, cols], anti_ref)
        for a, b, rows in GROUPS:
            ka = _plane(k_ref, 0, a, rows, cols).astype(F32)
            kb = _plane(k_ref, 0, b, rows, cols).astype(F32)
            bkb = sp[b] * kb
            if a == "a_ee":
                ya = sp[a] * ka - jnp.where(is0, 0.0, bkb)
                yb = jnp.where(is0, bkb, sp[a] * kb + sp[b] * ka)
            else:
                ya = sp[a] * ka - bkb
                yb = sp[a] * kb + sp[b] * ka
            y_ref[0, PLANES[a]:PLANES[a] + rows, cols] = ya.astype(y_ref.dtype)
            y_ref[0, PLANES[b]:PLANES[b] + rows, cols] = yb.astype(y_ref.dtype)


def _conv_fwd(x, x_blk0, kspec, order, tabs, batch, *, tn=512):
    w = HYENA_WIDTH
    ct = w // tn
    consts = [tabs[k] for k in ("fq", "fo", "anti")]
    return pl.pallas_call(
        _conv_fwd_kernel,
        grid=(ct, batch),
        in_specs=[pl.BlockSpec((SEQ, tn), lambda j, b: (b, x_blk0 * ct + j))]
        + [_resident(c.shape) for c in consts]
        + [pl.BlockSpec((1, 4 * HALF, tn), lambda j, b: (order, 0, j))],
        out_specs=pl.BlockSpec((1, 4 * HALF, tn), lambda j, b: (b, 0, j)),
        out_shape=jax.ShapeDtypeStruct((batch, 4 * HALF, w), BF16),
        compiler_params=_cparams(("arbitrary", "arbitrary")),
        name="long_conv_fwd",
    )(x, *consts, kspec)


def _conv_inv_kernel(*refs, gated):
    if gated:
        y_ref, iq_ref, io_ref, anti_ref, xm_ref, g_ref, o_ref, accq_ref, acco_ref = refs
    else:
        y_ref, iq_ref, io_ref, anti_ref, xm_ref, o_ref, accq_ref, acco_ref = refs
    chunks = _col_chunks(o_ref.shape[1])
    flipped = lambda v: _flip_rows(v.astype(BF16), anti_ref)

    def matmuls(t, slot):
        cols = chunks[t]
        for k, name in enumerate(("a_ee", "a_eo", "b_ee", "b_eo")):
            accq_ref[slot, k] = jnp.dot(iq_ref[k], _plane(y_ref, 0, name, QUARTER, cols),
                                        preferred_element_type=F32)
        for k, name in enumerate(("b_o", "a_o")):
            acco_ref[slot, k] = jnp.dot(io_ref[k], _plane(y_ref, 0, name, HALF, cols),
                                        preferred_element_type=F32)

    def finish(t, slot):
        cols = chunks[t]
        p, r, q, s = (accq_ref[slot, k] for k in range(4))
        osum = acco_ref[slot, 0] + acco_ref[slot, 1]
        odif = acco_ref[slot, 0] - acco_ref[slot, 1]
        e1, e2, d1, d2 = p + r, q + s, p - r, s - q
        quarters = (e1 + e2 + osum[:QUARTER],
                    flipped(d1 + d2) + osum[QUARTER:],
                    d1 - d2 + flipped(odif[QUARTER:]),
                    flipped(e1 - e2 + odif[:QUARTER]))
        for k, y in enumerate(quarters):
            rows = slice(k * QUARTER, (k + 1) * QUARTER)
            mult = xm_ref[rows, cols]
            if gated:
                mult = mult * g_ref[rows, cols]
            o_ref[rows, cols] = (y * mult.astype(F32)).astype(o_ref.dtype)

    _staggered(len(chunks), matmuls, finish)


def _conv_inv(y, tabs, p, m_blk0, g_blk0=None, *, tn=512):
    batch = y.shape[0]
    w = HYENA_WIDTH
    ct = w // tn
    gated = g_blk0 is not None
    col = lambda blk0: (lambda b, j: (b, blk0 * ct + j))
    consts = [tabs[k] for k in ("iq", "io", "anti")]
    in_specs = ([pl.BlockSpec((1, 4 * HALF, tn), lambda b, j: (b, 0, j))]
                + [_resident(c.shape) for c in consts]
                + [pl.BlockSpec((SEQ, tn), col(m_blk0))])
    args = [y, *consts, p]
    if gated:
        in_specs.append(pl.BlockSpec((SEQ, tn), col(g_blk0)))
        args.append(p)
    return pl.pallas_call(
        functools.partial(_conv_inv_kernel, gated=gated),
        grid=(batch, ct),
        in_specs=in_specs,
        out_specs=pl.BlockSpec((SEQ, tn), lambda b, j: (b, j)),
        out_shape=jax.ShapeDtypeStruct((batch * SEQ, w), BF16),
        scratch_shapes=[pltpu.VMEM((2, 4, QUARTER, MXU_COLS), F32),
                        pltpu.VMEM((2, 2, HALF, MXU_COLS), F32)],
        compiler_params=_cparams(("arbitrary", "arbitrary")),
        name="long_conv_inv",
    )(*args)


def _rope_tables():
    pos = np.arange(SEQ)
    row = (pos // GRID_W).astype(np.float32)
    col = (pos % GRID_W).astype(np.float32)
    half = HEAD_DIM // 2
    inv = (ROPE_BASE ** (-np.arange(0, half, 2, dtype=np.float32) / half)).astype(np.float32)
    ar = row[:, None] * inv[None]
    ac = col[:, None] * inv[None]
    cos = np.concatenate([np.cos(ar), np.cos(ar), np.cos(ac), np.cos(ac)], axis=1)
    sin = np.concatenate([-np.sin(ar), np.sin(ar), -np.sin(ac), np.sin(ac)], axis=1)
    return jnp.asarray(cos, F32), jnp.asarray(sin, F32)


def _tile_types(tn, kinds):
    per_tile = tn // LANES
    return [kinds[s0:s0 + per_tile] for s0 in range(0, len(kinds), per_tile)]


def kernel(x, c, ctx, c_ctx, norm_g, ada_w, ada_b, attn_w_in, attn_w_out, attn_sink, hy_w_in,
           hy_conv_w, hy_conv_b, hy_w1, hy_b1, hy_w2, hy_b2, hy_w3, hy_b3, hy_freq, hy_bias_d,
           hy_w_out, final_g):
    batch, seq, d = x.shape
    assert (seq, d) == (SEQ, D_MODEL) and ctx.shape[1] == CTX_LEN
    assert norm_g.shape[0] == 2 and attn_w_in.shape[0] == 1 and hy_w_in.shape[0] == 1
    w = HYENA_WIDTH

    ctx_row = batch
    cc = jnp.concatenate([c, c_ctx[None], jnp.zeros((MOD_ROWS - batch - 1, d), F32)], axis=0)
    mods, sd, w_out_attn, w_out_hyena = _prep(
        cc, ada_w, ada_b, hy_w1[0], hy_b1[0], hy_w2[0], hy_b2[0], hy_w3[0], hy_b3[0], hy_freq[0],
        hy_bias_d[0], attn_w_out[0], hy_w_out[0])
    mods = mods.reshape(-1, 1, d)

    x2 = x.reshape(batch * seq, d)
    ctx2 = ctx.reshape(batch * CTX_LEN, d)

    tn = 1024
    kinds0 = (["q"] * N_HEADS + ["k"] * N_KV_HEADS + [None] * N_KV_HEADS
              + ["g"] * (ATTN_WIDTH // LANES))
    hx = _norm_mod(x2, norm_g[0], mods, 0, tm=1024)
    px = _proj(hx, attn_w_in[0], 0, ATTN_IN, tm=SEQ, tn=tn, tile_types=_tile_types(tn, kinds0),
               rope=_rope_tables(), q_scale=HEAD_DIM ** -0.5 * LOG2E)
    ckv = _ctx_kv(ctx2, norm_g[0], mods, 0, ctx_row, attn_w_in[0], ATTN_WIDTH, 2 * KV_WIDTH)
    og = _attention(px, ckv, attn_sink[0], batch)
    x2, hx = _out_proj(og, w_out_attn, x2, mods, 0, next_norm_g=norm_g[1])

    kinds1 = ["c"] * (3 * w // LANES) + ["g"] * (w // LANES)
    tn1 = 1024
    p = _proj(hx, hy_w_in[0], 0, 4 * w, tm=SEQ, tn=tn1, tile_types=_tile_types(tn1, kinds1),
              conv=(hy_conv_w[0], hy_conv_b[0].reshape(1, 3 * w)))
    tabs = _fold_tables()
    kspec = _filter_spectra(sd, tabs)
    y1 = _conv_fwd(p, 2, kspec, 0, tabs, batch)
    z = _conv_inv(y1, tabs, p, 0)
    y2 = _conv_fwd(z, 0, kspec, 1, tabs, batch)
    yg = _conv_inv(y2, tabs, p, 1, g_blk0=3)
    out, = _out_proj(yg, w_out_hyena, x2, mods, 1, final_g=final_g)
    return out.reshape(batch, seq, d)
```

```python
import functools
import math

import jax
import jax.numpy as jnp
import numpy as np
from jax import lax
from jax.experimental import pallas as pl
from jax.experimental.pallas import tpu as pltpu

F32 = jnp.float32
BF16 = jnp.bfloat16

D_MODEL = 2048
SEQ = 2048
CTX_LEN = 256
GRID_W = 64
HEAD_DIM = 128
N_HEADS = 16
N_KV_HEADS = 4
GQA_GROUP = 4
ATTN_WIDTH = 2048
KV_WIDTH = 512
ATTN_IN = 2 * ATTN_WIDTH + 2 * KV_WIDTH
WINDOW = 128
BLOCK = 128
ROPE_BASE = 10000.0
HYENA_WIDTH = 2048
FILTER_EMB = 33
FILTER_HIDDEN = 64
DECAY_FAST = 0.3
DECAY_SLOW = 1.5
DECAY_TARGET = 1e-2
WINDOW_SHIFT = 0.05
NORM_EPS = 1e-6
NEG_INF = -1e30

LANES = 128
MOD_ROWS = 8
MXU_COLS = 256
ACC_ROWS = 1024
ONES_ROWS = 16
LOG2E = math.log2(math.e)
FFT_N = 2 * SEQ
VMEM_LIMIT = 56 * 1024 * 1024


def _cparams(sem):
    return pltpu.CompilerParams(dimension_semantics=sem, vmem_limit_bytes=VMEM_LIMIT)


def _mods_kernel(c_ref, w_ref, b_ref, o_ref):
    c = c_ref[...]
    s = c * jax.nn.sigmoid(c)
    s_hi = s.astype(BF16)
    s_lo = (s - s_hi.astype(F32)).astype(BF16)
    lhs = jnp.concatenate([s_hi, s_lo], axis=0)
    r = jnp.dot(lhs, w_ref[0].astype(BF16), preferred_element_type=F32)
    o_ref[0] = r[:MOD_ROWS] + r[MOD_ROWS:] + b_ref[0]


def _rope_slab(t, cos, sin):
    lane = lax.broadcasted_iota(jnp.int32, t.shape, 1)
    first = (lane % 64) < 32
    partner = jnp.where(first, pltpu.roll(t, 96, 1), pltpu.roll(t, 32, 1))
    return t * cos + partner * sin


def _norm_mod_rows(x, gain, add):
    ms = jnp.mean(x * x, axis=-1, keepdims=True)
    return x * lax.rsqrt(ms + NORM_EPS) * gain + add


def _norm_mod_kernel(x_ref, g_ref, sh_ref, sc_ref, o_ref, *, row_chunk):
    gain = g_ref[...] * (1.0 + sc_ref[0])
    add = sh_ref[0]
    for r in range(0, x_ref.shape[0], row_chunk):
        o_ref[r:r + row_chunk, :] = _norm_mod_rows(x_ref[r:r + row_chunk, :], gain, add).astype(o_ref.dtype)


def _mod_row(layer, row, kind):
    return (layer * MOD_ROWS + row) * 3 + kind


def _mod_spec(base, tm, rows_per_mod, d):
    return pl.BlockSpec((1, 1, d), lambda i: (base + 3 * ((i * tm) // rows_per_mod), 0, 0))


def _norm_mod(x, g, mods, layer, *, tm):
    m, d = x.shape
    return pl.pallas_call(
        functools.partial(_norm_mod_kernel, row_chunk=min(tm, 256)),
        grid=(m // tm,),
        in_specs=[pl.BlockSpec((tm, d), lambda i: (i, 0)), pl.BlockSpec((1, d), lambda i: (0, 0)),
                  _mod_spec(_mod_row(layer, 0, 0), tm, SEQ, d),
                  _mod_spec(_mod_row(layer, 0, 1), tm, SEQ, d)],
        out_specs=pl.BlockSpec((tm, d), lambda i: (i, 0)),
        out_shape=jax.ShapeDtypeStruct((m, d), BF16),
        compiler_params=_cparams(("arbitrary",)),
        name="norm_mod",
    )(x, g.reshape(1, d), mods, mods)


def _ctx_kv_kernel(x_ref, g_ref, sh_ref, sc_ref, w_ref, o_ref, hx_ref, *, row_chunk):
    gain = g_ref[...] * (1.0 + sc_ref[0])
    add = sh_ref[0]
    for r in range(0, x_ref.shape[0], row_chunk):
        hx_ref[r:r + row_chunk, :] = _norm_mod_rows(x_ref[r:r + row_chunk, :], gain, add).astype(BF16)
    wb = w_ref[...].astype(BF16)
    for c0 in range(0, o_ref.shape[1], MXU_COLS):
        o_ref[:, c0:c0 + MXU_COLS] = jnp.dot(hx_ref[...], wb[:, c0:c0 + MXU_COLS],
                                             preferred_element_type=F32).astype(o_ref.dtype)


def _ctx_kv(ctx, g, mods, layer, mod_row, w, col0, n):
    m, d = ctx.shape
    assert col0 % n == 0 and m <= ACC_ROWS
    shift = _mod_row(layer, mod_row, 0)
    scale = _mod_row(layer, mod_row, 1)
    return pl.pallas_call(
        functools.partial(_ctx_kv_kernel, row_chunk=256),
        grid=(1,),
        in_specs=[pl.BlockSpec((m, d), lambda i: (0, 0)), pl.BlockSpec((1, d), lambda i: (0, 0)),
                  pl.BlockSpec((1, 1, d), lambda i: (shift, 0, 0)),
                  pl.BlockSpec((1, 1, d), lambda i: (scale, 0, 0)),
                  pl.BlockSpec((d, n), lambda i: (0, col0 // n))],
        out_specs=pl.BlockSpec((m, n), lambda i: (0, 0)),
        out_shape=jax.ShapeDtypeStruct((m, n), BF16),
        scratch_shapes=[pltpu.VMEM((m, d), BF16)],
        compiler_params=_cparams(("arbitrary",)),
        name="ctx_kv",
    )(ctx, g.reshape(1, d), mods, mods, w)


def _proj_kernel(*refs, tile_types, has_rope, has_conv, q_scale):
    if has_rope:
        hx_ref, w_ref, cos_ref, sin_ref, o_ref, wb_ref = refs
    elif has_conv:
        hx_ref, w_ref, cw_ref, cb_ref, o_ref, wb_ref = refs
    else:
        hx_ref, w_ref, o_ref, wb_ref = refs
    j = pl.program_id(0)
    tm, tn = o_ref.shape

    @pl.when(pl.program_id(1) == 0)
    def _():
        wb_ref[...] = w_ref[...].astype(BF16)

    w_ref = wb_ref

    def plain():
        for c0 in range(0, tn, MXU_COLS):
            o_ref[:, c0:c0 + MXU_COLS] = chunk_dot(c0).astype(o_ref.dtype)

    def chunk_dot(c0):
        pieces = [jnp.dot(hx_ref[r:r + ACC_ROWS, :], w_ref[:, c0:c0 + MXU_COLS],
                          preferred_element_type=F32) for r in range(0, tm, ACC_ROWS)]
        return pieces[0] if len(pieces) == 1 else jnp.concatenate(pieces, axis=0)

    def gated():
        for c0 in range(0, tn, MXU_COLS):
            half = 0.5 * chunk_dot(c0)
            o_ref[:, c0:c0 + MXU_COLS] = (half + half * jnp.tanh(half)).astype(o_ref.dtype)

    def short_conv():
        for c0 in range(0, tn, MXU_COLS):
            acc = chunk_dot(c0)
            for h in range(MXU_COLS // LANES):
                lo = c0 + h * LANES
                u = acc[:, h * LANES:(h + 1) * LANES]
                w3 = cw_ref[:, lo:lo + LANES]
                b = cb_ref[:, lo:lo + LANES]
                y = pltpu.roll(u, 1, 0) * w3[0:1] + u * w3[1:2] + pltpu.roll(u, tm - 1, 0) * w3[2:3] + b
                o_ref[:, lo:lo + LANES] = y.astype(o_ref.dtype)
                first = u[0:1] * w3[1:2] + u[1:2] * w3[2:3] + b
                last = u[tm - 2:tm - 1] * w3[0:1] + u[tm - 1:tm] * w3[1:2] + b
                o_ref[0:1, lo:lo + LANES] = first.astype(o_ref.dtype)
                o_ref[tm - 1:tm, lo:lo + LANES] = last.astype(o_ref.dtype)

    def roped(types):
        tables = {"k": (cos_ref[...], sin_ref[...])}
        if "q" in types:
            tables["q"] = (cos_ref[...] * q_scale, sin_ref[...] * q_scale)
        for c0 in range(0, tn, MXU_COLS):
            acc = chunk_dot(c0)
            for h in range(MXU_COLS // LANES):
                ty = types[c0 // LANES + h]
                slab = acc[:, h * LANES:(h + 1) * LANES]
                if ty is not None:
                    slab = _rope_slab(slab, *tables[ty])
                lo = c0 + h * LANES
                o_ref[:, lo:lo + LANES] = slab.astype(o_ref.dtype)

    if tile_types is None:
        plain()
        return

    groups = {}
    for t, types in enumerate(tile_types):
        groups.setdefault(tuple(types), []).append(t)
    for types, tiles in groups.items():
        cond = functools.reduce(jnp.logical_or, [j == t for t in tiles])
        if all(ty is None for ty in types):
            pl.when(cond)(plain)
        elif all(ty == "g" for ty in types):
            pl.when(cond)(gated)
        elif all(ty == "c" for ty in types):
            pl.when(cond)(short_conv)
        else:
            assert "g" not in types and "c" not in types
            pl.when(cond)(functools.partial(roped, types))


def _proj(hx, w, col_blk0, n, *, tm, tn, tile_types=None, rope=None, conv=None, q_scale=1.0):
    m, d = hx.shape
    in_specs = [
        pl.BlockSpec((tm, d), lambda j, i: (i, 0)),
        pl.BlockSpec((d, tn), lambda j, i: (0, col_blk0 + j)),
    ]
    args = [hx, w]
    scratch = [pltpu.VMEM((d, tn), BF16)]
    if rope is not None:
        cos, sin = rope
        seq_tiles = cos.shape[0] // tm
        in_specs += [pl.BlockSpec((tm, LANES), lambda j, i: (i % seq_tiles, 0))] * 2
        args += [cos, sin]
    if conv is not None:
        assert rope is None and tm == SEQ
        cw, cb = conv
        last = cw.shape[1] // tn - 1
        in_specs += [pl.BlockSpec((cw.shape[0], tn), lambda j, i: (0, jnp.minimum(j, last))),
                     pl.BlockSpec((1, tn), lambda j, i: (0, jnp.minimum(j, last)))]
        args += [cw, cb]
    return pl.pallas_call(
        functools.partial(_proj_kernel, tile_types=tile_types, has_rope=rope is not None,
                          has_conv=conv is not None, q_scale=q_scale),
        grid=(n // tn, m // tm),
        in_specs=in_specs,
        out_specs=pl.BlockSpec((tm, tn), lambda j, i: (i, j)),
        out_shape=jax.ShapeDtypeStruct((m, n), BF16),
        scratch_shapes=scratch,
        compiler_params=_cparams(("arbitrary", "arbitrary")),
        name="proj",
    )(*args)


def _attn_kernel(sink_ref, q_ref, k_ref, v_ref, g_ref, kc_ref, vc_ref, o_ref,
                 vt_ref, vct_ref, bias_ref):
    kh = pl.program_id(1)
    band = 3 * BLOCK
    cols = GQA_GROUP * BLOCK
    n_blocks = SEQ // BLOCK

    vt_ref[:HEAD_DIM, :] = v_ref[...].T
    vct_ref[:HEAD_DIM, :] = vc_ref[...].T

    @pl.when((pl.program_id(0) == 0) & (kh == 0))
    def _():
        vt_ref[HEAD_DIM:, :] = jnp.ones((ONES_ROWS, SEQ), BF16)
        vct_ref[HEAD_DIM:, :] = jnp.ones((ONES_ROWS, CTX_LEN), BF16)
        krow = lax.broadcasted_iota(jnp.int32, (band, cols), 0)
        qcol = lax.broadcasted_iota(jnp.int32, (band, cols), 1) % BLOCK
        for idx, off in enumerate((0, -BLOCK, -2 * BLOCK)):
            bias_ref[idx] = jnp.where(jnp.abs(krow - qcol + off) <= WINDOW, 0.0, NEG_INF)

    kc = kc_ref[...]
    vct = vct_ref[...]
    sink_row = jnp.concatenate(
        [jnp.full((1, BLOCK), sink_ref[kh * GQA_GROUP + h] * LOG2E, F32) for h in range(GQA_GROUP)],
        axis=1)

    def body(n, carry):
        q0 = pl.multiple_of(n * BLOCK, BLOCK)
        ks = pl.multiple_of(jnp.clip((n - 1) * BLOCK, 0, SEQ - band), BLOCK)
        bidx = jnp.where(n == 0, 0, jnp.where(n == n_blocks - 1, 2, 1))
        qs = q_ref[pl.ds(q0, BLOCK), :]
        q4t = jnp.concatenate([qs[:, h * LANES:(h + 1) * LANES].T for h in range(GQA_GROUP)],
                              axis=1)
        kb = k_ref[pl.ds(ks, band), :]
        s_loc = jnp.dot(kb, q4t, preferred_element_type=F32) + bias_ref[bidx]
        s_ctx = jnp.dot(kc, q4t, preferred_element_type=F32)
        m = jnp.maximum(jnp.maximum(jnp.max(s_loc, axis=0, keepdims=True),
                                    jnp.max(s_ctx, axis=0, keepdims=True)), sink_row)
        p_loc = jnp.exp2(s_loc - m).astype(BF16)
        p_ctx = jnp.exp2(s_ctx - m).astype(BF16)
        ox = (jnp.dot(vt_ref[:, pl.ds(ks, band)], p_loc, preferred_element_type=F32)
              + jnp.dot(vct, p_ctx, preferred_element_type=F32))
        den = ox[HEAD_DIM:HEAD_DIM + 1, :] + jnp.exp2(sink_row - m)
        ot = ox[:HEAD_DIM, :] * (1.0 / den)
        gs = g_ref[pl.ds(q0, BLOCK), :].astype(F32)
        for h in range(GQA_GROUP):
            oh = ot[:, h * LANES:(h + 1) * LANES].T * gs[:, h * LANES:(h + 1) * LANES]
            o_ref[pl.ds(q0, BLOCK), h * LANES:(h + 1) * LANES] = oh.astype(o_ref.dtype)
        return carry

    lax.fori_loop(0, n_blocks, body, 0, unroll=8)


def _attention(px, ckv, sink, batch):
    gw = GQA_GROUP * HEAD_DIM
    k_blk0 = ATTN_WIDTH // HEAD_DIM
    v_blk0 = (ATTN_WIDTH + KV_WIDTH) // HEAD_DIM
    g_blk0 = (ATTN_WIDTH + 2 * KV_WIDTH) // gw
    return pl.pallas_call(
        _attn_kernel,
        grid=(batch, N_KV_HEADS),
        in_specs=[
            pl.BlockSpec(memory_space=pltpu.SMEM),
            pl.BlockSpec((SEQ, gw), lambda b, h: (b, h)),
            pl.BlockSpec((SEQ, HEAD_DIM), lambda b, h: (b, k_blk0 + h)),
            pl.BlockSpec((SEQ, HEAD_DIM), lambda b, h: (b, v_blk0 + h)),
            pl.BlockSpec((SEQ, gw), lambda b, h: (b, g_blk0 + h)),
            pl.BlockSpec((CTX_LEN, HEAD_DIM), lambda b, h: (b, h)),
            pl.BlockSpec((CTX_LEN, HEAD_DIM), lambda b, h: (b, N_KV_HEADS + h)),
        ],
        out_specs=pl.BlockSpec((SEQ, gw), lambda b, h: (b, h)),
        out_shape=jax.ShapeDtypeStruct((batch * SEQ, ATTN_WIDTH), BF16),
        scratch_shapes=[pltpu.VMEM((HEAD_DIM + ONES_ROWS, SEQ), BF16),
                        pltpu.VMEM((HEAD_DIM + ONES_ROWS, CTX_LEN), BF16),
                        pltpu.VMEM((3, 3 * BLOCK, GQA_GROUP * BLOCK), F32)],
        compiler_params=_cparams(("arbitrary", "arbitrary")),
        name="banded_attention",
    )(sink, px, px, px, px, ckv, ckv)


def _out_proj_kernel(*refs, final):
    if final:
        a_ref, w_ref, x_ref, gate_ref, fg_ref, o_ref = refs
    else:
        a_ref, w_ref, x_ref, gate_ref, ng_ref, sh_ref, sc_ref, o_ref, hx_ref = refs
    acc = jnp.dot(a_ref[...], w_ref[...], preferred_element_type=F32)
    y = x_ref[...] + gate_ref[0] * acc
    if final:
        ms = jnp.mean(y * y, axis=-1, keepdims=True)
        y = y * lax.rsqrt(ms + NORM_EPS) * fg_ref[...]
    else:
        gain = ng_ref[...] * (1.0 + sc_ref[0])
        hx_ref[...] = _norm_mod_rows(y, gain, sh_ref[0]).astype(hx_ref.dtype)
    o_ref[...] = y


def _out_proj(a, w, x, mods, layer, *, final_g=None, next_norm_g=None, tm=512):
    m, d = x.shape
    kdim = a.shape[1]
    final = final_g is not None
    row_tile = pl.BlockSpec((tm, d), lambda i: (i, 0))
    vec = pl.BlockSpec((1, d), lambda i: (0, 0))
    in_specs = [
        pl.BlockSpec((tm, kdim), lambda i: (i, 0)),
        pl.BlockSpec((kdim, d), lambda i: (0, 0)),
        row_tile,
        _mod_spec(_mod_row(layer, 0, 2), tm, SEQ, d),
    ]
    args = [a, w, x, mods]
    out_specs = [row_tile]
    out_shape = [jax.ShapeDtypeStruct((m, d), F32)]
    if final:
        in_specs.append(vec)
        args.append(final_g.reshape(1, d))
    else:
        in_specs += [vec, _mod_spec(_mod_row(layer + 1, 0, 0), tm, SEQ, d),
                     _mod_spec(_mod_row(layer + 1, 0, 1), tm, SEQ, d)]
        args += [next_norm_g.reshape(1, d), mods, mods]
        out_specs.append(row_tile)
        out_shape.append(jax.ShapeDtypeStruct((m, d), BF16))
    return pl.pallas_call(
        functools.partial(_out_proj_kernel, final=final),
        grid=(m // tm,),
        in_specs=in_specs,
        out_specs=out_specs,
        out_shape=out_shape,
        compiler_params=_cparams(("arbitrary",)),
        name="out_proj",
    )(*args)


HALF = SEQ // 2
QUARTER = SEQ // 4
FLIP_BLOCK = 256


def _fold_tables():
    def tab(f, n):
        t2 = 2 * np.arange(n, dtype=np.int64) + 1
        ang = ((f[:, None] * t2[None, :]) % (2 * FFT_N)).astype(np.float64) * (math.pi / FFT_N)
        return np.cos(ang), np.sin(ang)

    r = np.arange(QUARTER, dtype=np.int64)
    g = np.arange(HALF, dtype=np.int64)
    f_ee, f_eo, f_o = 4 * r, 4 * r + 2, 2 * g + 1
    cee, see = tab(f_ee, QUARTER)
    ceo, seo = tab(f_eo, QUARTER)
    co, so = tab(f_o, HALF)
    see[0, :] = np.where(np.arange(QUARTER) % 2 == 0, 1.0, -1.0)
    const = lambda a, dt: jnp.asarray(np.ascontiguousarray(a).astype(np.float32)).astype(dt)
    phase = lambda f: np.stack([np.cos(f * (math.pi / FFT_N)), np.sin(f * (math.pi / FFT_N))])
    return dict(
        fq=const(np.stack([cee, see, ceo, seo]), BF16),
        fo=const(np.stack([co, so]), BF16),
        iq=const(np.stack([cee.T, ceo.T, see.T, seo.T]), BF16),
        io=const(np.stack([so.T, co.T]), BF16),
        anti=const(np.eye(FLIP_BLOCK)[::-1], BF16),
        pq=const(np.stack([phase(f_ee), phase(f_eo)], axis=1)[..., None], F32),
        po=const(phase(f_o)[..., None], F32),
    )


def _flip_rows(h, anti):
    nb = h.shape[0] // FLIP_BLOCK
    return jnp.concatenate(
        [jnp.dot(anti[...], h[(nb - 1 - a) * FLIP_BLOCK:(nb - a) * FLIP_BLOCK, :], preferred_element_type=F32)
         for a in range(nb)], axis=0)


def _fold_two_levels(x, anti):
    q0, q1, q2, q3 = (x[k * QUARTER:(k + 1) * QUARTER] for k in range(4))
    f1, f2, f3 = (_flip_rows(q, anti) for q in (q1, q2, q3))
    q0, q1, q2 = (q.astype(F32) for q in (q0, q1, q2))
    xs0, xa0 = q0 + f3, q0 - f3
    fxs1, fxa1 = f1 + q2, f1 - q2
    bf = lambda v: v.astype(BF16)
    xs = jnp.concatenate([bf(xs0), bf(q1 + f2)], axis=0)
    xa = jnp.concatenate([bf(xa0), bf(q1 - f2)], axis=0)
    return xs, xa, bf(xs0 + fxs1), bf(xs0 - fxs1), bf(xa0 + fxa1), bf(xa0 - fxa1)


def _staggered(n, matmuls, finish):
    matmuls(0, 0)
    for t in range(1, n):
        matmuls(t, t % 2)
        finish(t - 1, (t - 1) % 2)
    finish(n - 1, (n - 1) % 2)


def _col_chunks(n):
    return [slice(c, c + MXU_COLS) for c in range(0, n, MXU_COLS)]


PLANES = {"a_ee": 0, "b_ee": QUARTER, "a_eo": 2 * QUARTER, "b_eo": 3 * QUARTER,
          "a_o": 2 * HALF, "b_o": 3 * HALF}
GROUPS = (("a_ee", "b_ee", QUARTER), ("a_eo", "b_eo", QUARTER), ("a_o", "b_o", HALF))


def _spectrum(fq_ref, fo_ref, x, anti):
    xs, xa, xss, xsa, xas, xaa = _fold_two_levels(x, anti)
    dq = lambda k, v: jnp.dot(fq_ref[k], v, preferred_element_type=F32)
    do = lambda k, v: jnp.dot(fo_ref[k], v, preferred_element_type=F32)
    return {"a_ee": dq(0, xss), "b_ee": dq(1, xaa), "a_eo": dq(2, xsa), "b_eo": dq(3, xas),
            "a_o": do(0, xa), "b_o": do(1, xs)}


def _plane(ref, lead, name, rows, cols):
    r0 = PLANES[name]
    return ref[lead, r0:r0 + rows, cols]


def _resident(shape):
    zeros = (0,) * len(shape)
    return pl.BlockSpec(shape, lambda *_: zeros, pipeline_mode=pl.Buffered(1))


def _pack_filter_mlp(w1, b1, w2, b2, freq):
    hpad = LANES - FILTER_HIDDEN
    bands = (FILTER_EMB - 1) // 2
    row = lambda v: jnp.pad(v.reshape(1, -1), ((0, 7), (0, hpad)))
    mat = lambda m: jnp.pad(m, ((0, LANES - m.shape[0]), (0, hpad)))
    return jnp.concatenate([row(w1[0]), row(b1), row(b2), row(freq), jnp.zeros((LANES - 32, LANES), F32),
                            mat(w1[1:1 + bands]), mat(w1[1 + bands:]), mat(w2)], axis=0)


def _filter_sd_body(fp_ref, w3f_ref, b3f_ref, w3b_ref, b3b_ref, d_ref, o_ref, hid_ref, *, first, j):
    length = o_ref.shape[1]
    tn = o_ref.shape[2]
    hp = lax.Precision.HIGHEST
    w1t_ref, b1_ref, b2_ref, fq_ref = (fp_ref.at[r:r + 1] for r in (0, 8, 16, 24))
    w1c_ref, w1s_ref, w2_ref = (fp_ref.at[r:r + LANES] for r in (LANES, 2 * LANES, 3 * LANES))

    @pl.when(first)
    def _():
        n = lax.broadcasted_iota(jnp.int32, (length, LANES), 0).astype(F32)
        bands = (FILTER_EMB - 1) // 2
        fr_step = (bands - 1 - 1e-4) / (bands - 1)

        def angles(rows, stride):
            r = lax.broadcasted_iota(jnp.int32, (rows, LANES), 0).astype(F32)
            lane = lax.broadcasted_iota(jnp.int32, (rows, LANES), 1)
            fr = jnp.where(lane < bands, 1e-4 + lane.astype(F32) * fr_step, 0.0)
            return (2.0 * math.pi * stride * r / length) * fr

        outer, inner = angles(length // TRIG_INNER, TRIG_INNER), angles(TRIG_INNER, 1)
        c1, s1 = jnp.cos(outer)[:, None, :], jnp.sin(outer)[:, None, :]
        c0, s0 = jnp.cos(inner)[None, :, :], jnp.sin(inner)[None, :, :]
        cos_ang = (c1 * c0 - s1 * s0).reshape(length, LANES)
        sin_ang = (s1 * c0 + c1 * s0).reshape(length, LANES)
        t = n / (length - 1)
        fq = fq_ref[...]
        pre = (t * w1t_ref[...]
               + jnp.dot(cos_ang, w1c_ref[...], precision=hp, preferred_element_type=F32)
               + jnp.dot(-sin_ang, w1s_ref[...], precision=hp, preferred_element_type=F32)
               + b1_ref[...])
        hid = jnp.sin(fq * pre)
        hid = jnp.sin(fq * (jnp.dot(hid, w2_ref[...], precision=hp, preferred_element_type=F32)
                            + b2_ref[...]))
        hid_hi = hid.astype(BF16)
        hid_ref[0] = hid_hi
        hid_ref[1] = (hid - hid_hi.astype(F32)).astype(BF16)

    def dot3(w_ref):
        wf = w_ref[...]
        wf = jnp.concatenate([wf, jnp.zeros((LANES - wf.shape[0], wf.shape[1]), F32)], axis=0)
        w_hi = wf.astype(BF16)
        w_lo = (wf - w_hi.astype(F32)).astype(BF16)
        return (jnp.dot(hid_ref[0], w_hi, preferred_element_type=F32)
                + jnp.dot(hid_ref[1], w_hi, preferred_element_type=F32)
                + jnp.dot(hid_ref[0], w_lo, preferred_element_type=F32))

    row = lax.broadcasted_iota(jnp.int32, (length, tn), 0)
    chan = (lax.broadcasted_iota(jnp.int32, (length, tn), 1) + j * tn).astype(F32)
    min_decay = math.log(DECAY_TARGET) / DECAY_SLOW
    max_decay = math.log(DECAY_TARGET) / DECAY_FAST
    delta = min_decay + chan * ((max_decay - min_decay) / (HYENA_WIDTH - 1))
    t = row.astype(F32) / (length - 1)
    window = jnp.exp(-t * jnp.abs(delta)) + WINDOW_SHIFT
    hf = (dot3(w3f_ref) + b3f_ref[...]) * window
    hb = (dot3(w3b_ref) + b3b_ref[...]) * window
    hf = jnp.where(row == 0, hf + d_ref[0], hf)
    hb = jnp.where(row == 0, 0.0, hb)
    o_ref[0] = (hf + hb).astype(o_ref.dtype)
    o_ref[1] = (hf - hb).astype(o_ref.dtype)


def _prep_kernel(c_ref, aw_ref, ab_ref, *refs, filter_steps, ct):
    wa_ref, wh_ref = refs[-7:-5]
    mods_ref, sd_ref, wa_bf_ref, wh_bf_ref, hid_ref = refs[-5:]
    s = pl.program_id(0)
    _mods_kernel(c_ref, aw_ref, ab_ref, mods_ref)

    @pl.when(s < filter_steps)
    def _():
        _filter_sd_body(*refs[:-7], sd_ref, hid_ref, first=s == 0, j=s % ct)

    @pl.when(s < CAST_STEPS)
    def _():
        wa_bf_ref[...] = wa_ref[...].astype(BF16)

    @pl.when((s >= CAST_STEPS) & (s < 2 * CAST_STEPS))
    def _():
        wh_bf_ref[...] = wh_ref[...].astype(BF16)


CAST_STEPS = 8
TRIG_INNER = 64


def _prep(cc, ada_w, ada_b, w1, b1, w2, b2, w3, b3, freq, bias_d, w_out_a, w_out_h, *,
          tn_mods=512, tn=256):
    depth, d, n_mods = ada_w.shape
    mt = n_mods // tn_mods
    w = HYENA_WIDTH
    ct = w // tn
    filter_steps = 2 * ct
    assert filter_steps <= depth * mt
    n = w3.shape[1]
    fp = _pack_filter_mlp(w1, b1, w2, b2, freq)
    b3r = b3.reshape(1, n)
    order = lambda s: jnp.minimum(s, filter_steps - 1) // ct
    chan = lambda s: jnp.minimum(s, filter_steps - 1) % ct
    fwd_cols = lambda s: (0, 2 * order(s) * ct + chan(s))
    bwd_cols = lambda s: (0, (2 * order(s) + 1) * ct + chan(s))
    assert w_out_a.shape == w_out_h.shape and 2 * CAST_STEPS <= depth * mt
    cast_blk = (w_out_a.shape[0] // CAST_STEPS, w_out_a.shape[1])
    first_rows = lambda s: (jnp.minimum(s, CAST_STEPS - 1), 0)
    second_rows = lambda s: (jnp.clip(s - CAST_STEPS, 0, CAST_STEPS - 1), 0)
    return pl.pallas_call(
        functools.partial(_prep_kernel, filter_steps=filter_steps, ct=ct),
        grid=(depth * mt,),
        in_specs=[pl.BlockSpec((MOD_ROWS, d), lambda s: (0, 0)),
                  pl.BlockSpec((1, d, tn_mods), lambda s: (s // mt, 0, s % mt)),
                  pl.BlockSpec((1, 1, tn_mods), lambda s: (s // mt, 0, s % mt)),
                  pl.BlockSpec(fp.shape, lambda s: (0, 0)),
                  pl.BlockSpec((FILTER_HIDDEN, tn), fwd_cols), pl.BlockSpec((1, tn), fwd_cols),
                  pl.BlockSpec((FILTER_HIDDEN, tn), bwd_cols), pl.BlockSpec((1, tn), bwd_cols),
                  pl.BlockSpec((1, 1, tn), lambda s: (order(s), 0, chan(s))),
                  pl.BlockSpec(cast_blk, first_rows), pl.BlockSpec(cast_blk, second_rows)],
        out_specs=[pl.BlockSpec((1, MOD_ROWS, tn_mods), lambda s: (s // mt, 0, s % mt)),
                   pl.BlockSpec((2, SEQ, tn), lambda s: (0, 0, order(s) * ct + chan(s))),
                   pl.BlockSpec(cast_blk, first_rows), pl.BlockSpec(cast_blk, second_rows)],
        out_shape=[jax.ShapeDtypeStruct((depth, MOD_ROWS, n_mods), F32),
                   jax.ShapeDtypeStruct((2, SEQ, 2 * w), BF16),
                   jax.ShapeDtypeStruct(w_out_a.shape, BF16),
                   jax.ShapeDtypeStruct(w_out_h.shape, BF16)],
        scratch_shapes=[pltpu.VMEM((2, SEQ, LANES), BF16)],
        compiler_params=_cparams(("arbitrary",)),
        name="mods_and_filters",
    )(cc, ada_w, ada_b.reshape(depth, 1, n_mods),
      fp, w3, b3r, w3, b3r, bias_d.reshape(2, 1, w),
      w_out_a, w_out_h)


def _filter_spec_kernel(s_ref, d_ref, fq_ref, fo_ref, anti_ref, pq_ref, po_ref, k_ref):
    phases = {"a_ee": (pq_ref[0, 0], pq_ref[1, 0]), "a_eo": (pq_ref[0, 1], pq_ref[1, 1]),
              "a_o": (po_ref[0], po_ref[1])}
    is0 = lax.broadcasted_iota(jnp.int32, (QUARTER, MXU_COLS), 0) == 0
    for cols in _col_chunks(k_ref.shape[2]):
        sp = _spectrum(fq_ref, fo_ref, s_ref[0, :, cols], anti_ref)
        dp = _spectrum(fq_ref, fo_ref, d_ref[0, :, cols], anti_ref)
        for a, b, rows in GROUPS:
            cos, sin = phases[a]
            ka = sp[a] * cos + sp[b] * sin
            kb = dp[b] * cos - dp[a] * sin
            if a == "a_ee":
                scale = jnp.where(is0, 1.0 / FFT_N, 2.0 / FFT_N)
                kb = jnp.where(is0, sp[b], kb)
            else:
                scale = 2.0 / FFT_N
            k_ref[0, PLANES[a]:PLANES[a] + rows, cols] = (ka * scale).astype(k_ref.dtype)
            k_ref[0, PLANES[b]:PLANES[b] + rows, cols] = (kb * scale).astype(k_ref.dtype)


def _filter_spectra(sd, tabs, *, tn=512):
    w = HYENA_WIDTH
    ct = w // tn
    consts = [tabs[k] for k in ("fq", "fo", "anti", "pq", "po")]
    return pl.pallas_call(
        _filter_spec_kernel,
        grid=(2, ct),
        in_specs=[
            pl.BlockSpec((1, SEQ, tn), lambda o, j: (0, 0, o * ct + j)),
            pl.BlockSpec((1, SEQ, tn), lambda o, j: (1, 0, o * ct + j)),
        ] + [_resident(c.shape) for c in consts],
        out_specs=pl.BlockSpec((1, 4 * HALF, tn), lambda o, j: (o, 0, j)),
        out_shape=jax.ShapeDtypeStruct((2, 4 * HALF, w), F32),
        compiler_params=_cparams(("arbitrary", "arbitrary")),
        name="filter_spectra",
    )(sd, sd, *consts)


def _conv_fwd_kernel(x_ref, fq_ref, fo_ref, anti_ref, k_ref, y_ref):
    is0 = lax.broadcasted_iota(jnp.int32, (QUARTER, MXU_COLS), 0) == 0
    for cols in _col_chunks(y_ref.shape[2]):
        sp = _spectrum(fq_ref, fo_ref, x_ref[:, cols], anti_ref)
        for a, b, rows in GROUPS:
            ka = _plane(k_ref, 0, a, rows, cols).astype(F32)
            kb = _plane(k_ref, 0, b, rows, cols).astype(F32)
            bkb = sp[b] * kb
            if a == "a_ee":
                ya = sp[a] * ka - jnp.where(is0, 0.0, bkb)
                yb = jnp.where(is0, bkb, sp[a] * kb + sp[b] * ka)
            else:
                ya = sp[a] * ka - bkb
                yb = sp[a] * kb + sp[b] * ka
            y_ref[0, PLANES[a]:PLANES[a] + rows, cols] = ya.astype(y_ref.dtype)
            y_ref[0, PLANES[b]:PLANES[b] + rows, cols] = yb.astype(y_ref.dtype)


def _conv_fwd(x, x_blk0, kspec, order, tabs, batch, *, tn=512):
    w = HYENA_WIDTH
    ct = w // tn
    consts = [tabs[k] for k in ("fq", "fo", "anti")]
    return pl.pallas_call(
        _conv_fwd_kernel,
        grid=(ct, batch),
        in_specs=[pl.BlockSpec((SEQ, tn), lambda j, b: (b, x_blk0 * ct + j))]
        + [_resident(c.shape) for c in consts]
        + [pl.BlockSpec((1, 4 * HALF, tn), lambda j, b: (order, 0, j))],
        out_specs=pl.BlockSpec((1, 4 * HALF, tn), lambda j, b: (b, 0, j)),
        out_shape=jax.ShapeDtypeStruct((batch, 4 * HALF, w), BF16),
        compiler_params=_cparams(("arbitrary", "arbitrary")),
        name="long_conv_fwd",
    )(x, *consts, kspec)


def _conv_inv_kernel(*refs, gated):
    if gated:
        y_ref, iq_ref, io_ref, anti_ref, xm_ref, g_ref, o_ref, accq_ref, acco_ref = refs
    else:
        y_ref, iq_ref, io_ref, anti_ref, xm_ref, o_ref, accq_ref, acco_ref = refs
    chunks = _col_chunks(o_ref.shape[1])
    flipped = lambda v: _flip_rows(v.astype(BF16), anti_ref)

    def matmuls(t, slot):
        cols = chunks[t]
        for k, name in enumerate(("a_ee", "a_eo", "b_ee", "b_eo")):
            accq_ref[slot, k] = jnp.dot(iq_ref[k], _plane(y_ref, 0, name, QUARTER, cols),
                                        preferred_element_type=F32)
        for k, name in enumerate(("b_o", "a_o")):
            acco_ref[slot, k] = jnp.dot(io_ref[k], _plane(y_ref, 0, name, HALF, cols),
                                        preferred_element_type=F32)

    def finish(t, slot):
        cols = chunks[t]
        p, r, q, s = (accq_ref[slot, k] for k in range(4))
        osum = acco_ref[slot, 0] + acco_ref[slot, 1]
        odif = acco_ref[slot, 0] - acco_ref[slot, 1]
        e1, e2, d1, d2 = p + r, q + s, p - r, s - q
        quarters = (e1 + e2 + osum[:QUARTER],
                    flipped(d1 + d2) + osum[QUARTER:],
                    d1 - d2 + flipped(odif[QUARTER:]),
                    flipped(e1 - e2 + odif[:QUARTER]))
        for k, y in enumerate(quarters):
            rows = slice(k * QUARTER, (k + 1) * QUARTER)
            mult = xm_ref[rows, cols]
            if gated:
                mult = mult * g_ref[rows, cols]
            o_ref[rows, cols] = (y * mult.astype(F32)).astype(o_ref.dtype)

    _staggered(len(chunks), matmuls, finish)


def _conv_inv(y, tabs, p, m_blk0, g_blk0=None, *, tn=512):
    batch = y.shape[0]
    w = HYENA_WIDTH
    ct = w // tn
    gated = g_blk0 is not None
    col = lambda blk0: (lambda b, j: (b, blk0 * ct + j))
    consts = [tabs[k] for k in ("iq", "io", "anti")]
    in_specs = ([pl.BlockSpec((1, 4 * HALF, tn), lambda b, j: (b, 0, j))]
                + [_resident(c.shape) for c in consts]
                + [pl.BlockSpec((SEQ, tn), col(m_blk0))])
    args = [y, *consts, p]
    if gated:
        in_specs.append(pl.BlockSpec((SEQ, tn), col(g_blk0)))
        args.append(p)
    return pl.pallas_call(
        functools.partial(_conv_inv_kernel, gated=gated),
        grid=(batch, ct),
        in_specs=in_specs,
        out_specs=pl.BlockSpec((SEQ, tn), lambda b, j: (b, j)),
        out_shape=jax.ShapeDtypeStruct((batch * SEQ, w), BF16),
        scratch_shapes=[pltpu.VMEM((2, 4, QUARTER, MXU_COLS), F32),
                        pltpu.VMEM((2, 2, HALF, MXU_COLS), F32)],
        compiler_params=_cparams(("arbitrary", "arbitrary")),
        name="long_conv_inv",
    )(*args)


def _rope_tables():
    pos = np.arange(SEQ)
    row = (pos // GRID_W).astype(np.float32)
    col = (pos % GRID_W).astype(np.float32)
    half = HEAD_DIM // 2
    inv = (ROPE_BASE ** (-np.arange(0, half, 2, dtype=np.float32) / half)).astype(np.float32)
    ar = row[:, None] * inv[None]
    ac = col[:, None] * inv[None]
    cos = np.concatenate([np.cos(ar), np.cos(ar), np.cos(ac), np.cos(ac)], axis=1)
    sin = np.concatenate([-np.sin(ar), np.sin(ar), -np.sin(ac), np.sin(ac)], axis=1)
    return jnp.asarray(cos, F32), jnp.asarray(sin, F32)


def _tile_types(tn, kinds):
    per_tile = tn // LANES
    return [kinds[s0:s0 + per_tile] for s0 in range(0, len(kinds), per_tile)]


def kernel(x, c, ctx, c_ctx, norm_g, ada_w, ada_b, attn_w_in, attn_w_out, attn_sink, hy_w_in,
           hy_conv_w, hy_conv_b, hy_w1, hy_b1, hy_w2, hy_b2, hy_w3, hy_b3, hy_freq, hy_bias_d,
           hy_w_out, final_g):
    batch, seq, d = x.shape
    assert (seq, d) == (SEQ, D_MODEL) and ctx.shape[1] == CTX_LEN
    assert norm_g.shape[0] == 2 and attn_w_in.shape[0] == 1 and hy_w_in.shape[0] == 1
    w = HYENA_WIDTH

    ctx_row = batch
    cc = jnp.concatenate([c, c_ctx[None], jnp.zeros((MOD_ROWS - batch - 1, d), F32)], axis=0)
    mods, sd, w_out_attn, w_out_hyena = _prep(
        cc, ada_w, ada_b, hy_w1[0], hy_b1[0], hy_w2[0], hy_b2[0], hy_w3[0], hy_b3[0], hy_freq[0],
        hy_bias_d[0], attn_w_out[0], hy_w_out[0])
    mods = mods.reshape(-1, 1, d)

    x2 = x.reshape(batch * seq, d)
    ctx2 = ctx.reshape(batch * CTX_LEN, d)

    tn = 1024
    kinds0 = (["q"] * N_HEADS + ["k"] * N_KV_HEADS + [None] * N_KV_HEADS
              + ["g"] * (ATTN_WIDTH // LANES))
    hx = _norm_mod(x2, norm_g[0], mods, 0, tm=1024)
    px = _proj(hx, attn_w_in[0], 0, ATTN_IN, tm=SEQ, tn=tn, tile_types=_tile_types(tn, kinds0),
               rope=_rope_tables(), q_scale=HEAD_DIM ** -0.5 * LOG2E)
    ckv = _ctx_kv(ctx2, norm_g[0], mods, 0, ctx_row, attn_w_in[0], ATTN_WIDTH, 2 * KV_WIDTH)
    og = _attention(px, ckv, attn_sink[0], batch)
    x2, hx = _out_proj(og, w_out_attn, x2, mods, 0, next_norm_g=norm_g[1])

    kinds1 = ["c"] * (3 * w // LANES) + ["g"] * (w // LANES)
    tn1 = 1024
    p = _proj(hx, hy_w_in[0], 0, 4 * w, tm=SEQ, tn=tn1, tile_types=_tile_types(tn1, kinds1),
              conv=(hy_conv_w[0], hy_conv_b[0].reshape(1, 3 * w)))
    tabs = _fold_tables()
    kspec = _filter_spectra(sd, tabs)
    y1 = _conv_fwd(p, 2, kspec, 0, tabs, batch)
    z = _conv_inv(y1, tabs, p, 0)
    y2 = _conv_fwd(z, 0, kspec, 1, tabs, batch)
    yg = _conv_inv(y2, tabs, p, 1, g_blk0=3)
    out, = _out_proj(yg, w_out_hyena, x2, mods, 1, final_g=final_g)
    return out.reshape(batch, seq, d)
```

```python
import functools
import math

import jax
import jax.numpy as jnp
import numpy as np
from jax import lax
from jax.experimental import pallas as pl
from jax.experimental.pallas import tpu as pltpu

F32 = jnp.float32
BF16 = jnp.bfloat16

D_MODEL = 2048
SEQ = 2048
CTX_LEN = 256
GRID_W = 64
HEAD_DIM = 128
N_HEADS = 16
N_KV_HEADS = 4
GQA_GROUP = 4
ATTN_WIDTH = 2048
KV_WIDTH = 512
ATTN_IN = 2 * ATTN_WIDTH + 2 * KV_WIDTH
WINDOW = 128
BLOCK = 128
ROPE_BASE = 10000.0
HYENA_WIDTH = 2048
FILTER_EMB = 33
FILTER_HIDDEN = 64
DECAY_FAST = 0.3
DECAY_SLOW = 1.5
DECAY_TARGET = 1e-2
WINDOW_SHIFT = 0.05
NORM_EPS = 1e-6
NEG_INF = -1e30

LANES = 128
MOD_ROWS = 8
MXU_COLS = 256
ACC_ROWS = 1024
ONES_ROWS = 16
LOG2E = math.log2(math.e)
FFT_N = 2 * SEQ
VMEM_LIMIT = 56 * 1024 * 1024


def _cparams(sem):
    return pltpu.CompilerParams(dimension_semantics=sem, vmem_limit_bytes=VMEM_LIMIT)


def _mods_kernel(c_ref, w_ref, b_ref, o_ref):
    c = c_ref[...]
    s = c * jax.nn.sigmoid(c)
    s_hi = s.astype(BF16)
    s_lo = (s - s_hi.astype(F32)).astype(BF16)
    lhs = jnp.concatenate([s_hi, s_lo], axis=0)
    r = jnp.dot(lhs, w_ref[0].astype(BF16), preferred_element_type=F32)
    o_ref[0] = r[:MOD_ROWS] + r[MOD_ROWS:] + b_ref[0]


def _rope_slab(t, cos, sin):
    lane = lax.broadcasted_iota(jnp.int32, t.shape, 1)
    first = (lane % 64) < 32
    partner = jnp.where(first, pltpu.roll(t, 96, 1), pltpu.roll(t, 32, 1))
    return t * cos + partner * sin


def _norm_mod_rows(x, gain, add):
    ms = jnp.mean(x * x, axis=-1, keepdims=True)
    return x * lax.rsqrt(ms + NORM_EPS) * gain + add


def _norm_mod_kernel(x_ref, g_ref, sh_ref, sc_ref, o_ref, *, row_chunk):
    gain = g_ref[...] * (1.0 + sc_ref[0])
    add = sh_ref[0]
    for r in range(0, x_ref.shape[0], row_chunk):
        o_ref[r:r + row_chunk, :] = _norm_mod_rows(x_ref[r:r + row_chunk, :], gain, add).astype(o_ref.dtype)


def _mod_row(layer, row, kind):
    return (layer * MOD_ROWS + row) * 3 + kind


def _mod_spec(base, tm, rows_per_mod, d):
    return pl.BlockSpec((1, 1, d), lambda i: (base + 3 * ((i * tm) // rows_per_mod), 0, 0))


def _norm_mod(x, g, mods, layer, *, tm):
    m, d = x.shape
    return pl.pallas_call(
        functools.partial(_norm_mod_kernel, row_chunk=min(tm, 256)),
        grid=(m // tm,),
        in_specs=[pl.BlockSpec((tm, d), lambda i: (i, 0)), pl.BlockSpec((1, d), lambda i: (0, 0)),
                  _mod_spec(_mod_row(layer, 0, 0), tm, SEQ, d),
                  _mod_spec(_mod_row(layer, 0, 1), tm, SEQ, d)],
        out_specs=pl.BlockSpec((tm, d), lambda i: (i, 0)),
        out_shape=jax.ShapeDtypeStruct((m, d), BF16),
        compiler_params=_cparams(("parallel",)),
        name="norm_mod",
    )(x, g.reshape(1, d), mods, mods)


def _ctx_kv_kernel(x_ref, g_ref, sh_ref, sc_ref, w_ref, o_ref, hx_ref, *, row_chunk):
    gain = g_ref[...] * (1.0 + sc_ref[0])
    add = sh_ref[0]
    for r in range(0, x_ref.shape[0], row_chunk):
        hx_ref[r:r + row_chunk, :] = _norm_mod_rows(x_ref[r:r + row_chunk, :], gain, add).astype(BF16)
    wb = w_ref[...].astype(BF16)
    for c0 in range(0, o_ref.shape[1], MXU_COLS):
        o_ref[:, c0:c0 + MXU_COLS] = jnp.dot(hx_ref[...], wb[:, c0:c0 + MXU_COLS],
                                             preferred_element_type=F32).astype(o_ref.dtype)


def _ctx_kv(ctx, g, mods, layer, mod_row, w, col0, n):
    m, d = ctx.shape
    assert col0 % n == 0 and m <= ACC_ROWS
    shift = _mod_row(layer, mod_row, 0)
    scale = _mod_row(layer, mod_row, 1)
    return pl.pallas_call(
        functools.partial(_ctx_kv_kernel, row_chunk=256),
        grid=(1,),
        in_specs=[pl.BlockSpec((m, d), lambda i: (0, 0)), pl.BlockSpec((1, d), lambda i: (0, 0)),
                  pl.BlockSpec((1, 1, d), lambda i: (shift, 0, 0)),
                  pl.BlockSpec((1, 1, d), lambda i: (scale, 0, 0)),
                  pl.BlockSpec((d, n), lambda i: (0, col0 // n))],
        out_specs=pl.BlockSpec((m, n), lambda i: (0, 0)),
        out_shape=jax.ShapeDtypeStruct((m, n), BF16),
        scratch_shapes=[pltpu.VMEM((m, d), BF16)],
        compiler_params=_cparams(("arbitrary",)),
        name="ctx_kv",
    )(ctx, g.reshape(1, d), mods, mods, w)


def _proj_kernel(*refs, tile_types, has_rope, has_conv, q_scale):
    if has_rope:
        hx_ref, w_ref, cos_ref, sin_ref, o_ref, wb_ref = refs
    elif has_conv:
        hx_ref, w_ref, cw_ref, cb_ref, o_ref, wb_ref = refs
    else:
        hx_ref, w_ref, o_ref, wb_ref = refs
    j = pl.program_id(0)
    tm, tn = o_ref.shape

    @pl.when(pl.program_id(1) == 0)
    def _():
        wb_ref[...] = w_ref[...].astype(BF16)

    w_ref = wb_ref

    def plain():
        for c0 in range(0, tn, MXU_COLS):
            o_ref[:, c0:c0 + MXU_COLS] = chunk_dot(c0).astype(o_ref.dtype)

    def chunk_dot(c0):
        pieces = [jnp.dot(hx_ref[r:r + ACC_ROWS, :], w_ref[:, c0:c0 + MXU_COLS],
                          preferred_element_type=F32) for r in range(0, tm, ACC_ROWS)]
        return pieces[0] if len(pieces) == 1 else jnp.concatenate(pieces, axis=0)

    def gated():
        for c0 in range(0, tn, MXU_COLS):
            half = 0.5 * chunk_dot(c0)
            o_ref[:, c0:c0 + MXU_COLS] = (half + half * jnp.tanh(half)).astype(o_ref.dtype)

    def short_conv():
        for c0 in range(0, tn, MXU_COLS):
            acc = chunk_dot(c0)
            for h in range(MXU_COLS // LANES):
                lo = c0 + h * LANES
                u = acc[:, h * LANES:(h + 1) * LANES]
                w3 = cw_ref[:, lo:lo + LANES]
                b = cb_ref[:, lo:lo + LANES]
                y = pltpu.roll(u, 1, 0) * w3[0:1] + u * w3[1:2] + pltpu.roll(u, tm - 1, 0) * w3[2:3] + b
                o_ref[:, lo:lo + LANES] = y.astype(o_ref.dtype)
                first = u[0:1] * w3[1:2] + u[1:2] * w3[2:3] + b
                last = u[tm - 2:tm - 1] * w3[0:1] + u[tm - 1:tm] * w3[1:2] + b
                o_ref[0:1, lo:lo + LANES] = first.astype(o_ref.dtype)
                o_ref[tm - 1:tm, lo:lo + LANES] = last.astype(o_ref.dtype)

    def roped(types):
        cos = cos_ref[...]
        sin = sin_ref[...]
        for c0 in range(0, tn, MXU_COLS):
            acc = chunk_dot(c0)
            for h in range(MXU_COLS // LANES):
                ty = types[c0 // LANES + h]
                slab = acc[:, h * LANES:(h + 1) * LANES]
                if ty is not None:
                    slab = _rope_slab(slab, cos, sin)
                    if ty == "q":
                        slab = slab * q_scale
                lo = c0 + h * LANES
                o_ref[:, lo:lo + LANES] = slab.astype(o_ref.dtype)

    if tile_types is None:
        plain()
        return

    groups = {}
    for t, types in enumerate(tile_types):
        groups.setdefault(tuple(types), []).append(t)
    for types, tiles in groups.items():
        cond = functools.reduce(jnp.logical_or, [j == t for t in tiles])
        if all(ty is None for ty in types):
            pl.when(cond)(plain)
        elif all(ty == "g" for ty in types):
            pl.when(cond)(gated)
        elif all(ty == "c" for ty in types):
            pl.when(cond)(short_conv)
        else:
            assert "g" not in types and "c" not in types
            pl.when(cond)(functools.partial(roped, types))


def _proj(hx, w, col_blk0, n, *, tm, tn, tile_types=None, rope=None, conv=None, q_scale=1.0):
    m, d = hx.shape
    in_specs = [
        pl.BlockSpec((tm, d), lambda j, i: (i, 0)),
        pl.BlockSpec((d, tn), lambda j, i: (0, col_blk0 + j)),
    ]
    args = [hx, w]
    scratch = [pltpu.VMEM((d, tn), BF16)]
    if rope is not None:
        cos, sin = rope
        seq_tiles = cos.shape[0] // tm
        in_specs += [pl.BlockSpec((tm, LANES), lambda j, i: (i % seq_tiles, 0))] * 2
        args += [cos, sin]
    if conv is not None:
        assert rope is None and tm == SEQ
        cw, cb = conv
        last = cw.shape[1] // tn - 1
        in_specs += [pl.BlockSpec((cw.shape[0], tn), lambda j, i: (0, jnp.minimum(j, last))),
                     pl.BlockSpec((1, tn), lambda j, i: (0, jnp.minimum(j, last)))]
        args += [cw, cb]
    return pl.pallas_call(
        functools.partial(_proj_kernel, tile_types=tile_types, has_rope=rope is not None,
                          has_conv=conv is not None, q_scale=q_scale),
        grid=(n // tn, m // tm),
        in_specs=in_specs,
        out_specs=pl.BlockSpec((tm, tn), lambda j, i: (i, j)),
        out_shape=jax.ShapeDtypeStruct((m, n), BF16),
        scratch_shapes=scratch,
        compiler_params=_cparams(("arbitrary", "arbitrary")),
        name="proj",
    )(*args)


def _attn_kernel(sink_ref, q_ref, k_ref, v_ref, g_ref, kc_ref, vc_ref, o_ref,
                 vt_ref, vct_ref, bias_ref):
    kh = pl.program_id(1)
    band = 3 * BLOCK
    cols = GQA_GROUP * BLOCK
    n_blocks = SEQ // BLOCK

    vt_ref[:HEAD_DIM, :] = v_ref[...].T
    vct_ref[:HEAD_DIM, :] = vc_ref[...].T

    @pl.when((pl.program_id(0) == 0) & (kh == 0))
    def _():
        vt_ref[HEAD_DIM:, :] = jnp.ones((ONES_ROWS, SEQ), BF16)
        vct_ref[HEAD_DIM:, :] = jnp.ones((ONES_ROWS, CTX_LEN), BF16)
        krow = lax.broadcasted_iota(jnp.int32, (band, cols), 0)
        qcol = lax.broadcasted_iota(jnp.int32, (band, cols), 1) % BLOCK
        for idx, off in enumerate((0, -BLOCK, -2 * BLOCK)):
            bias_ref[idx] = jnp.where(jnp.abs(krow - qcol + off) <= WINDOW, 0.0, NEG_INF)

    kc = kc_ref[...]
    vct = vct_ref[...]
    sink_row = jnp.concatenate(
        [jnp.full((1, BLOCK), sink_ref[kh * GQA_GROUP + h] * LOG2E, F32) for h in range(GQA_GROUP)],
        axis=1)

    def body(n, carry):
        q0 = pl.multiple_of(n * BLOCK, BLOCK)
        ks = pl.multiple_of(jnp.clip((n - 1) * BLOCK, 0, SEQ - band), BLOCK)
        bidx = jnp.where(n == 0, 0, jnp.where(n == n_blocks - 1, 2, 1))
        qs = q_ref[pl.ds(q0, BLOCK), :]
        q4t = jnp.concatenate([qs[:, h * LANES:(h + 1) * LANES].T for h in range(GQA_GROUP)],
                              axis=1)
        kb = k_ref[pl.ds(ks, band), :]
        s_loc = jnp.dot(kb, q4t, preferred_element_type=F32) + bias_ref[bidx]
        s_ctx = jnp.dot(kc, q4t, preferred_element_type=F32)
        m = jnp.maximum(jnp.maximum(jnp.max(s_loc, axis=0, keepdims=True),
                                    jnp.max(s_ctx, axis=0, keepdims=True)), sink_row)
        p_loc = jnp.exp2(s_loc - m).astype(BF16)
        p_ctx = jnp.exp2(s_ctx - m).astype(BF16)
        ox = (jnp.dot(vt_ref[:, pl.ds(ks, band)], p_loc, preferred_element_type=F32)
              + jnp.dot(vct, p_ctx, preferred_element_type=F32))
        den = ox[HEAD_DIM:HEAD_DIM + 1, :] + jnp.exp2(sink_row - m)
        ot = ox[:HEAD_DIM, :] * (1.0 / den)
        gs = g_ref[pl.ds(q0, BLOCK), :].astype(F32)
        for h in range(GQA_GROUP):
            oh = ot[:, h * LANES:(h + 1) * LANES].T * gs[:, h * LANES:(h + 1) * LANES]
            o_ref[pl.ds(q0, BLOCK), h * LANES:(h + 1) * LANES] = oh.astype(o_ref.dtype)
        return carry

    lax.fori_loop(0, n_blocks, body, 0, unroll=8)


def _attention(px, ckv, sink, batch):
    gw = GQA_GROUP * HEAD_DIM
    k_blk0 = ATTN_WIDTH // HEAD_DIM
    v_blk0 = (ATTN_WIDTH + KV_WIDTH) // HEAD_DIM
    g_blk0 = (ATTN_WIDTH + 2 * KV_WIDTH) // gw
    return pl.pallas_call(
        _attn_kernel,
        grid=(batch, N_KV_HEADS),
        in_specs=[
            pl.BlockSpec(memory_space=pltpu.SMEM),
            pl.BlockSpec((SEQ, gw), lambda b, h: (b, h)),
            pl.BlockSpec((SEQ, HEAD_DIM), lambda b, h: (b, k_blk0 + h)),
            pl.BlockSpec((SEQ, HEAD_DIM), lambda b, h: (b, v_blk0 + h)),
            pl.BlockSpec((SEQ, gw), lambda b, h: (b, g_blk0 + h)),
            pl.BlockSpec((CTX_LEN, HEAD_DIM), lambda b, h: (b, h)),
            pl.BlockSpec((CTX_LEN, HEAD_DIM), lambda b, h: (b, N_KV_HEADS + h)),
        ],
        out_specs=pl.BlockSpec((SEQ, gw), lambda b, h: (b, h)),
        out_shape=jax.ShapeDtypeStruct((batch * SEQ, ATTN_WIDTH), BF16),
        scratch_shapes=[pltpu.VMEM((HEAD_DIM + ONES_ROWS, SEQ), BF16),
                        pltpu.VMEM((HEAD_DIM + ONES_ROWS, CTX_LEN), BF16),
                        pltpu.VMEM((3, 3 * BLOCK, GQA_GROUP * BLOCK), F32)],
        compiler_params=_cparams(("arbitrary", "arbitrary")),
        name="banded_attention",
    )(sink, px, px, px, px, ckv, ckv)


def _out_proj_kernel(*refs, final):
    if final:
        a_ref, w_ref, x_ref, gate_ref, fg_ref, o_ref = refs
    else:
        a_ref, w_ref, x_ref, gate_ref, ng_ref, sh_ref, sc_ref, o_ref, hx_ref = refs
    acc = jnp.dot(a_ref[...], w_ref[...], preferred_element_type=F32)
    y = x_ref[...] + gate_ref[0] * acc
    if final:
        ms = jnp.mean(y * y, axis=-1, keepdims=True)
        y = y * lax.rsqrt(ms + NORM_EPS) * fg_ref[...]
    else:
        gain = ng_ref[...] * (1.0 + sc_ref[0])
        hx_ref[...] = _norm_mod_rows(y, gain, sh_ref[0]).astype(hx_ref.dtype)
    o_ref[...] = y


def _out_proj(a, w, x, mods, layer, *, final_g=None, next_norm_g=None, tm=512):
    m, d = x.shape
    kdim = a.shape[1]
    final = final_g is not None
    row_tile = pl.BlockSpec((tm, d), lambda i: (i, 0))
    vec = pl.BlockSpec((1, d), lambda i: (0, 0))
    in_specs = [
        pl.BlockSpec((tm, kdim), lambda i: (i, 0)),
        pl.BlockSpec((kdim, d), lambda i: (0, 0)),
        row_tile,
        _mod_spec(_mod_row(layer, 0, 2), tm, SEQ, d),
    ]
    args = [a, w, x, mods]
    out_specs = [row_tile]
    out_shape = [jax.ShapeDtypeStruct((m, d), F32)]
    if final:
        in_specs.append(vec)
        args.append(final_g.reshape(1, d))
    else:
        in_specs += [vec, _mod_spec(_mod_row(layer + 1, 0, 0), tm, SEQ, d),
                     _mod_spec(_mod_row(layer + 1, 0, 1), tm, SEQ, d)]
        args += [next_norm_g.reshape(1, d), mods, mods]
        out_specs.append(row_tile)
        out_shape.append(jax.ShapeDtypeStruct((m, d), BF16))
    return pl.pallas_call(
        functools.partial(_out_proj_kernel, final=final),
        grid=(m // tm,),
        in_specs=in_specs,
        out_specs=out_specs,
        out_shape=out_shape,
        compiler_params=_cparams(("parallel",)),
        name="out_proj",
    )(*args)


HALF = SEQ // 2
QUARTER = SEQ // 4
FLIP_BLOCK = 256


def _fold_tables():
    def tab(f, n):
        t2 = 2 * np.arange(n, dtype=np.int64) + 1
        ang = ((f[:, None] * t2[None, :]) % (2 * FFT_N)).astype(np.float64) * (math.pi / FFT_N)
        return np.cos(ang), np.sin(ang)

    r = np.arange(QUARTER, dtype=np.int64)
    g = np.arange(HALF, dtype=np.int64)
    f_ee, f_eo, f_o = 4 * r, 4 * r + 2, 2 * g + 1
    cee, see = tab(f_ee, QUARTER)
    ceo, seo = tab(f_eo, QUARTER)
    co, so = tab(f_o, HALF)
    see[0, :] = np.where(np.arange(QUARTER) % 2 == 0, 1.0, -1.0)
    const = lambda a, dt: jnp.asarray(np.ascontiguousarray(a).astype(np.float32)).astype(dt)
    phase = lambda f: np.stack([np.cos(f * (math.pi / FFT_N)), np.sin(f * (math.pi / FFT_N))])
    return dict(
        fq=const(np.stack([cee, see, ceo, seo]), BF16),
        fo=const(np.stack([co, so]), BF16),
        iq=const(np.stack([cee.T, ceo.T, see.T, seo.T]), BF16),
        io=const(np.stack([so.T, co.T]), BF16),
        anti=const(np.eye(FLIP_BLOCK)[::-1], BF16),
        pq=const(np.stack([phase(f_ee), phase(f_eo)], axis=1)[..., None], F32),
        po=const(phase(f_o)[..., None], F32),
    )


def _flip_rows(h, anti):
    nb = h.shape[0] // FLIP_BLOCK
    return jnp.concatenate(
        [jnp.dot(anti[...], h[(nb - 1 - a) * FLIP_BLOCK:(nb - a) * FLIP_BLOCK, :], preferred_element_type=F32)
         for a in range(nb)], axis=0)


def _fold_two_levels(x, anti):
    q0, q1, q2, q3 = (x[k * QUARTER:(k + 1) * QUARTER] for k in range(4))
    f1, f2, f3 = (_flip_rows(q, anti) for q in (q1, q2, q3))
    q0, q1, q2 = (q.astype(F32) for q in (q0, q1, q2))
    xs0, xa0 = q0 + f3, q0 - f3
    fxs1, fxa1 = f1 + q2, f1 - q2
    bf = lambda v: v.astype(BF16)
    xs = jnp.concatenate([bf(xs0), bf(q1 + f2)], axis=0)
    xa = jnp.concatenate([bf(xa0), bf(q1 - f2)], axis=0)
    return xs, xa, bf(xs0 + fxs1), bf(xs0 - fxs1), bf(xa0 + fxa1), bf(xa0 - fxa1)


def _staggered(n, matmuls, finish):
    matmuls(0, 0)
    for t in range(1, n):
        matmuls(t, t % 2)
        finish(t - 1, (t - 1) % 2)
    finish(n - 1, (n - 1) % 2)


def _col_chunks(n):
    return [slice(c, c + MXU_COLS) for c in range(0, n, MXU_COLS)]


PLANES = {"a_ee": 0, "b_ee": QUARTER, "a_eo": 2 * QUARTER, "b_eo": 3 * QUARTER,
          "a_o": 2 * HALF, "b_o": 3 * HALF}
GROUPS = (("a_ee", "b_ee", QUARTER), ("a_eo", "b_eo", QUARTER), ("a_o", "b_o", HALF))


def _spectrum(fq_ref, fo_ref, x, anti):
    xs, xa, xss, xsa, xas, xaa = _fold_two_levels(x, anti)
    dq = lambda k, v: jnp.dot(fq_ref[k], v, preferred_element_type=F32)
    do = lambda k, v: jnp.dot(fo_ref[k], v, preferred_element_type=F32)
    return {"a_ee": dq(0, xss), "b_ee": dq(1, xaa), "a_eo": dq(2, xsa), "b_eo": dq(3, xas),
            "a_o": do(0, xa), "b_o": do(1, xs)}


def _plane(ref, lead, name, rows, cols):
    r0 = PLANES[name]
    return ref[lead, r0:r0 + rows, cols]


def _resident(shape):
    zeros = (0,) * len(shape)
    return pl.BlockSpec(shape, lambda *_: zeros, pipeline_mode=pl.Buffered(1))


def _pack_filter_mlp(w1, b1, w2, b2, freq):
    hpad = LANES - FILTER_HIDDEN
    bands = (FILTER_EMB - 1) // 2
    row = lambda v: jnp.pad(v.reshape(1, -1), ((0, 7), (0, hpad)))
    mat = lambda m: jnp.pad(m, ((0, LANES - m.shape[0]), (0, hpad)))
    return jnp.concatenate([row(w1[0]), row(b1), row(b2), row(freq), jnp.zeros((LANES - 32, LANES), F32),
                            mat(w1[1:1 + bands]), mat(w1[1 + bands:]), mat(w2)], axis=0)


def _filter_sd_body(fp_ref, w3f_ref, b3f_ref, w3b_ref, b3b_ref, d_ref, o_ref, hid_ref, *, first, j):
    length = o_ref.shape[1]
    tn = o_ref.shape[2]
    hp = lax.Precision.HIGHEST
    w1t_ref, b1_ref, b2_ref, fq_ref = (fp_ref.at[r:r + 1] for r in (0, 8, 16, 24))
    w1c_ref, w1s_ref, w2_ref = (fp_ref.at[r:r + LANES] for r in (LANES, 2 * LANES, 3 * LANES))

    @pl.when(first)
    def _():
        n = lax.broadcasted_iota(jnp.int32, (length, LANES), 0).astype(F32)
        lane = lax.broadcasted_iota(jnp.int32, (length, LANES), 1)
        bands = (FILTER_EMB - 1) // 2
        fr_step = (bands - 1 - 1e-4) / (bands - 1)
        fr = jnp.where(lane < bands, 1e-4 + lane.astype(F32) * fr_step, 0.0)
        ang = (2.0 * math.pi * n / length) * fr
        t = n / (length - 1)
        fq = fq_ref[...]
        pre = (t * w1t_ref[...]
               + jnp.dot(jnp.cos(ang), w1c_ref[...], precision=hp, preferred_element_type=F32)
               + jnp.dot(-jnp.sin(ang), w1s_ref[...], precision=hp, preferred_element_type=F32)
               + b1_ref[...])
        hid = jnp.sin(fq * pre)
        hid = jnp.sin(fq * (jnp.dot(hid, w2_ref[...], precision=hp, preferred_element_type=F32)
                            + b2_ref[...]))
        hid_hi = hid.astype(BF16)
        hid_ref[0] = hid_hi
        hid_ref[1] = (hid - hid_hi.astype(F32)).astype(BF16)

    def dot3(w_ref):
        wf = w_ref[...]
        wf = jnp.concatenate([wf, jnp.zeros((LANES - wf.shape[0], wf.shape[1]), F32)], axis=0)
        w_hi = wf.astype(BF16)
        w_lo = (wf - w_hi.astype(F32)).astype(BF16)
        return (jnp.dot(hid_ref[0], w_hi, preferred_element_type=F32)
                + jnp.dot(hid_ref[1], w_hi, preferred_element_type=F32)
                + jnp.dot(hid_ref[0], w_lo, preferred_element_type=F32))

    row = lax.broadcasted_iota(jnp.int32, (length, tn), 0)
    chan = (lax.broadcasted_iota(jnp.int32, (length, tn), 1) + j * tn).astype(F32)
    min_decay = math.log(DECAY_TARGET) / DECAY_SLOW
    max_decay = math.log(DECAY_TARGET) / DECAY_FAST
    delta = min_decay + chan * ((max_decay - min_decay) / (HYENA_WIDTH - 1))
    t = row.astype(F32) / (length - 1)
    window = jnp.exp(-t * jnp.abs(delta)) + WINDOW_SHIFT
    hf = (dot3(w3f_ref) + b3f_ref[...]) * window
    hb = (dot3(w3b_ref) + b3b_ref[...]) * window
    hf = jnp.where(row == 0, hf + d_ref[0], hf)
    hb = jnp.where(row == 0, 0.0, hb)
    o_ref[0] = (hf + hb).astype(o_ref.dtype)
    o_ref[1] = (hf - hb).astype(o_ref.dtype)


def _prep_kernel(c_ref, aw_ref, ab_ref, *refs, filter_steps, ct):
    wa_ref, wh_ref = refs[-7:-5]
    mods_ref, sd_ref, wa_bf_ref, wh_bf_ref, hid_ref = refs[-5:]
    s = pl.program_id(0)
    _mods_kernel(c_ref, aw_ref, ab_ref, mods_ref)

    @pl.when(s < filter_steps)
    def _():
        _filter_sd_body(*refs[:-7], sd_ref, hid_ref, first=s == 0, j=s % ct)

    @pl.when(s < CAST_STEPS)
    def _():
        wa_bf_ref[...] = wa_ref[...].astype(BF16)

    @pl.when((s >= CAST_STEPS) & (s < 2 * CAST_STEPS))
    def _():
        wh_bf_ref[...] = wh_ref[...].astype(BF16)


CAST_STEPS = 8


def _prep(cc, ada_w, ada_b, w1, b1, w2, b2, w3, b3, freq, bias_d, w_out_a, w_out_h, *,
          tn_mods=512, tn=256):
    depth, d, n_mods = ada_w.shape
    mt = n_mods // tn_mods
    w = HYENA_WIDTH
    ct = w // tn
    filter_steps = 2 * ct
    assert filter_steps <= depth * mt
    n = w3.shape[1]
    fp = _pack_filter_mlp(w1, b1, w2, b2, freq)
    b3r = b3.reshape(1, n)
    order = lambda s: jnp.minimum(s, filter_steps - 1) // ct
    chan = lambda s: jnp.minimum(s, filter_steps - 1) % ct
    fwd_cols = lambda s: (0, 2 * order(s) * ct + chan(s))
    bwd_cols = lambda s: (0, (2 * order(s) + 1) * ct + chan(s))
    assert w_out_a.shape == w_out_h.shape and 2 * CAST_STEPS <= depth * mt
    cast_blk = (w_out_a.shape[0] // CAST_STEPS, w_out_a.shape[1])
    first_rows = lambda s: (jnp.minimum(s, CAST_STEPS - 1), 0)
    second_rows = lambda s: (jnp.clip(s - CAST_STEPS, 0, CAST_STEPS - 1), 0)
    return pl.pallas_call(
        functools.partial(_prep_kernel, filter_steps=filter_steps, ct=ct),
        grid=(depth * mt,),
        in_specs=[pl.BlockSpec((MOD_ROWS, d), lambda s: (0, 0)),
                  pl.BlockSpec((1, d, tn_mods), lambda s: (s // mt, 0, s % mt)),
                  pl.BlockSpec((1, 1, tn_mods), lambda s: (s // mt, 0, s % mt)),
                  pl.BlockSpec(fp.shape, lambda s: (0, 0)),
                  pl.BlockSpec((FILTER_HIDDEN, tn), fwd_cols), pl.BlockSpec((1, tn), fwd_cols),
                  pl.BlockSpec((FILTER_HIDDEN, tn), bwd_cols), pl.BlockSpec((1, tn), bwd_cols),
                  pl.BlockSpec((1, 1, tn), lambda s: (order(s), 0, chan(s))),
                  pl.BlockSpec(cast_blk, first_rows), pl.BlockSpec(cast_blk, second_rows)],
        out_specs=[pl.BlockSpec((1, MOD_ROWS, tn_mods), lambda s: (s // mt, 0, s % mt)),
                   pl.BlockSpec((2, SEQ, tn), lambda s: (0, 0, order(s) * ct + chan(s))),
                   pl.BlockSpec(cast_blk, first_rows), pl.BlockSpec(cast_blk, second_rows)],
        out_shape=[jax.ShapeDtypeStruct((depth, MOD_ROWS, n_mods), F32),
                   jax.ShapeDtypeStruct((2, SEQ, 2 * w), BF16),
                   jax.ShapeDtypeStruct(w_out_a.shape, BF16),
                   jax.ShapeDtypeStruct(w_out_h.shape, BF16)],
        scratch_shapes=[pltpu.VMEM((2, SEQ, LANES), BF16)],
        compiler_params=_cparams(("arbitrary",)),
        name="mods_and_filters",
    )(cc, ada_w, ada_b.reshape(depth, 1, n_mods),
      fp, w3, b3r, w3, b3r, bias_d.reshape(2, 1, w),
      w_out_a, w_out_h)


def _filter_spec_kernel(s_ref, d_ref, fq_ref, fo_ref, anti_ref, pq_ref, po_ref, k_ref):
    phases = {"a_ee": (pq_ref[0, 0], pq_ref[1, 0]), "a_eo": (pq_ref[0, 1], pq_ref[1, 1]),
              "a_o": (po_ref[0], po_ref[1])}
    is0 = lax.broadcasted_iota(jnp.int32, (QUARTER, MXU_COLS), 0) == 0
    for cols in _col_chunks(k_ref.shape[2]):
        sp = _spectrum(fq_ref, fo_ref, s_ref[0, :, cols], anti_ref)
        dp = _spectrum(fq_ref, fo_ref, d_ref[0, :, cols], anti_ref)
        for a, b, rows in GROUPS:
            cos, sin = phases[a]
            ka = sp[a] * cos + sp[b] * sin
            kb = dp[b] * cos - dp[a] * sin
            if a == "a_ee":
                scale = jnp.where(is0, 1.0 / FFT_N, 2.0 / FFT_N)
                kb = jnp.where(is0, sp[b], kb)
            else:
                scale = 2.0 / FFT_N
            k_ref[0, PLANES[a]:PLANES[a] + rows, cols] = (ka * scale).astype(k_ref.dtype)
            k_ref[0, PLANES[b]:PLANES[b] + rows, cols] = (kb * scale).astype(k_ref.dtype)


def _filter_spectra(sd, tabs, *, tn=512):
    w = HYENA_WIDTH
    ct = w // tn
    consts = [tabs[k] for k in ("fq", "fo", "anti", "pq", "po")]
    return pl.pallas_call(
        _filter_spec_kernel,
        grid=(2, ct),
        in_specs=[
            pl.BlockSpec((1, SEQ, tn), lambda o, j: (0, 0, o * ct + j)),
            pl.BlockSpec((1, SEQ, tn), lambda o, j: (1, 0, o * ct + j)),
        ] + [_resident(c.shape) for c in consts],
        out_specs=pl.BlockSpec((1, 4 * HALF, tn), lambda o, j: (o, 0, j)),
        out_shape=jax.ShapeDtypeStruct((2, 4 * HALF, w), F32),
        compiler_params=_cparams(("parallel", "parallel")),
        name="filter_spectra",
    )(sd, sd, *consts)


def _conv_fwd_kernel(x_ref, fq_ref, fo_ref, anti_ref, k_ref, y_ref):
    is0 = lax.broadcasted_iota(jnp.int32, (QUARTER, MXU_COLS), 0) == 0
    for cols in _col_chunks(y_ref.shape[2]):
        sp = _spectrum(fq_ref, fo_ref, x_ref[:, cols], anti_ref)
        for a, b, rows in GROUPS:
            ka = _plane(k_ref, 0, a, rows, cols).astype(F32)
            kb = _plane(k_ref, 0, b, rows, cols).astype(F32)
            bkb = sp[b] * kb
            if a == "a_ee":
                ya = sp[a] * ka - jnp.where(is0, 0.0, bkb)
                yb = jnp.where(is0, bkb, sp[a] * kb + sp[b] * ka)
            else:
                ya = sp[a] * ka - bkb
                yb = sp[a] * kb + sp[b] * ka
            y_ref[0, PLANES[a]:PLANES[a] + rows, cols] = ya.astype(y_ref.dtype)
            y_ref[0, PLANES[b]:PLANES[b] + rows, cols] = yb.astype(y_ref.dtype)


def _conv_fwd(x, x_blk0, kspec, order, tabs, batch, *, tn=512):
    w = HYENA_WIDTH
    ct = w // tn
    consts = [tabs[k] for k in ("fq", "fo", "anti")]
    return pl.pallas_call(
        _conv_fwd_kernel,
        grid=(ct, batch),
        in_specs=[pl.BlockSpec((SEQ, tn), lambda j, b: (b, x_blk0 * ct + j))]
        + [_resident(c.shape) for c in consts]
        + [pl.BlockSpec((1, 4 * HALF, tn), lambda j, b: (order, 0, j))],
        out_specs=pl.BlockSpec((1, 4 * HALF, tn), lambda j, b: (b, 0, j)),
        out_shape=jax.ShapeDtypeStruct((batch, 4 * HALF, w), BF16),
        compiler_params=_cparams(("parallel", "parallel")),
        name="long_conv_fwd",
    )(x, *consts, kspec)


def _conv_inv_kernel(*refs, gated):
    if gated:
        y_ref, iq_ref, io_ref, anti_ref, xm_ref, g_ref, o_ref, accq_ref, acco_ref = refs
    else:
        y_ref, iq_ref, io_ref, anti_ref, xm_ref, o_ref, accq_ref, acco_ref = refs
    chunks = _col_chunks(o_ref.shape[1])
    flipped = lambda v: _flip_rows(v.astype(BF16), anti_ref)

    def matmuls(t, slot):
        cols = chunks[t]
        for k, name in enumerate(("a_ee", "a_eo", "b_ee", "b_eo")):
            accq_ref[slot, k] = jnp.dot(iq_ref[k], _plane(y_ref, 0, name, QUARTER, cols),
                                        preferred_element_type=F32)
        for k, name in enumerate(("b_o", "a_o")):
            acco_ref[slot, k] = jnp.dot(io_ref[k], _plane(y_ref, 0, name, HALF, cols),
                                        preferred_element_type=F32)

    def finish(t, slot):
        cols = chunks[t]
        p, r, q, s = (accq_ref[slot, k] for k in range(4))
        osum = acco_ref[slot, 0] + acco_ref[slot, 1]
        odif = acco_ref[slot, 0] - acco_ref[slot, 1]
        e1, e2, d1, d2 = p + r, q + s, p - r, s - q
        quarters = (e1 + e2 + osum[:QUARTER],
                    flipped(d1 + d2) + osum[QUARTER:],
                    d1 - d2 + flipped(odif[QUARTER:]),
                    flipped(e1 - e2 + odif[:QUARTER]))
        for k, y in enumerate(quarters):
            rows = slice(k * QUARTER, (k + 1) * QUARTER)
            mult = xm_ref[rows, cols]
            if gated:
                mult = mult * g_ref[rows, cols]
            o_ref[rows, cols] = (y * mult.astype(F32)).astype(o_ref.dtype)

    _staggered(len(chunks), matmuls, finish)


def _conv_inv(y, tabs, p, m_blk0, g_blk0=None, *, tn=512):
    batch = y.shape[0]
    w = HYENA_WIDTH
    ct = w // tn
    gated = g_blk0 is not None
    col = lambda blk0: (lambda b, j: (b, blk0 * ct + j))
    consts = [tabs[k] for k in ("iq", "io", "anti")]
    in_specs = ([pl.BlockSpec((1, 4 * HALF, tn), lambda b, j: (b, 0, j))]
                + [_resident(c.shape) for c in consts]
                + [pl.BlockSpec((SEQ, tn), col(m_blk0))])
    args = [y, *consts, p]
    if gated:
        in_specs.append(pl.BlockSpec((SEQ, tn), col(g_blk0)))
        args.append(p)
    return pl.pallas_call(
        functools.partial(_conv_inv_kernel, gated=gated),
        grid=(batch, ct),
        in_specs=in_specs,
        out_specs=pl.BlockSpec((SEQ, tn), lambda b, j: (b, j)),
        out_shape=jax.ShapeDtypeStruct((batch * SEQ, w), BF16),
        scratch_shapes=[pltpu.VMEM((2, 4, QUARTER, MXU_COLS), F32),
                        pltpu.VMEM((2, 2, HALF, MXU_COLS), F32)],
        compiler_params=_cparams(("parallel", "parallel")),
        name="long_conv_inv",
    )(*args)


def _rope_tables():
    pos = np.arange(SEQ)
    row = (pos // GRID_W).astype(np.float32)
    col = (pos % GRID_W).astype(np.float32)
    half = HEAD_DIM // 2
    inv = (ROPE_BASE ** (-np.arange(0, half, 2, dtype=np.float32) / half)).astype(np.float32)
    ar = row[:, None] * inv[None]
    ac = col[:, None] * inv[None]
    cos = np.concatenate([np.cos(ar), np.cos(ar), np.cos(ac), np.cos(ac)], axis=1)
    sin = np.concatenate([-np.sin(ar), np.sin(ar), -np.sin(ac), np.sin(ac)], axis=1)
    return jnp.asarray(cos, F32), jnp.asarray(sin, F32)


def _tile_types(tn, kinds):
    per_tile = tn // LANES
    return [kinds[s0:s0 + per_tile] for s0 in range(0, len(kinds), per_tile)]


def kernel(x, c, ctx, c_ctx, norm_g, ada_w, ada_b, attn_w_in, attn_w_out, attn_sink, hy_w_in,
           hy_conv_w, hy_conv_b, hy_w1, hy_b1, hy_w2, hy_b2, hy_w3, hy_b3, hy_freq, hy_bias_d,
           hy_w_out, final_g):
    batch, seq, d = x.shape
    assert (seq, d) == (SEQ, D_MODEL) and ctx.shape[1] == CTX_LEN
    assert norm_g.shape[0] == 2 and attn_w_in.shape[0] == 1 and hy_w_in.shape[0] == 1
    w = HYENA_WIDTH

    ctx_row = batch
    cc = jnp.concatenate([c, c_ctx[None], jnp.zeros((MOD_ROWS - batch - 1, d), F32)], axis=0)
    mods, sd, w_out_attn, w_out_hyena = _prep(
        cc, ada_w, ada_b, hy_w1[0], hy_b1[0], hy_w2[0], hy_b2[0], hy_w3[0], hy_b3[0], hy_freq[0],
        hy_bias_d[0], attn_w_out[0], hy_w_out[0])
    mods = mods.reshape(-1, 1, d)

    x2 = x.reshape(batch * seq, d)
    ctx2 = ctx.reshape(batch * CTX_LEN, d)

    tn = 1024
    kinds0 = (["q"] * N_HEADS + ["k"] * N_KV_HEADS + [None] * N_KV_HEADS
              + ["g"] * (ATTN_WIDTH // LANES))
    hx = _norm_mod(x2, norm_g[0], mods, 0, tm=1024)
    px = _proj(hx, attn_w_in[0], 0, ATTN_IN, tm=SEQ, tn=tn, tile_types=_tile_types(tn, kinds0),
               rope=_rope_tables(), q_scale=HEAD_DIM ** -0.5 * LOG2E)
    ckv = _ctx_kv(ctx2, norm_g[0], mods, 0, ctx_row, attn_w_in[0], ATTN_WIDTH, 2 * KV_WIDTH)
    og = _attention(px, ckv, attn_sink[0], batch)
    x2, hx = _out_proj(og, w_out_attn, x2, mods, 0, next_norm_g=norm_g[1])

    kinds1 = ["c"] * (3 * w // LANES) + ["g"] * (w // LANES)
    tn1 = 1024
    p = _proj(hx, hy_w_in[0], 0, 4 * w, tm=SEQ, tn=tn1, tile_types=_tile_types(tn1, kinds1),
              conv=(hy_conv_w[0], hy_conv_b[0].reshape(1, 3 * w)))
    tabs = _fold_tables()
    kspec = _filter_spectra(sd, tabs)
    y1 = _conv_fwd(p, 2, kspec, 0, tabs, batch)
    z = _conv_inv(y1, tabs, p, 0)
    y2 = _conv_fwd(z, 0, kspec, 1, tabs, batch)
    yg = _conv_inv(y2, tabs, p, 1, g_blk0=3)
    out, = _out_proj(yg, w_out_hyena, x2, mods, 1, final_g=final_g)
    return out.reshape(batch, seq, d)
```
